```python
import math
import jax
import jax.numpy as jnp
from jax import lax
import numpy as np

D_MODEL = 2048
BATCH = 4
SEQ = 2048
DEPTH = 1

ATTN_HEAD_DIM = 64
D_ATTN = D_MODEL // 2
N_ATTN_HEADS = D_ATTN // ATTN_HEAD_DIM
ROT_DIM = ATTN_HEAD_DIM // 4
ROPE_THETA = 500000.0
DILATED_PATTERNS = ((128, 1), (512, 4), (2048, 16))
ATTN_BLOCK = 128
SSM_HEAD_DIM = 64
D_SSM = D_MODEL // 2
N_SSM_HEADS = D_SSM // SSM_HEAD_DIM
SSM_GROUPS = 4
SSM_STATE = 128
CONV_WIDTH = 4
SSD_CHUNK = 128
D_CONV = D_SSM + 2 * SSM_GROUPS * SSM_STATE
DT_MIN = 1e-3
DT_MAX = 1e-1
D_MIX = D_ATTN + D_SSM
D_IN = 3 * D_ATTN + D_SSM + D_CONV + N_SSM_HEADS
N_MEM = 256
N_CROSS_HEADS = 4
CROSS_HEAD_DIM = 128
D_CROSS = N_CROSS_HEADS * CROSS_HEAD_DIM
D_FF = 4 * D_MODEL
EPS = 1e-6

kernel_name = "hymba_ssd_dilated_attn_memxattn_sqrelu"


def rms_norm(x, g):
    xf = x.astype(jnp.float32)
    y = xf * lax.rsqrt(jnp.mean(xf * xf, axis=-1, keepdims=True) + EPS)
    return (y * g.astype(jnp.float32)).astype(x.dtype)


def partial_rope(x, positions):
    half = ROT_DIM // 2
    inv_freq = ROPE_THETA ** (-2.0 * jnp.arange(half, dtype=jnp.float32) / ROT_DIM)
    ang = positions.astype(jnp.float32)[..., None] * inv_freq
    cos = jnp.cos(ang)[:, :, None, :]
    sin = jnp.sin(ang)[:, :, None, :]
    xf = x.astype(jnp.float32)
    x1 = xf[..., :half]
    x2 = xf[..., half:ROT_DIM]
    out = jnp.concatenate([x1 * cos - x2 * sin, x2 * cos + x1 * sin, xf[..., ROT_DIM:]], axis=-1)
    return out.astype(x.dtype)


def dilated_window_branch(q, k, v, window, dilation):
    bsz, s_len, n_h, hd = q.shape
    steps = window // dilation
    span = dilation * ATTN_BLOCK
    s_pad = -(-s_len // span) * span
    nb = s_pad // span

    def to_blocks(t):
        t = jnp.pad(t, ((0, 0), (0, s_pad - s_len), (0, 0), (0, 0)))
        return t.reshape(bsz, nb, ATTN_BLOCK, dilation, n_h, hd)

    qb, kb, vb = to_blocks(q), to_blocks(k), to_blocks(v)

    def band(t):
        prev = jnp.concatenate([jnp.zeros_like(t[:, :1]), t[:, :-1]], axis=1)
        return jnp.concatenate([prev, t], axis=2)

    kband, vband = band(kb), band(vb)
    s = jnp.einsum('bnqrhd,bnkrhd->bnrhqk', qb, kband)
    qi = jnp.arange(ATTN_BLOCK)[:, None]
    kj = jnp.arange(2 * ATTN_BLOCK)[None, :]
    dist = qi + ATTN_BLOCK - kj
    in_window = (dist >= 0) & (dist <= steps)
    has_prev = (jnp.arange(nb)[:, None, None] > 0) | (kj[None] >= ATTN_BLOCK)
    mask = in_window[None] & has_prev
    s = jnp.where(mask[None, :, None, None], s, -jnp.inf)
    m = jnp.max(s, axis=-1, keepdims=True)
    p = jnp.exp(s - m)
    denom = jnp.sum(p, axis=-1)
    lse = m[..., 0] + jnp.log(denom)
    o = jnp.einsum('bnrhqk,bnkrhd->bnqrhd', p, vband)
    o = o / jnp.transpose(denom, (0, 1, 4, 2, 3))[..., None]
    o = o.reshape(bsz, s_pad, n_h, hd)[:, :s_len]
    lse = jnp.transpose(lse, (0, 1, 4, 2, 3)).reshape(bsz, s_pad, n_h)[:, :s_len]
    return o, lse


def dilated_attention(q, k, v, positions, g_q, g_k):
    q = partial_rope(rms_norm(q, g_q), positions).astype(jnp.float32) * (ATTN_HEAD_DIM ** -0.5)
    k = partial_rope(rms_norm(k, g_k), positions).astype(jnp.float32)
    v = v.astype(jnp.float32)
    outs, lses = [], []
    for window, dilation in DILATED_PATTERNS:
        o, l = dilated_window_branch(q, k, v, window, dilation)
        outs.append(o)
        lses.append(l)
    wts = jax.nn.softmax(jnp.stack(lses, axis=0), axis=0)
    return jnp.einsum('gbsh,gbshd->bshd', wts, jnp.stack(outs, axis=0))


def ssd_chunked(x, dt, a, b_mat, c_mat):
    bsz, l_len, n_h, p_dim = x.shape
    g, n = b_mat.shape[2], b_mat.shape[3]
    hg = n_h // g
    nc = l_len // SSD_CHUNK
    q = SSD_CHUNK
    xc = x.reshape(bsz, nc, q, g, hg, p_dim)
    dtc = dt.reshape(bsz, nc, q, g, hg)
    bc = b_mat.reshape(bsz, nc, q, g, n)
    cc = c_mat.reshape(bsz, nc, q, g, n)
    a_cs = jnp.cumsum(dtc * a.reshape(g, hg), axis=2)
    seg = a_cs[:, :, :, None] - a_cs[:, :, None, :]
    causal = jnp.tril(jnp.ones((q, q), dtype=bool))[:, :, None, None]
    l_mat = jnp.exp(jnp.where(causal, seg, -jnp.inf))
    cb = jnp.einsum('bclgn,bcsgn->bclsg', cc, bc)
    w = cb[..., None] * l_mat * dtc[:, :, None]
    y_diag = jnp.einsum('bclsgh,bcsghp->bclghp', w, xc)
    decay_states = jnp.exp(a_cs[:, :, -1:] - a_cs)
    states = jnp.einsum('bcsgn,bcsgh,bcsghp->bcghpn', bc, decay_states * dtc, xc)
    chunk_decay = jnp.exp(a_cs[:, :, -1])

    def step(h, inp):
        s_c, a_c = inp
        return h * a_c[..., None, None] + s_c, h

    h0 = jnp.zeros((bsz, g, hg, p_dim, n), jnp.float32)
    _, prev = lax.scan(step, h0, (jnp.moveaxis(states, 1, 0), jnp.moveaxis(chunk_decay, 1, 0)))
    prev = jnp.moveaxis(prev, 0, 1)
    y_off = jnp.einsum('bclgn,bcghpn->bclghp', cc, prev) * jnp.exp(a_cs)[..., None]
    return (y_diag + y_off).reshape(bsz, l_len, n_h, p_dim)


def ssd_mixer(z, xbc, dt_raw, conv_w, conv_b, dt_bias, a_log, d_skip, g_out):
    bsz, l_len, _ = xbc.shape
    xbc = lax.conv_general_dilated(
        xbc, conv_w.astype(xbc.dtype)[:, None, :], window_strides=(1,),
        padding=[(CONV_WIDTH - 1, 0)], dimension_numbers=('NWC', 'WIO', 'NWC'),
        feature_group_count=D_CONV) + conv_b.astype(xbc.dtype)
    xbc = jax.nn.silu(xbc)
    xs = xbc[..., :D_SSM].astype(jnp.float32).reshape(bsz, l_len, N_SSM_HEADS, SSM_HEAD_DIM)
    b_mat = xbc[..., D_SSM:D_SSM + SSM_GROUPS * SSM_STATE].astype(jnp.float32).reshape(bsz, l_len, SSM_GROUPS, SSM_STATE)
    c_mat = xbc[..., D_SSM + SSM_GROUPS * SSM_STATE:].astype(jnp.float32).reshape(bsz, l_len, SSM_GROUPS, SSM_STATE)
    dt = jax.nn.softplus(dt_raw.astype(jnp.float32) + dt_bias.astype(jnp.float32))
    a = -jnp.exp(a_log.astype(jnp.float32))
    y = ssd_chunked(xs, dt, a, b_mat, c_mat) + d_skip.astype(jnp.float32)[:, None] * xs
    y = y.reshape(bsz, l_len, D_SSM) * jax.nn.silu(z.astype(jnp.float32))
    y = rms_norm(y.reshape(bsz, l_len, SSM_GROUPS, D_SSM // SSM_GROUPS),
                 g_out.reshape(SSM_GROUPS, D_SSM // SSM_GROUPS))
    return y.reshape(bsz, l_len, D_SSM).astype(z.dtype)


def cross_attention(h, mem_h, w_q, w_kv, w_o, g_q, g_k):
    bsz, s_len, _ = h.shape
    q = (h @ w_q).reshape(bsz, s_len, N_CROSS_HEADS, CROSS_HEAD_DIM)
    kv = (mem_h @ w_kv).reshape(bsz, -1, 2, N_CROSS_HEADS, CROSS_HEAD_DIM)
    k, v = kv[:, :, 0], kv[:, :, 1]
    q = rms_norm(q, g_q).astype(jnp.float32) * (CROSS_HEAD_DIM ** -0.5)
    k = rms_norm(k, g_k).astype(jnp.float32)
    p = jax.nn.softmax(jnp.einsum('bshd,bmhd->bhsm', q, k), axis=-1)
    o = jnp.einsum('bhsm,bmhd->bshd', p, v.astype(jnp.float32)).astype(h.dtype)
    return o.reshape(bsz, s_len, D_CROSS) @ w_o


def setup_inputs(seed: int = 0) -> dict:
    key = jax.random.key(seed)
    ks = jax.random.split(key, 26)
    f32 = jnp.float32

    def nrm(k, shape, scale):
        return jax.random.normal(k, shape, f32) * scale

    def gain(k, shape):
        return 1.0 + 0.02 * jax.random.normal(k, shape, f32)

    dt0 = jnp.exp(jax.random.uniform(ks[9], (DEPTH, N_SSM_HEADS), f32, math.log(DT_MIN), math.log(DT_MAX)))
    return {
        "x": nrm(ks[0], (BATCH, SEQ, D_MODEL), 1.0),
        "mem": nrm(ks[1], (BATCH, N_MEM, D_MODEL), 1.0),
        "positions": jnp.broadcast_to(jnp.arange(SEQ, dtype=jnp.int32), (BATCH, SEQ)),
        "g_mix": gain(ks[2], (DEPTH, D_MODEL)),
        "w_in": nrm(ks[3], (DEPTH, D_MODEL, D_IN), D_MODEL ** -0.5),
        "g_q": gain(ks[4], (DEPTH, ATTN_HEAD_DIM)),
        "g_k": gain(ks[5], (DEPTH, ATTN_HEAD_DIM)),
        "g_attn_out": gain(ks[6], (DEPTH, D_ATTN)),
        "conv_w": nrm(ks[7], (DEPTH, CONV_WIDTH, D_CONV), CONV_WIDTH ** -0.5),
        "conv_b": nrm(ks[8], (DEPTH, D_CONV), 0.02),
        "dt_bias": dt0 + jnp.log(-jnp.expm1(-dt0)),
        "a_log": jnp.log(jax.random.uniform(ks[10], (DEPTH, N_SSM_HEADS), f32, 1.0, 16.0)),
        "d_skip": 1.0 + 0.1 * jax.random.normal(ks[11], (DEPTH, N_SSM_HEADS), f32),
        "g_ssm_out": gain(ks[12], (DEPTH, D_SSM)),
        "w_out": nrm(ks[13], (DEPTH, D_MIX, D_MODEL), D_MIX ** -0.5),
        "g_cross": gain(ks[14], (DEPTH, D_MODEL)),
        "g_mem": gain(ks[15], (DEPTH, D_MODEL)),
        "w_cq": nrm(ks[16], (DEPTH, D_MODEL, D_CROSS), D_MODEL ** -0.5),
        "w_ckv": nrm(ks[17], (DEPTH, D_MODEL, 2 * D_CROSS), D_MODEL ** -0.5),
        "g_cq": gain(ks[18], (DEPTH, CROSS_HEAD_DIM)),
        "g_ck": gain(ks[19], (DEPTH, CROSS_HEAD_DIM)),
        "w_co": nrm(ks[20], (DEPTH, D_CROSS, D_MODEL), D_CROSS ** -0.5),
        "g_mlp": gain(ks[21], (DEPTH, D_MODEL)),
        "w_up": nrm(ks[22], (DEPTH, D_MODEL, D_FF), D_MODEL ** -0.5),
        "w_down": nrm(ks[23], (DEPTH, D_FF, D_MODEL), D_FF ** -0.5),
    }


def reference(x, mem, positions, g_mix, w_in, g_q, g_k, g_attn_out, conv_w, conv_b, dt_bias,
              a_log, d_skip, g_ssm_out, w_out, g_cross, g_mem, w_cq, w_ckv, g_cq, g_ck, w_co,
              g_mlp, w_up, w_down):
    bsz, s_len, _ = x.shape
    splits = np.cumsum([D_ATTN, D_ATTN, D_ATTN, D_SSM, D_CONV]).tolist()
    for i in range(DEPTH):
        h = rms_norm(x, g_mix[i])
        q, k, v, z, xbc, dt_raw = jnp.split(h @ w_in[i], splits, axis=-1)
        shp = (bsz, s_len, N_ATTN_HEADS, ATTN_HEAD_DIM)
        attn = dilated_attention(q.reshape(shp), k.reshape(shp), v.reshape(shp), positions, g_q[i], g_k[i])
        attn = rms_norm(attn.reshape(bsz, s_len, D_ATTN), g_attn_out[i]).astype(x.dtype)
        ssm = ssd_mixer(z, xbc, dt_raw, conv_w[i], conv_b[i], dt_bias[i], a_log[i], d_skip[i], g_ssm_out[i])
        x = x + jnp.concatenate([attn, ssm], axis=-1) @ w_out[i]
        x = x + cross_attention(rms_norm(x, g_cross[i]), rms_norm(mem, g_mem[i]),
                                w_cq[i], w_ckv[i], w_co[i], g_cq[i], g_ck[i])
        hm = rms_norm(x, g_mlp[i])
        x = x + jnp.square(jax.nn.relu(hm @ w_up[i])) @ w_down[i]
    return x
```

```python
import functools
import math

import jax
import jax.numpy as jnp
from jax import lax
from jax.experimental import pallas as pl
from jax.experimental.pallas import tpu as pltpu

F32 = jnp.float32
BF16 = jnp.bfloat16
EPS = 1e-6

LANES = 128
ATTN_HEAD_DIM = 64
ROT_DIM = ATTN_HEAD_DIM // 4
ROPE_THETA = 500000.0
ATTN_BLOCK = 128
DILATIONS = (1, 4, 16)
SSM_HEAD_DIM = 64
SSM_GROUPS = 4
SSM_STATE = 128
CONV_WIDTH = 4
SSD_CHUNK = 128
N_CROSS_HEADS = 4
CROSS_HEAD_DIM = 128
VMEM_LIMIT = 52 * 1024 * 1024


def _cparams(sem):
    return pltpu.CompilerParams(dimension_semantics=sem, vmem_limit_bytes=VMEM_LIMIT)


def _rms_rows(x, g):
    ms = jnp.mean(x * x, axis=-1, keepdims=True)
    return x * lax.rsqrt(ms + EPS) * g


def _norm_rows_to(x_ref, g_ref, h_ref, rows=16):
    def body(c, carry):
        r0 = pl.multiple_of(c * rows, rows)
        h_ref[pl.ds(r0, rows), :] = _rms_rows(x_ref[pl.ds(r0, rows), :], g_ref[...]).astype(h_ref.dtype)
        return carry
    lax.fori_loop(0, x_ref.shape[0] // rows, body, 0)


def _in_proj_kernel(x_ref, g_ref, w_ref, wdt_ref, o_ref, dt_ref, h_ref):
    @pl.when(pl.program_id(1) == 0)
    def _():
        _norm_rows_to(x_ref, g_ref, h_ref)
        dt_ref[...] = jnp.dot(h_ref[...], wdt_ref[...], preferred_element_type=F32)
    o_ref[...] = jnp.dot(h_ref[...], w_ref[...], preferred_element_type=F32).astype(o_ref.dtype)


def in_proj(x2d, g, w_main, w_dt, tm=1024, tn=512):
    m, d = x2d.shape
    n = w_main.shape[1]
    return pl.pallas_call(
        _in_proj_kernel,
        grid=(m // tm, n // tn),
        in_specs=[pl.BlockSpec((tm, d), lambda i, j: (i, 0)),
                  pl.BlockSpec((1, d), lambda i, j: (0, 0)),
                  pl.BlockSpec((d, tn), lambda i, j: (0, j)),
                  pl.BlockSpec((d, LANES), lambda i, j: (0, 0))],
        out_specs=[pl.BlockSpec((tm, tn), lambda i, j: (i, j)),
                   pl.BlockSpec((tm, LANES), lambda i, j: (i, 0))],
        out_shape=[jax.ShapeDtypeStruct((m, n), BF16), jax.ShapeDtypeStruct((m, LANES), F32)],
        scratch_shapes=[pltpu.VMEM((tm, d), BF16)],
        compiler_params=_cparams(("parallel", "arbitrary")),
        name="in_proj",
    )(x2d, g, w_main, w_dt)


def _rope_kernel(pos_ref, invf_ref, cmask_ref, amask_ref, bmask_ref, cos_ref, sa_ref, sb_ref):
    ang = pos_ref[...].astype(F32) * invf_ref[...]
    c = jnp.cos(ang)
    s = jnp.sin(ang)
    cm = cmask_ref[...]
    cos_ref[...] = c * cm + (1.0 - cm)
    sa_ref[...] = -s * amask_ref[...]
    sb_ref[...] = s * bmask_ref[...]


def rope_tables(pos_col, tm=512):
    m = pos_col.shape[0]
    half = ROT_DIM // 2
    inv_freq = ROPE_THETA ** (-2.0 * jnp.arange(half, dtype=F32) / ROT_DIM)
    d = jnp.arange(LANES) % ATTN_HEAD_DIM
    invf = inv_freq[d % half][None, :]
    cmask = (d < ROT_DIM).astype(F32)[None, :]
    amask = (d < half).astype(F32)[None, :]
    bmask = ((d >= half) & (d < ROT_DIM)).astype(F32)[None, :]
    row = pl.BlockSpec((1, LANES), lambda i: (0, 0))
    tab = pl.BlockSpec((tm, LANES), lambda i: (i, 0))
    return pl.pallas_call(
        _rope_kernel,
        grid=(m // tm,),
        in_specs=[pl.BlockSpec((tm, 1), lambda i: (i, 0)), row, row, row, row],
        out_specs=[tab, tab, tab],
        out_shape=[jax.ShapeDtypeStruct((m, LANES), F32)] * 3,
        compiler_params=_cparams(("parallel",)),
        name="rope_tab",
    )(pos_col, invf, cmask, amask, bmask)


def _rows(start, size, dil):
    return pl.ds(start, size) if dil == 1 else pl.ds(start, size, stride=dil)


def _attn_kernel(q_ref, k_ref, v_ref, cos_ref, sa_ref, sb_ref, gq_ref, gk_ref, seg_ref, o_ref,
                 qf, kf, vf, ob, mb, lb):
    s_len = q_ref.shape[0]
    blk = ATTN_BLOCK
    half = ROT_DIM // 2
    lane = lax.broadcasted_iota(jnp.int32, (blk, LANES), 1)
    head0 = lane < ATTN_HEAD_DIM
    qi = lax.broadcasted_iota(jnp.int32, (blk, 2 * blk), 0)
    kj = lax.broadcasted_iota(jnp.int32, (blk, 2 * blk), 1)
    band_mask = (kj >= qi) & (kj <= qi + blk)
    first_mask = (lax.broadcasted_iota(jnp.int32, (blk, blk), 1)
                  <= lax.broadcasted_iota(jnp.int32, (blk, blk), 0))

    prep_rows = 256

    def prep(c, carry):
        rows = pl.ds(pl.multiple_of(c * prep_rows, prep_rows), prep_rows)
        cs, sa, sb = cos_ref[rows, :], sa_ref[rows, :], sb_ref[rows, :]

        def norm_rope(x_ref, g_ref):
            x = x_ref[rows, :].astype(F32)
            ss = jnp.dot((x * x).astype(BF16), seg_ref[...], preferred_element_type=F32)
            y = x * lax.rsqrt(ss * (1.0 / ATTN_HEAD_DIM) + EPS) * g_ref[...]
            return y * cs + pltpu.roll(y, LANES - half, 1) * sa + pltpu.roll(y, half, 1) * sb

        qf[rows, :] = norm_rope(q_ref, gq_ref) * (ATTN_HEAD_DIM ** -0.5)
        kf[rows, :] = norm_rope(k_ref, gk_ref)
        vf[rows, :] = v_ref[rows, :].astype(F32)
        return carry

    lax.fori_loop(0, s_len // prep_rows, prep, 0)

    def block(br, dil, q_start, k_start, n_keys, mask):
        qt = qf[_rows(q_start, blk, dil), :]
        kb = kf[_rows(k_start, n_keys, dil), :].astype(BF16)
        vb = vf[_rows(k_start, n_keys, dil), :].astype(BF16)
        res = []
        for hm in (head0, jnp.logical_not(head0)):
            qh = jnp.where(hm, qt, 0.0).astype(BF16)
            s = lax.dot_general(qh, kb, (((1,), (1,)), ((), ())), preferred_element_type=F32)
            s = jnp.where(mask, s, -jnp.inf)
            m = jnp.max(s, axis=-1, keepdims=True)
            p = jnp.exp(s - m)
            l = jnp.sum(p, axis=-1, keepdims=True)
            pv = jnp.dot(p.astype(BF16), vb, preferred_element_type=F32)
            res.append((pv, m, l))
        out_rows = _rows(q_start, blk, dil)
        ob[br, out_rows, :] = jnp.where(head0, res[0][0], res[1][0])
        mb[br, out_rows, :] = jnp.where(head0, res[0][1], res[1][1])
        lb[br, out_rows, :] = jnp.where(head0, res[0][2], res[1][2])

    for br, dil in enumerate(DILATIONS):
        span = dil * blk
        n_blocks = s_len // span

        def first(r, carry, br=br, dil=dil):
            block(br, dil, r, r, blk, first_mask)
            return carry

        def rest(i, carry, br=br, dil=dil, span=span):
            r = i % dil
            n = 1 + i // dil
            q_start = n * span + r
            if dil == 1:
                q_start = pl.multiple_of(q_start, blk)
            block(br, dil, q_start, q_start - span, 2 * blk, band_mask)
            return carry

        lax.fori_loop(0, dil, first, 0)
        if n_blocks > 1:
            lax.fori_loop(0, dil * (n_blocks - 1), rest, 0)

    def merge(c, carry):
        rows = pl.ds(pl.multiple_of(c * prep_rows, prep_rows), prep_rows)
        ms = [mb[g, rows, :] for g in range(len(DILATIONS))]
        m = functools.reduce(jnp.maximum, ms)
        num = jnp.zeros((prep_rows, LANES), F32)
        den = jnp.zeros((prep_rows, LANES), F32)
        for g in range(len(DILATIONS)):
            w = jnp.exp(ms[g] - m)
            num = num + w * ob[g, rows, :]
            den = den + w * lb[g, rows, :]
        o_ref[rows, :] = (num / den).astype(o_ref.dtype)
        return carry

    lax.fori_loop(0, s_len // prep_rows, merge, 0)


def dilated_attention(qkv, tabs, gq2, gk2, bsz, s_len, n_pairs):
    cos_t, sa_t, sb_t = tabs
    lane = jnp.arange(LANES)
    seg = (lane[:, None] // ATTN_HEAD_DIM == lane[None, :] // ATTN_HEAD_DIM).astype(BF16)
    blk = lambda off: pl.BlockSpec((s_len, LANES), lambda b, p, off=off: (b, off + p))
    tab = pl.BlockSpec((s_len, LANES), lambda b, p: (b, 0))
    row = pl.BlockSpec((1, LANES), lambda b, p: (0, 0))
    return pl.pallas_call(
        _attn_kernel,
        grid=(bsz, n_pairs),
        in_specs=[blk(0), blk(n_pairs), blk(2 * n_pairs), tab, tab, tab, row, row,
                  pl.BlockSpec((LANES, LANES), lambda b, p: (0, 0))],
        out_specs=pl.BlockSpec((s_len, LANES), lambda b, p: (b, p)),
        out_shape=jax.ShapeDtypeStruct((bsz * s_len, n_pairs * LANES), BF16),
        scratch_shapes=[pltpu.VMEM((s_len, LANES), F32)] * 3
                       + [pltpu.VMEM((len(DILATIONS), s_len, LANES), F32)] * 3,
        compiler_params=_cparams(("parallel", "arbitrary")),
        name="dilated_attn",
    )(qkv, qkv, qkv, cos_t, sa_t, sb_t, gq2, gk2, seg)


def _split3(x):
    hi = x.astype(BF16)
    r1 = x - hi.astype(F32)
    mid = r1.astype(BF16)
    lo = (r1 - mid.astype(F32)).astype(BF16)
    return hi, mid, lo


def _silu(x):
    return x * (1.0 / (1.0 + jnp.exp(-x)))


def _ssd_kernel(xbc_ref, z_ref, dt_ref, cw_ref, cb_ref, dtb_ref, alog_ref, dskip_ref, g_ref,
                expand_ref, tril_ref, o_ref, xpad, xc, st):
    q = SSD_CHUNK
    d_ssm = z_ref.shape[1]
    d_conv = xbc_ref.shape[1]
    gw = d_ssm // SSM_GROUPS
    heads_per_group = gw // SSM_HEAD_DIM
    pad = 8

    @pl.when(pl.program_id(1) == 0)
    def _():
        xpad[0:pad, :] = jnp.zeros((pad, d_conv), F32)
        st[...] = jnp.zeros_like(st)

    xpad[pad:pad + q, :] = xbc_ref[...].astype(F32)
    for c0 in range(0, d_conv, LANES):
        cols = slice(c0, c0 + LANES)
        acc = cb_ref[:, cols] + cw_ref[CONV_WIDTH - 1:CONV_WIDTH, cols] * xpad[pad:pad + q, cols]
        for w in range(CONV_WIDTH - 1):
            off = pad - (CONV_WIDTH - 1) + w
            acc = acc + cw_ref[w:w + 1, cols] * xpad[off:off + q, cols]
        xc[:, cols] = _silu(acc)
    xpad[0:pad, :] = xpad[q:q + pad, :]

    x_dt = dt_ref[...] + dtb_ref[...]
    dt = jnp.maximum(x_dt, 0.0) + jnp.log1p(jnp.exp(-jnp.abs(x_dt)))
    dta = dt * (-jnp.exp(alog_ref[...]))
    tril = tril_ref[...]
    acs = sum(jnp.dot(tril, part, preferred_element_type=F32) for part in _split3(dta))
    acs_t = acs.T
    last = acs[q - 1:q, :]
    expand = expand_ref[...]
    dt_e = jnp.dot(dt.astype(BF16), expand, preferred_element_type=F32)
    dec_e = jnp.dot(jnp.exp(acs).astype(BF16), expand, preferred_element_type=F32)
    w_e = jnp.dot((jnp.exp(last - acs) * dt).astype(BF16), expand, preferred_element_type=F32)

    li = lax.broadcasted_iota(jnp.int32, (q, q), 0)
    si = lax.broadcasted_iota(jnp.int32, (q, q), 1)
    causal = li >= si
    lane_g = lax.broadcasted_iota(jnp.int32, (q, gw), 1)

    for g in range(SSM_GROUPS):
        cols = slice(g * gw, (g + 1) * gw)
        b_f = xc[:, d_ssm + g * SSM_STATE:d_ssm + (g + 1) * SSM_STATE]
        c_b = xc[:, d_ssm + (SSM_GROUPS + g) * SSM_STATE:d_ssm + (SSM_GROUPS + g + 1) * SSM_STATE].astype(BF16)
        xs = xc[:, cols]
        cb = lax.dot_general(c_b, b_f.astype(BF16), (((1,), (1,)), ((), ())), preferred_element_type=F32)
        xdt = (xs * dt_e[:, cols]).astype(BF16)
        ws, rs = [], []
        for hh in range(heads_per_group):
            h = g * heads_per_group + hh
            seg = acs[:, h:h + 1] - acs_t[h:h + 1, :]
            l_mat = jnp.exp(jnp.where(causal, seg, -jnp.inf))
            ws.append((cb * l_mat).astype(BF16))
            in_head = (lane_g >= hh * SSM_HEAD_DIM) & (lane_g < (hh + 1) * SSM_HEAD_DIM)
            rs.append(jnp.where(in_head, xdt, jnp.zeros_like(xdt)))
        y = jnp.dot(jnp.concatenate(ws, axis=1), jnp.concatenate(rs, axis=0), preferred_element_type=F32)
        st_g = st[:, cols]
        y = y + jnp.dot(c_b, st_g.astype(BF16), preferred_element_type=F32) * dec_e[:, cols]
        xw = (xs * w_e[:, cols]).astype(BF16)
        st[:, cols] = st_g * dec_e[q - 1:q, cols] + jnp.dot(b_f.T.astype(BF16), xw, preferred_element_type=F32)
        y = y + dskip_ref[:, cols] * xs
        y = y * _silu(z_ref[:, cols].astype(F32))
        o_ref[:, cols] = _rms_rows(y, g_ref[:, cols]).astype(o_ref.dtype)


def ssd_mixer(packed, dt_raw, conv_w, conv_b, dt_bias, a_log, d_skip, g_out, bsz, s_len, d_ssm, xbc_col, z_col):
    n_heads = d_ssm // SSM_HEAD_DIM
    d_conv = d_ssm + 2 * SSM_GROUPS * SSM_STATE
    nc = s_len // SSD_CHUNK
    padl = lambda v: jnp.pad(v.astype(F32), (0, LANES - n_heads))[None, :]
    expand = (jnp.arange(LANES)[:, None] == (jnp.arange(d_ssm) // SSM_HEAD_DIM)[None, :]).astype(BF16)
    tril = (jnp.arange(SSD_CHUNK)[:, None] >= jnp.arange(SSD_CHUNK)[None, :]).astype(BF16)
    full = lambda shape: pl.BlockSpec(shape, lambda b, c: (0, 0))
    return pl.pallas_call(
        _ssd_kernel,
        grid=(bsz, nc),
        in_specs=[pl.BlockSpec((SSD_CHUNK, d_conv), lambda b, c: (b * nc + c, xbc_col // d_conv)),
                  pl.BlockSpec((SSD_CHUNK, d_ssm), lambda b, c: (b * nc + c, z_col // d_ssm)),
                  pl.BlockSpec((SSD_CHUNK, LANES), lambda b, c: (b * nc + c, 0)),
                  full((CONV_WIDTH, d_conv)), full((1, d_conv)), full((1, LANES)), full((1, LANES)),
                  full((1, d_ssm)), full((1, d_ssm)), full((LANES, d_ssm)), full((SSD_CHUNK, SSD_CHUNK))],
        out_specs=pl.BlockSpec((SSD_CHUNK, d_ssm), lambda b, c: (b * nc + c, 0)),
        out_shape=jax.ShapeDtypeStruct((bsz * s_len, d_ssm), BF16),
        scratch_shapes=[pltpu.VMEM((SSD_CHUNK + 8, d_conv), F32),
                        pltpu.VMEM((SSD_CHUNK, d_conv), F32),
                        pltpu.VMEM((SSM_STATE, d_ssm), F32)],
        compiler_params=_cparams(("parallel", "arbitrary")),
        name="ssd",
    )(packed, packed, dt_raw, conv_w.astype(F32), conv_b.astype(F32)[None, :], padl(dt_bias), padl(a_log),
      jnp.repeat(d_skip.astype(F32), SSM_HEAD_DIM)[None, :], g_out.astype(F32)[None, :], expand, tril)


def _out_proj_kernel(attn_ref, ssm_ref, x_ref, g_ref, w_ref, o_ref, mix_ref):
    d_attn = attn_ref.shape[1]

    @pl.when(pl.program_id(1) == 0)
    def _():
        rows = 32

        def body(c, carry):
            rs = pl.ds(pl.multiple_of(c * rows, rows), rows)
            mix_ref[rs, 0:d_attn] = _rms_rows(attn_ref[rs, :].astype(F32), g_ref[...]).astype(BF16)
            return carry
        lax.fori_loop(0, attn_ref.shape[0] // rows, body, 0)
        mix_ref[:, d_attn:] = ssm_ref[...]
    o_ref[...] = x_ref[...] + jnp.dot(mix_ref[...], w_ref[...], preferred_element_type=F32)


def out_proj(attn, ssm, x2d, g, w, tm=1024, tn=512):
    m, d = x2d.shape
    da, ds_ = attn.shape[1], ssm.shape[1]
    return pl.pallas_call(
        _out_proj_kernel,
        grid=(m // tm, d // tn),
        in_specs=[pl.BlockSpec((tm, da), lambda i, j: (i, 0)),
                  pl.BlockSpec((tm, ds_), lambda i, j: (i, 0)),
                  pl.BlockSpec((tm, tn), lambda i, j: (i, j)),
                  pl.BlockSpec((1, da), lambda i, j: (0, 0)),
                  pl.BlockSpec((da + ds_, tn), lambda i, j: (0, j))],
        out_specs=pl.BlockSpec((tm, tn), lambda i, j: (i, j)),
        out_shape=jax.ShapeDtypeStruct((m, d), F32),
        scratch_shapes=[pltpu.VMEM((tm, da + ds_), BF16)],
        compiler_params=_cparams(("parallel", "arbitrary")),
        name="out_proj",
    )(attn, ssm, x2d, g, w)


def _mem_kv_kernel(mem_ref, g_ref, w_ref, gk_ref, kv_ref, h_ref):
    d_cross = N_CROSS_HEADS * CROSS_HEAD_DIM
    _norm_rows_to(mem_ref, g_ref, h_ref)
    kv = jnp.dot(h_ref[...], w_ref[...], preferred_element_type=F32)
    for h in range(N_CROSS_HEADS):
        cols = slice(h * CROSS_HEAD_DIM, (h + 1) * CROSS_HEAD_DIM)
        kv_ref[:, cols] = _rms_rows(kv[:, cols], gk_ref[...]).astype(kv_ref.dtype)
    kv_ref[:, d_cross:] = kv[:, d_cross:].astype(kv_ref.dtype)


def mem_kv(mem2d, g, w, gk, tm=256):
    m, d = mem2d.shape
    n = w.shape[1]
    return pl.pallas_call(
        _mem_kv_kernel,
        grid=(m // tm,),
        in_specs=[pl.BlockSpec((tm, d), lambda i: (i, 0)),
                  pl.BlockSpec((1, d), lambda i: (0, 0)),
                  pl.BlockSpec((d, n), lambda i: (0, 0)),
                  pl.BlockSpec((1, CROSS_HEAD_DIM), lambda i: (0, 0))],
        out_specs=pl.BlockSpec((tm, n), lambda i: (i, 0)),
        out_shape=jax.ShapeDtypeStruct((m, n), BF16),
        scratch_shapes=[pltpu.VMEM((tm, d), BF16)],
        compiler_params=_cparams(("parallel",)),
        name="mem_kv",
    )(mem2d, g, w, gk)


def _cross_kernel(x_ref, g_ref, wq_ref, gq_ref, k_ref, v_ref, wo_ref, o_ref, h_ref, q_ref, a_ref):
    tm = x_ref.shape[0]
    _norm_rows_to(x_ref, g_ref, h_ref)
    q_ref[...] = jnp.dot(h_ref[...], wq_ref[...], preferred_element_type=F32)
    rows = 128

    def body(c, carry):
        rs = pl.ds(pl.multiple_of(c * rows, rows), rows)
        for h in range(N_CROSS_HEADS):
            cols = slice(h * CROSS_HEAD_DIM, (h + 1) * CROSS_HEAD_DIM)
            qh = (_rms_rows(q_ref[rs, cols], gq_ref[...]) * (CROSS_HEAD_DIM ** -0.5)).astype(BF16)
            s = lax.dot_general(qh, k_ref[:, cols], (((1,), (1,)), ((), ())), preferred_element_type=F32)
            p = jnp.exp(s - jnp.max(s, axis=-1, keepdims=True))
            l = jnp.sum(p, axis=-1, keepdims=True)
            pv = jnp.dot(p.astype(BF16), v_ref[:, cols], preferred_element_type=F32)
            a_ref[rs, cols] = (pv / l).astype(BF16)
        return carry
    lax.fori_loop(0, tm // rows, body, 0)
    o_ref[...] = x_ref[...] + jnp.dot(a_ref[...], wo_ref[...], preferred_element_type=F32)


def cross_attention(x2d, g, wq, gq, kv, wo, s_len, n_mem, tm=512):
    m, d = x2d.shape
    dc = wq.shape[1]
    per_batch = s_len // tm
    full = lambda shape: pl.BlockSpec(shape, lambda i: (0, 0))
    return pl.pallas_call(
        _cross_kernel,
        grid=(m // tm,),
        in_specs=[pl.BlockSpec((tm, d), lambda i: (i, 0)), full((1, d)), full((d, dc)), full((1, CROSS_HEAD_DIM)),
                  pl.BlockSpec((n_mem, dc), lambda i: (i // per_batch, 0)),
                  pl.BlockSpec((n_mem, dc), lambda i: (i // per_batch, 1)),
                  full((dc, d))],
        out_specs=pl.BlockSpec((tm, d), lambda i: (i, 0)),
        out_shape=jax.ShapeDtypeStruct((m, d), F32),
        scratch_shapes=[pltpu.VMEM((tm, d), BF16), pltpu.VMEM((tm, dc), F32), pltpu.VMEM((tm, dc), BF16)],
        compiler_params=_cparams(("parallel",)),
        name="cross_attn",
    )(x2d, g, wq, gq, kv, kv, wo)


def _mlp_kernel(x_ref, g_ref, wu_ref, wd_ref, o_ref, h_ref):
    @pl.when(pl.program_id(1) == 0)
    def _():
        _norm_rows_to(x_ref, g_ref, h_ref)
        o_ref[...] = x_ref[...]
    u = jnp.dot(h_ref[...], wu_ref[...], preferred_element_type=F32)
    u = jnp.square(jnp.maximum(u, 0.0)).astype(BF16)
    o_ref[...] += jnp.dot(u, wd_ref[...], preferred_element_type=F32)


def mlp(x2d, g, wu, wd, tm=512, tf=1024):
    m, d = x2d.shape
    f = wu.shape[1]
    return pl.pallas_call(
        _mlp_kernel,
        grid=(m // tm, f // tf),
        in_specs=[pl.BlockSpec((tm, d), lambda i, j: (i, 0)),
                  pl.BlockSpec((1, d), lambda i, j: (0, 0)),
                  pl.BlockSpec((d, tf), lambda i, j: (0, j)),
                  pl.BlockSpec((tf, d), lambda i, j: (j, 0))],
        out_specs=pl.BlockSpec((tm, d), lambda i, j: (i, 0)),
        out_shape=jax.ShapeDtypeStruct((m, d), F32),
        scratch_shapes=[pltpu.VMEM((tm, d), BF16)],
        compiler_params=_cparams(("parallel", "arbitrary")),
        name="mlp",
    )(x2d, g, wu, wd)


def _layer(x2d, mem2d, pos_col, bsz, s_len, n_mem, g_mix, w_in, g_q, g_k, g_attn_out, conv_w, conv_b, dt_bias,
           a_log, d_skip, g_ssm_out, w_out, g_cross, g_mem, w_cq, w_ckv, g_cq, g_ck, w_co, g_mlp, w_up, w_down):
    d_model = x2d.shape[1]
    d_attn = d_model // 2
    d_ssm = d_model // 2
    n_pairs = d_attn // LANES
    n_ssm_heads = d_ssm // SSM_HEAD_DIM
    d_conv = d_ssm + 2 * SSM_GROUPS * SSM_STATE
    d_packed = 3 * d_attn + d_ssm + d_conv
    row = lambda v: v.astype(F32)[None, :]

    w_main = w_in[:, :d_packed].astype(BF16)
    w_dt = jnp.pad(w_in[:, d_packed:], ((0, 0), (0, LANES - n_ssm_heads))).astype(BF16)
    packed, dt_raw = in_proj(x2d, row(g_mix), w_main, w_dt)

    tabs = rope_tables(pos_col)
    attn = dilated_attention(packed, tabs, row(jnp.tile(g_q, 2)), row(jnp.tile(g_k, 2)), bsz, s_len, n_pairs)
    ssm = ssd_mixer(packed, dt_raw, conv_w, conv_b, dt_bias, a_log, d_skip, g_ssm_out, bsz, s_len, d_ssm,
                    xbc_col=3 * d_attn + d_ssm, z_col=3 * d_attn)
    x2d = out_proj(attn, ssm, x2d, row(g_attn_out), w_out.astype(BF16))

    kv = mem_kv(mem2d, row(g_mem), w_ckv.astype(BF16), row(g_ck))
    x2d = cross_attention(x2d, row(g_cross), w_cq.astype(BF16), row(g_cq), kv, w_co.astype(BF16), s_len, n_mem)

    return mlp(x2d, row(g_mlp), w_up.astype(BF16), w_down.astype(BF16))


def kernel(x, mem, positions, g_mix, w_in, g_q, g_k, g_attn_out, conv_w, conv_b, dt_bias, a_log, d_skip, g_ssm_out,
           w_out, g_cross, g_mem, w_cq, w_ckv, g_cq, g_ck, w_co, g_mlp, w_up, w_down):
    bsz, s_len, d_model = x.shape
    n_mem = mem.shape[1]
    x2d = x.reshape(bsz * s_len, d_model)
    mem2d = mem.reshape(bsz * n_mem, d_model)
    pos_col = positions.reshape(bsz * s_len, 1)
    for i in range(g_mix.shape[0]):
        x2d = _layer(x2d, mem2d, pos_col, bsz, s_len, n_mem, g_mix[i], w_in[i], g_q[i], g_k[i], g_attn_out[i],
                     conv_w[i], conv_b[i], dt_bias[i], a_log[i], d_skip[i], g_ssm_out[i], w_out[i], g_cross[i],
                     g_mem[i], w_cq[i], w_ckv[i], g_cq[i], g_ck[i], w_co[i], g_mlp[i], w_up[i], w_down[i])
    return x2d.reshape(bsz, s_len, d_model)
```

```python
import functools
import math

import jax
import jax.numpy as jnp
from jax import lax
from jax.experimental import pallas as pl
from jax.experimental.pallas import tpu as pltpu

F32 = jnp.float32
BF16 = jnp.bfloat16
EPS = 1e-6

LANES = 128
ATTN_HEAD_DIM = 64
ROT_DIM = ATTN_HEAD_DIM // 4
ROPE_THETA = 500000.0
ATTN_BLOCK = 128
DILATIONS = (1, 4, 16)
SSM_HEAD_DIM = 64
SSM_GROUPS = 4
SSM_STATE = 128
CONV_WIDTH = 4
SSD_CHUNK = 128
N_CROSS_HEADS = 4
CROSS_HEAD_DIM = 128
VMEM_LIMIT = 52 * 1024 * 1024


def _cparams(sem):
    return pltpu.CompilerParams(dimension_semantics=sem, vmem_limit_bytes=VMEM_LIMIT)


def _rms_rows(x, g):
    ms = jnp.mean(x * x, axis=-1, keepdims=True)
    return x * lax.rsqrt(ms + EPS) * g


def _norm_rows_to(x_ref, g_ref, h_ref, rows=16):
    def body(c, carry):
        r0 = pl.multiple_of(c * rows, rows)
        h_ref[pl.ds(r0, rows), :] = _rms_rows(x_ref[pl.ds(r0, rows), :], g_ref[...]).astype(h_ref.dtype)
        return carry
    lax.fori_loop(0, x_ref.shape[0] // rows, body, 0, unroll=4)


def _in_proj_kernel(x_ref, g_ref, w_ref, wdt_ref, o_ref, dt_ref, h_ref):
    @pl.when(pl.program_id(1) == 0)
    def _():
        _norm_rows_to(x_ref, g_ref, h_ref)
        dt_ref[...] = jnp.dot(h_ref[...], wdt_ref[...], preferred_element_type=F32)
    o_ref[...] = jnp.dot(h_ref[...], w_ref[...], preferred_element_type=F32).astype(o_ref.dtype)


def in_proj(x2d, g, w_main, w_dt, n, tm=1024, tn=512):
    m, d = x2d.shape
    return pl.pallas_call(
        _in_proj_kernel,
        grid=(m // tm, n // tn),
        in_specs=[pl.BlockSpec((tm, d), lambda i, j: (i, 0)),
                  pl.BlockSpec((1, d), lambda i, j: (0, 0)),
                  pl.BlockSpec((d, tn), lambda i, j: (0, j)),
                  pl.BlockSpec((d, LANES), lambda i, j: (0, 0))],
        out_specs=[pl.BlockSpec((tm, tn), lambda i, j: (i, j)),
                   pl.BlockSpec((tm, LANES), lambda i, j: (i, 0))],
        out_shape=[jax.ShapeDtypeStruct((m, n), BF16), jax.ShapeDtypeStruct((m, LANES), F32)],
        scratch_shapes=[pltpu.VMEM((tm, d), BF16)],
        compiler_params=_cparams(("parallel", "arbitrary")),
        name="in_proj",
    )(x2d, g, w_main, w_dt)


def _rope_kernel(pos_ref, invf_ref, cmask_ref, amask_ref, bmask_ref, cos_ref, sa_ref, sb_ref):
    ang = pos_ref[...].astype(F32) * invf_ref[...]
    c = jnp.cos(ang)
    s = jnp.sin(ang)
    cm = cmask_ref[...]
    cos_ref[...] = c * cm + (1.0 - cm)
    sa_ref[...] = -s * amask_ref[...]
    sb_ref[...] = s * bmask_ref[...]


def rope_tables(pos_col, tm=512):
    m = pos_col.shape[0]
    half = ROT_DIM // 2
    inv_freq = ROPE_THETA ** (-2.0 * jnp.arange(half, dtype=F32) / ROT_DIM)
    d = jnp.arange(LANES) % ATTN_HEAD_DIM
    invf = inv_freq[d % half][None, :]
    cmask = (d < ROT_DIM).astype(F32)[None, :]
    amask = (d < half).astype(F32)[None, :]
    bmask = ((d >= half) & (d < ROT_DIM)).astype(F32)[None, :]
    row = pl.BlockSpec((1, LANES), lambda i: (0, 0))
    tab = pl.BlockSpec((tm, LANES), lambda i: (i, 0))
    return pl.pallas_call(
        _rope_kernel,
        grid=(m // tm,),
        in_specs=[pl.BlockSpec((tm, 1), lambda i: (i, 0)), row, row, row, row],
        out_specs=[tab, tab, tab],
        out_shape=[jax.ShapeDtypeStruct((m, LANES), F32)] * 3,
        compiler_params=_cparams(("parallel",)),
        name="rope_tab",
    )(pos_col, invf, cmask, amask, bmask)


def _rows(start, size, dil):
    return pl.ds(start, size) if dil == 1 else pl.ds(start, size, stride=dil)


ATTN_MAX_UNROLL = 4


def _unroll(count):
    return max([u for u in range(1, ATTN_MAX_UNROLL + 1) if count and count % u == 0] or [1])


def _attn_kernel(q_ref, k_ref, v_ref, cos_ref, sa_ref, sb_ref, gq_ref, gk_ref, seg_ref, o_ref,
                 qf, kf, vf, ob, mb, lb):
    s_len = q_ref.shape[0]
    blk = ATTN_BLOCK
    half = ROT_DIM // 2
    lane = lax.broadcasted_iota(jnp.int32, (blk, LANES), 1)
    head0 = lane < ATTN_HEAD_DIM
    qi = lax.broadcasted_iota(jnp.int32, (2 * blk, 2 * blk), 0) % blk
    kj = lax.broadcasted_iota(jnp.int32, (2 * blk, 2 * blk), 1)
    band_mask = (kj >= qi) & (kj <= qi + blk)
    first_mask = (lax.broadcasted_iota(jnp.int32, (2 * blk, blk), 1)
                  <= lax.broadcasted_iota(jnp.int32, (2 * blk, blk), 0) % blk)

    prep_rows = 256

    def prep(c, carry):
        rows = pl.ds(pl.multiple_of(c * prep_rows, prep_rows), prep_rows)
        cs, sa, sb = cos_ref[rows, :], sa_ref[rows, :], sb_ref[rows, :]

        def norm_rope(x_ref, g_ref):
            x = x_ref[rows, :].astype(F32)
            ss = jnp.dot((x * x).astype(BF16), seg_ref[...], preferred_element_type=F32)
            y = x * lax.rsqrt(ss * (1.0 / ATTN_HEAD_DIM) + EPS) * g_ref[...]
            return y * cs + pltpu.roll(y, LANES - half, 1) * sa + pltpu.roll(y, half, 1) * sb

        qf[rows, :] = norm_rope(q_ref, gq_ref) * (ATTN_HEAD_DIM ** -0.5)
        kf[rows, :] = norm_rope(k_ref, gk_ref)
        vf[rows, :] = v_ref[rows, :].astype(F32)
        return carry

    lax.fori_loop(0, s_len // prep_rows, prep, 0)

    def block(br, dil, q_start, k_start, n_keys, mask):
        qt = qf[_rows(q_start, blk, dil), :]
        zero = jnp.zeros_like(qt)
        q2 = jnp.concatenate([jnp.where(head0, qt, zero), jnp.where(head0, zero, qt)], axis=0).astype(BF16)
        kb = kf[_rows(k_start, n_keys, dil), :].astype(BF16)
        vb = vf[_rows(k_start, n_keys, dil), :].astype(BF16)
        v1 = jnp.concatenate([vb, jnp.ones_like(vb)], axis=1)
        s = lax.dot_general(q2, kb, (((1,), (1,)), ((), ())), preferred_element_type=F32)
        s = jnp.where(mask, s, -jnp.inf)
        m = jnp.max(s, axis=-1, keepdims=True)
        p = jnp.exp(s - m).astype(BF16)
        r = jnp.dot(p, v1, preferred_element_type=F32)
        out_rows = _rows(q_start, blk, dil)
        ob[br, out_rows, :] = jnp.where(head0, r[:blk, :LANES], r[blk:, :LANES])
        lb[br, out_rows, :] = jnp.where(head0, r[:blk, LANES:], r[blk:, LANES:])
        mb[br, out_rows, :] = jnp.where(head0, m[:blk], m[blk:])

    for br, dil in enumerate(DILATIONS):
        span = dil * blk
        n_rest = dil * (s_len // span - 1)
        u_first = _unroll(dil)
        u_rest = _unroll(n_rest)

        def first(i, carry, br=br, dil=dil, u_first=u_first):
            for u in range(u_first):
                r = i * u_first + u
                block(br, dil, r, r, blk, first_mask)
            return carry

        def rest(i, carry, br=br, dil=dil, span=span, u_rest=u_rest):
            for u in range(u_rest):
                idx = i * u_rest + u
                q_start = (1 + idx // dil) * span + idx % dil
                if dil == 1:
                    q_start = pl.multiple_of(q_start, blk)
                block(br, dil, q_start, q_start - span, 2 * blk, band_mask)
            return carry

        lax.fori_loop(0, dil // u_first, first, 0)
        if n_rest:
            lax.fori_loop(0, n_rest // u_rest, rest, 0)

    def merge(c, carry):
        rows = pl.ds(pl.multiple_of(c * prep_rows, prep_rows), prep_rows)
        ms = [mb[g, rows, :] for g in range(len(DILATIONS))]
        m = functools.reduce(jnp.maximum, ms)
        num = jnp.zeros((prep_rows, LANES), F32)
        den = jnp.zeros((prep_rows, LANES), F32)
        for g in range(len(DILATIONS)):
            w = jnp.exp(ms[g] - m)
            num = num + w * ob[g, rows, :]
            den = den + w * lb[g, rows, :]
        o_ref[rows, :] = (num / den).astype(o_ref.dtype)
        return carry

    lax.fori_loop(0, s_len // prep_rows, merge, 0)


def dilated_attention(qkv, tabs, gq2, gk2, bsz, s_len, n_pairs):
    cos_t, sa_t, sb_t = tabs
    lane = jnp.arange(LANES)
    seg = (lane[:, None] // ATTN_HEAD_DIM == lane[None, :] // ATTN_HEAD_DIM).astype(BF16)
    blk = lambda off: pl.BlockSpec((s_len, LANES), lambda b, p, off=off: (b, off + p))
    tab = pl.BlockSpec((s_len, LANES), lambda b, p: (b, 0))
    row = pl.BlockSpec((1, LANES), lambda b, p: (0, 0))
    return pl.pallas_call(
        _attn_kernel,
        grid=(bsz, n_pairs),
        in_specs=[blk(0), blk(n_pairs), blk(2 * n_pairs), tab, tab, tab, row, row,
                  pl.BlockSpec((LANES, LANES), lambda b, p: (0, 0))],
        out_specs=pl.BlockSpec((s_len, LANES), lambda b, p: (b, p)),
        out_shape=jax.ShapeDtypeStruct((bsz * s_len, n_pairs * LANES), BF16),
        scratch_shapes=[pltpu.VMEM((s_len, LANES), F32)] * 3
                       + [pltpu.VMEM((len(DILATIONS), s_len, LANES), F32)] * 3,
        compiler_params=_cparams(("parallel", "arbitrary")),
        name="dilated_attn",
    )(qkv, qkv, qkv, cos_t, sa_t, sb_t, gq2, gk2, seg)


def _split3(x):
    hi = x.astype(BF16)
    r1 = x - hi.astype(F32)
    mid = r1.astype(BF16)
    lo = (r1 - mid.astype(F32)).astype(BF16)
    return hi, mid, lo


def _silu(x):
    return x * (1.0 / (1.0 + jnp.exp(-x)))


def _ssd_kernel(xbc_ref, z_ref, dt_ref, cw_ref, cb_ref, dtb_ref, alog_ref, dskip_ref, g_ref,
                expand_ref, tril_ref, o_ref, xpad, xc, st):
    q = SSD_CHUNK
    d_ssm = z_ref.shape[1]
    d_conv = xbc_ref.shape[1]
    gw = d_ssm // SSM_GROUPS
    heads_per_group = gw // SSM_HEAD_DIM
    pad = 8

    @pl.when(pl.program_id(1) == 0)
    def _():
        xpad[0:pad, :] = jnp.zeros((pad, d_conv), F32)
        st[...] = jnp.zeros_like(st)

    xpad[pad:pad + q, :] = xbc_ref[...].astype(F32)
    for c0 in range(0, d_conv, LANES):
        cols = slice(c0, c0 + LANES)
        acc = cb_ref[:, cols] + cw_ref[CONV_WIDTH - 1:CONV_WIDTH, cols] * xpad[pad:pad + q, cols]
        for w in range(CONV_WIDTH - 1):
            off = pad - (CONV_WIDTH - 1) + w
            acc = acc + cw_ref[w:w + 1, cols] * xpad[off:off + q, cols]
        xc[:, cols] = _silu(acc)
    xpad[0:pad, :] = xpad[q:q + pad, :]

    x_dt = dt_ref[...] + dtb_ref[...]
    dt = jnp.maximum(x_dt, 0.0) + jnp.log1p(jnp.exp(-jnp.abs(x_dt)))
    dta = dt * (-jnp.exp(alog_ref[...]))
    tril = tril_ref[...]
    acs = sum(jnp.dot(tril, part, preferred_element_type=F32) for part in _split3(dta))
    acs_t = acs.T
    last = acs[q - 1:q, :]
    expand = expand_ref[...]
    dt_e = jnp.dot(dt.astype(BF16), expand, preferred_element_type=F32)
    dec_e = jnp.dot(jnp.exp(acs).astype(BF16), expand, preferred_element_type=F32)
    w_e = jnp.dot((jnp.exp(last - acs) * dt).astype(BF16), expand, preferred_element_type=F32)

    li = lax.broadcasted_iota(jnp.int32, (q, q), 0)
    si = lax.broadcasted_iota(jnp.int32, (q, q), 1)
    causal = li >= si
    lane_g = lax.broadcasted_iota(jnp.int32, (q, gw), 1)

    for g in range(SSM_GROUPS):
        cols = slice(g * gw, (g + 1) * gw)
        b_f = xc[:, d_ssm + g * SSM_STATE:d_ssm + (g + 1) * SSM_STATE]
        c_b = xc[:, d_ssm + (SSM_GROUPS + g) * SSM_STATE:d_ssm + (SSM_GROUPS + g + 1) * SSM_STATE].astype(BF16)
        xs = xc[:, cols]
        cb = lax.dot_general(c_b, b_f.astype(BF16), (((1,), (1,)), ((), ())), preferred_element_type=F32)
        xdt = (xs * dt_e[:, cols]).astype(BF16)
        ws, rs = [], []
        for hh in range(heads_per_group):
            h = g * heads_per_group + hh
            seg = acs[:, h:h + 1] - acs_t[h:h + 1, :]
            l_mat = jnp.exp(jnp.where(causal, seg, -jnp.inf))
            ws.append((cb * l_mat).astype(BF16))
            in_head = (lane_g >= hh * SSM_HEAD_DIM) & (lane_g < (hh + 1) * SSM_HEAD_DIM)
            rs.append(jnp.where(in_head, xdt, jnp.zeros_like(xdt)))
        y = jnp.dot(jnp.concatenate(ws, axis=1), jnp.concatenate(rs, axis=0), preferred_element_type=F32)
        st_g = st[:, cols]
        y = y + jnp.dot(c_b, st_g.astype(BF16), preferred_element_type=F32) * dec_e[:, cols]
        xw = (xs * w_e[:, cols]).astype(BF16)
        st[:, cols] = st_g * dec_e[q - 1:q, cols] + jnp.dot(b_f.T.astype(BF16), xw, preferred_element_type=F32)
        y = y + dskip_ref[:, cols] * xs
        y = y * _silu(z_ref[:, cols].astype(F32))
        o_ref[:, cols] = _rms_rows(y, g_ref[:, cols]).astype(o_ref.dtype)


def ssd_mixer(packed, dt_raw, conv_w, conv_b, dt_bias, a_log, d_skip, g_out, bsz, s_len, d_ssm, xbc_col, z_col):
    n_heads = d_ssm // SSM_HEAD_DIM
    d_conv = d_ssm + 2 * SSM_GROUPS * SSM_STATE
    nc = s_len // SSD_CHUNK
    padl = lambda v: jnp.pad(v.astype(F32), (0, LANES - n_heads))[None, :]
    expand = (jnp.arange(LANES)[:, None] == (jnp.arange(d_ssm) // SSM_HEAD_DIM)[None, :]).astype(BF16)
    tril = (jnp.arange(SSD_CHUNK)[:, None] >= jnp.arange(SSD_CHUNK)[None, :]).astype(BF16)
    full = lambda shape: pl.BlockSpec(shape, lambda b, c: (0, 0))
    return pl.pallas_call(
        _ssd_kernel,
        grid=(bsz, nc),
        in_specs=[pl.BlockSpec((SSD_CHUNK, d_conv), lambda b, c: (b * nc + c, xbc_col // d_conv)),
                  pl.BlockSpec((SSD_CHUNK, d_ssm), lambda b, c: (b * nc + c, z_col // d_ssm)),
                  pl.BlockSpec((SSD_CHUNK, LANES), lambda b, c: (b * nc + c, 0)),
                  full((CONV_WIDTH, d_conv)), full((1, d_conv)), full((1, LANES)), full((1, LANES)),
                  full((1, d_ssm)), full((1, d_ssm)), full((LANES, d_ssm)), full((SSD_CHUNK, SSD_CHUNK))],
        out_specs=pl.BlockSpec((SSD_CHUNK, d_ssm), lambda b, c: (b * nc + c, 0)),
        out_shape=jax.ShapeDtypeStruct((bsz * s_len, d_ssm), BF16),
        scratch_shapes=[pltpu.VMEM((SSD_CHUNK + 8, d_conv), F32),
                        pltpu.VMEM((SSD_CHUNK, d_conv), F32),
                        pltpu.VMEM((SSM_STATE, d_ssm), F32)],
        compiler_params=_cparams(("parallel", "arbitrary")),
        name="ssd",
    )(packed, packed, dt_raw, conv_w.astype(F32), conv_b.astype(F32)[None, :], padl(dt_bias), padl(a_log),
      jnp.repeat(d_skip.astype(F32), SSM_HEAD_DIM)[None, :], g_out.astype(F32)[None, :], expand, tril)


def _out_proj_kernel(attn_ref, ssm_ref, x_ref, g_ref, w_ref, o_ref, mix_ref):
    d_attn = attn_ref.shape[1]

    @pl.when(pl.program_id(1) == 0)
    def _():
        rows = 32

        def body(c, carry):
            rs = pl.ds(pl.multiple_of(c * rows, rows), rows)
            mix_ref[rs, 0:d_attn] = _rms_rows(attn_ref[rs, :].astype(F32), g_ref[...]).astype(BF16)
            return carry
        lax.fori_loop(0, attn_ref.shape[0] // rows, body, 0, unroll=4)
        mix_ref[:, d_attn:] = ssm_ref[...]
    o_ref[...] = x_ref[...] + jnp.dot(mix_ref[...], w_ref[...], preferred_element_type=F32)


def out_proj(attn, ssm, x2d, g, w, tm=1024, tn=512):
    m, d = x2d.shape
    da, ds_ = attn.shape[1], ssm.shape[1]
    return pl.pallas_call(
        _out_proj_kernel,
        grid=(m // tm, d // tn),
        in_specs=[pl.BlockSpec((tm, da), lambda i, j: (i, 0)),
                  pl.BlockSpec((tm, ds_), lambda i, j: (i, 0)),
                  pl.BlockSpec((tm, tn), lambda i, j: (i, j)),
                  pl.BlockSpec((1, da), lambda i, j: (0, 0)),
                  pl.BlockSpec((da + ds_, tn), lambda i, j: (0, j))],
        out_specs=pl.BlockSpec((tm, tn), lambda i, j: (i, j)),
        out_shape=jax.ShapeDtypeStruct((m, d), F32),
        scratch_shapes=[pltpu.VMEM((tm, da + ds_), BF16)],
        compiler_params=_cparams(("parallel", "arbitrary")),
        name="out_proj",
    )(attn, ssm, x2d, g, w)


def _mem_kv_kernel(mem_ref, g_ref, w_ref, gk_ref, kv_ref, h_ref):
    d_cross = N_CROSS_HEADS * CROSS_HEAD_DIM
    _norm_rows_to(mem_ref, g_ref, h_ref)
    kv = jnp.dot(h_ref[...], w_ref[...], preferred_element_type=F32)
    for h in range(N_CROSS_HEADS):
        cols = slice(h * CROSS_HEAD_DIM, (h + 1) * CROSS_HEAD_DIM)
        kv_ref[:, cols] = _rms_rows(kv[:, cols], gk_ref[...]).astype(kv_ref.dtype)
    kv_ref[:, d_cross:] = kv[:, d_cross:].astype(kv_ref.dtype)


def mem_kv(mem2d, g, w, gk, tm=256):
    m, d = mem2d.shape
    n = w.shape[1]
    return pl.pallas_call(
        _mem_kv_kernel,
        grid=(m // tm,),
        in_specs=[pl.BlockSpec((tm, d), lambda i: (i, 0)),
                  pl.BlockSpec((1, d), lambda i: (0, 0)),
                  pl.BlockSpec((d, n), lambda i: (0, 0)),
                  pl.BlockSpec((1, CROSS_HEAD_DIM), lambda i: (0, 0))],
        out_specs=pl.BlockSpec((tm, n), lambda i: (i, 0)),
        out_shape=jax.ShapeDtypeStruct((m, n), BF16),
        scratch_shapes=[pltpu.VMEM((tm, d), BF16)],
        compiler_params=_cparams(("parallel",)),
        name="mem_kv",
    )(mem2d, g, w, gk)


def _cross_kernel(x_ref, g_ref, wq_ref, gq_ref, k_ref, v_ref, wo_ref, o_ref, h_ref, q_ref, a_ref):
    tm = x_ref.shape[0]
    _norm_rows_to(x_ref, g_ref, h_ref)
    q_ref[...] = jnp.dot(h_ref[...], wq_ref[...], preferred_element_type=F32)
    rows = 128

    def body(c, carry):
        rs = pl.ds(pl.multiple_of(c * rows, rows), rows)
        for h in range(N_CROSS_HEADS):
            cols = slice(h * CROSS_HEAD_DIM, (h + 1) * CROSS_HEAD_DIM)
            qh = (_rms_rows(q_ref[rs, cols], gq_ref[...]) * (CROSS_HEAD_DIM ** -0.5)).astype(BF16)
            s = lax.dot_general(qh, k_ref[:, cols], (((1,), (1,)), ((), ())), preferred_element_type=F32)
            p = jnp.exp(s - jnp.max(s, axis=-1, keepdims=True))
            l = jnp.sum(p, axis=-1, keepdims=True)
            pv = jnp.dot(p.astype(BF16), v_ref[:, cols], preferred_element_type=F32)
            a_ref[rs, cols] = (pv / l).astype(BF16)
        return carry
    lax.fori_loop(0, tm // rows, body, 0)
    o_ref[...] = x_ref[...] + jnp.dot(a_ref[...], wo_ref[...], preferred_element_type=F32)


def cross_attention(x2d, g, wq, gq, kv, wo, s_len, n_mem, tm=512):
    m, d = x2d.shape
    dc = wq.shape[1]
    per_batch = s_len // tm
    full = lambda shape: pl.BlockSpec(shape, lambda i: (0, 0))
    return pl.pallas_call(
        _cross_kernel,
        grid=(m // tm,),
        in_specs=[pl.BlockSpec((tm, d), lambda i: (i, 0)), full((1, d)), full((d, dc)), full((1, CROSS_HEAD_DIM)),
                  pl.BlockSpec((n_mem, dc), lambda i: (i // per_batch, 0)),
                  pl.BlockSpec((n_mem, dc), lambda i: (i // per_batch, 1)),
                  full((dc, d))],
        out_specs=pl.BlockSpec((tm, d), lambda i: (i, 0)),
        out_shape=jax.ShapeDtypeStruct((m, d), F32),
        scratch_shapes=[pltpu.VMEM((tm, d), BF16), pltpu.VMEM((tm, dc), F32), pltpu.VMEM((tm, dc), BF16)],
        compiler_params=_cparams(("parallel",)),
        name="cross_attn",
    )(x2d, g, wq, gq, kv, kv, wo)


def _mlp_kernel(x_ref, g_ref, wu_ref, wd_ref, o_ref, h_ref):
    @pl.when(pl.program_id(1) == 0)
    def _():
        _norm_rows_to(x_ref, g_ref, h_ref)
        o_ref[...] = x_ref[...]
    u = jnp.dot(h_ref[...], wu_ref[...], preferred_element_type=F32)
    u = jnp.square(jnp.maximum(u, 0.0)).astype(BF16)
    o_ref[...] += jnp.dot(u, wd_ref[...], preferred_element_type=F32)


def mlp(x2d, g, wu, wd, tm=512, tf=1024):
    m, d = x2d.shape
    f = wu.shape[1]
    return pl.pallas_call(
        _mlp_kernel,
        grid=(m // tm, f // tf),
        in_specs=[pl.BlockSpec((tm, d), lambda i, j: (i, 0)),
                  pl.BlockSpec((1, d), lambda i, j: (0, 0)),
                  pl.BlockSpec((d, tf), lambda i, j: (0, j)),
                  pl.BlockSpec((tf, d), lambda i, j: (j, 0))],
        out_specs=pl.BlockSpec((tm, d), lambda i, j: (i, 0)),
        out_shape=jax.ShapeDtypeStruct((m, d), F32),
        scratch_shapes=[pltpu.VMEM((tm, d), BF16)],
        compiler_params=_cparams(("parallel", "arbitrary")),
        name="mlp",
    )(x2d, g, wu, wd)


def _layer(x2d, mem2d, pos_col, bsz, s_len, n_mem, g_mix, w_in, g_q, g_k, g_attn_out, conv_w, conv_b, dt_bias,
           a_log, d_skip, g_ssm_out, w_out, g_cross, g_mem, w_cq, w_ckv, g_cq, g_ck, w_co, g_mlp, w_up, w_down):
    d_model = x2d.shape[1]
    d_attn = d_model // 2
    d_ssm = d_model // 2
    n_pairs = d_attn // LANES
    n_ssm_heads = d_ssm // SSM_HEAD_DIM
    d_conv = d_ssm + 2 * SSM_GROUPS * SSM_STATE
    d_packed = 3 * d_attn + d_ssm + d_conv
    row = lambda v: v.astype(F32)[None, :]

    w_dt = jnp.pad(w_in[:, d_packed:], ((0, 0), (0, LANES - n_ssm_heads))).astype(BF16)
    packed, dt_raw = in_proj(x2d, row(g_mix), w_in.astype(BF16), w_dt, d_packed)

    tabs = rope_tables(pos_col)
    attn = dilated_attention(packed, tabs, row(jnp.tile(g_q, 2)), row(jnp.tile(g_k, 2)), bsz, s_len, n_pairs)
    ssm = ssd_mixer(packed, dt_raw, conv_w, conv_b, dt_bias, a_log, d_skip, g_ssm_out, bsz, s_len, d_ssm,
                    xbc_col=3 * d_attn + d_ssm, z_col=3 * d_attn)
    x2d = out_proj(attn, ssm, x2d, row(g_attn_out), w_out.astype(BF16))

    kv = mem_kv(mem2d, row(g_mem), w_ckv.astype(BF16), row(g_ck))
    x2d = cross_attention(x2d, row(g_cross), w_cq.astype(BF16), row(g_cq), kv, w_co.astype(BF16), s_len, n_mem)

    return mlp(x2d, row(g_mlp), w_up.astype(BF16), w_down.astype(BF16))


def kernel(x, mem, positions, g_mix, w_in, g_q, g_k, g_attn_out, conv_w, conv_b, dt_bias, a_log, d_skip, g_ssm_out,
           w_out, g_cross, g_mem, w_cq, w_ckv, g_cq, g_ck, w_co, g_mlp, w_up, w_down):
    bsz, s_len, d_model = x.shape
    n_mem = mem.shape[1]
    x2d = x.reshape(bsz * s_len, d_model)
    mem2d = mem.reshape(bsz * n_mem, d_model)
    pos_col = positions.reshape(bsz * s_len, 1)
    for i in range(g_mix.shape[0]):
        x2d = _layer(x2d, mem2d, pos_col, bsz, s_len, n_mem, g_mix[i], w_in[i], g_q[i], g_k[i], g_attn_out[i],
                     conv_w[i], conv_b[i], dt_bias[i], a_log[i], d_skip[i], g_ssm_out[i], w_out[i], g_cross[i],
                     g_mem[i], w_cq[i], w_ckv[i], g_cq[i], g_ck[i], w_co[i], g_mlp[i], w_up[i], w_down[i])
    return x2d.reshape(bsz, s_len, d_model)
```

```python
import functools
import math

import jax
import jax.numpy as jnp
from jax import lax
from jax.experimental import pallas as pl
from jax.experimental.pallas import tpu as pltpu

F32 = jnp.float32
BF16 = jnp.bfloat16
EPS = 1e-6

LANES = 128
ATTN_HEAD_DIM = 64
ROT_DIM = ATTN_HEAD_DIM // 4
ROPE_THETA = 500000.0
ATTN_BLOCK = 128
DILATIONS = (1, 4, 16)
SSM_HEAD_DIM = 64
SSM_GROUPS = 4
SSM_STATE = 128
CONV_WIDTH = 4
SSD_CHUNK = 128
N_CROSS_HEADS = 4
CROSS_HEAD_DIM = 128
VMEM_LIMIT = 52 * 1024 * 1024


def _cparams(sem):
    return pltpu.CompilerParams(dimension_semantics=sem, vmem_limit_bytes=VMEM_LIMIT)


def _rms_rows(x, g):
    ms = jnp.mean(x * x, axis=-1, keepdims=True)
    return x * lax.rsqrt(ms + EPS) * g


def _norm_rows_to(x_ref, g_ref, h_ref, rows=16):
    def body(c, carry):
        r0 = pl.multiple_of(c * rows, rows)
        h_ref[pl.ds(r0, rows), :] = _rms_rows(x_ref[pl.ds(r0, rows), :], g_ref[...]).astype(h_ref.dtype)
        return carry
    lax.fori_loop(0, x_ref.shape[0] // rows, body, 0, unroll=4)


def _in_proj_kernel(x_ref, g_ref, w_ref, wdt_ref, o_ref, dt_ref, h_ref):
    @pl.when(pl.program_id(1) == 0)
    def _():
        _norm_rows_to(x_ref, g_ref, h_ref)
        dt_ref[...] = jnp.dot(h_ref[...], wdt_ref[...], preferred_element_type=F32)
    o_ref[...] = jnp.dot(h_ref[...], w_ref[...], preferred_element_type=F32).astype(o_ref.dtype)


def in_proj(x2d, g, w_main, w_dt, n, tm=1024, tn=512):
    m, d = x2d.shape
    return pl.pallas_call(
        _in_proj_kernel,
        grid=(m // tm, n // tn),
        in_specs=[pl.BlockSpec((tm, d), lambda i, j: (i, 0)),
                  pl.BlockSpec((1, d), lambda i, j: (0, 0)),
                  pl.BlockSpec((d, tn), lambda i, j: (0, j)),
                  pl.BlockSpec((d, LANES), lambda i, j: (0, 0))],
        out_specs=[pl.BlockSpec((tm, tn), lambda i, j: (i, j)),
                   pl.BlockSpec((tm, LANES), lambda i, j: (i, 0))],
        out_shape=[jax.ShapeDtypeStruct((m, n), BF16), jax.ShapeDtypeStruct((m, LANES), F32)],
        scratch_shapes=[pltpu.VMEM((tm, d), BF16)],
        compiler_params=_cparams(("parallel", "arbitrary")),
        name="in_proj",
    )(x2d, g, w_main, w_dt)


def _rope_kernel(pos_ref, invf_ref, cmask_ref, amask_ref, bmask_ref, cos_ref, sa_ref, sb_ref):
    ang = pos_ref[...].astype(F32) * invf_ref[...]
    c = jnp.cos(ang)
    s = jnp.sin(ang)
    cm = cmask_ref[...]
    cos_ref[...] = c * cm + (1.0 - cm)
    sa_ref[...] = -s * amask_ref[...]
    sb_ref[...] = s * bmask_ref[...]


def rope_tables(pos_col, tm=512):
    m = pos_col.shape[0]
    half = ROT_DIM // 2
    inv_freq = ROPE_THETA ** (-2.0 * jnp.arange(half, dtype=F32) / ROT_DIM)
    d = jnp.arange(LANES) % ATTN_HEAD_DIM
    invf = inv_freq[d % half][None, :]
    cmask = (d < ROT_DIM).astype(F32)[None, :]
    amask = (d < half).astype(F32)[None, :]
    bmask = ((d >= half) & (d < ROT_DIM)).astype(F32)[None, :]
    row = pl.BlockSpec((1, LANES), lambda i: (0, 0))
    tab = pl.BlockSpec((tm, LANES), lambda i: (i, 0))
    return pl.pallas_call(
        _rope_kernel,
        grid=(m // tm,),
        in_specs=[pl.BlockSpec((tm, 1), lambda i: (i, 0)), row, row, row, row],
        out_specs=[tab, tab, tab],
        out_shape=[jax.ShapeDtypeStruct((m, LANES), F32)] * 3,
        compiler_params=_cparams(("parallel",)),
        name="rope_tab",
    )(pos_col, invf, cmask, amask, bmask)


def _rows(start, size, dil):
    return pl.ds(start, size) if dil == 1 else pl.ds(start, size, stride=dil)


ATTN_GROUP = 2


def _attn_kernel(q_ref, k_ref, v_ref, cos_ref, sa_ref, sb_ref, gq_ref, gk_ref, seg_ref, o_ref,
                 qf, kf, vf, qd, kd, vd, ob, mb, lb, s_scr, p_scr):
    s_len = q_ref.shape[0]
    blk = ATTN_BLOCK
    half = ROT_DIM // 2
    lane = lax.broadcasted_iota(jnp.int32, (blk, LANES), 1)
    head0 = lane < ATTN_HEAD_DIM
    qi = lax.broadcasted_iota(jnp.int32, (2 * blk, 2 * blk), 0) % blk
    kj = lax.broadcasted_iota(jnp.int32, (2 * blk, 2 * blk), 1)
    band_mask = (kj >= qi) & (kj <= qi + blk)
    first_mask = (lax.broadcasted_iota(jnp.int32, (2 * blk, blk), 1)
                  <= lax.broadcasted_iota(jnp.int32, (2 * blk, blk), 0) % blk)

    prep_rows = 256

    def prep(c, carry):
        rows = pl.ds(pl.multiple_of(c * prep_rows, prep_rows), prep_rows)
        cs, sa, sb = cos_ref[rows, :], sa_ref[rows, :], sb_ref[rows, :]

        def norm_rope(x_ref, g_ref):
            x = x_ref[rows, :].astype(F32)
            ss = jnp.dot((x * x).astype(BF16), seg_ref[...], preferred_element_type=F32)
            y = x * lax.rsqrt(ss * (1.0 / ATTN_HEAD_DIM) + EPS) * g_ref[...]
            return y * cs + pltpu.roll(y, LANES - half, 1) * sa + pltpu.roll(y, half, 1) * sb

        qf[rows, :] = norm_rope(q_ref, gq_ref) * (ATTN_HEAD_DIM ** -0.5)
        kf[rows, :] = norm_rope(k_ref, gk_ref)
        vf[rows, :] = v_ref[rows, :].astype(F32)
        return carry

    lax.fori_loop(0, s_len // prep_rows, prep, 0, unroll=2)

    dmid = DILATIONS[1]
    run = s_len // dmid

    def deinterleave(r, carry):
        dst_rows = pl.ds(pl.multiple_of(r * run, run), run)
        for src, dst in ((qf, qd), (kf, kd), (vf, vd)):
            dst[dst_rows, :] = src[pl.ds(r, run, stride=dmid), :]
        return carry

    lax.fori_loop(0, dmid, deinterleave, 0)

    natural, deint = (qf, kf, vf), (qd, kd, vd)
    descs = [(natural, 0, 1, 0, 0, blk, first_mask)]
    descs += [(natural, 0, 1, n * blk, (n - 1) * blk, 2 * blk, band_mask) for n in range(1, s_len // blk)]
    descs += [(deint, 1, 1, r * run, r * run, blk, first_mask) for r in range(dmid)]
    descs += [(deint, 1, 1, r * run + n * blk, r * run + (n - 1) * blk, 2 * blk, band_mask)
              for n in range(1, run // blk) for r in range(dmid)]
    descs += [(deint, 2, dmid, r_hi * run + r_lo, r_hi * run + r_lo, blk, first_mask)
              for r_hi in range(dmid) for r_lo in range(dmid)]

    def scores(desc, slot):
        (q_src, k_src, _), _, dil, q_start, k_start, n_keys, _ = desc
        qt = q_src[_rows(q_start, blk, dil), :]
        zero = jnp.zeros_like(qt)
        q2 = jnp.concatenate([jnp.where(head0, qt, zero), jnp.where(head0, zero, qt)], axis=0).astype(BF16)
        kb = k_src[_rows(k_start, n_keys, dil), :].astype(BF16)
        s_scr[slot, :, :n_keys] = lax.dot_general(q2, kb, (((1,), (1,)), ((), ())), preferred_element_type=F32)

    def softmax(desc, slot):
        _, br, dil, q_start, _, n_keys, mask = desc
        s = jnp.where(mask, s_scr[slot, :, :n_keys], -jnp.inf)
        m = jnp.max(s, axis=-1, keepdims=True)
        p_scr[slot, :, :n_keys] = jnp.exp(s - m).astype(BF16)
        mb[br, _rows(q_start, blk, dil), :] = jnp.where(head0, m[:blk], m[blk:])

    def values(desc, slot):
        (_, _, v_src), br, dil, q_start, k_start, n_keys, _ = desc
        vb = v_src[_rows(k_start, n_keys, dil), :].astype(BF16)
        v1 = jnp.concatenate([vb, jnp.ones_like(vb)], axis=1)
        r = jnp.dot(p_scr[slot, :, :n_keys], v1, preferred_element_type=F32)
        out_rows = _rows(q_start, blk, dil)
        ob[br, out_rows, :] = jnp.where(head0, r[:blk, :LANES], r[blk:, :LANES])
        lb[br, out_rows, :] = jnp.where(head0, r[:blk, LANES:], r[blk:, LANES:])

    groups = [descs[i:i + ATTN_GROUP] for i in range(0, len(descs), ATTN_GROUP)]
    stages = (scores, softmax, values)
    for t in range(len(groups) + len(stages) - 1):
        for lag, stage in enumerate(stages):
            if 0 <= t - lag < len(groups):
                for g, desc in enumerate(groups[t - lag]):
                    stage(desc, ((t - lag) % 2) * ATTN_GROUP + g)

    def merge(c, carry):
        d_start = pl.multiple_of(c * prep_rows, prep_rows)
        r = d_start // run
        nat_rows = pl.ds(dmid * (d_start - r * run) + r, prep_rows, stride=dmid)
        d_rows = pl.ds(d_start, prep_rows)
        rows = (nat_rows, d_rows, d_rows)
        ms = [mb[g, rows[g], :] for g in range(len(DILATIONS))]
        m = functools.reduce(jnp.maximum, ms)
        num = jnp.zeros((prep_rows, LANES), F32)
        den = jnp.zeros((prep_rows, LANES), F32)
        for g in range(len(DILATIONS)):
            w = jnp.exp(ms[g] - m)
            num = num + w * ob[g, rows[g], :]
            den = den + w * lb[g, rows[g], :]
        ob[0, nat_rows, :] = num / den
        return carry

    lax.fori_loop(0, s_len // prep_rows, merge, 0)

    def emit(c, carry):
        rows = pl.ds(pl.multiple_of(c * prep_rows, prep_rows), prep_rows)
        o_ref[rows, :] = ob[0, rows, :].astype(o_ref.dtype)
        return carry

    lax.fori_loop(0, s_len // prep_rows, emit, 0)


def dilated_attention(qkv, tabs, gq2, gk2, bsz, s_len, n_pairs):
    cos_t, sa_t, sb_t = tabs
    d1, d2, d3 = DILATIONS
    assert d1 == 1 and d3 == d2 * d2 and s_len == d3 * ATTN_BLOCK, "layout assumes dilations (1, d, d*d), one block per largest class"
    lane = jnp.arange(LANES)
    seg = (lane[:, None] // ATTN_HEAD_DIM == lane[None, :] // ATTN_HEAD_DIM).astype(BF16)
    blk = lambda off: pl.BlockSpec((s_len, LANES), lambda b, p, off=off: (b, off + p))
    tab = pl.BlockSpec((s_len, LANES), lambda b, p: (b, 0))
    row = pl.BlockSpec((1, LANES), lambda b, p: (0, 0))
    return pl.pallas_call(
        _attn_kernel,
        grid=(bsz, n_pairs),
        in_specs=[blk(0), blk(n_pairs), blk(2 * n_pairs), tab, tab, tab, row, row,
                  pl.BlockSpec((LANES, LANES), lambda b, p: (0, 0))],
        out_specs=pl.BlockSpec((s_len, LANES), lambda b, p: (b, p)),
        out_shape=jax.ShapeDtypeStruct((bsz * s_len, n_pairs * LANES), BF16),
        scratch_shapes=[pltpu.VMEM((s_len, LANES), F32)] * 6
                       + [pltpu.VMEM((len(DILATIONS), s_len, LANES), F32)] * 3
                       + [pltpu.VMEM((2 * ATTN_GROUP, 2 * ATTN_BLOCK, 2 * ATTN_BLOCK), F32),
                          pltpu.VMEM((2 * ATTN_GROUP, 2 * ATTN_BLOCK, 2 * ATTN_BLOCK), BF16)],
        compiler_params=_cparams(("parallel", "arbitrary")),
        name="dilated_attn",
    )(qkv, qkv, qkv, cos_t, sa_t, sb_t, gq2, gk2, seg)


def _split3(x):
    hi = x.astype(BF16)
    r1 = x - hi.astype(F32)
    mid = r1.astype(BF16)
    lo = (r1 - mid.astype(F32)).astype(BF16)
    return hi, mid, lo


def _silu(x):
    return x * (1.0 / (1.0 + jnp.exp(-x)))


def _ssd_kernel(xbc_ref, z_ref, dt_ref, cw_ref, cb_ref, dtb_ref, alog_ref, dskip_ref, g_ref,
                expand_ref, tril_ref, o_ref, xpad, xc, st):
    q = SSD_CHUNK
    d_ssm = z_ref.shape[1]
    d_conv = xbc_ref.shape[1]
    gw = d_ssm // SSM_GROUPS
    heads_per_group = gw // SSM_HEAD_DIM
    pad = 8

    @pl.when(pl.program_id(1) == 0)
    def _():
        xpad[:, 0:pad, :] = jnp.zeros((d_conv // LANES, pad, LANES), F32)
        st[...] = jnp.zeros_like(st)

    for c0 in range(0, d_conv, LANES):
        cols = slice(c0, c0 + LANES)
        slab = c0 // LANES
        xpad[slab, pad:pad + q, :] = xbc_ref[:, cols].astype(F32)
        acc = cb_ref[:, cols] + cw_ref[CONV_WIDTH - 1:CONV_WIDTH, cols] * xpad[slab, pad:pad + q, :]
        for w in range(CONV_WIDTH - 1):
            off = pad - (CONV_WIDTH - 1) + w
            acc = acc + cw_ref[w:w + 1, cols] * xpad[slab, pl.ds(off, q, stride=1), :]
        xc[:, cols] = _silu(acc)
    xpad[:, 0:pad, :] = xpad[:, q:q + pad, :]

    x_dt = dt_ref[...] + dtb_ref[...]
    dt = jnp.maximum(x_dt, 0.0) + jnp.log1p(jnp.exp(-jnp.abs(x_dt)))
    dta = dt * (-jnp.exp(alog_ref[...]))
    tril = tril_ref[...]
    acs = sum(jnp.dot(tril, part, preferred_element_type=F32) for part in _split3(dta))
    acs_t = acs.T
    last = acs[q - 1:q, :]
    expand = expand_ref[...]
    dt_e = jnp.dot(dt.astype(BF16), expand, preferred_element_type=F32)
    dec_e = jnp.dot(jnp.exp(acs).astype(BF16), expand, preferred_element_type=F32)
    w_e = jnp.dot((jnp.exp(last - acs) * dt).astype(BF16), expand, preferred_element_type=F32)

    li = lax.broadcasted_iota(jnp.int32, (q, q), 0)
    si = lax.broadcasted_iota(jnp.int32, (q, q), 1)
    causal = li >= si
    lane_g = lax.broadcasted_iota(jnp.int32, (q, gw), 1)

    for g in range(SSM_GROUPS):
        cols = slice(g * gw, (g + 1) * gw)
        b_f = xc[:, d_ssm + g * SSM_STATE:d_ssm + (g + 1) * SSM_STATE]
        c_b = xc[:, d_ssm + (SSM_GROUPS + g) * SSM_STATE:d_ssm + (SSM_GROUPS + g + 1) * SSM_STATE].astype(BF16)
        xs = xc[:, cols]
        cb = lax.dot_general(c_b, b_f.astype(BF16), (((1,), (1,)), ((), ())), preferred_element_type=F32)
        xdt = (xs * dt_e[:, cols]).astype(BF16)
        ws, rs = [], []
        for hh in range(heads_per_group):
            h = g * heads_per_group + hh
            seg = acs[:, h:h + 1] - acs_t[h:h + 1, :]
            l_mat = jnp.exp(jnp.where(causal, seg, -jnp.inf))
            ws.append((cb * l_mat).astype(BF16))
            in_head = (lane_g >= hh * SSM_HEAD_DIM) & (lane_g < (hh + 1) * SSM_HEAD_DIM)
            rs.append(jnp.where(in_head, xdt, jnp.zeros_like(xdt)))
        y = jnp.dot(jnp.concatenate(ws, axis=1), jnp.concatenate(rs, axis=0), preferred_element_type=F32)
        st_g = st[:, cols]
        y = y + jnp.dot(c_b, st_g.astype(BF16), preferred_element_type=F32) * dec_e[:, cols]
        xw = (xs * w_e[:, cols]).astype(BF16)
        st[:, cols] = st_g * dec_e[q - 1:q, cols] + jnp.dot(b_f.T.astype(BF16), xw, preferred_element_type=F32)
        y = y + dskip_ref[:, cols] * xs
        y = y * _silu(z_ref[:, cols].astype(F32))
        o_ref[:, cols] = _rms_rows(y, g_ref[:, cols]).astype(o_ref.dtype)


def ssd_mixer(packed, dt_raw, conv_w, conv_b, dt_bias, a_log, d_skip, g_out, bsz, s_len, d_ssm, xbc_col, z_col):
    n_heads = d_ssm // SSM_HEAD_DIM
    d_conv = d_ssm + 2 * SSM_GROUPS * SSM_STATE
    nc = s_len // SSD_CHUNK
    padl = lambda v: jnp.pad(v.astype(F32), (0, LANES - n_heads))[None, :]
    expand = (jnp.arange(LANES)[:, None] == (jnp.arange(d_ssm) // SSM_HEAD_DIM)[None, :]).astype(BF16)
    tril = (jnp.arange(SSD_CHUNK)[:, None] >= jnp.arange(SSD_CHUNK)[None, :]).astype(BF16)
    full = lambda shape: pl.BlockSpec(shape, lambda b, c: (0, 0))
    return pl.pallas_call(
        _ssd_kernel,
        grid=(bsz, nc),
        in_specs=[pl.BlockSpec((SSD_CHUNK, d_conv), lambda b, c: (b * nc + c, xbc_col // d_conv)),
                  pl.BlockSpec((SSD_CHUNK, d_ssm), lambda b, c: (b * nc + c, z_col // d_ssm)),
                  pl.BlockSpec((SSD_CHUNK, LANES), lambda b, c: (b * nc + c, 0)),
                  full((CONV_WIDTH, d_conv)), full((1, d_conv)), full((1, LANES)), full((1, LANES)),
                  full((1, d_ssm)), full((1, d_ssm)), full((LANES, d_ssm)), full((SSD_CHUNK, SSD_CHUNK))],
        out_specs=pl.BlockSpec((SSD_CHUNK, d_ssm), lambda b, c: (b * nc + c, 0)),
        out_shape=jax.ShapeDtypeStruct((bsz * s_len, d_ssm), BF16),
        scratch_shapes=[pltpu.VMEM((d_conv // LANES, SSD_CHUNK + 8, LANES), F32),
                        pltpu.VMEM((SSD_CHUNK, d_conv), F32),
                        pltpu.VMEM((SSM_STATE, d_ssm), F32)],
        compiler_params=_cparams(("parallel", "arbitrary")),
        name="ssd",
    )(packed, packed, dt_raw, conv_w.astype(F32), conv_b.astype(F32)[None, :], padl(dt_bias), padl(a_log),
      jnp.repeat(d_skip.astype(F32), SSM_HEAD_DIM)[None, :], g_out.astype(F32)[None, :], expand, tril)


def _out_proj_kernel(attn_ref, ssm_ref, x_ref, g_ref, w_ref, o_ref, mix_ref):
    d_attn = attn_ref.shape[1]

    @pl.when(pl.program_id(1) == 0)
    def _():
        rows = 32

        def body(c, carry):
            rs = pl.ds(pl.multiple_of(c * rows, rows), rows)
            mix_ref[rs, 0:d_attn] = _rms_rows(attn_ref[rs, :].astype(F32), g_ref[...]).astype(BF16)
            return carry
        lax.fori_loop(0, attn_ref.shape[0] // rows, body, 0, unroll=4)
        mix_ref[:, d_attn:] = ssm_ref[...]
    o_ref[...] = x_ref[...] + jnp.dot(mix_ref[...], w_ref[...], preferred_element_type=F32)


def out_proj(attn, ssm, x2d, g, w, tm=1024, tn=512):
    m, d = x2d.shape
    da, ds_ = attn.shape[1], ssm.shape[1]
    return pl.pallas_call(
        _out_proj_kernel,
        grid=(m // tm, d // tn),
        in_specs=[pl.BlockSpec((tm, da), lambda i, j: (i, 0)),
                  pl.BlockSpec((tm, ds_), lambda i, j: (i, 0)),
                  pl.BlockSpec((tm, tn), lambda i, j: (i, j)),
                  pl.BlockSpec((1, da), lambda i, j: (0, 0)),
                  pl.BlockSpec((da + ds_, tn), lambda i, j: (0, j))],
        out_specs=pl.BlockSpec((tm, tn), lambda i, j: (i, j)),
        out_shape=jax.ShapeDtypeStruct((m, d), F32),
        scratch_shapes=[pltpu.VMEM((tm, da + ds_), BF16)],
        compiler_params=_cparams(("parallel", "arbitrary")),
        name="out_proj",
    )(attn, ssm, x2d, g, w)


def _mem_kv_kernel(mem_ref, g_ref, w_ref, gk_ref, kv_ref, h_ref):
    d_cross = N_CROSS_HEADS * CROSS_HEAD_DIM
    _norm_rows_to(mem_ref, g_ref, h_ref)
    kv = jnp.dot(h_ref[...], w_ref[...], preferred_element_type=F32)
    for h in range(N_CROSS_HEADS):
        cols = slice(h * CROSS_HEAD_DIM, (h + 1) * CROSS_HEAD_DIM)
        kv_ref[:, cols] = _rms_rows(kv[:, cols], gk_ref[...]).astype(kv_ref.dtype)
    kv_ref[:, d_cross:] = kv[:, d_cross:].astype(kv_ref.dtype)


def mem_kv(mem2d, g, w, gk, tm=256):
    m, d = mem2d.shape
    n = w.shape[1]
    return pl.pallas_call(
        _mem_kv_kernel,
        grid=(m // tm,),
        in_specs=[pl.BlockSpec((tm, d), lambda i: (i, 0)),
                  pl.BlockSpec((1, d), lambda i: (0, 0)),
                  pl.BlockSpec((d, n), lambda i: (0, 0)),
                  pl.BlockSpec((1, CROSS_HEAD_DIM), lambda i: (0, 0))],
        out_specs=pl.BlockSpec((tm, n), lambda i: (i, 0)),
        out_shape=jax.ShapeDtypeStruct((m, n), BF16),
        scratch_shapes=[pltpu.VMEM((tm, d), BF16)],
        compiler_params=_cparams(("parallel",)),
        name="mem_kv",
    )(mem2d, g, w, gk)


def _cross_kernel(x_ref, g_ref, wq_ref, gq_ref, k_ref, v_ref, wo_ref, o_ref, h_ref, q_ref, a_ref):
    tm = x_ref.shape[0]
    _norm_rows_to(x_ref, g_ref, h_ref)
    q_ref[...] = jnp.dot(h_ref[...], wq_ref[...], preferred_element_type=F32)
    rows = 128

    def body(c, carry):
        rs = pl.ds(pl.multiple_of(c * rows, rows), rows)
        for h in range(N_CROSS_HEADS):
            cols = slice(h * CROSS_HEAD_DIM, (h + 1) * CROSS_HEAD_DIM)
            qh = (_rms_rows(q_ref[rs, cols], gq_ref[...]) * (CROSS_HEAD_DIM ** -0.5)).astype(BF16)
            s = lax.dot_general(qh, k_ref[:, cols], (((1,), (1,)), ((), ())), preferred_element_type=F32)
            p = jnp.exp(s - jnp.max(s, axis=-1, keepdims=True))
            l = jnp.sum(p, axis=-1, keepdims=True)
            pv = jnp.dot(p.astype(BF16), v_ref[:, cols], preferred_element_type=F32)
            a_ref[rs, cols] = (pv / l).astype(BF16)
        return carry
    lax.fori_loop(0, tm // rows, body, 0)
    o_ref[...] = x_ref[...] + jnp.dot(a_ref[...], wo_ref[...], preferred_element_type=F32)


def cross_attention(x2d, g, wq, gq, kv, wo, s_len, n_mem, tm=512):
    m, d = x2d.shape
    dc = wq.shape[1]
    per_batch = s_len // tm
    full = lambda shape: pl.BlockSpec(shape, lambda i: (0, 0))
    return pl.pallas_call(
        _cross_kernel,
        grid=(m // tm,),
        in_specs=[pl.BlockSpec((tm, d), lambda i: (i, 0)), full((1, d)), full((d, dc)), full((1, CROSS_HEAD_DIM)),
                  pl.BlockSpec((n_mem, dc), lambda i: (i // per_batch, 0)),
                  pl.BlockSpec((n_mem, dc), lambda i: (i // per_batch, 1)),
                  full((dc, d))],
        out_specs=pl.BlockSpec((tm, d), lambda i: (i, 0)),
        out_shape=jax.ShapeDtypeStruct((m, d), F32),
        scratch_shapes=[pltpu.VMEM((tm, d), BF16), pltpu.VMEM((tm, dc), F32), pltpu.VMEM((tm, dc), BF16)],
        compiler_params=_cparams(("parallel",)),
        name="cross_attn",
    )(x2d, g, wq, gq, kv, kv, wo)


def _mlp_kernel(x_ref, g_ref, wu_ref, wd_ref, o_ref, h_ref):
    @pl.when(pl.program_id(1) == 0)
    def _():
        _norm_rows_to(x_ref, g_ref, h_ref)
        o_ref[...] = x_ref[...]
    u = jnp.dot(h_ref[...], wu_ref[...], preferred_element_type=F32)
    u = jnp.square(jnp.maximum(u, 0.0)).astype(BF16)
    o_ref[...] += jnp.dot(u, wd_ref[...], preferred_element_type=F32)


def mlp(x2d, g, wu, wd, tm=512, tf=1024):
    m, d = x2d.shape
    f = wu.shape[1]
    return pl.pallas_call(
        _mlp_kernel,
        grid=(m // tm, f // tf),
        in_specs=[pl.BlockSpec((tm, d), lambda i, j: (i, 0)),
                  pl.BlockSpec((1, d), lambda i, j: (0, 0)),
                  pl.BlockSpec((d, tf), lambda i, j: (0, j)),
                  pl.BlockSpec((tf, d), lambda i, j: (j, 0))],
        out_specs=pl.BlockSpec((tm, d), lambda i, j: (i, 0)),
        out_shape=jax.ShapeDtypeStruct((m, d), F32),
        scratch_shapes=[pltpu.VMEM((tm, d), BF16)],
        compiler_params=_cparams(("parallel", "arbitrary")),
        name="mlp",
    )(x2d, g, wu, wd)


def _layer(x2d, mem2d, pos_col, bsz, s_len, n_mem, g_mix, w_in, g_q, g_k, g_attn_out, conv_w, conv_b, dt_bias,
           a_log, d_skip, g_ssm_out, w_out, g_cross, g_mem, w_cq, w_ckv, g_cq, g_ck, w_co, g_mlp, w_up, w_down):
    d_model = x2d.shape[1]
    d_attn = d_model // 2
    d_ssm = d_model // 2
    n_pairs = d_attn // LANES
    n_ssm_heads = d_ssm // SSM_HEAD_DIM
    d_conv = d_ssm + 2 * SSM_GROUPS * SSM_STATE
    d_packed = 3 * d_attn + d_ssm + d_conv
    row = lambda v: v.astype(F32)[None, :]

    w_dt = jnp.pad(w_in[:, d_packed:], ((0, 0), (0, LANES - n_ssm_heads))).astype(BF16)
    packed, dt_raw = in_proj(x2d, row(g_mix), w_in.astype(BF16), w_dt, d_packed)

    tabs = rope_tables(pos_col)
    attn = dilated_attention(packed, tabs, row(jnp.tile(g_q, 2)), row(jnp.tile(g_k, 2)), bsz, s_len, n_pairs)
    ssm = ssd_mixer(packed, dt_raw, conv_w, conv_b, dt_bias, a_log, d_skip, g_ssm_out, bsz, s_len, d_ssm,
                    xbc_col=3 * d_attn + d_ssm, z_col=3 * d_attn)
    x2d = out_proj(attn, ssm, x2d, row(g_attn_out), w_out.astype(BF16))

    kv = mem_kv(mem2d, row(g_mem), w_ckv.astype(BF16), row(g_ck))
    x2d = cross_attention(x2d, row(g_cross), w_cq.astype(BF16), row(g_cq), kv, w_co.astype(BF16), s_len, n_mem)

    return mlp(x2d, row(g_mlp), w_up.astype(BF16), w_down.astype(BF16))


def kernel(x, mem, positions, g_mix, w_in, g_q, g_k, g_attn_out, conv_w, conv_b, dt_bias, a_log, d_skip, g_ssm_out,
           w_out, g_cross, g_mem, w_cq, w_ckv, g_cq, g_ck, w_co, g_mlp, w_up, w_down):
    bsz, s_len, d_model = x.shape
    n_mem = mem.shape[1]
    x2d = x.reshape(bsz * s_len, d_model)
    mem2d = mem.reshape(bsz * n_mem, d_model)
    pos_col = positions.reshape(bsz * s_len, 1)
    for i in range(g_mix.shape[0]):
        x2d = _layer(x2d, mem2d, pos_col, bsz, s_len, n_mem, g_mix[i], w_in[i], g_q[i], g_k[i], g_attn_out[i],
                     conv_w[i], conv_b[i], dt_bias[i], a_log[i], d_skip[i], g_ssm_out[i], w_out[i], g_cross[i],
                     g_mem[i], w_cq[i], w_ckv[i], g_cq[i], g_ck[i], w_co[i], g_mlp[i], w_up[i], w_down[i])
    return x2d.reshape(bsz, s_len, d_model)
```

```python
import functools
import math

import jax
import jax.numpy as jnp
from jax import lax
from jax.experimental import pallas as pl
from jax.experimental.pallas import tpu as pltpu

F32 = jnp.float32
BF16 = jnp.bfloat16
EPS = 1e-6

LANES = 128
ATTN_HEAD_DIM = 64
ROT_DIM = ATTN_HEAD_DIM // 4
ROPE_THETA = 500000.0
ATTN_BLOCK = 128
DILATIONS = (1, 4, 16)
SSM_HEAD_DIM = 64
SSM_GROUPS = 4
SSM_STATE = 128
CONV_WIDTH = 4
SSD_CHUNK = 128
N_CROSS_HEADS = 4
CROSS_HEAD_DIM = 128
CROSS_ROWS = 256
CROSS_GROUP = 2
VMEM_LIMIT = 52 * 1024 * 1024


def _cparams(sem):
    return pltpu.CompilerParams(dimension_semantics=sem, vmem_limit_bytes=VMEM_LIMIT)


def _rms_rows(x, g):
    ms = jnp.mean(x * x, axis=-1, keepdims=True)
    return x * lax.rsqrt(ms + EPS) * g


def _norm_rows_to(x_ref, g_ref, h_ref, rows=16):
    def body(c, carry):
        r0 = pl.multiple_of(c * rows, rows)
        h_ref[pl.ds(r0, rows), :] = _rms_rows(x_ref[pl.ds(r0, rows), :], g_ref[...]).astype(h_ref.dtype)
        return carry
    lax.fori_loop(0, x_ref.shape[0] // rows, body, 0, unroll=4)


def _in_proj_kernel(x_ref, g_ref, w_ref, wdt_ref, o_ref, dt_ref, h_ref):
    @pl.when(pl.program_id(1) == 0)
    def _():
        _norm_rows_to(x_ref, g_ref, h_ref)
        dt_ref[...] = jnp.dot(h_ref[...], wdt_ref[...], preferred_element_type=F32)
    o_ref[...] = jnp.dot(h_ref[...], w_ref[...].astype(BF16), preferred_element_type=F32).astype(o_ref.dtype)


def in_proj(x2d, g, w_main, w_dt, n, tm=1024, tn=512):
    m, d = x2d.shape
    return pl.pallas_call(
        _in_proj_kernel,
        grid=(m // tm, n // tn),
        in_specs=[pl.BlockSpec((tm, d), lambda i, j: (i, 0)),
                  pl.BlockSpec((1, d), lambda i, j: (0, 0)),
                  pl.BlockSpec((d, tn), lambda i, j: (0, j)),
                  pl.BlockSpec((d, LANES), lambda i, j: (0, 0))],
        out_specs=[pl.BlockSpec((tm, tn), lambda i, j: (i, j)),
                   pl.BlockSpec((tm, LANES), lambda i, j: (i, 0))],
        out_shape=[jax.ShapeDtypeStruct((m, n), BF16), jax.ShapeDtypeStruct((m, LANES), F32)],
        scratch_shapes=[pltpu.VMEM((tm, d), BF16)],
        compiler_params=_cparams(("parallel", "arbitrary")),
        name="in_proj",
    )(x2d, g, w_main, w_dt)


def _rope_kernel(pos_ref, invf_ref, cmask_ref, amask_ref, bmask_ref, cos_ref, sa_ref, sb_ref):
    ang = pos_ref[...].astype(F32) * invf_ref[...]
    c = jnp.cos(ang)
    s = jnp.sin(ang)
    cm = cmask_ref[...]
    cos_ref[...] = c * cm + (1.0 - cm)
    sa_ref[...] = -s * amask_ref[...]
    sb_ref[...] = s * bmask_ref[...]


def rope_tables(pos_col, tm=512):
    m = pos_col.shape[0]
    half = ROT_DIM // 2
    inv_freq = ROPE_THETA ** (-2.0 * jnp.arange(half, dtype=F32) / ROT_DIM)
    d = jnp.arange(LANES) % ATTN_HEAD_DIM
    invf = inv_freq[d % half][None, :]
    cmask = (d < ROT_DIM).astype(F32)[None, :]
    amask = (d < half).astype(F32)[None, :]
    bmask = ((d >= half) & (d < ROT_DIM)).astype(F32)[None, :]
    row = pl.BlockSpec((1, LANES), lambda i: (0, 0))
    tab = pl.BlockSpec((tm, LANES), lambda i: (i, 0))
    return pl.pallas_call(
        _rope_kernel,
        grid=(m // tm,),
        in_specs=[pl.BlockSpec((tm, 1), lambda i: (i, 0)), row, row, row, row],
        out_specs=[tab, tab, tab],
        out_shape=[jax.ShapeDtypeStruct((m, LANES), F32)] * 3,
        compiler_params=_cparams(("parallel",)),
        name="rope_tab",
    )(pos_col, invf, cmask, amask, bmask)


def _rows(start, size, dil):
    return pl.ds(start, size) if dil == 1 else pl.ds(start, size, stride=dil)


ATTN_GROUP = 2


def _software_pipeline(stages, items, group):
    groups = [items[i:i + group] for i in range(0, len(items), group)]
    for t in range(len(groups) + len(stages) - 1):
        for lag, stage in enumerate(stages):
            if 0 <= t - lag < len(groups):
                for g, item in enumerate(groups[t - lag]):
                    stage(item, ((t - lag) % 2) * group + g)


def _attn_kernel(q_ref, k_ref, v_ref, cos_ref, sa_ref, sb_ref, gq_ref, gk_ref, seg_ref, o_ref,
                 qf, kf, vf, qd, kd, vd, ob, mb, lb, s_scr, p_scr):
    s_len = q_ref.shape[0]
    blk = ATTN_BLOCK
    half = ROT_DIM // 2
    lane = lax.broadcasted_iota(jnp.int32, (blk, LANES), 1)
    head0 = lane < ATTN_HEAD_DIM
    qi = lax.broadcasted_iota(jnp.int32, (2 * blk, 2 * blk), 0) % blk
    kj = lax.broadcasted_iota(jnp.int32, (2 * blk, 2 * blk), 1)
    band_mask = (kj >= qi) & (kj <= qi + blk)
    first_mask = (lax.broadcasted_iota(jnp.int32, (2 * blk, blk), 1)
                  <= lax.broadcasted_iota(jnp.int32, (2 * blk, blk), 0) % blk)

    prep_rows = 256

    def prep(c, carry):
        rows = pl.ds(pl.multiple_of(c * prep_rows, prep_rows), prep_rows)
        cs, sa, sb = cos_ref[rows, :], sa_ref[rows, :], sb_ref[rows, :]

        def norm_rope(x_ref, g_ref):
            x = x_ref[rows, :].astype(F32)
            ss = jnp.dot((x * x).astype(BF16), seg_ref[...], preferred_element_type=F32)
            y = x * lax.rsqrt(ss * (1.0 / ATTN_HEAD_DIM) + EPS) * g_ref[...]
            return y * cs + pltpu.roll(y, LANES - half, 1) * sa + pltpu.roll(y, half, 1) * sb

        qf[rows, :] = norm_rope(q_ref, gq_ref) * (ATTN_HEAD_DIM ** -0.5)
        kf[rows, :] = norm_rope(k_ref, gk_ref)
        vf[rows, :] = v_ref[rows, :].astype(F32)
        return carry

    lax.fori_loop(0, s_len // prep_rows, prep, 0, unroll=2)

    dmid = DILATIONS[1]
    run = s_len // dmid

    def deinterleave(r, carry):
        dst_rows = pl.ds(pl.multiple_of(r * run, run), run)
        for src, dst in ((qf, qd), (kf, kd), (vf, vd)):
            dst[dst_rows, :] = src[pl.ds(r, run, stride=dmid), :]
        return carry

    lax.fori_loop(0, dmid, deinterleave, 0)

    natural, deint = (qf, kf, vf), (qd, kd, vd)
    descs = [(natural, 0, 1, 0, 0, blk, first_mask)]
    descs += [(natural, 0, 1, n * blk, (n - 1) * blk, 2 * blk, band_mask) for n in range(1, s_len // blk)]
    descs += [(deint, 1, 1, r * run, r * run, blk, first_mask) for r in range(dmid)]
    descs += [(deint, 1, 1, r * run + n * blk, r * run + (n - 1) * blk, 2 * blk, band_mask)
              for n in range(1, run // blk) for r in range(dmid)]
    descs += [(deint, 2, dmid, r_hi * run + r_lo, r_hi * run + r_lo, blk, first_mask)
              for r_hi in range(dmid) for r_lo in range(dmid)]

    def scores(desc, slot):
        (q_src, k_src, _), _, dil, q_start, k_start, n_keys, _ = desc
        qt = q_src[_rows(q_start, blk, dil), :]
        zero = jnp.zeros_like(qt)
        q2 = jnp.concatenate([jnp.where(head0, qt, zero), jnp.where(head0, zero, qt)], axis=0).astype(BF16)
        kb = k_src[_rows(k_start, n_keys, dil), :].astype(BF16)
        s_scr[slot, :, :n_keys] = lax.dot_general(q2, kb, (((1,), (1,)), ((), ())), preferred_element_type=F32)

    def softmax(desc, slot):
        _, br, dil, q_start, _, n_keys, mask = desc
        s = jnp.where(mask, s_scr[slot, :, :n_keys], -jnp.inf)
        m = jnp.max(s, axis=-1, keepdims=True)
        p_scr[slot, :, :n_keys] = jnp.exp(s - m).astype(BF16)
        mb[br, _rows(q_start, blk, dil), :] = jnp.where(head0, m[:blk], m[blk:])

    def values(desc, slot):
        (_, _, v_src), br, dil, q_start, k_start, n_keys, _ = desc
        vb = v_src[_rows(k_start, n_keys, dil), :].astype(BF16)
        v1 = jnp.concatenate([vb, jnp.ones_like(vb)], axis=1)
        r = jnp.dot(p_scr[slot, :, :n_keys], v1, preferred_element_type=F32)
        out_rows = _rows(q_start, blk, dil)
        ob[br, out_rows, :] = jnp.where(head0, r[:blk, :LANES], r[blk:, :LANES])
        lb[br, out_rows, :] = jnp.where(head0, r[:blk, LANES:], r[blk:, LANES:])

    _software_pipeline((scores, softmax, values), descs, ATTN_GROUP)

    def merge(c, carry):
        d_start = pl.multiple_of(c * prep_rows, prep_rows)
        r = d_start // run
        nat_rows = pl.ds(dmid * (d_start - r * run) + r, prep_rows, stride=dmid)
        d_rows = pl.ds(d_start, prep_rows)
        rows = (nat_rows, d_rows, d_rows)
        ms = [mb[g, rows[g], :] for g in range(len(DILATIONS))]
        m = functools.reduce(jnp.maximum, ms)
        num = jnp.zeros((prep_rows, LANES), F32)
        den = jnp.zeros((prep_rows, LANES), F32)
        for g in range(len(DILATIONS)):
            w = jnp.exp(ms[g] - m)
            num = num + w * ob[g, rows[g], :]
            den = den + w * lb[g, rows[g], :]
        ob[0, nat_rows, :] = num / den
        return carry

    lax.fori_loop(0, s_len // prep_rows, merge, 0)

    def emit(c, carry):
        rows = pl.ds(pl.multiple_of(c * prep_rows, prep_rows), prep_rows)
        o_ref[rows, :] = ob[0, rows, :].astype(o_ref.dtype)
        return carry

    lax.fori_loop(0, s_len // prep_rows, emit, 0)


def dilated_attention(qkv, tabs, gq2, gk2, bsz, s_len, n_pairs):
    cos_t, sa_t, sb_t = tabs
    d1, d2, d3 = DILATIONS
    assert d1 == 1 and d3 == d2 * d2 and s_len == d3 * ATTN_BLOCK, "layout assumes dilations (1, d, d*d), one block per largest class"
    lane = jnp.arange(LANES)
    seg = (lane[:, None] // ATTN_HEAD_DIM == lane[None, :] // ATTN_HEAD_DIM).astype(BF16)
    blk = lambda off: pl.BlockSpec((s_len, LANES), lambda b, p, off=off: (b, off + p))
    tab = pl.BlockSpec((s_len, LANES), lambda b, p: (b, 0))
    row = pl.BlockSpec((1, LANES), lambda b, p: (0, 0))
    return pl.pallas_call(
        _attn_kernel,
        grid=(bsz, n_pairs),
        in_specs=[blk(0), blk(n_pairs), blk(2 * n_pairs), tab, tab, tab, row, row,
                  pl.BlockSpec((LANES, LANES), lambda b, p: (0, 0))],
        out_specs=pl.BlockSpec((s_len, LANES), lambda b, p: (b, p)),
        out_shape=jax.ShapeDtypeStruct((bsz * s_len, n_pairs * LANES), BF16),
        scratch_shapes=[pltpu.VMEM((s_len, LANES), F32)] * 6
                       + [pltpu.VMEM((len(DILATIONS), s_len, LANES), F32)] * 3
                       + [pltpu.VMEM((2 * ATTN_GROUP, 2 * ATTN_BLOCK, 2 * ATTN_BLOCK), F32),
                          pltpu.VMEM((2 * ATTN_GROUP, 2 * ATTN_BLOCK, 2 * ATTN_BLOCK), BF16)],
        compiler_params=_cparams(("parallel", "arbitrary")),
        name="dilated_attn",
    )(qkv, qkv, qkv, cos_t, sa_t, sb_t, gq2, gk2, seg)


def _split3(x):
    hi = x.astype(BF16)
    r1 = x - hi.astype(F32)
    mid = r1.astype(BF16)
    lo = (r1 - mid.astype(F32)).astype(BF16)
    return hi, mid, lo


def _silu(x):
    return x * (1.0 / (1.0 + jnp.exp(-x)))


def _ssd_kernel(xbc_ref, z_ref, dt_ref, cw_ref, cb_ref, dtb_ref, alog_ref, dskip_ref, g_ref,
                expand_ref, tril_ref, o_ref, xpad, xc, st):
    q = SSD_CHUNK
    d_ssm = z_ref.shape[1]
    d_conv = xbc_ref.shape[1]
    gw = d_ssm // SSM_GROUPS
    heads_per_group = gw // SSM_HEAD_DIM
    pad = 8

    @pl.when(pl.program_id(1) == 0)
    def _():
        xpad[:, 0:pad, :] = jnp.zeros((d_conv // LANES, pad, LANES), F32)
        st[...] = jnp.zeros_like(st)

    for c0 in range(0, d_conv, LANES):
        cols = slice(c0, c0 + LANES)
        slab = c0 // LANES
        xpad[slab, pad:pad + q, :] = xbc_ref[:, cols].astype(F32)
        acc = cb_ref[:, cols] + cw_ref[CONV_WIDTH - 1:CONV_WIDTH, cols] * xpad[slab, pad:pad + q, :]
        for w in range(CONV_WIDTH - 1):
            off = pad - (CONV_WIDTH - 1) + w
            acc = acc + cw_ref[w:w + 1, cols] * xpad[slab, pl.ds(off, q, stride=1), :]
        xc[:, cols] = _silu(acc)
    xpad[:, 0:pad, :] = xpad[:, q:q + pad, :]

    x_dt = dt_ref[...] + dtb_ref[...]
    dt = jnp.maximum(x_dt, 0.0) + jnp.log1p(jnp.exp(-jnp.abs(x_dt)))
    dta = dt * (-jnp.exp(alog_ref[...]))
    tril = tril_ref[...]
    acs = sum(jnp.dot(tril, part, preferred_element_type=F32) for part in _split3(dta))
    acs_t = acs.T
    last = acs[q - 1:q, :]
    expand = expand_ref[...]
    dt_e = jnp.dot(dt.astype(BF16), expand, preferred_element_type=F32)
    dec_e = jnp.dot(jnp.exp(acs).astype(BF16), expand, preferred_element_type=F32)
    w_e = jnp.dot((jnp.exp(last - acs) * dt).astype(BF16), expand, preferred_element_type=F32)

    li = lax.broadcasted_iota(jnp.int32, (q, q), 0)
    si = lax.broadcasted_iota(jnp.int32, (q, q), 1)
    causal = li >= si
    lane_g = lax.broadcasted_iota(jnp.int32, (q, gw), 1)

    for g in range(SSM_GROUPS):
        cols = slice(g * gw, (g + 1) * gw)
        b_f = xc[:, d_ssm + g * SSM_STATE:d_ssm + (g + 1) * SSM_STATE]
        c_b = xc[:, d_ssm + (SSM_GROUPS + g) * SSM_STATE:d_ssm + (SSM_GROUPS + g + 1) * SSM_STATE].astype(BF16)
        xs = xc[:, cols]
        cb = lax.dot_general(c_b, b_f.astype(BF16), (((1,), (1,)), ((), ())), preferred_element_type=F32)
        xdt = (xs * dt_e[:, cols]).astype(BF16)
        ws, rs = [], []
        for hh in range(heads_per_group):
            h = g * heads_per_group + hh
            seg = acs[:, h:h + 1] - acs_t[h:h + 1, :]
            l_mat = jnp.exp(jnp.where(causal, seg, -jnp.inf))
            ws.append((cb * l_mat).astype(BF16))
            in_head = (lane_g >= hh * SSM_HEAD_DIM) & (lane_g < (hh + 1) * SSM_HEAD_DIM)
            rs.append(jnp.where(in_head, xdt, jnp.zeros_like(xdt)))
        y = jnp.dot(jnp.concatenate(ws, axis=1), jnp.concatenate(rs, axis=0), preferred_element_type=F32)
        st_g = st[:, cols]
        y = y + jnp.dot(c_b, st_g.astype(BF16), preferred_element_type=F32) * dec_e[:, cols]
        xw = (xs * w_e[:, cols]).astype(BF16)
        st[:, cols] = st_g * dec_e[q - 1:q, cols] + jnp.dot(b_f.T.astype(BF16), xw, preferred_element_type=F32)
        y = y + dskip_ref[:, cols] * xs
        y = y * _silu(z_ref[:, cols].astype(F32))
        o_ref[:, cols] = _rms_rows(y, g_ref[:, cols]).astype(o_ref.dtype)


def ssd_mixer(packed, dt_raw, conv_w, conv_b, dt_bias, a_log, d_skip, g_out, bsz, s_len, d_ssm, xbc_col, z_col):
    n_heads = d_ssm // SSM_HEAD_DIM
    d_conv = d_ssm + 2 * SSM_GROUPS * SSM_STATE
    nc = s_len // SSD_CHUNK
    padl = lambda v: jnp.pad(v.astype(F32), (0, LANES - n_heads))[None, :]
    expand = (jnp.arange(LANES)[:, None] == (jnp.arange(d_ssm) // SSM_HEAD_DIM)[None, :]).astype(BF16)
    tril = (jnp.arange(SSD_CHUNK)[:, None] >= jnp.arange(SSD_CHUNK)[None, :]).astype(BF16)
    full = lambda shape: pl.BlockSpec(shape, lambda b, c: (0, 0))
    return pl.pallas_call(
        _ssd_kernel,
        grid=(bsz, nc),
        in_specs=[pl.BlockSpec((SSD_CHUNK, d_conv), lambda b, c: (b * nc + c, xbc_col // d_conv)),
                  pl.BlockSpec((SSD_CHUNK, d_ssm), lambda b, c: (b * nc + c, z_col // d_ssm)),
                  pl.BlockSpec((SSD_CHUNK, LANES), lambda b, c: (b * nc + c, 0)),
                  full((CONV_WIDTH, d_conv)), full((1, d_conv)), full((1, LANES)), full((1, LANES)),
                  full((1, d_ssm)), full((1, d_ssm)), full((LANES, d_ssm)), full((SSD_CHUNK, SSD_CHUNK))],
        out_specs=pl.BlockSpec((SSD_CHUNK, d_ssm), lambda b, c: (b * nc + c, 0)),
        out_shape=jax.ShapeDtypeStruct((bsz * s_len, d_ssm), BF16),
        scratch_shapes=[pltpu.VMEM((d_conv // LANES, SSD_CHUNK + 8, LANES), F32),
                        pltpu.VMEM((SSD_CHUNK, d_conv), F32),
                        pltpu.VMEM((SSM_STATE, d_ssm), F32)],
        compiler_params=_cparams(("parallel", "arbitrary")),
        name="ssd",
    )(packed, packed, dt_raw, conv_w.astype(F32), conv_b.astype(F32)[None, :], padl(dt_bias), padl(a_log),
      jnp.repeat(d_skip.astype(F32), SSM_HEAD_DIM)[None, :], g_out.astype(F32)[None, :], expand, tril)


def _out_proj_kernel(attn_ref, ssm_ref, x_ref, g_ref, w_ref, o_ref, mix_ref):
    d_attn = attn_ref.shape[1]

    @pl.when(pl.program_id(1) == 0)
    def _():
        rows = 32

        def body(c, carry):
            rs = pl.ds(pl.multiple_of(c * rows, rows), rows)
            mix_ref[rs, 0:d_attn] = _rms_rows(attn_ref[rs, :].astype(F32), g_ref[...]).astype(BF16)
            return carry
        lax.fori_loop(0, attn_ref.shape[0] // rows, body, 0, unroll=4)
        mix_ref[:, d_attn:] = ssm_ref[...]
    o_ref[...] = x_ref[...] + jnp.dot(mix_ref[...], w_ref[...].astype(BF16), preferred_element_type=F32)


def out_proj(attn, ssm, x2d, g, w, tm=2048, tn=256):
    m, d = x2d.shape
    da, ds_ = attn.shape[1], ssm.shape[1]
    return pl.pallas_call(
        _out_proj_kernel,
        grid=(m // tm, d // tn),
        in_specs=[pl.BlockSpec((tm, da), lambda i, j: (i, 0)),
                  pl.BlockSpec((tm, ds_), lambda i, j: (i, 0)),
                  pl.BlockSpec((tm, tn), lambda i, j: (i, j)),
                  pl.BlockSpec((1, da), lambda i, j: (0, 0)),
                  pl.BlockSpec((da + ds_, tn), lambda i, j: (0, j))],
        out_specs=pl.BlockSpec((tm, tn), lambda i, j: (i, j)),
        out_shape=jax.ShapeDtypeStruct((m, d), F32),
        scratch_shapes=[pltpu.VMEM((tm, da + ds_), BF16)],
        compiler_params=_cparams(("parallel", "arbitrary")),
        name="out_proj",
    )(attn, ssm, x2d, g, w)


def _mem_kv_kernel(mem_ref, g_ref, w_ref, gk_ref, kv_ref, h_ref):
    d_cross = N_CROSS_HEADS * CROSS_HEAD_DIM
    _norm_rows_to(mem_ref, g_ref, h_ref)
    kv = jnp.dot(h_ref[...], w_ref[...], preferred_element_type=F32)
    for h in range(N_CROSS_HEADS):
        cols = slice(h * CROSS_HEAD_DIM, (h + 1) * CROSS_HEAD_DIM)
        kv_ref[:, cols] = _rms_rows(kv[:, cols], gk_ref[...]).astype(kv_ref.dtype)
    kv_ref[:, d_cross:] = kv[:, d_cross:].astype(kv_ref.dtype)


def mem_kv(mem2d, g, w, gk, tm=256):
    m, d = mem2d.shape
    n = w.shape[1]
    return pl.pallas_call(
        _mem_kv_kernel,
        grid=(m // tm,),
        in_specs=[pl.BlockSpec((tm, d), lambda i: (i, 0)),
                  pl.BlockSpec((1, d), lambda i: (0, 0)),
                  pl.BlockSpec((d, n), lambda i: (0, 0)),
                  pl.BlockSpec((1, CROSS_HEAD_DIM), lambda i: (0, 0))],
        out_specs=pl.BlockSpec((tm, n), lambda i: (i, 0)),
        out_shape=jax.ShapeDtypeStruct((m, n), BF16),
        scratch_shapes=[pltpu.VMEM((tm, d), BF16)],
        compiler_params=_cparams(("parallel",)),
        name="mem_kv",
    )(mem2d, g, w, gk)


def _cross_kernel(x_ref, g_ref, wq_ref, gq_ref, k_ref, v_ref, wo_ref, o_ref, h_ref, q_ref, a_ref, s_scr, p_scr):
    tm = x_ref.shape[0]
    rows = CROSS_ROWS
    _norm_rows_to(x_ref, g_ref, h_ref)
    q_ref[...] = jnp.dot(h_ref[...], wq_ref[...], preferred_element_type=F32)

    def scores(item, slot):
        r0, h = item
        cols = slice(h * CROSS_HEAD_DIM, (h + 1) * CROSS_HEAD_DIM)
        qh = (_rms_rows(q_ref[r0:r0 + rows, cols], gq_ref[...]) * (CROSS_HEAD_DIM ** -0.5)).astype(BF16)
        s_scr[slot] = lax.dot_general(qh, k_ref[:, cols], (((1,), (1,)), ((), ())), preferred_element_type=F32)

    def softmax(item, slot):
        s = s_scr[slot]
        p_scr[slot] = jnp.exp(s - jnp.max(s, axis=-1, keepdims=True)).astype(BF16)

    def values(item, slot):
        r0, h = item
        cols = slice(h * CROSS_HEAD_DIM, (h + 1) * CROSS_HEAD_DIM)
        vh = v_ref[:, cols]
        r = jnp.dot(p_scr[slot], jnp.concatenate([vh, jnp.ones_like(vh)], axis=1), preferred_element_type=F32)
        a_ref[r0:r0 + rows, cols] = (r[:, :CROSS_HEAD_DIM] / r[:, CROSS_HEAD_DIM:]).astype(BF16)

    items = [(r0, h) for r0 in range(0, tm, rows) for h in range(N_CROSS_HEADS)]
    _software_pipeline((scores, softmax, values), items, CROSS_GROUP)
    o_ref[...] = x_ref[...] + jnp.dot(a_ref[...], wo_ref[...], preferred_element_type=F32)


def cross_attention(x2d, g, wq, gq, kv, wo, s_len, n_mem, tm=512):
    m, d = x2d.shape
    dc = wq.shape[1]
    per_batch = s_len // tm
    full = lambda shape: pl.BlockSpec(shape, lambda i: (0, 0))
    return pl.pallas_call(
        _cross_kernel,
        grid=(m // tm,),
        in_specs=[pl.BlockSpec((tm, d), lambda i: (i, 0)), full((1, d)), full((d, dc)), full((1, CROSS_HEAD_DIM)),
                  pl.BlockSpec((n_mem, dc), lambda i: (i // per_batch, 0)),
                  pl.BlockSpec((n_mem, dc), lambda i: (i // per_batch, 1)),
                  full((dc, d))],
        out_specs=pl.BlockSpec((tm, d), lambda i: (i, 0)),
        out_shape=jax.ShapeDtypeStruct((m, d), F32),
        scratch_shapes=[pltpu.VMEM((tm, d), BF16), pltpu.VMEM((tm, dc), F32), pltpu.VMEM((tm, dc), BF16),
                        pltpu.VMEM((2 * CROSS_GROUP, CROSS_ROWS, n_mem), F32),
                        pltpu.VMEM((2 * CROSS_GROUP, CROSS_ROWS, n_mem), BF16)],
        compiler_params=_cparams(("parallel",)),
        name="cross_attn",
    )(x2d, g, wq, gq, kv, kv, wo)


def _mlp_kernel(x_ref, g_ref, wu_ref, wd_ref, o_ref, h_ref):
    @pl.when(pl.program_id(1) == 0)
    def _():
        _norm_rows_to(x_ref, g_ref, h_ref)
        o_ref[...] = x_ref[...]
    u = jnp.dot(h_ref[...], wu_ref[...], preferred_element_type=F32)
    u = jnp.square(jnp.maximum(u, 0.0)).astype(BF16)
    o_ref[...] += jnp.dot(u, wd_ref[...], preferred_element_type=F32)


def mlp(x2d, g, wu, wd, tm=512, tf=1024):
    m, d = x2d.shape
    f = wu.shape[1]
    return pl.pallas_call(
        _mlp_kernel,
        grid=(m // tm, f // tf),
        in_specs=[pl.BlockSpec((tm, d), lambda i, j: (i, 0)),
                  pl.BlockSpec((1, d), lambda i, j: (0, 0)),
                  pl.BlockSpec((d, tf), lambda i, j: (0, j)),
                  pl.BlockSpec((tf, d), lambda i, j: (j, 0))],
        out_specs=pl.BlockSpec((tm, d), lambda i, j: (i, 0)),
        out_shape=jax.ShapeDtypeStruct((m, d), F32),
        scratch_shapes=[pltpu.VMEM((tm, d), BF16)],
        compiler_params=_cparams(("parallel", "arbitrary")),
        name="mlp",
    )(x2d, g, wu, wd)


def _layer(x2d, mem2d, pos_col, bsz, s_len, n_mem, g_mix, w_in, g_q, g_k, g_attn_out, conv_w, conv_b, dt_bias,
           a_log, d_skip, g_ssm_out, w_out, g_cross, g_mem, w_cq, w_ckv, g_cq, g_ck, w_co, g_mlp, w_up, w_down):
    d_model = x2d.shape[1]
    d_attn = d_model // 2
    d_ssm = d_model // 2
    n_pairs = d_attn // LANES
    n_ssm_heads = d_ssm // SSM_HEAD_DIM
    d_conv = d_ssm + 2 * SSM_GROUPS * SSM_STATE
    d_packed = 3 * d_attn + d_ssm + d_conv
    row = lambda v: v.astype(F32)[None, :]

    w_dt = jnp.pad(w_in[:, d_packed:], ((0, 0), (0, LANES - n_ssm_heads))).astype(BF16)
    packed, dt_raw = in_proj(x2d, row(g_mix), w_in, w_dt, d_packed)

    tabs = rope_tables(pos_col)
    attn = dilated_attention(packed, tabs, row(jnp.tile(g_q, 2)), row(jnp.tile(g_k, 2)), bsz, s_len, n_pairs)
    ssm = ssd_mixer(packed, dt_raw, conv_w, conv_b, dt_bias, a_log, d_skip, g_ssm_out, bsz, s_len, d_ssm,
                    xbc_col=3 * d_attn + d_ssm, z_col=3 * d_attn)
    x2d = out_proj(attn, ssm, x2d, row(g_attn_out), w_out)

    kv = mem_kv(mem2d, row(g_mem), w_ckv.astype(BF16), row(g_ck))
    x2d = cross_attention(x2d, row(g_cross), w_cq.astype(BF16), row(g_cq), kv, w_co.astype(BF16), s_len, n_mem)

    return mlp(x2d, row(g_mlp), w_up.astype(BF16), w_down.astype(BF16))


def kernel(x, mem, positions, g_mix, w_in, g_q, g_k, g_attn_out, conv_w, conv_b, dt_bias, a_log, d_skip, g_ssm_out,
           w_out, g_cross, g_mem, w_cq, w_ckv, g_cq, g_ck, w_co, g_mlp, w_up, w_down):
    bsz, s_len, d_model = x.shape
    n_mem = mem.shape[1]
    x2d = x.reshape(bsz * s_len, d_model)
    mem2d = mem.reshape(bsz * n_mem, d_model)
    pos_col = positions.reshape(bsz * s_len, 1)
    for i in range(g_mix.shape[0]):
        x2d = _layer(x2d, mem2d, pos_col, bsz, s_len, n_mem, g_mix[i], w_in[i], g_q[i], g_k[i], g_attn_out[i],
                     conv_w[i], conv_b[i], dt_bias[i], a_log[i], d_skip[i], g_ssm_out[i], w_out[i], g_cross[i],
                     g_mem[i], w_cq[i], w_ckv[i], g_cq[i], g_ck[i], w_co[i], g_mlp[i], w_up[i], w_down[i])
    return x2d.reshape(bsz, s_len, d_model)
```

```python
import functools
import math

import jax
import jax.numpy as jnp
from jax import lax
from jax.experimental import pallas as pl
from jax.experimental.pallas import tpu as pltpu

F32 = jnp.float32
BF16 = jnp.bfloat16
EPS = 1e-6

LANES = 128
ATTN_HEAD_DIM = 64
ROT_DIM = ATTN_HEAD_DIM // 4
ROPE_THETA = 500000.0
ATTN_BLOCK = 128
DILATIONS = (1, 4, 16)
SSM_HEAD_DIM = 64
SSM_GROUPS = 4
SSM_STATE = 128
CONV_WIDTH = 4
SSD_CHUNK = 128
N_CROSS_HEADS = 4
CROSS_HEAD_DIM = 128
CROSS_ROWS = 256
CROSS_GROUP = 2
VMEM_LIMIT = 52 * 1024 * 1024


def _cparams(sem):
    return pltpu.CompilerParams(dimension_semantics=sem, vmem_limit_bytes=VMEM_LIMIT)


def _rms_rows(x, g):
    ms = jnp.mean(x * x, axis=-1, keepdims=True)
    return x * lax.rsqrt(ms + EPS) * g


def _norm_rows_to(x_ref, g_ref, h_ref, rows=16):
    def body(c, carry):
        r0 = pl.multiple_of(c * rows, rows)
        h_ref[pl.ds(r0, rows), :] = _rms_rows(x_ref[pl.ds(r0, rows), :], g_ref[...]).astype(h_ref.dtype)
        return carry
    lax.fori_loop(0, x_ref.shape[0] // rows, body, 0, unroll=4)


def _in_proj_kernel(x_ref, g_ref, w_ref, wdt_ref, o_ref, dt_ref, h_ref):
    @pl.when(pl.program_id(1) == 0)
    def _():
        _norm_rows_to(x_ref, g_ref, h_ref)
        dt_ref[...] = lax.dot_general(h_ref[...], wdt_ref[...].astype(BF16), (((1,), (1,)), ((), ())),
                                      preferred_element_type=F32)
    o_ref[...] = lax.dot_general(h_ref[...], w_ref[...].astype(BF16), (((1,), (1,)), ((), ())),
                                 preferred_element_type=F32).astype(o_ref.dtype)


def in_proj(x2d, g, w_t, w_dt, n, tm=1024, tn=512):
    m, d = x2d.shape
    return pl.pallas_call(
        _in_proj_kernel,
        grid=(m // tm, n // tn),
        in_specs=[pl.BlockSpec((tm, d), lambda i, j: (i, 0)),
                  pl.BlockSpec((1, d), lambda i, j: (0, 0)),
                  pl.BlockSpec((tn, d), lambda i, j: (j, 0)),
                  pl.BlockSpec((LANES, d), lambda i, j: (0, 0))],
        out_specs=[pl.BlockSpec((tm, tn), lambda i, j: (i, j)),
                   pl.BlockSpec((tm, LANES), lambda i, j: (i, 0))],
        out_shape=[jax.ShapeDtypeStruct((m, n), BF16), jax.ShapeDtypeStruct((m, LANES), F32)],
        scratch_shapes=[pltpu.VMEM((tm, d), BF16)],
        compiler_params=_cparams(("parallel", "arbitrary")),
        name="in_proj",
    )(x2d, g, w_t, w_dt)


def _rope_kernel(pos_ref, invf_ref, cmask_ref, amask_ref, bmask_ref, cos_ref, sa_ref, sb_ref):
    ang = pos_ref[...].astype(F32) * invf_ref[...]
    c = jnp.cos(ang)
    s = jnp.sin(ang)
    cm = cmask_ref[...]
    cos_ref[...] = c * cm + (1.0 - cm)
    sa_ref[...] = -s * amask_ref[...]
    sb_ref[...] = s * bmask_ref[...]


def rope_tables(pos_col, tm=512):
    m = pos_col.shape[0]
    half = ROT_DIM // 2
    inv_freq = ROPE_THETA ** (-2.0 * jnp.arange(half, dtype=F32) / ROT_DIM)
    d = jnp.arange(LANES) % ATTN_HEAD_DIM
    invf = inv_freq[d % half][None, :]
    cmask = (d < ROT_DIM).astype(F32)[None, :]
    amask = (d < half).astype(F32)[None, :]
    bmask = ((d >= half) & (d < ROT_DIM)).astype(F32)[None, :]
    row = pl.BlockSpec((1, LANES), lambda i: (0, 0))
    tab = pl.BlockSpec((tm, LANES), lambda i: (i, 0))
    return pl.pallas_call(
        _rope_kernel,
        grid=(m // tm,),
        in_specs=[pl.BlockSpec((tm, 1), lambda i: (i, 0)), row, row, row, row],
        out_specs=[tab, tab, tab],
        out_shape=[jax.ShapeDtypeStruct((m, LANES), F32)] * 3,
        compiler_params=_cparams(("parallel",)),
        name="rope_tab",
    )(pos_col, invf, cmask, amask, bmask)


def _rows(start, size, dil):
    return pl.ds(start, size) if dil == 1 else pl.ds(start, size, stride=dil)


ATTN_GROUP = 2


def _software_pipeline(stages, items, group):
    groups = [items[i:i + group] for i in range(0, len(items), group)]
    for t in range(len(groups) + len(stages) - 1):
        for lag, stage in enumerate(stages):
            if 0 <= t - lag < len(groups):
                for g, item in enumerate(groups[t - lag]):
                    stage(item, ((t - lag) % 2) * group + g)


def _attn_kernel(q_ref, k_ref, v_ref, cos_ref, sa_ref, sb_ref, gq_ref, gk_ref, seg_ref, o_ref,
                 qf, kf, vf, qd, kd, vd, ob, mb, lb, s_scr, p_scr):
    s_len = q_ref.shape[0]
    blk = ATTN_BLOCK
    half = ROT_DIM // 2
    lane = lax.broadcasted_iota(jnp.int32, (blk, LANES), 1)
    head0 = lane < ATTN_HEAD_DIM
    qi = lax.broadcasted_iota(jnp.int32, (2 * blk, 2 * blk), 0) % blk
    kj = lax.broadcasted_iota(jnp.int32, (2 * blk, 2 * blk), 1)
    band_mask = (kj >= qi) & (kj <= qi + blk)
    first_mask = (lax.broadcasted_iota(jnp.int32, (2 * blk, blk), 1)
                  <= lax.broadcasted_iota(jnp.int32, (2 * blk, blk), 0) % blk)

    prep_rows = 256

    def prep(c, carry):
        rows = pl.ds(pl.multiple_of(c * prep_rows, prep_rows), prep_rows)
        cs, sa, sb = cos_ref[rows, :], sa_ref[rows, :], sb_ref[rows, :]

        def norm_rope(x_ref, g_ref):
            x = x_ref[rows, :].astype(F32)
            ss = jnp.dot((x * x).astype(BF16), seg_ref[...], preferred_element_type=F32)
            y = x * lax.rsqrt(ss * (1.0 / ATTN_HEAD_DIM) + EPS) * g_ref[...]
            return y * cs + pltpu.roll(y, LANES - half, 1) * sa + pltpu.roll(y, half, 1) * sb

        qf[rows, :] = norm_rope(q_ref, gq_ref) * (ATTN_HEAD_DIM ** -0.5)
        kf[rows, :] = norm_rope(k_ref, gk_ref)
        vf[rows, :] = v_ref[rows, :].astype(F32)
        return carry

    lax.fori_loop(0, s_len // prep_rows, prep, 0, unroll=2)

    dmid = DILATIONS[1]
    run = s_len // dmid

    def deinterleave(r, carry):
        dst_rows = pl.ds(pl.multiple_of(r * run, run), run)
        for src, dst in ((qf, qd), (kf, kd), (vf, vd)):
            dst[dst_rows, :] = src[pl.ds(r, run, stride=dmid), :]
        return carry

    lax.fori_loop(0, dmid, deinterleave, 0)

    natural, deint = (qf, kf, vf), (qd, kd, vd)
    descs = [(natural, 0, 1, 0, 0, blk, first_mask)]
    descs += [(natural, 0, 1, n * blk, (n - 1) * blk, 2 * blk, band_mask) for n in range(1, s_len // blk)]
    descs += [(deint, 1, 1, r * run, r * run, blk, first_mask) for r in range(dmid)]
    descs += [(deint, 1, 1, r * run + n * blk, r * run + (n - 1) * blk, 2 * blk, band_mask)
              for n in range(1, run // blk) for r in range(dmid)]
    descs += [(deint, 2, dmid, r_hi * run + r_lo, r_hi * run + r_lo, blk, first_mask)
              for r_hi in range(dmid) for r_lo in range(dmid)]

    def scores(desc, slot):
        (q_src, k_src, _), _, dil, q_start, k_start, n_keys, _ = desc
        qt = q_src[_rows(q_start, blk, dil), :]
        zero = jnp.zeros_like(qt)
        q2 = jnp.concatenate([jnp.where(head0, qt, zero), jnp.where(head0, zero, qt)], axis=0).astype(BF16)
        kb = k_src[_rows(k_start, n_keys, dil), :].astype(BF16)
        s_scr[slot, :, :n_keys] = lax.dot_general(q2, kb, (((1,), (1,)), ((), ())), preferred_element_type=F32)

    def softmax(desc, slot):
        _, br, dil, q_start, _, n_keys, mask = desc
        s = jnp.where(mask, s_scr[slot, :, :n_keys], -jnp.inf)
        m = jnp.max(s, axis=-1, keepdims=True)
        p_scr[slot, :, :n_keys] = jnp.exp(s - m).astype(BF16)
        mb[br, _rows(q_start, blk, dil), :] = jnp.where(head0, m[:blk], m[blk:])

    def values(desc, slot):
        (_, _, v_src), br, dil, q_start, k_start, n_keys, _ = desc
        vb = v_src[_rows(k_start, n_keys, dil), :].astype(BF16)
        v1 = jnp.concatenate([vb, jnp.ones_like(vb)], axis=1)
        r = jnp.dot(p_scr[slot, :, :n_keys], v1, preferred_element_type=F32)
        out_rows = _rows(q_start, blk, dil)
        ob[br, out_rows, :] = jnp.where(head0, r[:blk, :LANES], r[blk:, :LANES])
        lb[br, out_rows, :] = jnp.where(head0, r[:blk, LANES:], r[blk:, LANES:])

    _software_pipeline((scores, softmax, values), descs, ATTN_GROUP)

    def merge(c, carry):
        d_start = pl.multiple_of(c * prep_rows, prep_rows)
        r = d_start // run
        nat_rows = pl.ds(dmid * (d_start - r * run) + r, prep_rows, stride=dmid)
        d_rows = pl.ds(d_start, prep_rows)
        rows = (nat_rows, d_rows, d_rows)
        ms = [mb[g, rows[g], :] for g in range(len(DILATIONS))]
        m = functools.reduce(jnp.maximum, ms)
        num = jnp.zeros((prep_rows, LANES), F32)
        den = jnp.zeros((prep_rows, LANES), F32)
        for g in range(len(DILATIONS)):
            w = jnp.exp(ms[g] - m)
            num = num + w * ob[g, rows[g], :]
            den = den + w * lb[g, rows[g], :]
        ob[0, nat_rows, :] = num / den
        return carry

    lax.fori_loop(0, s_len // prep_rows, merge, 0)

    def emit(c, carry):
        rows = pl.ds(pl.multiple_of(c * prep_rows, prep_rows), prep_rows)
        o_ref[rows, :] = ob[0, rows, :].astype(o_ref.dtype)
        return carry

    lax.fori_loop(0, s_len // prep_rows, emit, 0)


def dilated_attention(qkv, tabs, gq2, gk2, bsz, s_len, n_pairs):
    cos_t, sa_t, sb_t = tabs
    d1, d2, d3 = DILATIONS
    assert d1 == 1 and d3 == d2 * d2 and s_len == d3 * ATTN_BLOCK, "layout assumes dilations (1, d, d*d), one block per largest class"
    lane = jnp.arange(LANES)
    seg = (lane[:, None] // ATTN_HEAD_DIM == lane[None, :] // ATTN_HEAD_DIM).astype(BF16)
    blk = lambda off: pl.BlockSpec((s_len, LANES), lambda b, p, off=off: (b, off + p))
    tab = pl.BlockSpec((s_len, LANES), lambda b, p: (b, 0))
    row = pl.BlockSpec((1, LANES), lambda b, p: (0, 0))
    return pl.pallas_call(
        _attn_kernel,
        grid=(bsz, n_pairs),
        in_specs=[blk(0), blk(n_pairs), blk(2 * n_pairs), tab, tab, tab, row, row,
                  pl.BlockSpec((LANES, LANES), lambda b, p: (0, 0))],
        out_specs=pl.BlockSpec((s_len, LANES), lambda b, p: (b, p)),
        out_shape=jax.ShapeDtypeStruct((bsz * s_len, n_pairs * LANES), BF16),
        scratch_shapes=[pltpu.VMEM((s_len, LANES), F32)] * 6
                       + [pltpu.VMEM((len(DILATIONS), s_len, LANES), F32)] * 3
                       + [pltpu.VMEM((2 * ATTN_GROUP, 2 * ATTN_BLOCK, 2 * ATTN_BLOCK), F32),
                          pltpu.VMEM((2 * ATTN_GROUP, 2 * ATTN_BLOCK, 2 * ATTN_BLOCK), BF16)],
        compiler_params=_cparams(("parallel", "arbitrary")),
        name="dilated_attn",
    )(qkv, qkv, qkv, cos_t, sa_t, sb_t, gq2, gk2, seg)


def _split3(x):
    hi = x.astype(BF16)
    r1 = x - hi.astype(F32)
    mid = r1.astype(BF16)
    lo = (r1 - mid.astype(F32)).astype(BF16)
    return hi, mid, lo


def _silu(x):
    return x * (1.0 / (1.0 + jnp.exp(-x)))


def _ssd_kernel(xbc_ref, z_ref, dt_ref, cw_ref, cb_ref, dtb_ref, alog_ref, dskip_ref, g_ref,
                expand_ref, tril_ref, o_ref, xpad, xc, st):
    q = SSD_CHUNK
    d_ssm = z_ref.shape[1]
    d_conv = xbc_ref.shape[1]
    gw = d_ssm // SSM_GROUPS
    heads_per_group = gw // SSM_HEAD_DIM
    pad = 8

    @pl.when(pl.program_id(1) == 0)
    def _():
        xpad[:, 0:pad, :] = jnp.zeros((d_conv // LANES, pad, LANES), F32)
        st[...] = jnp.zeros_like(st)

    for c0 in range(0, d_conv, LANES):
        cols = slice(c0, c0 + LANES)
        slab = c0 // LANES
        xpad[slab, pad:pad + q, :] = xbc_ref[:, cols].astype(F32)
        acc = cb_ref[:, cols] + cw_ref[CONV_WIDTH - 1:CONV_WIDTH, cols] * xpad[slab, pad:pad + q, :]
        for w in range(CONV_WIDTH - 1):
            off = pad - (CONV_WIDTH - 1) + w
            acc = acc + cw_ref[w:w + 1, cols] * xpad[slab, pl.ds(off, q, stride=1), :]
        xc[:, cols] = _silu(acc)
    xpad[:, 0:pad, :] = xpad[:, q:q + pad, :]

    x_dt = dt_ref[...] + dtb_ref[...]
    dt = jnp.maximum(x_dt, 0.0) + jnp.log1p(jnp.exp(-jnp.abs(x_dt)))
    dta = dt * (-jnp.exp(alog_ref[...]))
    tril = tril_ref[...]
    acs = sum(jnp.dot(tril, part, preferred_element_type=F32) for part in _split3(dta))
    acs_t = acs.T
    last = acs[q - 1:q, :]
    expand = expand_ref[...]
    dt_e = jnp.dot(dt.astype(BF16), expand, preferred_element_type=F32)
    dec_e = jnp.dot(jnp.exp(acs).astype(BF16), expand, preferred_element_type=F32)
    w_e = jnp.dot((jnp.exp(last - acs) * dt).astype(BF16), expand, preferred_element_type=F32)

    li = lax.broadcasted_iota(jnp.int32, (q, q), 0)
    si = lax.broadcasted_iota(jnp.int32, (q, q), 1)
    causal = li >= si
    lane_g = lax.broadcasted_iota(jnp.int32, (q, gw), 1)

    for g in range(SSM_GROUPS):
        cols = slice(g * gw, (g + 1) * gw)
        b_f = xc[:, d_ssm + g * SSM_STATE:d_ssm + (g + 1) * SSM_STATE]
        c_b = xc[:, d_ssm + (SSM_GROUPS + g) * SSM_STATE:d_ssm + (SSM_GROUPS + g + 1) * SSM_STATE].astype(BF16)
        xs = xc[:, cols]
        cb = lax.dot_general(c_b, b_f.astype(BF16), (((1,), (1,)), ((), ())), preferred_element_type=F32)
        xdt = (xs * dt_e[:, cols]).astype(BF16)
        ws, rs = [], []
        for hh in range(heads_per_group):
            h = g * heads_per_group + hh
            seg = acs[:, h:h + 1] - acs_t[h:h + 1, :]
            l_mat = jnp.exp(jnp.where(causal, seg, -jnp.inf))
            ws.append((cb * l_mat).astype(BF16))
            in_head = (lane_g >= hh * SSM_HEAD_DIM) & (lane_g < (hh + 1) * SSM_HEAD_DIM)
            rs.append(jnp.where(in_head, xdt, jnp.zeros_like(xdt)))
        y = jnp.dot(jnp.concatenate(ws, axis=1), jnp.concatenate(rs, axis=0), preferred_element_type=F32)
        st_g = st[:, cols]
        y = y + jnp.dot(c_b, st_g.astype(BF16), preferred_element_type=F32) * dec_e[:, cols]
        xw = (xs * w_e[:, cols]).astype(BF16)
        st[:, cols] = st_g * dec_e[q - 1:q, cols] + jnp.dot(b_f.T.astype(BF16), xw, preferred_element_type=F32)
        y = y + dskip_ref[:, cols] * xs
        y = y * _silu(z_ref[:, cols].astype(F32))
        o_ref[:, cols] = _rms_rows(y, g_ref[:, cols]).astype(o_ref.dtype)


def ssd_mixer(packed, dt_raw, conv_w, conv_b, dt_bias, a_log, d_skip, g_out, bsz, s_len, d_ssm, xbc_col, z_col):
    n_heads = d_ssm // SSM_HEAD_DIM
    d_conv = d_ssm + 2 * SSM_GROUPS * SSM_STATE
    nc = s_len // SSD_CHUNK
    padl = lambda v: jnp.pad(v.astype(F32), (0, LANES - n_heads))[None, :]
    expand = (jnp.arange(LANES)[:, None] == (jnp.arange(d_ssm) // SSM_HEAD_DIM)[None, :]).astype(BF16)
    tril = (jnp.arange(SSD_CHUNK)[:, None] >= jnp.arange(SSD_CHUNK)[None, :]).astype(BF16)
    full = lambda shape: pl.BlockSpec(shape, lambda b, c: (0, 0))
    return pl.pallas_call(
        _ssd_kernel,
        grid=(bsz, nc),
        in_specs=[pl.BlockSpec((SSD_CHUNK, d_conv), lambda b, c: (b * nc + c, xbc_col // d_conv)),
                  pl.BlockSpec((SSD_CHUNK, d_ssm), lambda b, c: (b * nc + c, z_col // d_ssm)),
                  pl.BlockSpec((SSD_CHUNK, LANES), lambda b, c: (b * nc + c, 0)),
                  full((CONV_WIDTH, d_conv)), full((1, d_conv)), full((1, LANES)), full((1, LANES)),
                  full((1, d_ssm)), full((1, d_ssm)), full((LANES, d_ssm)), full((SSD_CHUNK, SSD_CHUNK))],
        out_specs=pl.BlockSpec((SSD_CHUNK, d_ssm), lambda b, c: (b * nc + c, 0)),
        out_shape=jax.ShapeDtypeStruct((bsz * s_len, d_ssm), BF16),
        scratch_shapes=[pltpu.VMEM((d_conv // LANES, SSD_CHUNK + 8, LANES), F32),
                        pltpu.VMEM((SSD_CHUNK, d_conv), F32),
                        pltpu.VMEM((SSM_STATE, d_ssm), F32)],
        compiler_params=_cparams(("parallel", "arbitrary")),
        name="ssd",
    )(packed, packed, dt_raw, conv_w.astype(F32), conv_b.astype(F32)[None, :], padl(dt_bias), padl(a_log),
      jnp.repeat(d_skip.astype(F32), SSM_HEAD_DIM)[None, :], g_out.astype(F32)[None, :], expand, tril)


def _out_proj_kernel(attn_ref, ssm_ref, x_ref, g_ref, w_ref, o_ref, mix_ref):
    d_attn = attn_ref.shape[1]

    @pl.when(pl.program_id(1) == 0)
    def _():
        rows = 32

        def body(c, carry):
            rs = pl.ds(pl.multiple_of(c * rows, rows), rows)
            mix_ref[rs, 0:d_attn] = _rms_rows(attn_ref[rs, :].astype(F32), g_ref[...]).astype(BF16)
            return carry
        lax.fori_loop(0, attn_ref.shape[0] // rows, body, 0, unroll=4)
        mix_ref[:, d_attn:] = ssm_ref[...]
    o_ref[...] = x_ref[...] + jnp.dot(mix_ref[...], w_ref[...].astype(BF16), preferred_element_type=F32)


def out_proj(attn, ssm, x2d, g, w, tm=2048, tn=256):
    m, d = x2d.shape
    da, ds_ = attn.shape[1], ssm.shape[1]
    return pl.pallas_call(
        _out_proj_kernel,
        grid=(m // tm, d // tn),
        in_specs=[pl.BlockSpec((tm, da), lambda i, j: (i, 0)),
                  pl.BlockSpec((tm, ds_), lambda i, j: (i, 0)),
                  pl.BlockSpec((tm, tn), lambda i, j: (i, j)),
                  pl.BlockSpec((1, da), lambda i, j: (0, 0)),
                  pl.BlockSpec((da + ds_, tn), lambda i, j: (0, j))],
        out_specs=pl.BlockSpec((tm, tn), lambda i, j: (i, j)),
        out_shape=jax.ShapeDtypeStruct((m, d), F32),
        scratch_shapes=[pltpu.VMEM((tm, da + ds_), BF16)],
        compiler_params=_cparams(("parallel", "arbitrary")),
        name="out_proj",
    )(attn, ssm, x2d, g, w)


def _mem_kv_kernel(mem_ref, g_ref, w_ref, gk_ref, kv_ref, h_ref):
    d_cross = N_CROSS_HEADS * CROSS_HEAD_DIM
    _norm_rows_to(mem_ref, g_ref, h_ref)
    kv = jnp.dot(h_ref[...], w_ref[...], preferred_element_type=F32)
    for h in range(N_CROSS_HEADS):
        cols = slice(h * CROSS_HEAD_DIM, (h + 1) * CROSS_HEAD_DIM)
        kv_ref[:, cols] = _rms_rows(kv[:, cols], gk_ref[...]).astype(kv_ref.dtype)
    kv_ref[:, d_cross:] = kv[:, d_cross:].astype(kv_ref.dtype)


def mem_kv(mem2d, g, w, gk, tm=256):
    m, d = mem2d.shape
    n = w.shape[1]
    return pl.pallas_call(
        _mem_kv_kernel,
        grid=(m // tm,),
        in_specs=[pl.BlockSpec((tm, d), lambda i: (i, 0)),
                  pl.BlockSpec((1, d), lambda i: (0, 0)),
                  pl.BlockSpec((d, n), lambda i: (0, 0)),
                  pl.BlockSpec((1, CROSS_HEAD_DIM), lambda i: (0, 0))],
        out_specs=pl.BlockSpec((tm, n), lambda i: (i, 0)),
        out_shape=jax.ShapeDtypeStruct((m, n), BF16),
        scratch_shapes=[pltpu.VMEM((tm, d), BF16)],
        compiler_params=_cparams(("parallel",)),
        name="mem_kv",
    )(mem2d, g, w, gk)


def _cross_kernel(x_ref, g_ref, wq_ref, gq_ref, k_ref, v_ref, wo_ref, o_ref, h_ref, q_ref, a_ref, s_scr, p_scr):
    tm = x_ref.shape[0]
    rows = CROSS_ROWS
    _norm_rows_to(x_ref, g_ref, h_ref)
    q_ref[...] = jnp.dot(h_ref[...], wq_ref[...], preferred_element_type=F32)

    def scores(item, slot):
        r0, h = item
        cols = slice(h * CROSS_HEAD_DIM, (h + 1) * CROSS_HEAD_DIM)
        qh = (_rms_rows(q_ref[r0:r0 + rows, cols], gq_ref[...]) * (CROSS_HEAD_DIM ** -0.5)).astype(BF16)
        s_scr[slot] = lax.dot_general(qh, k_ref[:, cols], (((1,), (1,)), ((), ())), preferred_element_type=F32)

    def softmax(item, slot):
        s = s_scr[slot]
        p_scr[slot] = jnp.exp(s - jnp.max(s, axis=-1, keepdims=True)).astype(BF16)

    def values(item, slot):
        r0, h = item
        cols = slice(h * CROSS_HEAD_DIM, (h + 1) * CROSS_HEAD_DIM)
        vh = v_ref[:, cols]
        r = jnp.dot(p_scr[slot], jnp.concatenate([vh, jnp.ones_like(vh)], axis=1), preferred_element_type=F32)
        a_ref[r0:r0 + rows, cols] = (r[:, :CROSS_HEAD_DIM] / r[:, CROSS_HEAD_DIM:]).astype(BF16)

    items = [(r0, h) for r0 in range(0, tm, rows) for h in range(N_CROSS_HEADS)]
    _software_pipeline((scores, softmax, values), items, CROSS_GROUP)
    o_ref[...] = x_ref[...] + jnp.dot(a_ref[...], wo_ref[...], preferred_element_type=F32)


def cross_attention(x2d, g, wq, gq, kv, wo, s_len, n_mem, tm=512):
    m, d = x2d.shape
    dc = wq.shape[1]
    per_batch = s_len // tm
    full = lambda shape: pl.BlockSpec(shape, lambda i: (0, 0))
    return pl.pallas_call(
        _cross_kernel,
        grid=(m // tm,),
        in_specs=[pl.BlockSpec((tm, d), lambda i: (i, 0)), full((1, d)), full((d, dc)), full((1, CROSS_HEAD_DIM)),
                  pl.BlockSpec((n_mem, dc), lambda i: (i // per_batch, 0)),
                  pl.BlockSpec((n_mem, dc), lambda i: (i // per_batch, 1)),
                  full((dc, d))],
        out_specs=pl.BlockSpec((tm, d), lambda i: (i, 0)),
        out_shape=jax.ShapeDtypeStruct((m, d), F32),
        scratch_shapes=[pltpu.VMEM((tm, d), BF16), pltpu.VMEM((tm, dc), F32), pltpu.VMEM((tm, dc), BF16),
                        pltpu.VMEM((2 * CROSS_GROUP, CROSS_ROWS, n_mem), F32),
                        pltpu.VMEM((2 * CROSS_GROUP, CROSS_ROWS, n_mem), BF16)],
        compiler_params=_cparams(("parallel",)),
        name="cross_attn",
    )(x2d, g, wq, gq, kv, kv, wo)


def _mlp_kernel(x_ref, g_ref, wu_ref, wd_ref, o_ref, h_ref):
    @pl.when(pl.program_id(1) == 0)
    def _():
        _norm_rows_to(x_ref, g_ref, h_ref)
        o_ref[...] = x_ref[...]
    u = jnp.dot(h_ref[...], wu_ref[...], preferred_element_type=F32)
    u = jnp.square(jnp.maximum(u, 0.0)).astype(BF16)
    o_ref[...] += jnp.dot(u, wd_ref[...], preferred_element_type=F32)


def mlp(x2d, g, wu, wd, tm=512, tf=1024):
    m, d = x2d.shape
    f = wu.shape[1]
    return pl.pallas_call(
        _mlp_kernel,
        grid=(m // tm, f // tf),
        in_specs=[pl.BlockSpec((tm, d), lambda i, j: (i, 0)),
                  pl.BlockSpec((1, d), lambda i, j: (0, 0)),
                  pl.BlockSpec((d, tf), lambda i, j: (0, j)),
                  pl.BlockSpec((tf, d), lambda i, j: (j, 0))],
        out_specs=pl.BlockSpec((tm, d), lambda i, j: (i, 0)),
        out_shape=jax.ShapeDtypeStruct((m, d), F32),
        scratch_shapes=[pltpu.VMEM((tm, d), BF16)],
        compiler_params=_cparams(("parallel", "arbitrary")),
        name="mlp",
    )(x2d, g, wu, wd)


def _layer(x2d, mem2d, pos_col, bsz, s_len, n_mem, g_mix, w_in, g_q, g_k, g_attn_out, conv_w, conv_b, dt_bias,
           a_log, d_skip, g_ssm_out, w_out, g_cross, g_mem, w_cq, w_ckv, g_cq, g_ck, w_co, g_mlp, w_up, w_down):
    d_model = x2d.shape[1]
    d_attn = d_model // 2
    d_ssm = d_model // 2
    n_pairs = d_attn // LANES
    n_ssm_heads = d_ssm // SSM_HEAD_DIM
    d_conv = d_ssm + 2 * SSM_GROUPS * SSM_STATE
    d_packed = 3 * d_attn + d_ssm + d_conv
    row = lambda v: v.astype(F32)[None, :]

    w_in_t = w_in.T
    w_dt_t = jnp.pad(w_in_t[d_packed:], ((0, LANES - n_ssm_heads), (0, 0)))
    packed, dt_raw = in_proj(x2d, row(g_mix), w_in_t, w_dt_t, d_packed)

    tabs = rope_tables(pos_col)
    attn = dilated_attention(packed, tabs, row(jnp.tile(g_q, 2)), row(jnp.tile(g_k, 2)), bsz, s_len, n_pairs)
    ssm = ssd_mixer(packed, dt_raw, conv_w, conv_b, dt_bias, a_log, d_skip, g_ssm_out, bsz, s_len, d_ssm,
                    xbc_col=3 * d_attn + d_ssm, z_col=3 * d_attn)
    x2d = out_proj(attn, ssm, x2d, row(g_attn_out), w_out)

    kv = mem_kv(mem2d, row(g_mem), w_ckv.astype(BF16), row(g_ck))
    x2d = cross_attention(x2d, row(g_cross), w_cq.astype(BF16), row(g_cq), kv, w_co.astype(BF16), s_len, n_mem)

    return mlp(x2d, row(g_mlp), w_up.astype(BF16), w_down.astype(BF16))


def kernel(x, mem, positions, g_mix, w_in, g_q, g_k, g_attn_out, conv_w, conv_b, dt_bias, a_log, d_skip, g_ssm_out,
           w_out, g_cross, g_mem, w_cq, w_ckv, g_cq, g_ck, w_co, g_mlp, w_up, w_down):
    bsz, s_len, d_model = x.shape
    n_mem = mem.shape[1]
    x2d = x.reshape(bsz * s_len, d_model)
    mem2d = mem.reshape(bsz * n_mem, d_model)
    pos_col = positions.reshape(bsz * s_len, 1)
    for i in range(g_mix.shape[0]):
        x2d = _layer(x2d, mem2d, pos_col, bsz, s_len, n_mem, g_mix[i], w_in[i], g_q[i], g_k[i], g_attn_out[i],
                     conv_w[i], conv_b[i], dt_bias[i], a_log[i], d_skip[i], g_ssm_out[i], w_out[i], g_cross[i],
                     g_mem[i], w_cq[i], w_ckv[i], g_cq[i], g_ck[i], w_co[i], g_mlp[i], w_up[i], w_down[i])
    return x2d.reshape(bsz, s_len, d_model)
```

```python
import functools
import math

import jax
import jax.numpy as jnp
from jax import lax
from jax.experimental import pallas as pl
from jax.experimental.pallas import tpu as pltpu

F32 = jnp.float32
BF16 = jnp.bfloat16
EPS = 1e-6

LANES = 128
ATTN_HEAD_DIM = 64
ROT_DIM = ATTN_HEAD_DIM // 4
ROPE_THETA = 500000.0
ATTN_BLOCK = 128
DILATIONS = (1, 4, 16)
SSM_HEAD_DIM = 64
SSM_GROUPS = 4
SSM_STATE = 128
CONV_WIDTH = 4
SSD_CHUNK = 128
N_CROSS_HEADS = 4
CROSS_HEAD_DIM = 128
CROSS_ROWS = 256
CROSS_GROUP = 2
VMEM_LIMIT = 52 * 1024 * 1024


def _cparams(sem):
    return pltpu.CompilerParams(dimension_semantics=sem, vmem_limit_bytes=VMEM_LIMIT)


def _rms_rows(x, g):
    ms = jnp.mean(x * x, axis=-1, keepdims=True)
    return x * lax.rsqrt(ms + EPS) * g


def _norm_rows_to(x_ref, g_ref, h_ref, rows=16):
    def body(c, carry):
        r0 = pl.multiple_of(c * rows, rows)
        h_ref[pl.ds(r0, rows), :] = _rms_rows(x_ref[pl.ds(r0, rows), :], g_ref[...]).astype(h_ref.dtype)
        return carry
    lax.fori_loop(0, x_ref.shape[0] // rows, body, 0, unroll=4)


def _in_proj_kernel(x_ref, g_ref, w_ref, wdt_ref, o_ref, dt_ref, h_ref):
    @pl.when(pl.program_id(1) == 0)
    def _():
        _norm_rows_to(x_ref, g_ref, h_ref)
        dt_ref[...] = lax.dot_general(h_ref[...], wdt_ref[...].astype(BF16), (((1,), (1,)), ((), ())),
                                      preferred_element_type=F32)
    o_ref[...] = lax.dot_general(h_ref[...], w_ref[...].astype(BF16), (((1,), (1,)), ((), ())),
                                 preferred_element_type=F32).astype(o_ref.dtype)


def in_proj(x2d, g, w_t, w_dt, n, tm=1024, tn=512):
    m, d = x2d.shape
    return pl.pallas_call(
        _in_proj_kernel,
        grid=(m // tm, n // tn),
        in_specs=[pl.BlockSpec((tm, d), lambda i, j: (i, 0)),
                  pl.BlockSpec((1, d), lambda i, j: (0, 0)),
                  pl.BlockSpec((tn, d), lambda i, j: (j, 0)),
                  pl.BlockSpec((LANES, d), lambda i, j: (0, 0))],
        out_specs=[pl.BlockSpec((tm, tn), lambda i, j: (i, j)),
                   pl.BlockSpec((tm, LANES), lambda i, j: (i, 0))],
        out_shape=[jax.ShapeDtypeStruct((m, n), BF16), jax.ShapeDtypeStruct((m, LANES), F32)],
        scratch_shapes=[pltpu.VMEM((tm, d), BF16)],
        compiler_params=_cparams(("parallel", "arbitrary")),
        name="in_proj",
    )(x2d, g, w_t, w_dt)


def _rope_kernel(pos_ref, invf_ref, cmask_ref, cos_ref, sin_ref):
    ang = pos_ref[...].astype(F32) * invf_ref[...]
    cm = cmask_ref[...]
    cos_ref[...] = jnp.cos(ang) * cm + (1.0 - cm)
    sin_ref[...] = jnp.sin(ang) * cm


def rope_tables(pos_col, tm=512):
    m = pos_col.shape[0]
    half = ROT_DIM // 2
    inv_freq = ROPE_THETA ** (-2.0 * jnp.arange(half, dtype=F32) / ROT_DIM)
    d = jnp.arange(LANES) % ATTN_HEAD_DIM
    invf = inv_freq[d % half][None, :]
    cmask = (d < ROT_DIM).astype(F32)[None, :]
    row = pl.BlockSpec((1, LANES), lambda i: (0, 0))
    tab = pl.BlockSpec((tm, LANES), lambda i: (i, 0))
    return pl.pallas_call(
        _rope_kernel,
        grid=(m // tm,),
        in_specs=[pl.BlockSpec((tm, 1), lambda i: (i, 0)), row, row],
        out_specs=[tab, tab],
        out_shape=[jax.ShapeDtypeStruct((m, LANES), F32)] * 2,
        compiler_params=_cparams(("parallel",)),
        name="rope_tab",
    )(pos_col, invf, cmask)


def _rotate_half_matrix():
    half = ROT_DIM // 2
    src = jnp.arange(LANES)[:, None]
    dst = jnp.arange(LANES)[None, :]
    d = dst % ATTN_HEAD_DIM
    first = (d < half) & (src == dst + half)
    second = (d >= half) & (d < ROT_DIM) & (src == dst - half)
    return (second.astype(F32) - first.astype(F32)).astype(BF16)


def _rows(start, size, dil):
    return pl.ds(start, size) if dil == 1 else pl.ds(start, size, stride=dil)


ATTN_GROUP = 2


def _software_pipeline(stages, items, group):
    groups = [items[i:i + group] for i in range(0, len(items), group)]
    for t in range(len(groups) + len(stages) - 1):
        for lag, stage in enumerate(stages):
            if 0 <= t - lag < len(groups):
                for g, item in enumerate(groups[t - lag]):
                    stage(item, ((t - lag) % 2) * group + g)


def _attn_kernel(q_ref, k_ref, v_ref, cos_ref, sin_ref, gq_ref, gk_ref, seg_ref, rot_ref, wu_ref, wd_ref,
                 o_ref, wu_out, wd_out, qf, kf, vf, qd, kd, vd, ob, mb, lb, s_scr, p_scr):
    s_len = q_ref.shape[0]
    blk = ATTN_BLOCK
    lane = lax.broadcasted_iota(jnp.int32, (blk, LANES), 1)
    head0 = lane < ATTN_HEAD_DIM
    qi = lax.broadcasted_iota(jnp.int32, (2 * blk, 2 * blk), 0) % blk
    kj = lax.broadcasted_iota(jnp.int32, (2 * blk, 2 * blk), 1)
    band_mask = (kj >= qi) & (kj <= qi + blk)
    first_mask = (lax.broadcasted_iota(jnp.int32, (2 * blk, blk), 1)
                  <= lax.broadcasted_iota(jnp.int32, (2 * blk, blk), 0) % blk)

    wu_out[...] = wu_ref[...].astype(BF16)
    wd_out[...] = wd_ref[...].astype(BF16)

    prep_rows = 256

    def prep(c, carry):
        rows = pl.ds(pl.multiple_of(c * prep_rows, prep_rows), prep_rows)
        cs, sn = cos_ref[rows, :], sin_ref[rows, :]

        def norm_rope(x_ref, g_ref):
            x = x_ref[rows, :].astype(F32)
            ss = jnp.dot((x * x).astype(BF16), seg_ref[...], preferred_element_type=F32)
            y = x * lax.rsqrt(ss * (1.0 / ATTN_HEAD_DIM) + EPS) * g_ref[...]
            return y * cs + jnp.dot(y.astype(BF16), rot_ref[...], preferred_element_type=F32) * sn

        qf[rows, :] = norm_rope(q_ref, gq_ref)
        kf[rows, :] = norm_rope(k_ref, gk_ref)
        vf[rows, :] = v_ref[rows, :].astype(F32)
        return carry

    lax.fori_loop(0, s_len // prep_rows, prep, 0, unroll=2)

    dmid = DILATIONS[1]
    run = s_len // dmid

    def deinterleave(r, carry):
        dst_rows = pl.ds(pl.multiple_of(r * run, run), run)
        for src, dst in ((qf, qd), (kf, kd), (vf, vd)):
            dst[dst_rows, :] = src[pl.ds(r, run, stride=dmid), :]
        return carry

    lax.fori_loop(0, dmid, deinterleave, 0)

    natural, deint = (qf, kf, vf), (qd, kd, vd)
    descs = [(natural, 0, 1, 0, 0, blk, first_mask)]
    descs += [(natural, 0, 1, n * blk, (n - 1) * blk, 2 * blk, band_mask) for n in range(1, s_len // blk)]
    descs += [(deint, 1, 1, r * run, r * run, blk, first_mask) for r in range(dmid)]
    descs += [(deint, 1, 1, r * run + n * blk, r * run + (n - 1) * blk, 2 * blk, band_mask)
              for n in range(1, run // blk) for r in range(dmid)]
    descs += [(deint, 2, dmid, r_hi * run + r_lo, r_hi * run + r_lo, blk, first_mask)
              for r_hi in range(dmid) for r_lo in range(dmid)]

    def scores(desc, slot):
        (q_src, k_src, _), _, dil, q_start, k_start, n_keys, _ = desc
        qt = q_src[_rows(q_start, blk, dil), :]
        zero = jnp.zeros_like(qt)
        q2 = jnp.concatenate([jnp.where(head0, qt, zero), jnp.where(head0, zero, qt)], axis=0).astype(BF16)
        kb = k_src[_rows(k_start, n_keys, dil), :].astype(BF16)
        s_scr[slot, :, :n_keys] = lax.dot_general(q2, kb, (((1,), (1,)), ((), ())), preferred_element_type=F32)

    def softmax(desc, slot):
        _, br, dil, q_start, _, n_keys, mask = desc
        s = jnp.where(mask, s_scr[slot, :, :n_keys], -jnp.inf)
        m = jnp.max(s, axis=-1, keepdims=True)
        p_scr[slot, :, :n_keys] = jnp.exp(s - m).astype(BF16)
        mb[br, _rows(q_start, blk, dil), :] = jnp.where(head0, m[:blk], m[blk:])

    def values(desc, slot):
        (_, _, v_src), br, dil, q_start, k_start, n_keys, _ = desc
        vb = v_src[_rows(k_start, n_keys, dil), :].astype(BF16)
        v1 = jnp.concatenate([vb, jnp.ones_like(vb)], axis=1)
        r = jnp.dot(p_scr[slot, :, :n_keys], v1, preferred_element_type=F32)
        out_rows = _rows(q_start, blk, dil)
        ob[br, out_rows, :] = jnp.where(head0, r[:blk, :LANES], r[blk:, :LANES])
        lb[br, out_rows, :] = jnp.where(head0, r[:blk, LANES:], r[blk:, LANES:])

    _software_pipeline((scores, softmax, values), descs, ATTN_GROUP)

    def merge(c, carry):
        d_start = pl.multiple_of(c * prep_rows, prep_rows)
        r = d_start // run
        nat_rows = pl.ds(dmid * (d_start - r * run) + r, prep_rows, stride=dmid)
        d_rows = pl.ds(d_start, prep_rows)
        rows = (nat_rows, d_rows, d_rows)
        ms = [mb[g, rows[g], :] for g in range(len(DILATIONS))]
        m = functools.reduce(jnp.maximum, ms)
        ws = [jnp.exp(mg - m) for mg in ms]
        num = functools.reduce(jnp.add, [w * ob[g, rows[g], :] for g, w in enumerate(ws)])
        den = functools.reduce(jnp.add, [w * lb[g, rows[g], :] for g, w in enumerate(ws)])
        ob[0, nat_rows, :] = num / den
        return carry

    lax.fori_loop(0, s_len // prep_rows, merge, 0)

    def emit(c, carry):
        rows = pl.ds(pl.multiple_of(c * prep_rows, prep_rows), prep_rows)
        o_ref[rows, :] = ob[0, rows, :].astype(o_ref.dtype)
        return carry

    lax.fori_loop(0, s_len // prep_rows, emit, 0)


def dilated_attention(qkv, tabs, gq2, gk2, w_up, w_down, bsz, s_len, n_pairs):
    cos_t, sin_t = tabs
    d1, d2, d3 = DILATIONS
    assert d1 == 1 and d3 == d2 * d2 and s_len == d3 * ATTN_BLOCK, "layout assumes dilations (1, d, d*d), one block per largest class"
    lane = jnp.arange(LANES)
    seg = (lane[:, None] // ATTN_HEAD_DIM == lane[None, :] // ATTN_HEAD_DIM).astype(BF16)
    n_steps = bsz * n_pairs
    wu_rows, wd_rows = w_up.shape[0] // n_steps, w_down.shape[0] // n_steps
    blk = lambda off: pl.BlockSpec((s_len, LANES), lambda b, p, off=off: (b, off + p))
    tab = pl.BlockSpec((s_len, LANES), lambda b, p: (b, 0))
    row = pl.BlockSpec((1, LANES), lambda b, p: (0, 0))
    sq = pl.BlockSpec((LANES, LANES), lambda b, p: (0, 0))
    wu_spec = pl.BlockSpec((wu_rows, w_up.shape[1]), lambda b, p: (b * n_pairs + p, 0))
    wd_spec = pl.BlockSpec((wd_rows, w_down.shape[1]), lambda b, p: (b * n_pairs + p, 0))
    return pl.pallas_call(
        _attn_kernel,
        grid=(bsz, n_pairs),
        in_specs=[blk(0), blk(n_pairs), blk(2 * n_pairs), tab, tab, row, row, sq, sq, wu_spec, wd_spec],
        out_specs=[pl.BlockSpec((s_len, LANES), lambda b, p: (b, p)), wu_spec, wd_spec],
        out_shape=[jax.ShapeDtypeStruct((bsz * s_len, n_pairs * LANES), BF16),
                   jax.ShapeDtypeStruct(w_up.shape, BF16), jax.ShapeDtypeStruct(w_down.shape, BF16)],
        scratch_shapes=[pltpu.VMEM((s_len, LANES), F32)] * 6
                       + [pltpu.VMEM((len(DILATIONS), s_len, LANES), F32)] * 3
                       + [pltpu.VMEM((2 * ATTN_GROUP, 2 * ATTN_BLOCK, 2 * ATTN_BLOCK), F32),
                          pltpu.VMEM((2 * ATTN_GROUP, 2 * ATTN_BLOCK, 2 * ATTN_BLOCK), BF16)],
        compiler_params=_cparams(("parallel", "arbitrary")),
        name="dilated_attn",
    )(qkv, qkv, qkv, cos_t, sin_t, gq2, gk2, seg, _rotate_half_matrix(), w_up, w_down)


def _split3(x):
    hi = x.astype(BF16)
    r1 = x - hi.astype(F32)
    mid = r1.astype(BF16)
    lo = (r1 - mid.astype(F32)).astype(BF16)
    return hi, mid, lo


def _silu(x):
    return x * (1.0 / (1.0 + jnp.exp(-x)))


def _ssd_kernel(xbc_ref, z_ref, dt_ref, cw_ref, cb_ref, dtb_ref, alog_ref, dskip_ref, g_ref,
                expand_ref, tril_ref, o_ref, xpad, xc, st):
    q = SSD_CHUNK
    d_ssm = z_ref.shape[1]
    d_conv = xbc_ref.shape[1]
    gw = d_ssm // SSM_GROUPS
    heads_per_group = gw // SSM_HEAD_DIM
    pad = 8

    @pl.when(pl.program_id(1) == 0)
    def _():
        xpad[:, 0:pad, :] = jnp.zeros((d_conv // LANES, pad, LANES), F32)
        st[...] = jnp.zeros_like(st)

    for c0 in range(0, d_conv, LANES):
        cols = slice(c0, c0 + LANES)
        slab = c0 // LANES
        xpad[slab, pad:pad + q, :] = xbc_ref[:, cols].astype(F32)
        acc = cb_ref[:, cols] + cw_ref[CONV_WIDTH - 1:CONV_WIDTH, cols] * xpad[slab, pad:pad + q, :]
        for w in range(CONV_WIDTH - 1):
            off = pad - (CONV_WIDTH - 1) + w
            acc = acc + cw_ref[w:w + 1, cols] * xpad[slab, pl.ds(off, q, stride=1), :]
        xc[:, cols] = _silu(acc)
    xpad[:, 0:pad, :] = xpad[:, q:q + pad, :]

    x_dt = dt_ref[...] + dtb_ref[...]
    dt = jnp.maximum(x_dt, 0.0) + jnp.log1p(jnp.exp(-jnp.abs(x_dt)))
    dta = dt * (-jnp.exp(alog_ref[...]))
    tril = tril_ref[...]
    acs = sum(jnp.dot(tril, part, preferred_element_type=F32) for part in _split3(dta))
    acs_t = acs.T
    last = acs[q - 1:q, :]
    expand = expand_ref[...]
    dt_e = jnp.dot(dt.astype(BF16), expand, preferred_element_type=F32)
    dec_e = jnp.dot(jnp.exp(acs).astype(BF16), expand, preferred_element_type=F32)
    w_e = jnp.dot((jnp.exp(last - acs) * dt).astype(BF16), expand, preferred_element_type=F32)

    li = lax.broadcasted_iota(jnp.int32, (q, q), 0)
    si = lax.broadcasted_iota(jnp.int32, (q, q), 1)
    causal = li >= si
    lane_g = lax.broadcasted_iota(jnp.int32, (q, gw), 1)

    for g in range(SSM_GROUPS):
        cols = slice(g * gw, (g + 1) * gw)
        b_f = xc[:, d_ssm + g * SSM_STATE:d_ssm + (g + 1) * SSM_STATE]
        c_b = xc[:, d_ssm + (SSM_GROUPS + g) * SSM_STATE:d_ssm + (SSM_GROUPS + g + 1) * SSM_STATE].astype(BF16)
        xs = xc[:, cols]
        cb = lax.dot_general(c_b, b_f.astype(BF16), (((1,), (1,)), ((), ())), preferred_element_type=F32)
        xdt = (xs * dt_e[:, cols]).astype(BF16)
        ws, rs = [], []
        for hh in range(heads_per_group):
            h = g * heads_per_group + hh
            seg = acs[:, h:h + 1] - acs_t[h:h + 1, :]
            l_mat = jnp.exp(jnp.where(causal, seg, -jnp.inf))
            ws.append((cb * l_mat).astype(BF16))
            in_head = (lane_g >= hh * SSM_HEAD_DIM) & (lane_g < (hh + 1) * SSM_HEAD_DIM)
            rs.append(jnp.where(in_head, xdt, jnp.zeros_like(xdt)))
        y = jnp.dot(jnp.concatenate(ws, axis=1), jnp.concatenate(rs, axis=0), preferred_element_type=F32)
        st_g = st[:, cols]
        y = y + jnp.dot(c_b, st_g.astype(BF16), preferred_element_type=F32) * dec_e[:, cols]
        xw = (xs * w_e[:, cols]).astype(BF16)
        st[:, cols] = st_g * dec_e[q - 1:q, cols] + jnp.dot(b_f.T.astype(BF16), xw, preferred_element_type=F32)
        y = y + dskip_ref[:, cols] * xs
        y = y * _silu(z_ref[:, cols].astype(F32))
        o_ref[:, cols] = _rms_rows(y, g_ref[:, cols]).astype(o_ref.dtype)


def ssd_mixer(packed, dt_raw, conv_w, conv_b, dt_bias, a_log, d_skip, g_out, bsz, s_len, d_ssm, xbc_col, z_col):
    n_heads = d_ssm // SSM_HEAD_DIM
    d_conv = d_ssm + 2 * SSM_GROUPS * SSM_STATE
    nc = s_len // SSD_CHUNK
    padl = lambda v: jnp.pad(v.astype(F32), (0, LANES - n_heads))[None, :]
    expand = (jnp.arange(LANES)[:, None] == (jnp.arange(d_ssm) // SSM_HEAD_DIM)[None, :]).astype(BF16)
    tril = (jnp.arange(SSD_CHUNK)[:, None] >= jnp.arange(SSD_CHUNK)[None, :]).astype(BF16)
    full = lambda shape: pl.BlockSpec(shape, lambda b, c: (0, 0))
    return pl.pallas_call(
        _ssd_kernel,
        grid=(bsz, nc),
        in_specs=[pl.BlockSpec((SSD_CHUNK, d_conv), lambda b, c: (b * nc + c, xbc_col // d_conv)),
                  pl.BlockSpec((SSD_CHUNK, d_ssm), lambda b, c: (b * nc + c, z_col // d_ssm)),
                  pl.BlockSpec((SSD_CHUNK, LANES), lambda b, c: (b * nc + c, 0)),
                  full((CONV_WIDTH, d_conv)), full((1, d_conv)), full((1, LANES)), full((1, LANES)),
                  full((1, d_ssm)), full((1, d_ssm)), full((LANES, d_ssm)), full((SSD_CHUNK, SSD_CHUNK))],
        out_specs=pl.BlockSpec((SSD_CHUNK, d_ssm), lambda b, c: (b * nc + c, 0)),
        out_shape=jax.ShapeDtypeStruct((bsz * s_len, d_ssm), BF16),
        scratch_shapes=[pltpu.VMEM((d_conv // LANES, SSD_CHUNK + 8, LANES), F32),
                        pltpu.VMEM((SSD_CHUNK, d_conv), F32),
                        pltpu.VMEM((SSM_STATE, d_ssm), F32)],
        compiler_params=_cparams(("parallel", "arbitrary")),
        name="ssd",
    )(packed, packed, dt_raw, conv_w.astype(F32), conv_b.astype(F32)[None, :], padl(dt_bias), padl(a_log),
      jnp.repeat(d_skip.astype(F32), SSM_HEAD_DIM)[None, :], g_out.astype(F32)[None, :], expand, tril)


def _out_proj_kernel(attn_ref, ssm_ref, x_ref, g_ref, w_ref, o_ref, mix_ref):
    d_attn = attn_ref.shape[1]

    @pl.when(pl.program_id(1) == 0)
    def _():
        rows = 32

        def body(c, carry):
            rs = pl.ds(pl.multiple_of(c * rows, rows), rows)
            mix_ref[rs, 0:d_attn] = _rms_rows(attn_ref[rs, :].astype(F32), g_ref[...]).astype(BF16)
            return carry
        lax.fori_loop(0, attn_ref.shape[0] // rows, body, 0, unroll=4)
        mix_ref[:, d_attn:] = ssm_ref[...]
    o_ref[...] = x_ref[...] + jnp.dot(mix_ref[...], w_ref[...].astype(BF16), preferred_element_type=F32)


def out_proj(attn, ssm, x2d, g, w, tm=2048, tn=256):
    m, d = x2d.shape
    da, ds_ = attn.shape[1], ssm.shape[1]
    return pl.pallas_call(
        _out_proj_kernel,
        grid=(m // tm, d // tn),
        in_specs=[pl.BlockSpec((tm, da), lambda i, j: (i, 0)),
                  pl.BlockSpec((tm, ds_), lambda i, j: (i, 0)),
                  pl.BlockSpec((tm, tn), lambda i, j: (i, j)),
                  pl.BlockSpec((1, da), lambda i, j: (0, 0)),
                  pl.BlockSpec((da + ds_, tn), lambda i, j: (0, j))],
        out_specs=pl.BlockSpec((tm, tn), lambda i, j: (i, j)),
        out_shape=jax.ShapeDtypeStruct((m, d), F32),
        scratch_shapes=[pltpu.VMEM((tm, da + ds_), BF16)],
        compiler_params=_cparams(("parallel", "arbitrary")),
        name="out_proj",
    )(attn, ssm, x2d, g, w)


def _mem_kv_kernel(mem_ref, g_ref, w_ref, gk_ref, kv_ref, h_ref):
    d_cross = N_CROSS_HEADS * CROSS_HEAD_DIM
    _norm_rows_to(mem_ref, g_ref, h_ref)
    kv = jnp.dot(h_ref[...], w_ref[...], preferred_element_type=F32)
    for h in range(N_CROSS_HEADS):
        cols = slice(h * CROSS_HEAD_DIM, (h + 1) * CROSS_HEAD_DIM)
        kv_ref[:, cols] = _rms_rows(kv[:, cols], gk_ref[...]).astype(kv_ref.dtype)
    kv_ref[:, d_cross:] = kv[:, d_cross:].astype(kv_ref.dtype)


def mem_kv(mem2d, g, w, gk, tm=256):
    m, d = mem2d.shape
    n = w.shape[1]
    return pl.pallas_call(
        _mem_kv_kernel,
        grid=(m // tm,),
        in_specs=[pl.BlockSpec((tm, d), lambda i: (i, 0)),
                  pl.BlockSpec((1, d), lambda i: (0, 0)),
                  pl.BlockSpec((d, n), lambda i: (0, 0)),
                  pl.BlockSpec((1, CROSS_HEAD_DIM), lambda i: (0, 0))],
        out_specs=pl.BlockSpec((tm, n), lambda i: (i, 0)),
        out_shape=jax.ShapeDtypeStruct((m, n), BF16),
        scratch_shapes=[pltpu.VMEM((tm, d), BF16)],
        compiler_params=_cparams(("parallel",)),
        name="mem_kv",
    )(mem2d, g, w, gk)


def _cross_kernel(x_ref, g_ref, wq_ref, gq_ref, k_ref, v_ref, wo_ref, o_ref, h_ref, q_ref, a_ref, s_scr, p_scr):
    tm = x_ref.shape[0]
    rows = CROSS_ROWS
    _norm_rows_to(x_ref, g_ref, h_ref)
    q_ref[...] = jnp.dot(h_ref[...], wq_ref[...], preferred_element_type=F32)

    def scores(item, slot):
        r0, h = item
        cols = slice(h * CROSS_HEAD_DIM, (h + 1) * CROSS_HEAD_DIM)
        qh = (_rms_rows(q_ref[r0:r0 + rows, cols], gq_ref[...]) * (CROSS_HEAD_DIM ** -0.5)).astype(BF16)
        s_scr[slot] = lax.dot_general(qh, k_ref[:, cols], (((1,), (1,)), ((), ())), preferred_element_type=F32)

    def softmax(item, slot):
        s = s_scr[slot]
        p_scr[slot] = jnp.exp(s - jnp.max(s, axis=-1, keepdims=True)).astype(BF16)

    def values(item, slot):
        r0, h = item
        cols = slice(h * CROSS_HEAD_DIM, (h + 1) * CROSS_HEAD_DIM)
        vh = v_ref[:, cols]
        r = jnp.dot(p_scr[slot], jnp.concatenate([vh, jnp.ones_like(vh)], axis=1), preferred_element_type=F32)
        a_ref[r0:r0 + rows, cols] = (r[:, :CROSS_HEAD_DIM] / r[:, CROSS_HEAD_DIM:]).astype(BF16)

    items = [(r0, h) for r0 in range(0, tm, rows) for h in range(N_CROSS_HEADS)]
    _software_pipeline((scores, softmax, values), items, CROSS_GROUP)
    o_ref[...] = x_ref[...] + jnp.dot(a_ref[...], wo_ref[...], preferred_element_type=F32)


def cross_attention(x2d, g, wq, gq, kv, wo, s_len, n_mem, tm=512):
    m, d = x2d.shape
    dc = wq.shape[1]
    per_batch = s_len // tm
    full = lambda shape: pl.BlockSpec(shape, lambda i: (0, 0))
    return pl.pallas_call(
        _cross_kernel,
        grid=(m // tm,),
        in_specs=[pl.BlockSpec((tm, d), lambda i: (i, 0)), full((1, d)), full((d, dc)), full((1, CROSS_HEAD_DIM)),
                  pl.BlockSpec((n_mem, dc), lambda i: (i // per_batch, 0)),
                  pl.BlockSpec((n_mem, dc), lambda i: (i // per_batch, 1)),
                  full((dc, d))],
        out_specs=pl.BlockSpec((tm, d), lambda i: (i, 0)),
        out_shape=jax.ShapeDtypeStruct((m, d), F32),
        scratch_shapes=[pltpu.VMEM((tm, d), BF16), pltpu.VMEM((tm, dc), F32), pltpu.VMEM((tm, dc), BF16),
                        pltpu.VMEM((2 * CROSS_GROUP, CROSS_ROWS, n_mem), F32),
                        pltpu.VMEM((2 * CROSS_GROUP, CROSS_ROWS, n_mem), BF16)],
        compiler_params=_cparams(("parallel",)),
        name="cross_attn",
    )(x2d, g, wq, gq, kv, kv, wo)


def _mlp_kernel(x_ref, g_ref, wu_ref, wd_ref, o_ref, h_ref):
    @pl.when(pl.program_id(1) == 0)
    def _():
        _norm_rows_to(x_ref, g_ref, h_ref)
        o_ref[...] = x_ref[...]
    u = jnp.dot(h_ref[...], wu_ref[...], preferred_element_type=F32)
    u = jnp.square(jnp.maximum(u, 0.0)).astype(BF16)
    o_ref[...] += jnp.dot(u, wd_ref[...], preferred_element_type=F32)


def mlp(x2d, g, wu, wd, tm=512, tf=1024):
    m, d = x2d.shape
    f = wu.shape[1]
    return pl.pallas_call(
        _mlp_kernel,
        grid=(m // tm, f // tf),
        in_specs=[pl.BlockSpec((tm, d), lambda i, j: (i, 0)),
                  pl.BlockSpec((1, d), lambda i, j: (0, 0)),
                  pl.BlockSpec((d, tf), lambda i, j: (0, j)),
                  pl.BlockSpec((tf, d), lambda i, j: (j, 0))],
        out_specs=pl.BlockSpec((tm, d), lambda i, j: (i, 0)),
        out_shape=jax.ShapeDtypeStruct((m, d), F32),
        scratch_shapes=[pltpu.VMEM((tm, d), BF16)],
        compiler_params=_cparams(("parallel", "arbitrary")),
        name="mlp",
    )(x2d, g, wu, wd)


def _layer(x2d, mem2d, pos_col, bsz, s_len, n_mem, g_mix, w_in, g_q, g_k, g_attn_out, conv_w, conv_b, dt_bias,
           a_log, d_skip, g_ssm_out, w_out, g_cross, g_mem, w_cq, w_ckv, g_cq, g_ck, w_co, g_mlp, w_up, w_down):
    d_model = x2d.shape[1]
    d_attn = d_model // 2
    d_ssm = d_model // 2
    n_pairs = d_attn // LANES
    n_ssm_heads = d_ssm // SSM_HEAD_DIM
    d_conv = d_ssm + 2 * SSM_GROUPS * SSM_STATE
    d_packed = 3 * d_attn + d_ssm + d_conv
    row = lambda v: v.astype(F32)[None, :]

    w_in_t = w_in.T
    w_dt_t = jnp.pad(w_in_t[d_packed:], ((0, LANES - n_ssm_heads), (0, 0)))
    packed, dt_raw = in_proj(x2d, row(g_mix), w_in_t, w_dt_t, d_packed)

    tabs = rope_tables(pos_col)
    attn, w_up_bf, w_down_bf = dilated_attention(
        packed, tabs, row(jnp.tile(g_q, 2)) * (ATTN_HEAD_DIM ** -0.5), row(jnp.tile(g_k, 2)), w_up, w_down,
        bsz, s_len, n_pairs)
    ssm = ssd_mixer(packed, dt_raw, conv_w, conv_b, dt_bias, a_log, d_skip, g_ssm_out, bsz, s_len, d_ssm,
                    xbc_col=3 * d_attn + d_ssm, z_col=3 * d_attn)
    x2d = out_proj(attn, ssm, x2d, row(g_attn_out), w_out)

    kv = mem_kv(mem2d, row(g_mem), w_ckv.astype(BF16), row(g_ck))
    x2d = cross_attention(x2d, row(g_cross), w_cq.astype(BF16), row(g_cq), kv, w_co.astype(BF16), s_len, n_mem)

    return mlp(x2d, row(g_mlp), w_up_bf, w_down_bf)


def kernel(x, mem, positions, g_mix, w_in, g_q, g_k, g_attn_out, conv_w, conv_b, dt_bias, a_log, d_skip, g_ssm_out,
           w_out, g_cross, g_mem, w_cq, w_ckv, g_cq, g_ck, w_co, g_mlp, w_up, w_down):
    bsz, s_len, d_model = x.shape
    n_mem = mem.shape[1]
    x2d = x.reshape(bsz * s_len, d_model)
    mem2d = mem.reshape(bsz * n_mem, d_model)
    pos_col = positions.reshape(bsz * s_len, 1)
    for i in range(g_mix.shape[0]):
        x2d = _layer(x2d, mem2d, pos_col, bsz, s_len, n_mem, g_mix[i], w_in[i], g_q[i], g_k[i], g_attn_out[i],
                     conv_w[i], conv_b[i], dt_bias[i], a_log[i], d_skip[i], g_ssm_out[i], w_out[i], g_cross[i],
                     g_mem[i], w_cq[i], w_ckv[i], g_cq[i], g_ck[i], w_co[i], g_mlp[i], w_up[i], w_down[i])
    return x2d.reshape(bsz, s_len, d_model)
```

```python
import functools
import math

import jax
import jax.numpy as jnp
from jax import lax
from jax.experimental import pallas as pl
from jax.experimental.pallas import tpu as pltpu

F32 = jnp.float32
BF16 = jnp.bfloat16
EPS = 1e-6

LANES = 128
ATTN_HEAD_DIM = 64
ROT_DIM = ATTN_HEAD_DIM // 4
ROPE_THETA = 500000.0
ATTN_BLOCK = 128
DILATIONS = (1, 4, 16)
SSM_HEAD_DIM = 64
SSM_GROUPS = 4
SSM_STATE = 128
CONV_WIDTH = 4
SSD_CHUNK = 128
N_CROSS_HEADS = 4
CROSS_HEAD_DIM = 128
CROSS_ROWS = 256
CROSS_GROUP = 2
VMEM_LIMIT = 52 * 1024 * 1024


def _cparams(sem):
    return pltpu.CompilerParams(dimension_semantics=sem, vmem_limit_bytes=VMEM_LIMIT)


def _rms_rows(x, g):
    ms = jnp.mean(x * x, axis=-1, keepdims=True)
    return x * lax.rsqrt(ms + EPS) * g


def _norm_rows_to(x_ref, g_ref, h_ref, rows=16):
    def body(c, carry):
        r0 = pl.multiple_of(c * rows, rows)
        h_ref[pl.ds(r0, rows), :] = _rms_rows(x_ref[pl.ds(r0, rows), :], g_ref[...]).astype(h_ref.dtype)
        return carry
    lax.fori_loop(0, x_ref.shape[0] // rows, body, 0, unroll=4)


def _norm_chunk_to(x_ref, g_ref, h_ref, start, n_rows, rows=16):
    for r in range(0, n_rows, rows):
        rs = pl.ds(start + r, rows)
        h_ref[rs, :] = _rms_rows(x_ref[rs, :], g_ref[...]).astype(h_ref.dtype)


def _in_proj_kernel(x0_ref, xn_ref, g_ref, w_ref, wdt_ref, o_ref, dt_ref, h_ref):
    i, j = pl.program_id(0), pl.program_id(1)
    tm = xn_ref.shape[0]
    cur, nxt = i % 2, (i + 1) % 2

    @pl.when((i == 0) & (j == 0))
    def _():
        _norm_rows_to(x0_ref, g_ref, h_ref.at[0])

    @pl.when(j == 0)
    def _():
        dt_ref[...] = lax.dot_general(h_ref[cur], wdt_ref[...].astype(BF16), (((1,), (1,)), ((), ())),
                                      preferred_element_type=F32)
    o_ref[...] = lax.dot_general(h_ref[cur], w_ref[...].astype(BF16), (((1,), (1,)), ((), ())),
                                 preferred_element_type=F32).astype(o_ref.dtype)
    chunk = tm // IN_PROJ_NORM_STEPS
    start = pl.multiple_of(jnp.minimum(j, IN_PROJ_NORM_STEPS - 1) * chunk, chunk)
    _norm_chunk_to(xn_ref, g_ref, h_ref.at[nxt], start, chunk)


IN_PROJ_NORM_STEPS = 8


def in_proj(x2d, g, w_t, w_dt, n, tm=1024, tn=512):
    m, d = x2d.shape
    n_i = m // tm
    assert n // tn >= IN_PROJ_NORM_STEPS
    return pl.pallas_call(
        _in_proj_kernel,
        grid=(n_i, n // tn),
        in_specs=[pl.BlockSpec((tm, d), lambda i, j: (0, 0), pipeline_mode=pl.Buffered(1)),
                  pl.BlockSpec((tm, d), lambda i, j: (jnp.minimum(i + 1, n_i - 1), 0)),
                  pl.BlockSpec((1, d), lambda i, j: (0, 0)),
                  pl.BlockSpec((tn, d), lambda i, j: (j, 0)),
                  pl.BlockSpec((LANES, d), lambda i, j: (0, 0))],
        out_specs=[pl.BlockSpec((tm, tn), lambda i, j: (i, j)),
                   pl.BlockSpec((tm, LANES), lambda i, j: (i, 0))],
        out_shape=[jax.ShapeDtypeStruct((m, n), BF16), jax.ShapeDtypeStruct((m, LANES), F32)],
        scratch_shapes=[pltpu.VMEM((2, tm, d), BF16)],
        compiler_params=_cparams(("arbitrary", "arbitrary")),
        name="in_proj",
    )(x2d, x2d, g, w_t, w_dt)


def _rope_kernel(pos_ref, invf_ref, cmask_ref, cos_ref, sin_ref):
    ang = pos_ref[...].astype(F32) * invf_ref[...]
    cm = cmask_ref[...]
    cos_ref[...] = jnp.cos(ang) * cm + (1.0 - cm)
    sin_ref[...] = jnp.sin(ang) * cm


def rope_tables(pos_col, tm=512):
    m = pos_col.shape[0]
    half = ROT_DIM // 2
    inv_freq = ROPE_THETA ** (-2.0 * jnp.arange(half, dtype=F32) / ROT_DIM)
    d = jnp.arange(LANES) % ATTN_HEAD_DIM
    invf = inv_freq[d % half][None, :]
    cmask = (d < ROT_DIM).astype(F32)[None, :]
    row = pl.BlockSpec((1, LANES), lambda i: (0, 0))
    tab = pl.BlockSpec((tm, LANES), lambda i: (i, 0))
    return pl.pallas_call(
        _rope_kernel,
        grid=(m // tm,),
        in_specs=[pl.BlockSpec((tm, 1), lambda i: (i, 0)), row, row],
        out_specs=[tab, tab],
        out_shape=[jax.ShapeDtypeStruct((m, LANES), F32)] * 2,
        compiler_params=_cparams(("parallel",)),
        name="rope_tab",
    )(pos_col, invf, cmask)


def _rotate_half_matrix():
    half = ROT_DIM // 2
    src = jnp.arange(LANES)[:, None]
    dst = jnp.arange(LANES)[None, :]
    d = dst % ATTN_HEAD_DIM
    first = (d < half) & (src == dst + half)
    second = (d >= half) & (d < ROT_DIM) & (src == dst - half)
    return (second.astype(F32) - first.astype(F32)).astype(BF16)


def _rows(start, size, dil):
    return pl.ds(start, size) if dil == 1 else pl.ds(start, size, stride=dil)


ATTN_GROUP = 2


def _software_pipeline(stages, items, group):
    groups = [items[i:i + group] for i in range(0, len(items), group)]
    for t in range(len(groups) + len(stages) - 1):
        for lag, stage in enumerate(stages):
            if 0 <= t - lag < len(groups):
                for g, item in enumerate(groups[t - lag]):
                    stage(item, ((t - lag) % 2) * group + g)


def _attn_kernel(q_ref, k_ref, v_ref, cos_ref, sin_ref, gq_ref, gk_ref, seg_ref, rot_ref, wu_ref, wd_ref,
                 o_ref, wu_out, wd_out, qf, kf, vf, qd, kd, vd, ob, mb, lb, s_scr, p_scr):
    s_len = q_ref.shape[0]
    blk = ATTN_BLOCK
    lane = lax.broadcasted_iota(jnp.int32, (blk, LANES), 1)
    head0 = lane < ATTN_HEAD_DIM
    qi = lax.broadcasted_iota(jnp.int32, (2 * blk, 2 * blk), 0) % blk
    kj = lax.broadcasted_iota(jnp.int32, (2 * blk, 2 * blk), 1)
    band_mask = (kj >= qi) & (kj <= qi + blk)
    first_mask = (lax.broadcasted_iota(jnp.int32, (2 * blk, blk), 1)
                  <= lax.broadcasted_iota(jnp.int32, (2 * blk, blk), 0) % blk)

    wu_out[...] = wu_ref[...].astype(BF16)
    wd_out[...] = wd_ref[...].astype(BF16)

    prep_rows = 256

    def prep(c, carry):
        rows = pl.ds(pl.multiple_of(c * prep_rows, prep_rows), prep_rows)
        cs, sn = cos_ref[rows, :], sin_ref[rows, :]

        def norm_rope(x_ref, g_ref):
            x = x_ref[rows, :].astype(F32)
            ss = jnp.dot((x * x).astype(BF16), seg_ref[...], preferred_element_type=F32)
            y = x * lax.rsqrt(ss * (1.0 / ATTN_HEAD_DIM) + EPS) * g_ref[...]
            return y * cs + jnp.dot(y.astype(BF16), rot_ref[...], preferred_element_type=F32) * sn

        qf[rows, :] = norm_rope(q_ref, gq_ref)
        kf[rows, :] = norm_rope(k_ref, gk_ref)
        vf[rows, :] = v_ref[rows, :].astype(F32)
        return carry

    lax.fori_loop(0, s_len // prep_rows, prep, 0, unroll=2)

    dmid = DILATIONS[1]
    run = s_len // dmid

    def deinterleave(r, carry):
        dst_rows = pl.ds(pl.multiple_of(r * run, run), run)
        for src, dst in ((qf, qd), (kf, kd), (vf, vd)):
            dst[dst_rows, :] = src[pl.ds(r, run, stride=dmid), :]
        return carry

    lax.fori_loop(0, dmid, deinterleave, 0)

    natural, deint = (qf, kf, vf), (qd, kd, vd)
    descs = [(natural, 0, 1, 0, 0, blk, first_mask)]
    descs += [(natural, 0, 1, n * blk, (n - 1) * blk, 2 * blk, band_mask) for n in range(1, s_len // blk)]
    descs += [(deint, 1, 1, r * run, r * run, blk, first_mask) for r in range(dmid)]
    descs += [(deint, 1, 1, r * run + n * blk, r * run + (n - 1) * blk, 2 * blk, band_mask)
              for n in range(1, run // blk) for r in range(dmid)]
    descs += [(deint, 2, dmid, r_hi * run + r_lo, r_hi * run + r_lo, blk, first_mask)
              for r_hi in range(dmid) for r_lo in range(dmid)]

    def scores(desc, slot):
        (q_src, k_src, _), _, dil, q_start, k_start, n_keys, _ = desc
        qt = q_src[_rows(q_start, blk, dil), :]
        zero = jnp.zeros_like(qt)
        q2 = jnp.concatenate([jnp.where(head0, qt, zero), jnp.where(head0, zero, qt)], axis=0).astype(BF16)
        kb = k_src[_rows(k_start, n_keys, dil), :].astype(BF16)
        s_scr[slot, :, :n_keys] = lax.dot_general(q2, kb, (((1,), (1,)), ((), ())), preferred_element_type=F32)

    def softmax(desc, slot):
        _, br, dil, q_start, _, n_keys, mask = desc
        s = jnp.where(mask, s_scr[slot, :, :n_keys], -jnp.inf)
        m = jnp.max(s, axis=-1, keepdims=True)
        p_scr[slot, :, :n_keys] = jnp.exp(s - m).astype(BF16)
        mb[br, _rows(q_start, blk, dil), :] = jnp.where(head0, m[:blk], m[blk:])

    def values(desc, slot):
        (_, _, v_src), br, dil, q_start, k_start, n_keys, _ = desc
        vb = v_src[_rows(k_start, n_keys, dil), :].astype(BF16)
        v1 = jnp.concatenate([vb, jnp.ones_like(vb)], axis=1)
        r = jnp.dot(p_scr[slot, :, :n_keys], v1, preferred_element_type=F32)
        out_rows = _rows(q_start, blk, dil)
        ob[br, out_rows, :] = jnp.where(head0, r[:blk, :LANES], r[blk:, :LANES])
        lb[br, out_rows, :] = jnp.where(head0, r[:blk, LANES:], r[blk:, LANES:])

    _software_pipeline((scores, softmax, values), descs, ATTN_GROUP)

    def merge(c, carry):
        d_start = pl.multiple_of(c * prep_rows, prep_rows)
        r = d_start // run
        nat_rows = pl.ds(dmid * (d_start - r * run) + r, prep_rows, stride=dmid)
        d_rows = pl.ds(d_start, prep_rows)
        rows = (nat_rows, d_rows, d_rows)
        ms = [mb[g, rows[g], :] for g in range(len(DILATIONS))]
        m = functools.reduce(jnp.maximum, ms)
        ws = [jnp.exp(mg - m) for mg in ms]
        num = functools.reduce(jnp.add, [w * ob[g, rows[g], :] for g, w in enumerate(ws)])
        den = functools.reduce(jnp.add, [w * lb[g, rows[g], :] for g, w in enumerate(ws)])
        ob[0, nat_rows, :] = num / den
        return carry

    lax.fori_loop(0, s_len // prep_rows, merge, 0)

    def emit(c, carry):
        rows = pl.ds(pl.multiple_of(c * prep_rows, prep_rows), prep_rows)
        o_ref[rows, :] = ob[0, rows, :].astype(o_ref.dtype)
        return carry

    lax.fori_loop(0, s_len // prep_rows, emit, 0)


def dilated_attention(qkv, tabs, gq2, gk2, w_up, w_down, bsz, s_len, n_pairs):
    cos_t, sin_t = tabs
    d1, d2, d3 = DILATIONS
    assert d1 == 1 and d3 == d2 * d2 and s_len == d3 * ATTN_BLOCK, "layout assumes dilations (1, d, d*d), one block per largest class"
    lane = jnp.arange(LANES)
    seg = (lane[:, None] // ATTN_HEAD_DIM == lane[None, :] // ATTN_HEAD_DIM).astype(BF16)
    n_steps = bsz * n_pairs
    wu_rows, wd_rows = w_up.shape[0] // n_steps, w_down.shape[0] // n_steps
    blk = lambda off: pl.BlockSpec((s_len, LANES), lambda b, p, off=off: (b, off + p))
    tab = pl.BlockSpec((s_len, LANES), lambda b, p: (b, 0))
    row = pl.BlockSpec((1, LANES), lambda b, p: (0, 0))
    sq = pl.BlockSpec((LANES, LANES), lambda b, p: (0, 0))
    wu_spec = pl.BlockSpec((wu_rows, w_up.shape[1]), lambda b, p: (b * n_pairs + p, 0))
    wd_spec = pl.BlockSpec((wd_rows, w_down.shape[1]), lambda b, p: (b * n_pairs + p, 0))
    return pl.pallas_call(
        _attn_kernel,
        grid=(bsz, n_pairs),
        in_specs=[blk(0), blk(n_pairs), blk(2 * n_pairs), tab, tab, row, row, sq, sq, wu_spec, wd_spec],
        out_specs=[pl.BlockSpec((s_len, LANES), lambda b, p: (b, p)), wu_spec, wd_spec],
        out_shape=[jax.ShapeDtypeStruct((bsz * s_len, n_pairs * LANES), BF16),
                   jax.ShapeDtypeStruct(w_up.shape, BF16), jax.ShapeDtypeStruct(w_down.shape, BF16)],
        scratch_shapes=[pltpu.VMEM((s_len, LANES), F32)] * 6
                       + [pltpu.VMEM((len(DILATIONS), s_len, LANES), F32)] * 3
                       + [pltpu.VMEM((2 * ATTN_GROUP, 2 * ATTN_BLOCK, 2 * ATTN_BLOCK), F32),
                          pltpu.VMEM((2 * ATTN_GROUP, 2 * ATTN_BLOCK, 2 * ATTN_BLOCK), BF16)],
        compiler_params=_cparams(("parallel", "arbitrary")),
        name="dilated_attn",
    )(qkv, qkv, qkv, cos_t, sin_t, gq2, gk2, seg, _rotate_half_matrix(), w_up, w_down)


def _split3(x):
    hi = x.astype(BF16)
    r1 = x - hi.astype(F32)
    mid = r1.astype(BF16)
    lo = (r1 - mid.astype(F32)).astype(BF16)
    return hi, mid, lo


def _silu(x):
    return x * (1.0 / (1.0 + jnp.exp(-x)))


def _ssd_kernel(xbc_ref, z_ref, dt_ref, cw_ref, cb_ref, dtb_ref, alog_ref, dskip_ref, g_ref,
                expand_ref, tril_ref, o_ref, xpad, xc, st):
    q = SSD_CHUNK
    d_ssm = z_ref.shape[1]
    d_conv = xbc_ref.shape[1]
    gw = d_ssm // SSM_GROUPS
    heads_per_group = gw // SSM_HEAD_DIM
    pad = 8

    @pl.when(pl.program_id(1) == 0)
    def _():
        xpad[:, 0:pad, :] = jnp.zeros((d_conv // LANES, pad, LANES), F32)
        st[...] = jnp.zeros_like(st)

    for c0 in range(0, d_conv, LANES):
        cols = slice(c0, c0 + LANES)
        slab = c0 // LANES
        xpad[slab, pad:pad + q, :] = xbc_ref[:, cols].astype(F32)
        acc = cb_ref[:, cols] + cw_ref[CONV_WIDTH - 1:CONV_WIDTH, cols] * xpad[slab, pad:pad + q, :]
        for w in range(CONV_WIDTH - 1):
            off = pad - (CONV_WIDTH - 1) + w
            acc = acc + cw_ref[w:w + 1, cols] * xpad[slab, pl.ds(off, q, stride=1), :]
        xc[:, cols] = _silu(acc)
    xpad[:, 0:pad, :] = xpad[:, q:q + pad, :]

    x_dt = dt_ref[...] + dtb_ref[...]
    dt = jnp.maximum(x_dt, 0.0) + jnp.log1p(jnp.exp(-jnp.abs(x_dt)))
    dta = dt * (-jnp.exp(alog_ref[...]))
    tril = tril_ref[...]
    acs = sum(jnp.dot(tril, part, preferred_element_type=F32) for part in _split3(dta))
    acs_t = acs.T
    last = acs[q - 1:q, :]
    expand = expand_ref[...]
    dt_e = jnp.dot(dt.astype(BF16), expand, preferred_element_type=F32)
    dec_e = jnp.dot(jnp.exp(acs).astype(BF16), expand, preferred_element_type=F32)
    w_e = jnp.dot((jnp.exp(last - acs) * dt).astype(BF16), expand, preferred_element_type=F32)

    li = lax.broadcasted_iota(jnp.int32, (q, q), 0)
    si = lax.broadcasted_iota(jnp.int32, (q, q), 1)
    causal = li >= si
    lane_g = lax.broadcasted_iota(jnp.int32, (q, gw), 1)

    for g in range(SSM_GROUPS):
        cols = slice(g * gw, (g + 1) * gw)
        b_f = xc[:, d_ssm + g * SSM_STATE:d_ssm + (g + 1) * SSM_STATE]
        c_b = xc[:, d_ssm + (SSM_GROUPS + g) * SSM_STATE:d_ssm + (SSM_GROUPS + g + 1) * SSM_STATE].astype(BF16)
        xs = xc[:, cols]
        cb = lax.dot_general(c_b, b_f.astype(BF16), (((1,), (1,)), ((), ())), preferred_element_type=F32)
        xdt = (xs * dt_e[:, cols]).astype(BF16)
        ws, rs = [], []
        for hh in range(heads_per_group):
            h = g * heads_per_group + hh
            seg = acs[:, h:h + 1] - acs_t[h:h + 1, :]
            l_mat = jnp.exp(jnp.where(causal, seg, -jnp.inf))
            ws.append((cb * l_mat).astype(BF16))
            in_head = (lane_g >= hh * SSM_HEAD_DIM) & (lane_g < (hh + 1) * SSM_HEAD_DIM)
            rs.append(jnp.where(in_head, xdt, jnp.zeros_like(xdt)))
        y = jnp.dot(jnp.concatenate(ws, axis=1), jnp.concatenate(rs, axis=0), preferred_element_type=F32)
        st_g = st[:, cols]
        y = y + jnp.dot(c_b, st_g.astype(BF16), preferred_element_type=F32) * dec_e[:, cols]
        xw = (xs * w_e[:, cols]).astype(BF16)
        st[:, cols] = st_g * dec_e[q - 1:q, cols] + jnp.dot(b_f.T.astype(BF16), xw, preferred_element_type=F32)
        y = y + dskip_ref[:, cols] * xs
        y = y * _silu(z_ref[:, cols].astype(F32))
        o_ref[:, cols] = _rms_rows(y, g_ref[:, cols]).astype(o_ref.dtype)


def ssd_mixer(packed, dt_raw, conv_w, conv_b, dt_bias, a_log, d_skip, g_out, bsz, s_len, d_ssm, xbc_col, z_col):
    n_heads = d_ssm // SSM_HEAD_DIM
    d_conv = d_ssm + 2 * SSM_GROUPS * SSM_STATE
    nc = s_len // SSD_CHUNK
    padl = lambda v: jnp.pad(v.astype(F32), (0, LANES - n_heads))[None, :]
    expand = (jnp.arange(LANES)[:, None] == (jnp.arange(d_ssm) // SSM_HEAD_DIM)[None, :]).astype(BF16)
    tril = (jnp.arange(SSD_CHUNK)[:, None] >= jnp.arange(SSD_CHUNK)[None, :]).astype(BF16)
    full = lambda shape: pl.BlockSpec(shape, lambda b, c: (0, 0))
    return pl.pallas_call(
        _ssd_kernel,
        grid=(bsz, nc),
        in_specs=[pl.BlockSpec((SSD_CHUNK, d_conv), lambda b, c: (b * nc + c, xbc_col // d_conv)),
                  pl.BlockSpec((SSD_CHUNK, d_ssm), lambda b, c: (b * nc + c, z_col // d_ssm)),
                  pl.BlockSpec((SSD_CHUNK, LANES), lambda b, c: (b * nc + c, 0)),
                  full((CONV_WIDTH, d_conv)), full((1, d_conv)), full((1, LANES)), full((1, LANES)),
                  full((1, d_ssm)), full((1, d_ssm)), full((LANES, d_ssm)), full((SSD_CHUNK, SSD_CHUNK))],
        out_specs=pl.BlockSpec((SSD_CHUNK, d_ssm), lambda b, c: (b * nc + c, 0)),
        out_shape=jax.ShapeDtypeStruct((bsz * s_len, d_ssm), BF16),
        scratch_shapes=[pltpu.VMEM((d_conv // LANES, SSD_CHUNK + 8, LANES), F32),
                        pltpu.VMEM((SSD_CHUNK, d_conv), F32),
                        pltpu.VMEM((SSM_STATE, d_ssm), F32)],
        compiler_params=_cparams(("parallel", "arbitrary")),
        name="ssd",
    )(packed, packed, dt_raw, conv_w.astype(F32), conv_b.astype(F32)[None, :], padl(dt_bias), padl(a_log),
      jnp.repeat(d_skip.astype(F32), SSM_HEAD_DIM)[None, :], g_out.astype(F32)[None, :], expand, tril)


def _out_proj_kernel(attn_ref, ssm_ref, x_ref, g_ref, w_ref, o_ref, mix_ref):
    d_attn = attn_ref.shape[1]

    @pl.when(pl.program_id(1) == 0)
    def _():
        rows = 32

        def body(c, carry):
            rs = pl.ds(pl.multiple_of(c * rows, rows), rows)
            mix_ref[rs, 0:d_attn] = _rms_rows(attn_ref[rs, :].astype(F32), g_ref[...]).astype(BF16)
            return carry
        lax.fori_loop(0, attn_ref.shape[0] // rows, body, 0, unroll=4)
        mix_ref[:, d_attn:] = ssm_ref[...]
    o_ref[...] = x_ref[...] + jnp.dot(mix_ref[...], w_ref[...].astype(BF16), preferred_element_type=F32)


def out_proj(attn, ssm, x2d, g, w, tm=2048, tn=256):
    m, d = x2d.shape
    da, ds_ = attn.shape[1], ssm.shape[1]
    return pl.pallas_call(
        _out_proj_kernel,
        grid=(m // tm, d // tn),
        in_specs=[pl.BlockSpec((tm, da), lambda i, j: (i, 0)),
                  pl.BlockSpec((tm, ds_), lambda i, j: (i, 0)),
                  pl.BlockSpec((tm, tn), lambda i, j: (i, j)),
                  pl.BlockSpec((1, da), lambda i, j: (0, 0)),
                  pl.BlockSpec((da + ds_, tn), lambda i, j: (0, j))],
        out_specs=pl.BlockSpec((tm, tn), lambda i, j: (i, j)),
        out_shape=jax.ShapeDtypeStruct((m, d), F32),
        scratch_shapes=[pltpu.VMEM((tm, da + ds_), BF16)],
        compiler_params=_cparams(("parallel", "arbitrary")),
        name="out_proj",
    )(attn, ssm, x2d, g, w)


def _mem_kv_kernel(mem_ref, g_ref, w_ref, gk_ref, kv_ref, h_ref):
    d_cross = N_CROSS_HEADS * CROSS_HEAD_DIM
    _norm_rows_to(mem_ref, g_ref, h_ref)
    kv = jnp.dot(h_ref[...], w_ref[...], preferred_element_type=F32)
    for h in range(N_CROSS_HEADS):
        cols = slice(h * CROSS_HEAD_DIM, (h + 1) * CROSS_HEAD_DIM)
        kv_ref[:, cols] = _rms_rows(kv[:, cols], gk_ref[...]).astype(kv_ref.dtype)
    kv_ref[:, d_cross:] = kv[:, d_cross:].astype(kv_ref.dtype)


def mem_kv(mem2d, g, w, gk, tm=256):
    m, d = mem2d.shape
    n = w.shape[1]
    return pl.pallas_call(
        _mem_kv_kernel,
        grid=(m // tm,),
        in_specs=[pl.BlockSpec((tm, d), lambda i: (i, 0)),
                  pl.BlockSpec((1, d), lambda i: (0, 0)),
                  pl.BlockSpec((d, n), lambda i: (0, 0)),
                  pl.BlockSpec((1, CROSS_HEAD_DIM), lambda i: (0, 0))],
        out_specs=pl.BlockSpec((tm, n), lambda i: (i, 0)),
        out_shape=jax.ShapeDtypeStruct((m, n), BF16),
        scratch_shapes=[pltpu.VMEM((tm, d), BF16)],
        compiler_params=_cparams(("parallel",)),
        name="mem_kv",
    )(mem2d, g, w, gk)


def _cross_kernel(x_ref, xn_ref, g_ref, wq_ref, gq_ref, k_ref, v_ref, wo_ref, o_ref, h_ref, q_ref, a_ref, s_scr, p_scr):
    i = pl.program_id(0)
    tm = x_ref.shape[0]
    rows = CROSS_ROWS
    cur, nxt = i % 2, (i + 1) % 2

    @pl.when(i == 0)
    def _():
        _norm_rows_to(x_ref, g_ref, h_ref.at[0])

    q_ref[...] = jnp.dot(h_ref[cur], wq_ref[...], preferred_element_type=F32)

    def scores(item, slot):
        r0, h = item
        cols = slice(h * CROSS_HEAD_DIM, (h + 1) * CROSS_HEAD_DIM)
        qh = (_rms_rows(q_ref[r0:r0 + rows, cols], gq_ref[...]) * (CROSS_HEAD_DIM ** -0.5)).astype(BF16)
        s_scr[slot] = lax.dot_general(qh, k_ref[:, cols], (((1,), (1,)), ((), ())), preferred_element_type=F32)

    def softmax(item, slot):
        s = s_scr[slot]
        p_scr[slot] = jnp.exp(s - jnp.max(s, axis=-1, keepdims=True)).astype(BF16)

    def values(item, slot):
        r0, h = item
        cols = slice(h * CROSS_HEAD_DIM, (h + 1) * CROSS_HEAD_DIM)
        vh = v_ref[:, cols]
        r = jnp.dot(p_scr[slot], jnp.concatenate([vh, jnp.ones_like(vh)], axis=1), preferred_element_type=F32)
        a_ref[r0:r0 + rows, cols] = (r[:, :CROSS_HEAD_DIM] / r[:, CROSS_HEAD_DIM:]).astype(BF16)

    items = [(r0, h) for r0 in range(0, tm, rows) for h in range(N_CROSS_HEADS)]
    _software_pipeline((scores, softmax, values), items, CROSS_GROUP)
    _norm_chunk_to(xn_ref, g_ref, h_ref.at[nxt], 0, tm)
    o_ref[...] = x_ref[...] + jnp.dot(a_ref[...], wo_ref[...], preferred_element_type=F32)


def cross_attention(x2d, g, wq, gq, kv, wo, s_len, n_mem, tm=512):
    m, d = x2d.shape
    dc = wq.shape[1]
    per_batch = s_len // tm
    full = lambda shape: pl.BlockSpec(shape, lambda i: (0, 0))
    return pl.pallas_call(
        _cross_kernel,
        grid=(m // tm,),
        in_specs=[pl.BlockSpec((tm, d), lambda i: (i, 0)),
                  pl.BlockSpec((tm, d), lambda i: (jnp.minimum(i + 1, m // tm - 1), 0)),
                  full((1, d)), full((d, dc)), full((1, CROSS_HEAD_DIM)),
                  pl.BlockSpec((n_mem, dc), lambda i: (i // per_batch, 0)),
                  pl.BlockSpec((n_mem, dc), lambda i: (i // per_batch, 1)),
                  full((dc, d))],
        out_specs=pl.BlockSpec((tm, d), lambda i: (i, 0)),
        out_shape=jax.ShapeDtypeStruct((m, d), F32),
        scratch_shapes=[pltpu.VMEM((2, tm, d), BF16), pltpu.VMEM((tm, dc), F32), pltpu.VMEM((tm, dc), BF16),
                        pltpu.VMEM((2 * CROSS_GROUP, CROSS_ROWS, n_mem), F32),
                        pltpu.VMEM((2 * CROSS_GROUP, CROSS_ROWS, n_mem), BF16)],
        compiler_params=_cparams(("arbitrary",)),
        name="cross_attn",
    )(x2d, x2d, g, wq, gq, kv, kv, wo)


def _mlp_kernel(x_ref, xn_ref, g_ref, wu_ref, wd_ref, o_ref, h_ref):
    i, f = pl.program_id(0), pl.program_id(1)
    tm = x_ref.shape[0]
    cur, nxt = i % 2, (i + 1) % 2

    @pl.when((i == 0) & (f == 0))
    def _():
        _norm_rows_to(x_ref, g_ref, h_ref.at[0])

    @pl.when(f == 0)
    def _():
        o_ref[...] = x_ref[...]
    u = jnp.dot(h_ref[cur], wu_ref[...], preferred_element_type=F32)
    u = jnp.square(jnp.maximum(u, 0.0)).astype(BF16)
    o_ref[...] += jnp.dot(u, wd_ref[...], preferred_element_type=F32)
    chunk = tm // pl.num_programs(1)
    _norm_chunk_to(xn_ref, g_ref, h_ref.at[nxt], pl.multiple_of(f * chunk, chunk), chunk)


def mlp(x2d, g, wu, wd, tm=512, tf=1024):
    m, d = x2d.shape
    f = wu.shape[1]
    n_i = m // tm
    assert tm % (f // tf) == 0
    return pl.pallas_call(
        _mlp_kernel,
        grid=(n_i, f // tf),
        in_specs=[pl.BlockSpec((tm, d), lambda i, j: (i, 0)),
                  pl.BlockSpec((tm, d), lambda i, j: (jnp.minimum(i + 1, n_i - 1), 0)),
                  pl.BlockSpec((1, d), lambda i, j: (0, 0)),
                  pl.BlockSpec((d, tf), lambda i, j: (0, j)),
                  pl.BlockSpec((tf, d), lambda i, j: (j, 0))],
        out_specs=pl.BlockSpec((tm, d), lambda i, j: (i, 0)),
        out_shape=jax.ShapeDtypeStruct((m, d), F32),
        scratch_shapes=[pltpu.VMEM((2, tm, d), BF16)],
        compiler_params=_cparams(("arbitrary", "arbitrary")),
        name="mlp",
    )(x2d, x2d, g, wu, wd)


def _layer(x2d, mem2d, pos_col, bsz, s_len, n_mem, g_mix, w_in, g_q, g_k, g_attn_out, conv_w, conv_b, dt_bias,
           a_log, d_skip, g_ssm_out, w_out, g_cross, g_mem, w_cq, w_ckv, g_cq, g_ck, w_co, g_mlp, w_up, w_down):
    d_model = x2d.shape[1]
    d_attn = d_model // 2
    d_ssm = d_model // 2
    n_pairs = d_attn // LANES
    n_ssm_heads = d_ssm // SSM_HEAD_DIM
    d_conv = d_ssm + 2 * SSM_GROUPS * SSM_STATE
    d_packed = 3 * d_attn + d_ssm + d_conv
    row = lambda v: v.astype(F32)[None, :]

    w_in_t = w_in.T
    w_dt_t = jnp.pad(w_in_t[d_packed:], ((0, LANES - n_ssm_heads), (0, 0)))
    packed, dt_raw = in_proj(x2d, row(g_mix), w_in_t, w_dt_t, d_packed)

    tabs = rope_tables(pos_col)
    attn, w_up_bf, w_down_bf = dilated_attention(
        packed, tabs, row(jnp.tile(g_q, 2)) * (ATTN_HEAD_DIM ** -0.5), row(jnp.tile(g_k, 2)), w_up, w_down,
        bsz, s_len, n_pairs)
    ssm = ssd_mixer(packed, dt_raw, conv_w, conv_b, dt_bias, a_log, d_skip, g_ssm_out, bsz, s_len, d_ssm,
                    xbc_col=3 * d_attn + d_ssm, z_col=3 * d_attn)
    x2d = out_proj(attn, ssm, x2d, row(g_attn_out), w_out)

    kv = mem_kv(mem2d, row(g_mem), w_ckv.astype(BF16), row(g_ck))
    x2d = cross_attention(x2d, row(g_cross), w_cq.astype(BF16), row(g_cq), kv, w_co.astype(BF16), s_len, n_mem)

    return mlp(x2d, row(g_mlp), w_up_bf, w_down_bf)


def kernel(x, mem, positions, g_mix, w_in, g_q, g_k, g_attn_out, conv_w, conv_b, dt_bias, a_log, d_skip, g_ssm_out,
           w_out, g_cross, g_mem, w_cq, w_ckv, g_cq, g_ck, w_co, g_mlp, w_up, w_down):
    bsz, s_len, d_model = x.shape
    n_mem = mem.shape[1]
    x2d = x.reshape(bsz * s_len, d_model)
    mem2d = mem.reshape(bsz * n_mem, d_model)
    pos_col = positions.reshape(bsz * s_len, 1)
    for i in range(g_mix.shape[0]):
        x2d = _layer(x2d, mem2d, pos_col, bsz, s_len, n_mem, g_mix[i], w_in[i], g_q[i], g_k[i], g_attn_out[i],
                     conv_w[i], conv_b[i], dt_bias[i], a_log[i], d_skip[i], g_ssm_out[i], w_out[i], g_cross[i],
                     g_mem[i], w_cq[i], w_ckv[i], g_cq[i], g_ck[i], w_co[i], g_mlp[i], w_up[i], w_down[i])
    return x2d.reshape(bsz, s_len, d_model)
```

```python
import functools
import math

import jax
import jax.numpy as jnp
from jax import lax
from jax.experimental import pallas as pl
from jax.experimental.pallas import tpu as pltpu

F32 = jnp.float32
BF16 = jnp.bfloat16
EPS = 1e-6

LANES = 128
ATTN_HEAD_DIM = 64
ROT_DIM = ATTN_HEAD_DIM // 4
ROPE_THETA = 500000.0
ATTN_BLOCK = 128
DILATIONS = (1, 4, 16)
SSM_HEAD_DIM = 64
SSM_GROUPS = 4
SSM_STATE = 128
CONV_WIDTH = 4
SSD_CHUNK = 128
N_CROSS_HEADS = 4
CROSS_HEAD_DIM = 128
CROSS_ROWS = 256
CROSS_GROUP = 2
VMEM_LIMIT = 52 * 1024 * 1024


def _cparams(sem):
    return pltpu.CompilerParams(dimension_semantics=sem, vmem_limit_bytes=VMEM_LIMIT)


def _rms_rows(x, g):
    ms = jnp.mean(x * x, axis=-1, keepdims=True)
    return x * lax.rsqrt(ms + EPS) * g


def _norm_rows_to(x_ref, g_ref, h_ref, rows=16):
    def body(c, carry):
        r0 = pl.multiple_of(c * rows, rows)
        h_ref[pl.ds(r0, rows), :] = _rms_rows(x_ref[pl.ds(r0, rows), :], g_ref[...]).astype(h_ref.dtype)
        return carry
    lax.fori_loop(0, x_ref.shape[0] // rows, body, 0, unroll=4)


def _norm_chunk_to(x_ref, g_ref, h_ref, start, n_rows, rows=16):
    for r in range(0, n_rows, rows):
        rs = pl.ds(start + r, rows)
        h_ref[rs, :] = _rms_rows(x_ref[rs, :], g_ref[...]).astype(h_ref.dtype)


def _in_proj_kernel(x0_ref, xn_ref, g_ref, w_ref, wdt_ref, o_ref, dt_ref, h_ref):
    i, j = pl.program_id(0), pl.program_id(1)
    tm = xn_ref.shape[0]
    cur, nxt = i % 2, (i + 1) % 2

    @pl.when((i == 0) & (j == 0))
    def _():
        _norm_rows_to(x0_ref, g_ref, h_ref.at[0])

    @pl.when(j == 0)
    def _():
        dt_ref[...] = lax.dot_general(h_ref[cur], wdt_ref[...].astype(BF16), (((1,), (1,)), ((), ())),
                                      preferred_element_type=F32)
    o_ref[...] = lax.dot_general(h_ref[cur], w_ref[...].astype(BF16), (((1,), (1,)), ((), ())),
                                 preferred_element_type=F32).astype(o_ref.dtype)
    chunk = tm // IN_PROJ_NORM_STEPS
    start = pl.multiple_of(jnp.minimum(j, IN_PROJ_NORM_STEPS - 1) * chunk, chunk)
    _norm_chunk_to(xn_ref, g_ref, h_ref.at[nxt], start, chunk)


IN_PROJ_NORM_STEPS = 8


def in_proj(x2d, g, w_t, w_dt, n, tm=1024, tn=512):
    m, d = x2d.shape
    n_i = m // tm
    assert n // tn >= IN_PROJ_NORM_STEPS
    return pl.pallas_call(
        _in_proj_kernel,
        grid=(n_i, n // tn),
        in_specs=[pl.BlockSpec((tm, d), lambda i, j: (0, 0), pipeline_mode=pl.Buffered(1)),
                  pl.BlockSpec((tm, d), lambda i, j: (jnp.minimum(i + 1, n_i - 1), 0)),
                  pl.BlockSpec((1, d), lambda i, j: (0, 0)),
                  pl.BlockSpec((tn, d), lambda i, j: (j, 0)),
                  pl.BlockSpec((LANES, d), lambda i, j: (0, 0))],
        out_specs=[pl.BlockSpec((tm, tn), lambda i, j: (i, j)),
                   pl.BlockSpec((tm, LANES), lambda i, j: (i, 0))],
        out_shape=[jax.ShapeDtypeStruct((m, n), BF16), jax.ShapeDtypeStruct((m, LANES), F32)],
        scratch_shapes=[pltpu.VMEM((2, tm, d), BF16)],
        compiler_params=_cparams(("arbitrary", "arbitrary")),
        name="in_proj",
    )(x2d, x2d, g, w_t, w_dt)


def _rope_kernel(pos_ref, invf_ref, cmask_ref, cos_ref, sin_ref):
    ang = pos_ref[...].astype(F32) * invf_ref[...]
    cm = cmask_ref[...]
    cos_ref[...] = jnp.cos(ang) * cm + (1.0 - cm)
    sin_ref[...] = jnp.sin(ang) * cm


def rope_tables(pos_col, tm=512):
    m = pos_col.shape[0]
    half = ROT_DIM // 2
    inv_freq = ROPE_THETA ** (-2.0 * jnp.arange(half, dtype=F32) / ROT_DIM)
    d = jnp.arange(LANES) % ATTN_HEAD_DIM
    invf = inv_freq[d % half][None, :]
    cmask = (d < ROT_DIM).astype(F32)[None, :]
    row = pl.BlockSpec((1, LANES), lambda i: (0, 0))
    tab = pl.BlockSpec((tm, LANES), lambda i: (i, 0))
    return pl.pallas_call(
        _rope_kernel,
        grid=(m // tm,),
        in_specs=[pl.BlockSpec((tm, 1), lambda i: (i, 0)), row, row],
        out_specs=[tab, tab],
        out_shape=[jax.ShapeDtypeStruct((m, LANES), F32)] * 2,
        compiler_params=_cparams(("parallel",)),
        name="rope_tab",
    )(pos_col, invf, cmask)


def _rotate_half_matrix():
    half = ROT_DIM // 2
    src = jnp.arange(LANES)[:, None]
    dst = jnp.arange(LANES)[None, :]
    d = dst % ATTN_HEAD_DIM
    first = (d < half) & (src == dst + half)
    second = (d >= half) & (d < ROT_DIM) & (src == dst - half)
    return (second.astype(F32) - first.astype(F32)).astype(BF16)


def _rows(start, size, dil):
    return pl.ds(start, size) if dil == 1 else pl.ds(start, size, stride=dil)


ATTN_GROUP = 2


def _software_pipeline(stages, items, group):
    groups = [items[i:i + group] for i in range(0, len(items), group)]
    for t in range(len(groups) + len(stages) - 1):
        for lag, stage in enumerate(stages):
            if 0 <= t - lag < len(groups):
                for g, item in enumerate(groups[t - lag]):
                    stage(item, ((t - lag) % 2) * group + g)


def _attn_kernel(q_ref, k_ref, v_ref, cos_ref, sin_ref, gq_ref, gk_ref, seg_ref, rot_ref, wu_ref, wd_ref,
                 o_ref, wu_out, wd_out, qf, kf, vf, qd, kd, vd, ob, mb, lb, s_scr, p_scr):
    s_len = q_ref.shape[0]
    blk = ATTN_BLOCK
    lane = lax.broadcasted_iota(jnp.int32, (blk, LANES), 1)
    head0 = lane < ATTN_HEAD_DIM
    qi = lax.broadcasted_iota(jnp.int32, (2 * blk, 2 * blk), 0) % blk
    kj = lax.broadcasted_iota(jnp.int32, (2 * blk, 2 * blk), 1)
    band_mask = (kj >= qi) & (kj <= qi + blk)
    first_mask = (lax.broadcasted_iota(jnp.int32, (2 * blk, blk), 1)
                  <= lax.broadcasted_iota(jnp.int32, (2 * blk, blk), 0) % blk)

    wu_out[...] = wu_ref[...].astype(BF16)
    wd_out[...] = wd_ref[...].astype(BF16)

    prep_rows = 256

    def prep(c, carry):
        rows = pl.ds(pl.multiple_of(c * prep_rows, prep_rows), prep_rows)
        cs, sn = cos_ref[rows, :], sin_ref[rows, :]

        def norm_rope(x_ref, g_ref):
            x = x_ref[rows, :].astype(F32)
            ss = jnp.dot((x * x).astype(BF16), seg_ref[...], preferred_element_type=F32)
            y = x * lax.rsqrt(ss * (1.0 / ATTN_HEAD_DIM) + EPS) * g_ref[...]
            return y * cs + jnp.dot(y.astype(BF16), rot_ref[...], preferred_element_type=F32) * sn

        qf[rows, :] = norm_rope(q_ref, gq_ref)
        kf[rows, :] = norm_rope(k_ref, gk_ref)
        vf[rows, :] = v_ref[rows, :].astype(F32)
        return carry

    lax.fori_loop(0, s_len // prep_rows, prep, 0, unroll=2)

    dmid = DILATIONS[1]
    run = s_len // dmid

    def deinterleave(r, carry):
        dst_rows = pl.ds(pl.multiple_of(r * run, run), run)
        for src, dst in ((qf, qd), (kf, kd), (vf, vd)):
            dst[dst_rows, :] = src[pl.ds(r, run, stride=dmid), :]
        return carry

    lax.fori_loop(0, dmid, deinterleave, 0)

    natural, deint = (qf, kf, vf), (qd, kd, vd)
    descs = [(natural, 0, 1, 0, 0, blk, first_mask)]
    descs += [(natural, 0, 1, n * blk, (n - 1) * blk, 2 * blk, band_mask) for n in range(1, s_len // blk)]
    descs += [(deint, 1, 1, r * run, r * run, blk, first_mask) for r in range(dmid)]
    descs += [(deint, 1, 1, r * run + n * blk, r * run + (n - 1) * blk, 2 * blk, band_mask)
              for n in range(1, run // blk) for r in range(dmid)]
    descs += [(deint, 2, dmid, r_hi * run + r_lo, r_hi * run + r_lo, blk, first_mask)
              for r_hi in range(dmid) for r_lo in range(dmid)]

    def scores(desc, slot):
        (q_src, k_src, _), _, dil, q_start, k_start, n_keys, _ = desc
        qt = q_src[_rows(q_start, blk, dil), :]
        zero = jnp.zeros_like(qt)
        q2 = jnp.concatenate([jnp.where(head0, qt, zero), jnp.where(head0, zero, qt)], axis=0).astype(BF16)
        kb = k_src[_rows(k_start, n_keys, dil), :].astype(BF16)
        s_scr[slot, :, :n_keys] = lax.dot_general(q2, kb, (((1,), (1,)), ((), ())), preferred_element_type=F32)

    def softmax(desc, slot):
        _, br, dil, q_start, _, n_keys, mask = desc
        s = jnp.where(mask, s_scr[slot, :, :n_keys], -jnp.inf)
        m = jnp.max(s, axis=-1, keepdims=True)
        p_scr[slot, :, :n_keys] = jnp.exp(s - m).astype(BF16)
        mb[br, _rows(q_start, blk, dil), :] = jnp.where(head0, m[:blk], m[blk:])

    def values(desc, slot):
        (_, _, v_src), br, dil, q_start, k_start, n_keys, _ = desc
        vb = v_src[_rows(k_start, n_keys, dil), :].astype(BF16)
        v1 = jnp.concatenate([vb, jnp.ones_like(vb)], axis=1)
        r = jnp.dot(p_scr[slot, :, :n_keys], v1, preferred_element_type=F32)
        out_rows = _rows(q_start, blk, dil)
        ob[br, out_rows, :] = jnp.where(head0, r[:blk, :LANES], r[blk:, :LANES])
        lb[br, out_rows, :] = jnp.where(head0, r[:blk, LANES:], r[blk:, LANES:])

    _software_pipeline((scores, softmax, values), descs, ATTN_GROUP)

    def merge(c, carry):
        d_start = pl.multiple_of(c * prep_rows, prep_rows)
        r = d_start // run
        nat_rows = pl.ds(dmid * (d_start - r * run) + r, prep_rows, stride=dmid)
        d_rows = pl.ds(d_start, prep_rows)
        rows = (nat_rows, d_rows, d_rows)
        ms = [mb[g, rows[g], :] for g in range(len(DILATIONS))]
        m = functools.reduce(jnp.maximum, ms)
        ws = [jnp.exp(mg - m) for mg in ms]
        num = functools.reduce(jnp.add, [w * ob[g, rows[g], :] for g, w in enumerate(ws)])
        den = functools.reduce(jnp.add, [w * lb[g, rows[g], :] for g, w in enumerate(ws)])
        ob[0, nat_rows, :] = num / den
        return carry

    lax.fori_loop(0, s_len // prep_rows, merge, 0)

    def emit(c, carry):
        rows = pl.ds(pl.multiple_of(c * prep_rows, prep_rows), prep_rows)
        o_ref[rows, :] = ob[0, rows, :].astype(o_ref.dtype)
        return carry

    lax.fori_loop(0, s_len // prep_rows, emit, 0)


def dilated_attention(qkv, tabs, gq2, gk2, w_up, w_down, bsz, s_len, n_pairs):
    cos_t, sin_t = tabs
    d1, d2, d3 = DILATIONS
    assert d1 == 1 and d3 == d2 * d2 and s_len == d3 * ATTN_BLOCK, "layout assumes dilations (1, d, d*d), one block per largest class"
    lane = jnp.arange(LANES)
    seg = (lane[:, None] // ATTN_HEAD_DIM == lane[None, :] // ATTN_HEAD_DIM).astype(BF16)
    n_steps = bsz * n_pairs
    wu_rows, wd_rows = w_up.shape[0] // n_steps, w_down.shape[0] // n_steps
    blk = lambda off: pl.BlockSpec((s_len, LANES), lambda b, p, off=off: (b, off + p))
    tab = pl.BlockSpec((s_len, LANES), lambda b, p: (b, 0))
    row = pl.BlockSpec((1, LANES), lambda b, p: (0, 0))
    sq = pl.BlockSpec((LANES, LANES), lambda b, p: (0, 0))
    wu_spec = pl.BlockSpec((wu_rows, w_up.shape[1]), lambda b, p: (b * n_pairs + p, 0))
    wd_spec = pl.BlockSpec((wd_rows, w_down.shape[1]), lambda b, p: (b * n_pairs + p, 0))
    return pl.pallas_call(
        _attn_kernel,
        grid=(bsz, n_pairs),
        in_specs=[blk(0), blk(n_pairs), blk(2 * n_pairs), tab, tab, row, row, sq, sq, wu_spec, wd_spec],
        out_specs=[pl.BlockSpec((s_len, LANES), lambda b, p: (b, p)), wu_spec, wd_spec],
        out_shape=[jax.ShapeDtypeStruct((bsz * s_len, n_pairs * LANES), BF16),
                   jax.ShapeDtypeStruct(w_up.shape, BF16), jax.ShapeDtypeStruct(w_down.shape, BF16)],
        scratch_shapes=[pltpu.VMEM((s_len, LANES), F32)] * 6
                       + [pltpu.VMEM((len(DILATIONS), s_len, LANES), F32)] * 3
                       + [pltpu.VMEM((2 * ATTN_GROUP, 2 * ATTN_BLOCK, 2 * ATTN_BLOCK), F32),
                          pltpu.VMEM((2 * ATTN_GROUP, 2 * ATTN_BLOCK, 2 * ATTN_BLOCK), BF16)],
        compiler_params=_cparams(("parallel", "arbitrary")),
        name="dilated_attn",
    )(qkv, qkv, qkv, cos_t, sin_t, gq2, gk2, seg, _rotate_half_matrix(), w_up, w_down)


def _split3(x):
    hi = x.astype(BF16)
    r1 = x - hi.astype(F32)
    mid = r1.astype(BF16)
    lo = (r1 - mid.astype(F32)).astype(BF16)
    return hi, mid, lo


def _silu(x):
    return x * (1.0 / (1.0 + jnp.exp(-x)))


def _ssd_kernel(xbc_ref, z_ref, dt_ref, cw_ref, cb_ref, dtb_ref, alog_ref, dskip_ref, g_ref,
                expand_ref, tril_ref, o_ref, xpad, xc, st):
    q = SSD_CHUNK
    d_ssm = z_ref.shape[1]
    d_conv = xbc_ref.shape[1]
    gw = d_ssm // SSM_GROUPS
    heads_per_group = gw // SSM_HEAD_DIM
    pad = 8

    @pl.when(pl.program_id(1) == 0)
    def _():
        xpad[:, 0:pad, :] = jnp.zeros((d_conv // LANES, pad, LANES), F32)
        st[...] = jnp.zeros_like(st)

    for c0 in range(0, d_conv, LANES):
        cols = slice(c0, c0 + LANES)
        slab = c0 // LANES
        xpad[slab, pad:pad + q, :] = xbc_ref[:, cols].astype(F32)
        acc = cb_ref[:, cols] + cw_ref[CONV_WIDTH - 1:CONV_WIDTH, cols] * xpad[slab, pad:pad + q, :]
        for w in range(CONV_WIDTH - 1):
            off = pad - (CONV_WIDTH - 1) + w
            acc = acc + cw_ref[w:w + 1, cols] * xpad[slab, pl.ds(off, q, stride=1), :]
        xc[:, cols] = _silu(acc)
    xpad[:, 0:pad, :] = xpad[:, q:q + pad, :]

    x_dt = dt_ref[...] + dtb_ref[...]
    dt = jnp.maximum(x_dt, 0.0) + jnp.log1p(jnp.exp(-jnp.abs(x_dt)))
    dta = dt * (-jnp.exp(alog_ref[...]))
    tril = tril_ref[...]
    acs = sum(jnp.dot(tril, part, preferred_element_type=F32) for part in _split3(dta))
    acs_t = acs.T
    last = acs[q - 1:q, :]
    expand = expand_ref[...]
    dt_e = jnp.dot(dt.astype(BF16), expand, preferred_element_type=F32)
    dec_e = jnp.dot(jnp.exp(acs).astype(BF16), expand, preferred_element_type=F32)
    w_e = jnp.dot((jnp.exp(last - acs) * dt).astype(BF16), expand, preferred_element_type=F32)

    li = lax.broadcasted_iota(jnp.int32, (q, q), 0)
    si = lax.broadcasted_iota(jnp.int32, (q, q), 1)
    causal = li >= si
    lane_g = lax.broadcasted_iota(jnp.int32, (q, gw), 1)

    for g in range(SSM_GROUPS):
        cols = slice(g * gw, (g + 1) * gw)
        b_f = xc[:, d_ssm + g * SSM_STATE:d_ssm + (g + 1) * SSM_STATE]
        c_b = xc[:, d_ssm + (SSM_GROUPS + g) * SSM_STATE:d_ssm + (SSM_GROUPS + g + 1) * SSM_STATE].astype(BF16)
        xs = xc[:, cols]
        cb = lax.dot_general(c_b, b_f.astype(BF16), (((1,), (1,)), ((), ())), preferred_element_type=F32)
        xdt = (xs * dt_e[:, cols]).astype(BF16)
        ws, rs = [], []
        for hh in range(heads_per_group):
            h = g * heads_per_group + hh
            seg = acs[:, h:h + 1] - acs_t[h:h + 1, :]
            l_mat = jnp.exp(jnp.where(causal, seg, -jnp.inf))
            ws.append((cb * l_mat).astype(BF16))
            in_head = (lane_g >= hh * SSM_HEAD_DIM) & (lane_g < (hh + 1) * SSM_HEAD_DIM)
            rs.append(jnp.where(in_head, xdt, jnp.zeros_like(xdt)))
        y = jnp.dot(jnp.concatenate(ws, axis=1), jnp.concatenate(rs, axis=0), preferred_element_type=F32)
        st_g = st[:, cols]
        y = y + jnp.dot(c_b, st_g.astype(BF16), preferred_element_type=F32) * dec_e[:, cols]
        xw = (xs * w_e[:, cols]).astype(BF16)
        st[:, cols] = st_g * dec_e[q - 1:q, cols] + jnp.dot(b_f.T.astype(BF16), xw, preferred_element_type=F32)
        y = y + dskip_ref[:, cols] * xs
        y = y * _silu(z_ref[:, cols].astype(F32))
        o_ref[:, cols] = _rms_rows(y, g_ref[:, cols]).astype(o_ref.dtype)


def ssd_mixer(packed, dt_raw, conv_w, conv_b, dt_bias, a_log, d_skip, g_out, bsz, s_len, d_ssm, xbc_col, z_col):
    n_heads = d_ssm // SSM_HEAD_DIM
    d_conv = d_ssm + 2 * SSM_GROUPS * SSM_STATE
    nc = s_len // SSD_CHUNK
    padl = lambda v: jnp.pad(v.astype(F32), (0, LANES - n_heads))[None, :]
    expand = (jnp.arange(LANES)[:, None] == (jnp.arange(d_ssm) // SSM_HEAD_DIM)[None, :]).astype(BF16)
    tril = (jnp.arange(SSD_CHUNK)[:, None] >= jnp.arange(SSD_CHUNK)[None, :]).astype(BF16)
    full = lambda shape: pl.BlockSpec(shape, lambda b, c: (0, 0))
    return pl.pallas_call(
        _ssd_kernel,
        grid=(bsz, nc),
        in_specs=[pl.BlockSpec((SSD_CHUNK, d_conv), lambda b, c: (b * nc + c, xbc_col // d_conv)),
                  pl.BlockSpec((SSD_CHUNK, d_ssm), lambda b, c: (b * nc + c, z_col // d_ssm)),
                  pl.BlockSpec((SSD_CHUNK, LANES), lambda b, c: (b * nc + c, 0)),
                  full((CONV_WIDTH, d_conv)), full((1, d_conv)), full((1, LANES)), full((1, LANES)),
                  full((1, d_ssm)), full((1, d_ssm)), full((LANES, d_ssm)), full((SSD_CHUNK, SSD_CHUNK))],
        out_specs=pl.BlockSpec((SSD_CHUNK, d_ssm), lambda b, c: (b * nc + c, 0)),
        out_shape=jax.ShapeDtypeStruct((bsz * s_len, d_ssm), BF16),
        scratch_shapes=[pltpu.VMEM((d_conv // LANES, SSD_CHUNK + 8, LANES), F32),
                        pltpu.VMEM((SSD_CHUNK, d_conv), F32),
                        pltpu.VMEM((SSM_STATE, d_ssm), F32)],
        compiler_params=_cparams(("parallel", "arbitrary")),
        name="ssd",
    )(packed, packed, dt_raw, conv_w.astype(F32), conv_b.astype(F32)[None, :], padl(dt_bias), padl(a_log),
      jnp.repeat(d_skip.astype(F32), SSM_HEAD_DIM)[None, :], g_out.astype(F32)[None, :], expand, tril)


def _out_proj_kernel(attn_ref, ssm_ref, x_ref, g_ref, w_ref, o_ref, wb_ref, an_ref):
    d_attn = attn_ref.shape[1]

    @pl.when(pl.program_id(0) == 0)
    def _():
        wb_ref[...] = w_ref[...].astype(BF16)

    rows = 32

    def body(c, carry):
        rs = pl.ds(pl.multiple_of(c * rows, rows), rows)
        an_ref[rs, :] = _rms_rows(attn_ref[rs, :].astype(F32), g_ref[...]).astype(BF16)
        return carry
    lax.fori_loop(0, attn_ref.shape[0] // rows, body, 0, unroll=4)
    o_ref[...] = (x_ref[...] + jnp.dot(an_ref[...], wb_ref[:d_attn, :], preferred_element_type=F32)
                  + jnp.dot(ssm_ref[...], wb_ref[d_attn:, :], preferred_element_type=F32))


def out_proj(attn, ssm, x2d, g, w, tm=512):
    m, d = x2d.shape
    da, ds_ = attn.shape[1], ssm.shape[1]
    return pl.pallas_call(
        _out_proj_kernel,
        grid=(m // tm,),
        in_specs=[pl.BlockSpec((tm, da), lambda i: (i, 0)),
                  pl.BlockSpec((tm, ds_), lambda i: (i, 0)),
                  pl.BlockSpec((tm, d), lambda i: (i, 0)),
                  pl.BlockSpec((1, da), lambda i: (0, 0)),
                  pl.BlockSpec((da + ds_, d), lambda i: (0, 0), pipeline_mode=pl.Buffered(1))],
        out_specs=pl.BlockSpec((tm, d), lambda i: (i, 0)),
        out_shape=jax.ShapeDtypeStruct((m, d), F32),
        scratch_shapes=[pltpu.VMEM((da + ds_, d), BF16), pltpu.VMEM((tm, da), BF16)],
        compiler_params=_cparams(("arbitrary",)),
        name="out_proj",
    )(attn, ssm, x2d, g, w)


def _mem_kv_kernel(mem_ref, g_ref, w_ref, gk_ref, kv_ref, h_ref):
    d_cross = N_CROSS_HEADS * CROSS_HEAD_DIM
    _norm_rows_to(mem_ref, g_ref, h_ref)
    kv = jnp.dot(h_ref[...], w_ref[...], preferred_element_type=F32)
    for h in range(N_CROSS_HEADS):
        cols = slice(h * CROSS_HEAD_DIM, (h + 1) * CROSS_HEAD_DIM)
        kv_ref[:, cols] = _rms_rows(kv[:, cols], gk_ref[...]).astype(kv_ref.dtype)
    kv_ref[:, d_cross:] = kv[:, d_cross:].astype(kv_ref.dtype)


def mem_kv(mem2d, g, w, gk, tm=256):
    m, d = mem2d.shape
    n = w.shape[1]
    return pl.pallas_call(
        _mem_kv_kernel,
        grid=(m // tm,),
        in_specs=[pl.BlockSpec((tm, d), lambda i: (i, 0)),
                  pl.BlockSpec((1, d), lambda i: (0, 0)),
                  pl.BlockSpec((d, n), lambda i: (0, 0)),
                  pl.BlockSpec((1, CROSS_HEAD_DIM), lambda i: (0, 0))],
        out_specs=pl.BlockSpec((tm, n), lambda i: (i, 0)),
        out_shape=jax.ShapeDtypeStruct((m, n), BF16),
        scratch_shapes=[pltpu.VMEM((tm, d), BF16)],
        compiler_params=_cparams(("parallel",)),
        name="mem_kv",
    )(mem2d, g, w, gk)


def _cross_kernel(x_ref, xn_ref, g_ref, wq_ref, gq_ref, k_ref, v_ref, wo_ref, o_ref, h_ref, q_ref, a_ref, s_scr, p_scr):
    i = pl.program_id(0)
    tm = x_ref.shape[0]
    rows = CROSS_ROWS
    cur, nxt = i % 2, (i + 1) % 2

    @pl.when(i == 0)
    def _():
        _norm_rows_to(x_ref, g_ref, h_ref.at[0])

    q_ref[...] = jnp.dot(h_ref[cur], wq_ref[...], preferred_element_type=F32)

    def scores(item, slot):
        r0, h = item
        cols = slice(h * CROSS_HEAD_DIM, (h + 1) * CROSS_HEAD_DIM)
        qh = (_rms_rows(q_ref[r0:r0 + rows, cols], gq_ref[...]) * (CROSS_HEAD_DIM ** -0.5)).astype(BF16)
        s_scr[slot] = lax.dot_general(qh, k_ref[:, cols], (((1,), (1,)), ((), ())), preferred_element_type=F32)

    def softmax(item, slot):
        s = s_scr[slot]
        p_scr[slot] = jnp.exp(s - jnp.max(s, axis=-1, keepdims=True)).astype(BF16)

    def values(item, slot):
        r0, h = item
        cols = slice(h * CROSS_HEAD_DIM, (h + 1) * CROSS_HEAD_DIM)
        vh = v_ref[:, cols]
        r = jnp.dot(p_scr[slot], jnp.concatenate([vh, jnp.ones_like(vh)], axis=1), preferred_element_type=F32)
        a_ref[r0:r0 + rows, cols] = (r[:, :CROSS_HEAD_DIM] / r[:, CROSS_HEAD_DIM:]).astype(BF16)

    items = [(r0, h) for r0 in range(0, tm, rows) for h in range(N_CROSS_HEADS)]
    _software_pipeline((scores, softmax, values), items, CROSS_GROUP)
    _norm_chunk_to(xn_ref, g_ref, h_ref.at[nxt], 0, tm)
    o_ref[...] = x_ref[...] + jnp.dot(a_ref[...], wo_ref[...], preferred_element_type=F32)


def cross_attention(x2d, g, wq, gq, kv, wo, s_len, n_mem, tm=512):
    m, d = x2d.shape
    dc = wq.shape[1]
    per_batch = s_len // tm
    full = lambda shape: pl.BlockSpec(shape, lambda i: (0, 0))
    return pl.pallas_call(
        _cross_kernel,
        grid=(m // tm,),
        in_specs=[pl.BlockSpec((tm, d), lambda i: (i, 0)),
                  pl.BlockSpec((tm, d), lambda i: (jnp.minimum(i + 1, m // tm - 1), 0)),
                  full((1, d)), full((d, dc)), full((1, CROSS_HEAD_DIM)),
                  pl.BlockSpec((n_mem, dc), lambda i: (i // per_batch, 0)),
                  pl.BlockSpec((n_mem, dc), lambda i: (i // per_batch, 1)),
                  full((dc, d))],
        out_specs=pl.BlockSpec((tm, d), lambda i: (i, 0)),
        out_shape=jax.ShapeDtypeStruct((m, d), F32),
        scratch_shapes=[pltpu.VMEM((2, tm, d), BF16), pltpu.VMEM((tm, dc), F32), pltpu.VMEM((tm, dc), BF16),
                        pltpu.VMEM((2 * CROSS_GROUP, CROSS_ROWS, n_mem), F32),
                        pltpu.VMEM((2 * CROSS_GROUP, CROSS_ROWS, n_mem), BF16)],
        compiler_params=_cparams(("arbitrary",)),
        name="cross_attn",
    )(x2d, x2d, g, wq, gq, kv, kv, wo)


def _mlp_kernel(x_ref, g_ref, wu_ref, wd_ref, o_ref, h_ref):
    @pl.when(pl.program_id(1) == 0)
    def _():
        _norm_rows_to(x_ref, g_ref, h_ref)
        o_ref[...] = x_ref[...]
    u = jnp.dot(h_ref[...], wu_ref[...], preferred_element_type=F32)
    u = jnp.square(jnp.maximum(u, 0.0)).astype(BF16)
    o_ref[...] += jnp.dot(u, wd_ref[...], preferred_element_type=F32)


def mlp(x2d, g, wu, wd, tm=512, tf=1024):
    m, d = x2d.shape
    f = wu.shape[1]
    return pl.pallas_call(
        _mlp_kernel,
        grid=(m // tm, f // tf),
        in_specs=[pl.BlockSpec((tm, d), lambda i, j: (i, 0)),
                  pl.BlockSpec((1, d), lambda i, j: (0, 0)),
                  pl.BlockSpec((d, tf), lambda i, j: (0, j)),
                  pl.BlockSpec((tf, d), lambda i, j: (j, 0))],
        out_specs=pl.BlockSpec((tm, d), lambda i, j: (i, 0)),
        out_shape=jax.ShapeDtypeStruct((m, d), F32),
        scratch_shapes=[pltpu.VMEM((tm, d), BF16)],
        compiler_params=_cparams(("parallel", "arbitrary")),
        name="mlp",
    )(x2d, g, wu, wd)


def _layer(x2d, mem2d, pos_col, bsz, s_len, n_mem, g_mix, w_in, g_q, g_k, g_attn_out, conv_w, conv_b, dt_bias,
           a_log, d_skip, g_ssm_out, w_out, g_cross, g_mem, w_cq, w_ckv, g_cq, g_ck, w_co, g_mlp, w_up, w_down):
    d_model = x2d.shape[1]
    d_attn = d_model // 2
    d_ssm = d_model // 2
    n_pairs = d_attn // LANES
    n_ssm_heads = d_ssm // SSM_HEAD_DIM
    d_conv = d_ssm + 2 * SSM_GROUPS * SSM_STATE
    d_packed = 3 * d_attn + d_ssm + d_conv
    row = lambda v: v.astype(F32)[None, :]

    w_in_t = w_in.T
    w_dt_t = jnp.pad(w_in_t[d_packed:], ((0, LANES - n_ssm_heads), (0, 0)))
    packed, dt_raw = in_proj(x2d, row(g_mix), w_in_t, w_dt_t, d_packed)

    tabs = rope_tables(pos_col)
    attn, w_up_bf, w_down_bf = dilated_attention(
        packed, tabs, row(jnp.tile(g_q, 2)) * (ATTN_HEAD_DIM ** -0.5), row(jnp.tile(g_k, 2)), w_up, w_down,
        bsz, s_len, n_pairs)
    ssm = ssd_mixer(packed, dt_raw, conv_w, conv_b, dt_bias, a_log, d_skip, g_ssm_out, bsz, s_len, d_ssm,
                    xbc_col=3 * d_attn + d_ssm, z_col=3 * d_attn)
    x2d = out_proj(attn, ssm, x2d, row(g_attn_out), w_out)

    kv = mem_kv(mem2d, row(g_mem), w_ckv.astype(BF16), row(g_ck))
    x2d = cross_attention(x2d, row(g_cross), w_cq.astype(BF16), row(g_cq), kv, w_co.astype(BF16), s_len, n_mem)

    return mlp(x2d, row(g_mlp), w_up_bf, w_down_bf)


def kernel(x, mem, positions, g_mix, w_in, g_q, g_k, g_attn_out, conv_w, conv_b, dt_bias, a_log, d_skip, g_ssm_out,
           w_out, g_cross, g_mem, w_cq, w_ckv, g_cq, g_ck, w_co, g_mlp, w_up, w_down):
    bsz, s_len, d_model = x.shape
    n_mem = mem.shape[1]
    x2d = x.reshape(bsz * s_len, d_model)
    mem2d = mem.reshape(bsz * n_mem, d_model)
    pos_col = positions.reshape(bsz * s_len, 1)
    for i in range(g_mix.shape[0]):
        x2d = _layer(x2d, mem2d, pos_col, bsz, s_len, n_mem, g_mix[i], w_in[i], g_q[i], g_k[i], g_attn_out[i],
                     conv_w[i], conv_b[i], dt_bias[i], a_log[i], d_skip[i], g_ssm_out[i], w_out[i], g_cross[i],
                     g_mem[i], w_cq[i], w_ckv[i], g_cq[i], g_ck[i], w_co[i], g_mlp[i], w_up[i], w_down[i])
    return x2d.reshape(bsz, s_len, d_model)
```

```python
import functools
import math

import jax
import jax.numpy as jnp
from jax import lax
from jax.experimental import pallas as pl
from jax.experimental.pallas import tpu as pltpu

F32 = jnp.float32
BF16 = jnp.bfloat16
EPS = 1e-6

LANES = 128
ATTN_HEAD_DIM = 64
ROT_DIM = ATTN_HEAD_DIM // 4
ROPE_THETA = 500000.0
ATTN_BLOCK = 128
DILATIONS = (1, 4, 16)
SSM_HEAD_DIM = 64
SSM_GROUPS = 4
SSM_STATE = 128
CONV_WIDTH = 4
SSD_CHUNK = 128
N_CROSS_HEADS = 4
CROSS_HEAD_DIM = 128
CROSS_ROWS = 256
CROSS_GROUP = 2
VMEM_LIMIT = 52 * 1024 * 1024


def _cparams(sem):
    return pltpu.CompilerParams(dimension_semantics=sem, vmem_limit_bytes=VMEM_LIMIT)


def _rms_rows(x, g):
    ms = jnp.mean(x * x, axis=-1, keepdims=True)
    return x * lax.rsqrt(ms + EPS) * g


def _norm_rows_to(x_ref, g_ref, h_ref, rows=16):
    def body(c, carry):
        r0 = pl.multiple_of(c * rows, rows)
        h_ref[pl.ds(r0, rows), :] = _rms_rows(x_ref[pl.ds(r0, rows), :], g_ref[...]).astype(h_ref.dtype)
        return carry
    lax.fori_loop(0, x_ref.shape[0] // rows, body, 0, unroll=4)


def _norm_chunk_to(x_ref, g_ref, h_ref, start, n_rows, rows=16):
    for r in range(0, n_rows, rows):
        rs = pl.ds(start + r, rows)
        h_ref[rs, :] = _rms_rows(x_ref[rs, :], g_ref[...]).astype(h_ref.dtype)


def _in_proj_kernel(x0_ref, xn_ref, g_ref, w_ref, wdt_ref, o_ref, dt_ref, h_ref):
    i, j = pl.program_id(0), pl.program_id(1)
    tm = xn_ref.shape[0]
    cur, nxt = i % 2, (i + 1) % 2

    @pl.when((i == 0) & (j == 0))
    def _():
        _norm_rows_to(x0_ref, g_ref, h_ref.at[0])

    @pl.when(j == 0)
    def _():
        dt_ref[...] = lax.dot_general(h_ref[cur], wdt_ref[...].astype(BF16), (((1,), (1,)), ((), ())),
                                      preferred_element_type=F32)
    o_ref[...] = lax.dot_general(h_ref[cur], w_ref[...].astype(BF16), (((1,), (1,)), ((), ())),
                                 preferred_element_type=F32).astype(o_ref.dtype)
    chunk = tm // IN_PROJ_NORM_STEPS
    start = pl.multiple_of(jnp.minimum(j, IN_PROJ_NORM_STEPS - 1) * chunk, chunk)
    _norm_chunk_to(xn_ref, g_ref, h_ref.at[nxt], start, chunk)


IN_PROJ_NORM_STEPS = 8


def in_proj(x2d, g, w_t, w_dt, n, tm=1024, tn=512):
    m, d = x2d.shape
    n_i = m // tm
    assert n // tn >= IN_PROJ_NORM_STEPS
    return pl.pallas_call(
        _in_proj_kernel,
        grid=(n_i, n // tn),
        in_specs=[pl.BlockSpec((tm, d), lambda i, j: (0, 0), pipeline_mode=pl.Buffered(1)),
                  pl.BlockSpec((tm, d), lambda i, j: (jnp.minimum(i + 1, n_i - 1), 0)),
                  pl.BlockSpec((1, d), lambda i, j: (0, 0)),
                  pl.BlockSpec((tn, d), lambda i, j: (j, 0)),
                  pl.BlockSpec((LANES, d), lambda i, j: (0, 0))],
        out_specs=[pl.BlockSpec((tm, tn), lambda i, j: (i, j)),
                   pl.BlockSpec((tm, LANES), lambda i, j: (i, 0))],
        out_shape=[jax.ShapeDtypeStruct((m, n), BF16), jax.ShapeDtypeStruct((m, LANES), F32)],
        scratch_shapes=[pltpu.VMEM((2, tm, d), BF16)],
        compiler_params=_cparams(("arbitrary", "arbitrary")),
        name="in_proj",
    )(x2d, x2d, g, w_t, w_dt)


def _rope_kernel(pos_ref, expo_ref, cmask_ref, cos_ref, sin_ref):
    inv_freq = jnp.power(jnp.float32(ROPE_THETA), expo_ref[...])
    ang = pos_ref[...].astype(F32) * inv_freq
    cm = cmask_ref[...]
    cos_ref[...] = jnp.cos(ang) * cm + (1.0 - cm)
    sin_ref[...] = jnp.sin(ang) * cm


def rope_tables(pos_col, tm=512):
    m = pos_col.shape[0]
    half = ROT_DIM // 2
    d = jnp.arange(LANES) % ATTN_HEAD_DIM
    expo = (-2.0 * (d % half).astype(F32) / ROT_DIM)[None, :]
    cmask = (d < ROT_DIM).astype(F32)[None, :]
    row = pl.BlockSpec((1, LANES), lambda i: (0, 0))
    tab = pl.BlockSpec((tm, LANES), lambda i: (i, 0))
    return pl.pallas_call(
        _rope_kernel,
        grid=(m // tm,),
        in_specs=[pl.BlockSpec((tm, 1), lambda i: (i, 0)), row, row],
        out_specs=[tab, tab],
        out_shape=[jax.ShapeDtypeStruct((m, LANES), F32)] * 2,
        compiler_params=_cparams(("parallel",)),
        name="rope_tab",
    )(pos_col, expo, cmask)


def _rotate_half_matrix():
    half = ROT_DIM // 2
    src = jnp.arange(LANES)[:, None]
    dst = jnp.arange(LANES)[None, :]
    d = dst % ATTN_HEAD_DIM
    first = (d < half) & (src == dst + half)
    second = (d >= half) & (d < ROT_DIM) & (src == dst - half)
    return (second.astype(F32) - first.astype(F32)).astype(BF16)


def _rows(start, size, dil):
    return pl.ds(start, size) if dil == 1 else pl.ds(start, size, stride=dil)


ATTN_GROUP = 2


def _software_pipeline(stages, items, group):
    groups = [items[i:i + group] for i in range(0, len(items), group)]
    for t in range(len(groups) + len(stages) - 1):
        for lag, stage in enumerate(stages):
            if 0 <= t - lag < len(groups):
                for g, item in enumerate(groups[t - lag]):
                    stage(item, ((t - lag) % 2) * group + g)


def _attn_kernel(q_ref, k_ref, v_ref, cos_ref, sin_ref, gq_ref, gk_ref, seg_ref, rot_ref, wu_ref, wd_ref,
                 o_ref, wu_out, wd_out, qf, kf, vf, qd, kd, vd, ob, mb, lb, s_scr, p_scr):
    s_len = q_ref.shape[0]
    blk = ATTN_BLOCK
    lane = lax.broadcasted_iota(jnp.int32, (blk, LANES), 1)
    head0 = lane < ATTN_HEAD_DIM
    qi = lax.broadcasted_iota(jnp.int32, (2 * blk, 2 * blk), 0) % blk
    kj = lax.broadcasted_iota(jnp.int32, (2 * blk, 2 * blk), 1)
    band_mask = (kj >= qi) & (kj <= qi + blk)
    first_mask = (lax.broadcasted_iota(jnp.int32, (2 * blk, blk), 1)
                  <= lax.broadcasted_iota(jnp.int32, (2 * blk, blk), 0) % blk)

    wu_out[...] = wu_ref[...].astype(BF16)
    wd_out[...] = wd_ref[...].astype(BF16)

    prep_rows = 256

    def prep(c, carry):
        rows = pl.ds(pl.multiple_of(c * prep_rows, prep_rows), prep_rows)
        cs, sn = cos_ref[rows, :], sin_ref[rows, :]

        def norm_rope(x_ref, g_ref):
            x = x_ref[rows, :].astype(F32)
            ss = jnp.dot((x * x).astype(BF16), seg_ref[...], preferred_element_type=F32)
            y = x * lax.rsqrt(ss * (1.0 / ATTN_HEAD_DIM) + EPS) * g_ref[...]
            return y * cs + jnp.dot(y.astype(BF16), rot_ref[...], preferred_element_type=F32) * sn

        qf[rows, :] = norm_rope(q_ref, gq_ref)
        kf[rows, :] = norm_rope(k_ref, gk_ref)
        vf[rows, :] = v_ref[rows, :].astype(F32)
        return carry

    lax.fori_loop(0, s_len // prep_rows, prep, 0, unroll=2)

    dmid = DILATIONS[1]
    run = s_len // dmid

    def deinterleave(r, carry):
        dst_rows = pl.ds(pl.multiple_of(r * run, run), run)
        for src, dst in ((qf, qd), (kf, kd), (vf, vd)):
            dst[dst_rows, :] = src[pl.ds(r, run, stride=dmid), :]
        return carry

    lax.fori_loop(0, dmid, deinterleave, 0)

    natural, deint = (qf, kf, vf), (qd, kd, vd)
    descs = [(natural, 0, 1, 0, 0, blk, first_mask)]
    descs += [(natural, 0, 1, n * blk, (n - 1) * blk, 2 * blk, band_mask) for n in range(1, s_len // blk)]
    descs += [(deint, 1, 1, r * run, r * run, blk, first_mask) for r in range(dmid)]
    descs += [(deint, 1, 1, r * run + n * blk, r * run + (n - 1) * blk, 2 * blk, band_mask)
              for n in range(1, run // blk) for r in range(dmid)]
    descs += [(deint, 2, dmid, r_hi * run + r_lo, r_hi * run + r_lo, blk, first_mask)
              for r_hi in range(dmid) for r_lo in range(dmid)]

    def scores(desc, slot):
        (q_src, k_src, _), _, dil, q_start, k_start, n_keys, _ = desc
        qt = q_src[_rows(q_start, blk, dil), :]
        zero = jnp.zeros_like(qt)
        q2 = jnp.concatenate([jnp.where(head0, qt, zero), jnp.where(head0, zero, qt)], axis=0).astype(BF16)
        kb = k_src[_rows(k_start, n_keys, dil), :].astype(BF16)
        s_scr[slot, :, :n_keys] = lax.dot_general(q2, kb, (((1,), (1,)), ((), ())), preferred_element_type=F32)

    def softmax(desc, slot):
        _, br, dil, q_start, _, n_keys, mask = desc
        s = jnp.where(mask, s_scr[slot, :, :n_keys], -jnp.inf)
        m = jnp.max(s, axis=-1, keepdims=True)
        p_scr[slot, :, :n_keys] = jnp.exp(s - m).astype(BF16)
        mb[br, _rows(q_start, blk, dil), :] = jnp.where(head0, m[:blk], m[blk:])

    def values(desc, slot):
        (_, _, v_src), br, dil, q_start, k_start, n_keys, _ = desc
        vb = v_src[_rows(k_start, n_keys, dil), :].astype(BF16)
        v1 = jnp.concatenate([vb, jnp.ones_like(vb)], axis=1)
        r = jnp.dot(p_scr[slot, :, :n_keys], v1, preferred_element_type=F32)
        out_rows = _rows(q_start, blk, dil)
        ob[br, out_rows, :] = jnp.where(head0, r[:blk, :LANES], r[blk:, :LANES])
        lb[br, out_rows, :] = jnp.where(head0, r[:blk, LANES:], r[blk:, LANES:])

    _software_pipeline((scores, softmax, values), descs, ATTN_GROUP)

    def merge(c, carry):
        d_start = pl.multiple_of(c * prep_rows, prep_rows)
        r = d_start // run
        nat_rows = pl.ds(dmid * (d_start - r * run) + r, prep_rows, stride=dmid)
        d_rows = pl.ds(d_start, prep_rows)
        rows = (nat_rows, d_rows, d_rows)
        ms = [mb[g, rows[g], :] for g in range(len(DILATIONS))]
        m = functools.reduce(jnp.maximum, ms)
        ws = [jnp.exp(mg - m) for mg in ms]
        num = functools.reduce(jnp.add, [w * ob[g, rows[g], :] for g, w in enumerate(ws)])
        den = functools.reduce(jnp.add, [w * lb[g, rows[g], :] for g, w in enumerate(ws)])
        ob[0, nat_rows, :] = num / den
        return carry

    lax.fori_loop(0, s_len // prep_rows, merge, 0)

    def emit(c, carry):
        rows = pl.ds(pl.multiple_of(c * prep_rows, prep_rows), prep_rows)
        o_ref[rows, :] = ob[0, rows, :].astype(o_ref.dtype)
        return carry

    lax.fori_loop(0, s_len // prep_rows, emit, 0)


def dilated_attention(qkv, tabs, gq2, gk2, w_up, w_down, bsz, s_len, n_pairs):
    cos_t, sin_t = tabs
    d1, d2, d3 = DILATIONS
    assert d1 == 1 and d3 == d2 * d2 and s_len == d3 * ATTN_BLOCK, "layout assumes dilations (1, d, d*d), one block per largest class"
    lane = jnp.arange(LANES)
    seg = (lane[:, None] // ATTN_HEAD_DIM == lane[None, :] // ATTN_HEAD_DIM).astype(BF16)
    n_steps = bsz * n_pairs
    wu_rows, wd_rows = w_up.shape[0] // n_steps, w_down.shape[0] // n_steps
    blk = lambda off: pl.BlockSpec((s_len, LANES), lambda b, p, off=off: (b, off + p))
    tab = pl.BlockSpec((s_len, LANES), lambda b, p: (b, 0))
    row = pl.BlockSpec((1, LANES), lambda b, p: (0, 0))
    sq = pl.BlockSpec((LANES, LANES), lambda b, p: (0, 0))
    wu_spec = pl.BlockSpec((wu_rows, w_up.shape[1]), lambda b, p: (b * n_pairs + p, 0))
    wd_spec = pl.BlockSpec((wd_rows, w_down.shape[1]), lambda b, p: (b * n_pairs + p, 0))
    return pl.pallas_call(
        _attn_kernel,
        grid=(bsz, n_pairs),
        in_specs=[blk(0), blk(n_pairs), blk(2 * n_pairs), tab, tab, row, row, sq, sq, wu_spec, wd_spec],
        out_specs=[pl.BlockSpec((s_len, LANES), lambda b, p: (b, p)), wu_spec, wd_spec],
        out_shape=[jax.ShapeDtypeStruct((bsz * s_len, n_pairs * LANES), BF16),
                   jax.ShapeDtypeStruct(w_up.shape, BF16), jax.ShapeDtypeStruct(w_down.shape, BF16)],
        scratch_shapes=[pltpu.VMEM((s_len, LANES), F32)] * 6
                       + [pltpu.VMEM((len(DILATIONS), s_len, LANES), F32)] * 3
                       + [pltpu.VMEM((2 * ATTN_GROUP, 2 * ATTN_BLOCK, 2 * ATTN_BLOCK), F32),
                          pltpu.VMEM((2 * ATTN_GROUP, 2 * ATTN_BLOCK, 2 * ATTN_BLOCK), BF16)],
        compiler_params=_cparams(("parallel", "arbitrary")),
        name="dilated_attn",
    )(qkv, qkv, qkv, cos_t, sin_t, gq2, gk2, seg, _rotate_half_matrix(), w_up, w_down)


def _split3(x):
    hi = x.astype(BF16)
    r1 = x - hi.astype(F32)
    mid = r1.astype(BF16)
    lo = (r1 - mid.astype(F32)).astype(BF16)
    return hi, mid, lo


def _silu(x):
    h = 0.5 * x
    return h * jnp.tanh(h) + h


LOG2E = 1.4426950408889634
SSD_STEP_CHUNKS = 4


def _ssd_kernel(xbc_ref, z_ref, dt_ref, cw_ref, cb_ref, dtb_ref, alog_ref, dskip_ref, g_ref,
                expand_ref, tril_ref, o_ref, xpad, xc, st):
    q = SSD_CHUNK
    n_rows = xbc_ref.shape[0]
    d_ssm = z_ref.shape[1]
    d_conv = xbc_ref.shape[1]
    gw = d_ssm // SSM_GROUPS
    heads_per_group = gw // SSM_HEAD_DIM
    pad = 8

    @pl.when(pl.program_id(1) == 0)
    def _():
        xpad[:, 0:pad, :] = jnp.zeros((d_conv // LANES, pad, LANES), F32)
        st[...] = jnp.zeros_like(st)

    for c0 in range(0, d_conv, LANES):
        cols = slice(c0, c0 + LANES)
        slab = c0 // LANES
        xpad[slab, pad:pad + n_rows, :] = xbc_ref[:, cols].astype(F32)
        acc = cb_ref[:, cols] + cw_ref[CONV_WIDTH - 1:CONV_WIDTH, cols] * xpad[slab, pad:pad + n_rows, :]
        for w in range(CONV_WIDTH - 1):
            off = pad - (CONV_WIDTH - 1) + w
            acc = acc + cw_ref[w:w + 1, cols] * xpad[slab, pl.ds(off, n_rows, stride=1), :]
        xc[:, cols] = _silu(acc)
    xpad[:, 0:pad, :] = xpad[:, n_rows:n_rows + pad, :]

    li = lax.broadcasted_iota(jnp.int32, (q, q), 0)
    si = lax.broadcasted_iota(jnp.int32, (q, q), 1)
    causal = li >= si
    lane_g = lax.broadcasted_iota(jnp.int32, (q, gw), 1)
    tril = tril_ref[...]
    expand = expand_ref[...]

    for r0 in range(0, n_rows, q):
        rows = slice(r0, r0 + q)
        x_dt = dt_ref[rows, :] + dtb_ref[...]
        dt = jnp.maximum(x_dt, 0.0) + jnp.log1p(jnp.exp(-jnp.abs(x_dt)))
        dta = dt * (-LOG2E * jnp.exp(alog_ref[...]))
        acs = sum(jnp.dot(tril, part, preferred_element_type=F32) for part in _split3(dta))
        acs_t = acs.T
        last = acs[q - 1:q, :]
        dt_e = jnp.dot(dt.astype(BF16), expand, preferred_element_type=F32)
        dec_e = jnp.dot(jnp.exp2(acs).astype(BF16), expand, preferred_element_type=F32)
        w_e = jnp.dot((jnp.exp2(last - acs) * dt).astype(BF16), expand, preferred_element_type=F32)

        for g in range(SSM_GROUPS):
            cols = slice(g * gw, (g + 1) * gw)
            b_f = xc[rows, d_ssm + g * SSM_STATE:d_ssm + (g + 1) * SSM_STATE]
            c_b = xc[rows, d_ssm + (SSM_GROUPS + g) * SSM_STATE:d_ssm + (SSM_GROUPS + g + 1) * SSM_STATE].astype(BF16)
            xs = xc[rows, cols]
            cb = lax.dot_general(c_b, b_f.astype(BF16), (((1,), (1,)), ((), ())), preferred_element_type=F32)
            xdt = (xs * dt_e[:, cols]).astype(BF16)
            ws, rs = [], []
            for hh in range(heads_per_group):
                h = g * heads_per_group + hh
                seg = acs[:, h:h + 1] - acs_t[h:h + 1, :]
                l_mat = jnp.exp2(jnp.where(causal, seg, -jnp.inf))
                ws.append((cb * l_mat).astype(BF16))
                in_head = (lane_g >= hh * SSM_HEAD_DIM) & (lane_g < (hh + 1) * SSM_HEAD_DIM)
                rs.append(jnp.where(in_head, xdt, jnp.zeros_like(xdt)))
            y = jnp.dot(jnp.concatenate(ws, axis=1), jnp.concatenate(rs, axis=0), preferred_element_type=F32)
            st_g = st[:, cols]
            y = y + jnp.dot(c_b, st_g.astype(BF16), preferred_element_type=F32) * dec_e[:, cols]
            xw = (xs * w_e[:, cols]).astype(BF16)
            st[:, cols] = st_g * dec_e[q - 1:q, cols] + jnp.dot(b_f.T.astype(BF16), xw, preferred_element_type=F32)
            y = y + dskip_ref[:, cols] * xs
            y = y * _silu(z_ref[rows, cols].astype(F32))
            o_ref[rows, cols] = _rms_rows(y, g_ref[:, cols]).astype(o_ref.dtype)


def ssd_mixer(packed, dt_raw, conv_w, conv_b, dt_bias, a_log, d_skip, g_out, bsz, s_len, d_ssm, xbc_col, z_col):
    n_heads = d_ssm // SSM_HEAD_DIM
    d_conv = d_ssm + 2 * SSM_GROUPS * SSM_STATE
    step_rows = SSD_STEP_CHUNKS * SSD_CHUNK
    nc = s_len // step_rows
    padl = lambda v: jnp.pad(v.astype(F32), (0, LANES - n_heads))[None, :]
    expand = (jnp.arange(LANES)[:, None] == (jnp.arange(d_ssm) // SSM_HEAD_DIM)[None, :]).astype(BF16)
    tril = (jnp.arange(SSD_CHUNK)[:, None] >= jnp.arange(SSD_CHUNK)[None, :]).astype(BF16)
    full = lambda shape: pl.BlockSpec(shape, lambda b, c: (0, 0))
    return pl.pallas_call(
        _ssd_kernel,
        grid=(bsz, nc),
        in_specs=[pl.BlockSpec((step_rows, d_conv), lambda b, c: (b * nc + c, xbc_col // d_conv)),
                  pl.BlockSpec((step_rows, d_ssm), lambda b, c: (b * nc + c, z_col // d_ssm)),
                  pl.BlockSpec((step_rows, LANES), lambda b, c: (b * nc + c, 0)),
                  full((CONV_WIDTH, d_conv)), full((1, d_conv)), full((1, LANES)), full((1, LANES)),
                  full((1, d_ssm)), full((1, d_ssm)), full((LANES, d_ssm)), full((SSD_CHUNK, SSD_CHUNK))],
        out_specs=pl.BlockSpec((step_rows, d_ssm), lambda b, c: (b * nc + c, 0)),
        out_shape=jax.ShapeDtypeStruct((bsz * s_len, d_ssm), BF16),
        scratch_shapes=[pltpu.VMEM((d_conv // LANES, step_rows + 8, LANES), F32),
                        pltpu.VMEM((step_rows, d_conv), F32),
                        pltpu.VMEM((SSM_STATE, d_ssm), F32)],
        compiler_params=_cparams(("parallel", "arbitrary")),
        name="ssd",
    )(packed, packed, dt_raw, conv_w.astype(F32), conv_b.astype(F32)[None, :], padl(dt_bias), padl(a_log),
      jnp.repeat(d_skip.astype(F32), SSM_HEAD_DIM)[None, :], g_out.astype(F32)[None, :], expand, tril)


def _out_proj_kernel(attn_ref, ssm_ref, x_ref, g_ref, w_ref, o_ref, wb_ref, an_ref):
    d_attn = attn_ref.shape[1]

    @pl.when(pl.program_id(0) == 0)
    def _():
        wb_ref[...] = w_ref[...].astype(BF16)

    rows = 32

    def body(c, carry):
        rs = pl.ds(pl.multiple_of(c * rows, rows), rows)
        an_ref[rs, :] = _rms_rows(attn_ref[rs, :].astype(F32), g_ref[...]).astype(BF16)
        return carry
    lax.fori_loop(0, attn_ref.shape[0] // rows, body, 0, unroll=4)
    o_ref[...] = (x_ref[...] + jnp.dot(an_ref[...], wb_ref[:d_attn, :], preferred_element_type=F32)
                  + jnp.dot(ssm_ref[...], wb_ref[d_attn:, :], preferred_element_type=F32))


def out_proj(attn, ssm, x2d, g, w, tm=512):
    m, d = x2d.shape
    da, ds_ = attn.shape[1], ssm.shape[1]
    return pl.pallas_call(
        _out_proj_kernel,
        grid=(m // tm,),
        in_specs=[pl.BlockSpec((tm, da), lambda i: (i, 0)),
                  pl.BlockSpec((tm, ds_), lambda i: (i, 0)),
                  pl.BlockSpec((tm, d), lambda i: (i, 0)),
                  pl.BlockSpec((1, da), lambda i: (0, 0)),
                  pl.BlockSpec((da + ds_, d), lambda i: (0, 0), pipeline_mode=pl.Buffered(1))],
        out_specs=pl.BlockSpec((tm, d), lambda i: (i, 0)),
        out_shape=jax.ShapeDtypeStruct((m, d), F32),
        scratch_shapes=[pltpu.VMEM((da + ds_, d), BF16), pltpu.VMEM((tm, da), BF16)],
        compiler_params=_cparams(("arbitrary",)),
        name="out_proj",
    )(attn, ssm, x2d, g, w)


def _mem_kv_kernel(mem_ref, g_ref, w_ref, gk_ref, kv_ref, h_ref):
    d_cross = N_CROSS_HEADS * CROSS_HEAD_DIM
    _norm_rows_to(mem_ref, g_ref, h_ref)
    kv = jnp.dot(h_ref[...], w_ref[...], preferred_element_type=F32)
    for h in range(N_CROSS_HEADS):
        cols = slice(h * CROSS_HEAD_DIM, (h + 1) * CROSS_HEAD_DIM)
        kv_ref[:, cols] = _rms_rows(kv[:, cols], gk_ref[...]).astype(kv_ref.dtype)
    kv_ref[:, d_cross:] = kv[:, d_cross:].astype(kv_ref.dtype)


def mem_kv(mem2d, g, w, gk, tm=256):
    m, d = mem2d.shape
    n = w.shape[1]
    return pl.pallas_call(
        _mem_kv_kernel,
        grid=(m // tm,),
        in_specs=[pl.BlockSpec((tm, d), lambda i: (i, 0)),
                  pl.BlockSpec((1, d), lambda i: (0, 0)),
                  pl.BlockSpec((d, n), lambda i: (0, 0)),
                  pl.BlockSpec((1, CROSS_HEAD_DIM), lambda i: (0, 0))],
        out_specs=pl.BlockSpec((tm, n), lambda i: (i, 0)),
        out_shape=jax.ShapeDtypeStruct((m, n), BF16),
        scratch_shapes=[pltpu.VMEM((tm, d), BF16)],
        compiler_params=_cparams(("parallel",)),
        name="mem_kv",
    )(mem2d, g, w, gk)


def _cross_kernel(x_ref, xn_ref, g_ref, wq_ref, gq_ref, k_ref, v_ref, wo_ref, o_ref, h_ref, q_ref, a_ref, s_scr, p_scr):
    i = pl.program_id(0)
    tm = x_ref.shape[0]
    rows = CROSS_ROWS
    cur, nxt = i % 2, (i + 1) % 2

    @pl.when(i == 0)
    def _():
        _norm_rows_to(x_ref, g_ref, h_ref.at[0])

    q_ref[...] = jnp.dot(h_ref[cur], wq_ref[...], preferred_element_type=F32)

    def scores(item, slot):
        r0, h = item
        cols = slice(h * CROSS_HEAD_DIM, (h + 1) * CROSS_HEAD_DIM)
        qh = (_rms_rows(q_ref[r0:r0 + rows, cols], gq_ref[...]) * (CROSS_HEAD_DIM ** -0.5)).astype(BF16)
        s_scr[slot] = lax.dot_general(qh, k_ref[:, cols], (((1,), (1,)), ((), ())), preferred_element_type=F32)

    def softmax(item, slot):
        s = s_scr[slot]
        p_scr[slot] = jnp.exp(s - jnp.max(s, axis=-1, keepdims=True)).astype(BF16)

    def values(item, slot):
        r0, h = item
        cols = slice(h * CROSS_HEAD_DIM, (h + 1) * CROSS_HEAD_DIM)
        vh = v_ref[:, cols]
        r = jnp.dot(p_scr[slot], jnp.concatenate([vh, jnp.ones_like(vh)], axis=1), preferred_element_type=F32)
        a_ref[r0:r0 + rows, cols] = (r[:, :CROSS_HEAD_DIM] / r[:, CROSS_HEAD_DIM:]).astype(BF16)

    items = [(r0, h) for r0 in range(0, tm, rows) for h in range(N_CROSS_HEADS)]
    _software_pipeline((scores, softmax, values), items, CROSS_GROUP)
    _norm_chunk_to(xn_ref, g_ref, h_ref.at[nxt], 0, tm)
    o_ref[...] = x_ref[...] + jnp.dot(a_ref[...], wo_ref[...], preferred_element_type=F32)


def cross_attention(x2d, g, wq, gq, kv, wo, s_len, n_mem, tm=512):
    m, d = x2d.shape
    dc = wq.shape[1]
    per_batch = s_len // tm
    full = lambda shape: pl.BlockSpec(shape, lambda i: (0, 0))
    return pl.pallas_call(
        _cross_kernel,
        grid=(m // tm,),
        in_specs=[pl.BlockSpec((tm, d), lambda i: (i, 0)),
                  pl.BlockSpec((tm, d), lambda i: (jnp.minimum(i + 1, m // tm - 1), 0)),
                  full((1, d)), full((d, dc)), full((1, CROSS_HEAD_DIM)),
                  pl.BlockSpec((n_mem, dc), lambda i: (i // per_batch, 0)),
                  pl.BlockSpec((n_mem, dc), lambda i: (i // per_batch, 1)),
                  full((dc, d))],
        out_specs=pl.BlockSpec((tm, d), lambda i: (i, 0)),
        out_shape=jax.ShapeDtypeStruct((m, d), F32),
        scratch_shapes=[pltpu.VMEM((2, tm, d), BF16), pltpu.VMEM((tm, dc), F32), pltpu.VMEM((tm, dc), BF16),
                        pltpu.VMEM((2 * CROSS_GROUP, CROSS_ROWS, n_mem), F32),
                        pltpu.VMEM((2 * CROSS_GROUP, CROSS_ROWS, n_mem), BF16)],
        compiler_params=_cparams(("arbitrary",)),
        name="cross_attn",
    )(x2d, x2d, g, wq, gq, kv, kv, wo)


def _mlp_kernel(x_ref, g_ref, wu_ref, wd_ref, o_ref, h_ref):
    @pl.when(pl.program_id(1) == 0)
    def _():
        _norm_rows_to(x_ref, g_ref, h_ref)
        o_ref[...] = x_ref[...]
    u = jnp.dot(h_ref[...], wu_ref[...], preferred_element_type=F32)
    u = jnp.square(jnp.maximum(u, 0.0)).astype(BF16)
    o_ref[...] += jnp.dot(u, wd_ref[...], preferred_element_type=F32)


def mlp(x2d, g, wu, wd, tm=512, tf=1024):
    m, d = x2d.shape
    f = wu.shape[1]
    return pl.pallas_call(
        _mlp_kernel,
        grid=(m // tm, f // tf),
        in_specs=[pl.BlockSpec((tm, d), lambda i, j: (i, 0)),
                  pl.BlockSpec((1, d), lambda i, j: (0, 0)),
                  pl.BlockSpec((d, tf), lambda i, j: (0, j)),
                  pl.BlockSpec((tf, d), lambda i, j: (j, 0))],
        out_specs=pl.BlockSpec((tm, d), lambda i, j: (i, 0)),
        out_shape=jax.ShapeDtypeStruct((m, d), F32),
        scratch_shapes=[pltpu.VMEM((tm, d), BF16)],
        compiler_params=_cparams(("parallel", "arbitrary")),
        name="mlp",
    )(x2d, g, wu, wd)


def _layer(x2d, mem2d, pos_col, bsz, s_len, n_mem, g_mix, w_in, g_q, g_k, g_attn_out, conv_w, conv_b, dt_bias,
           a_log, d_skip, g_ssm_out, w_out, g_cross, g_mem, w_cq, w_ckv, g_cq, g_ck, w_co, g_mlp, w_up, w_down):
    d_model = x2d.shape[1]
    d_attn = d_model // 2
    d_ssm = d_model // 2
    n_pairs = d_attn // LANES
    n_ssm_heads = d_ssm // SSM_HEAD_DIM
    d_conv = d_ssm + 2 * SSM_GROUPS * SSM_STATE
    d_packed = 3 * d_attn + d_ssm + d_conv
    row = lambda v: v.astype(F32)[None, :]

    w_in_t = w_in.T
    w_dt_t = jnp.pad(w_in_t[d_packed:], ((0, LANES - n_ssm_heads), (0, 0)))
    packed, dt_raw = in_proj(x2d, row(g_mix), w_in_t, w_dt_t, d_packed)

    tabs = rope_tables(pos_col)
    attn, w_up_bf, w_down_bf = dilated_attention(
        packed, tabs, row(jnp.tile(g_q, 2)) * (ATTN_HEAD_DIM ** -0.5), row(jnp.tile(g_k, 2)), w_up, w_down,
        bsz, s_len, n_pairs)
    ssm = ssd_mixer(packed, dt_raw, conv_w, conv_b, dt_bias, a_log, d_skip, g_ssm_out, bsz, s_len, d_ssm,
                    xbc_col=3 * d_attn + d_ssm, z_col=3 * d_attn)
    x2d = out_proj(attn, ssm, x2d, row(g_attn_out), w_out)

    kv = mem_kv(mem2d, row(g_mem), w_ckv.astype(BF16), row(g_ck))
    x2d = cross_attention(x2d, row(g_cross), w_cq.astype(BF16), row(g_cq), kv, w_co.astype(BF16), s_len, n_mem)

    return mlp(x2d, row(g_mlp), w_up_bf, w_down_bf)


def kernel(x, mem, positions, g_mix, w_in, g_q, g_k, g_attn_out, conv_w, conv_b, dt_bias, a_log, d_skip, g_ssm_out,
           w_out, g_cross, g_mem, w_cq, w_ckv, g_cq, g_ck, w_co, g_mlp, w_up, w_down):
    bsz, s_len, d_model = x.shape
    n_mem = mem.shape[1]
    x2d = x.reshape(bsz * s_len, d_model)
    mem2d = mem.reshape(bsz * n_mem, d_model)
    pos_col = positions.reshape(bsz * s_len, 1)
    for i in range(g_mix.shape[0]):
        x2d = _layer(x2d, mem2d, pos_col, bsz, s_len, n_mem, g_mix[i], w_in[i], g_q[i], g_k[i], g_attn_out[i],
                     conv_w[i], conv_b[i], dt_bias[i], a_log[i], d_skip[i], g_ssm_out[i], w_out[i], g_cross[i],
                     g_mem[i], w_cq[i], w_ckv[i], g_cq[i], g_ck[i], w_co[i], g_mlp[i], w_up[i], w_down[i])
    return x2d.reshape(bsz, s_len, d_model)
```

```python
import functools
import math

import jax
import jax.numpy as jnp
from jax import lax
from jax.experimental import pallas as pl
from jax.experimental.pallas import tpu as pltpu

F32 = jnp.float32
BF16 = jnp.bfloat16
EPS = 1e-6

LANES = 128
ATTN_HEAD_DIM = 64
ROT_DIM = ATTN_HEAD_DIM // 4
ROPE_THETA = 500000.0
ATTN_BLOCK = 128
DILATIONS = (1, 4, 16)
SSM_HEAD_DIM = 64
SSM_GROUPS = 4
SSM_STATE = 128
CONV_WIDTH = 4
SSD_CHUNK = 128
N_CROSS_HEADS = 4
CROSS_HEAD_DIM = 128
CROSS_ROWS = 256
CROSS_GROUP = 2
VMEM_LIMIT = 52 * 1024 * 1024


def _cparams(sem):
    return pltpu.CompilerParams(dimension_semantics=sem, vmem_limit_bytes=VMEM_LIMIT)


def _rms_rows(x, g):
    ms = jnp.mean(x * x, axis=-1, keepdims=True)
    return x * lax.rsqrt(ms + EPS) * g


def _norm_rows_to(x_ref, g_ref, h_ref, rows=16):
    def body(c, carry):
        r0 = pl.multiple_of(c * rows, rows)
        h_ref[pl.ds(r0, rows), :] = _rms_rows(x_ref[pl.ds(r0, rows), :], g_ref[...]).astype(h_ref.dtype)
        return carry
    lax.fori_loop(0, x_ref.shape[0] // rows, body, 0, unroll=4)


def _norm_chunk_to(x_ref, g_ref, h_ref, start, n_rows, rows=16):
    for r in range(0, n_rows, rows):
        rs = pl.ds(start + r, rows)
        h_ref[rs, :] = _rms_rows(x_ref[rs, :], g_ref[...]).astype(h_ref.dtype)


def _in_proj_kernel(x0_ref, xn_ref, g_ref, w_ref, wdt_ref, o_ref, dt_ref, h_ref):
    i, j = pl.program_id(0), pl.program_id(1)
    tm = xn_ref.shape[0]
    cur, nxt = i % 2, (i + 1) % 2

    @pl.when((i == 0) & (j == 0))
    def _():
        _norm_rows_to(x0_ref, g_ref, h_ref.at[0])

    @pl.when(j == 0)
    def _():
        dt_ref[...] = lax.dot_general(h_ref[cur], wdt_ref[...].astype(BF16), (((1,), (1,)), ((), ())),
                                      preferred_element_type=F32)
    o_ref[...] = lax.dot_general(h_ref[cur], w_ref[...].astype(BF16), (((1,), (1,)), ((), ())),
                                 preferred_element_type=F32).astype(o_ref.dtype)
    chunk = tm // IN_PROJ_NORM_STEPS
    start = pl.multiple_of(jnp.minimum(j, IN_PROJ_NORM_STEPS - 1) * chunk, chunk)
    _norm_chunk_to(xn_ref, g_ref, h_ref.at[nxt], start, chunk)


IN_PROJ_NORM_STEPS = 8


def in_proj(x2d, g, w_t, w_dt, n, tm=1024, tn=512):
    m, d = x2d.shape
    n_i = m // tm
    assert n // tn >= IN_PROJ_NORM_STEPS
    return pl.pallas_call(
        _in_proj_kernel,
        grid=(n_i, n // tn),
        in_specs=[pl.BlockSpec((tm, d), lambda i, j: (0, 0), pipeline_mode=pl.Buffered(1)),
                  pl.BlockSpec((tm, d), lambda i, j: (jnp.minimum(i + 1, n_i - 1), 0)),
                  pl.BlockSpec((1, d), lambda i, j: (0, 0)),
                  pl.BlockSpec((tn, d), lambda i, j: (j, 0)),
                  pl.BlockSpec((LANES, d), lambda i, j: (0, 0))],
        out_specs=[pl.BlockSpec((tm, tn), lambda i, j: (i, j)),
                   pl.BlockSpec((tm, LANES), lambda i, j: (i, 0))],
        out_shape=[jax.ShapeDtypeStruct((m, n), BF16), jax.ShapeDtypeStruct((m, LANES), F32)],
        scratch_shapes=[pltpu.VMEM((2, tm, d), BF16)],
        compiler_params=_cparams(("arbitrary", "arbitrary")),
        name="in_proj",
    )(x2d, x2d, g, w_t, w_dt)


def _rope_kernel(pos_ref, expo_ref, cmask_ref, cos_ref, sin_ref):
    inv_freq = jnp.power(jnp.float32(ROPE_THETA), expo_ref[...])
    ang = pos_ref[...].astype(F32) * inv_freq
    cm = cmask_ref[...]
    cos_ref[...] = jnp.cos(ang) * cm + (1.0 - cm)
    sin_ref[...] = jnp.sin(ang) * cm


def rope_tables(pos_col, tm=512):
    m = pos_col.shape[0]
    half = ROT_DIM // 2
    d = jnp.arange(LANES) % ATTN_HEAD_DIM
    expo = (-2.0 * (d % half).astype(F32) / ROT_DIM)[None, :]
    cmask = (d < ROT_DIM).astype(F32)[None, :]
    row = pl.BlockSpec((1, LANES), lambda i: (0, 0))
    tab = pl.BlockSpec((tm, LANES), lambda i: (i, 0))
    return pl.pallas_call(
        _rope_kernel,
        grid=(m // tm,),
        in_specs=[pl.BlockSpec((tm, 1), lambda i: (i, 0)), row, row],
        out_specs=[tab, tab],
        out_shape=[jax.ShapeDtypeStruct((m, LANES), F32)] * 2,
        compiler_params=_cparams(("parallel",)),
        name="rope_tab",
    )(pos_col, expo, cmask)


def _rotate_half_matrix():
    half = ROT_DIM // 2
    src = jnp.arange(LANES)[:, None]
    dst = jnp.arange(LANES)[None, :]
    d = dst % ATTN_HEAD_DIM
    first = (d < half) & (src == dst + half)
    second = (d >= half) & (d < ROT_DIM) & (src == dst - half)
    return (second.astype(F32) - first.astype(F32)).astype(BF16)


def _rows(start, size, dil):
    return pl.ds(start, size) if dil == 1 else pl.ds(start, size, stride=dil)


ATTN_GROUP = 2


def _software_pipeline(stages, items, group, extras=()):
    groups = [items[i:i + group] for i in range(0, len(items), group)]
    n_trips = len(groups) + len(stages) - 1
    for t in range(n_trips):
        for lag, stage in enumerate(stages):
            if 0 <= t - lag < len(groups):
                for g, item in enumerate(groups[t - lag]):
                    stage(item, ((t - lag) % 2) * group + g)
        for k, extra in enumerate(extras):
            if k * n_trips // len(extras) == t:
                extra()


def _attn_kernel(q0_ref, k0_ref, v0_ref, cos0_ref, sin0_ref, qn_ref, kn_ref, vn_ref, cosn_ref, sinn_ref,
                 gq_ref, gk_ref, seg_ref, rot_ref, wu_ref, wd_ref, o_ref, wu_out, wd_out,
                 set_a, set_b, ob, mb, lb, s_scr, p_scr):
    s_len = qn_ref.shape[0]
    blk = ATTN_BLOCK
    step = pl.program_id(0) * pl.num_programs(1) + pl.program_id(1)
    lane = lax.broadcasted_iota(jnp.int32, (blk, LANES), 1)
    head0 = lane < ATTN_HEAD_DIM
    qi = lax.broadcasted_iota(jnp.int32, (2 * blk, 2 * blk), 0) % blk
    kj = lax.broadcasted_iota(jnp.int32, (2 * blk, 2 * blk), 1)
    band_mask = (kj >= qi) & (kj <= qi + blk)
    first_mask = (lax.broadcasted_iota(jnp.int32, (2 * blk, blk), 1)
                  <= lax.broadcasted_iota(jnp.int32, (2 * blk, blk), 0) % blk)

    wu_out[...] = wu_ref[...].astype(BF16)
    wd_out[...] = wd_ref[...].astype(BF16)

    prep_rows = 256

    def prep(srcs, dst, r0):
        q_ref, k_ref, v_ref, cos_ref, sin_ref = srcs
        qf, kf, vf = dst[:3]
        rows = slice(r0, r0 + prep_rows)
        cs, sn = cos_ref[rows, :], sin_ref[rows, :]

        def norm_rope(x_ref, g_ref):
            x = x_ref[rows, :].astype(F32)
            ss = jnp.dot((x * x).astype(BF16), seg_ref[...], preferred_element_type=F32)
            y = x * lax.rsqrt(ss * (1.0 / ATTN_HEAD_DIM) + EPS) * g_ref[...]
            return y * cs + jnp.dot(y.astype(BF16), rot_ref[...], preferred_element_type=F32) * sn

        qf[rows, :] = norm_rope(q_ref, gq_ref)
        kf[rows, :] = norm_rope(k_ref, gk_ref)
        vf[rows, :] = v_ref[rows, :].astype(F32)

    dmid = DILATIONS[1]
    run = s_len // dmid

    def deinterleave(bufs, r):
        for src, dst in zip(bufs[:3], bufs[3:]):
            dst[r * run:(r + 1) * run, :] = src[pl.ds(r, run, stride=dmid), :]

    def prepare(srcs, bufs):
        return ([functools.partial(prep, srcs, bufs, r0) for r0 in range(0, s_len, prep_rows)]
                + [functools.partial(deinterleave, bufs, r) for r in range(dmid)])

    @pl.when(step == 0)
    def _():
        for piece in prepare((q0_ref, k0_ref, v0_ref, cos0_ref, sin0_ref), set_a):
            piece()

    def block_descs(bufs):
        natural, deint = bufs[:3], bufs[3:]
        descs = [(natural, 0, 1, 0, 0, blk, first_mask)]
        descs += [(natural, 0, 1, n * blk, (n - 1) * blk, 2 * blk, band_mask) for n in range(1, s_len // blk)]
        descs += [(deint, 1, 1, r * run, r * run, blk, first_mask) for r in range(dmid)]
        descs += [(deint, 1, 1, r * run + n * blk, r * run + (n - 1) * blk, 2 * blk, band_mask)
                  for n in range(1, run // blk) for r in range(dmid)]
        descs += [(deint, 2, dmid, r_hi * run + r_lo, r_hi * run + r_lo, blk, first_mask)
                  for r_hi in range(dmid) for r_lo in range(dmid)]
        return descs

    def scores(desc, slot):
        (q_src, k_src, _), _, dil, q_start, k_start, n_keys, _ = desc
        qt = q_src[_rows(q_start, blk, dil), :]
        zero = jnp.zeros_like(qt)
        q2 = jnp.concatenate([jnp.where(head0, qt, zero), jnp.where(head0, zero, qt)], axis=0).astype(BF16)
        kb = k_src[_rows(k_start, n_keys, dil), :].astype(BF16)
        s_scr[slot, :, :n_keys] = lax.dot_general(q2, kb, (((1,), (1,)), ((), ())), preferred_element_type=F32)

    def softmax(desc, slot):
        _, br, dil, q_start, _, n_keys, mask = desc
        s = jnp.where(mask, s_scr[slot, :, :n_keys], -jnp.inf)
        m = jnp.max(s, axis=-1, keepdims=True)
        p_scr[slot, :, :n_keys] = jnp.exp2(s - m).astype(BF16)
        mb[br, _rows(q_start, blk, dil), :] = jnp.where(head0, m[:blk], m[blk:])

    def values(desc, slot):
        (_, _, v_src), br, dil, q_start, k_start, n_keys, _ = desc
        vb = v_src[_rows(k_start, n_keys, dil), :].astype(BF16)
        v1 = jnp.concatenate([vb, jnp.ones_like(vb)], axis=1)
        r = jnp.dot(p_scr[slot, :, :n_keys], v1, preferred_element_type=F32)
        out_rows = _rows(q_start, blk, dil)
        ob[br, out_rows, :] = jnp.where(head0, r[:blk, :LANES], r[blk:, :LANES])
        lb[br, out_rows, :] = jnp.where(head0, r[:blk, LANES:], r[blk:, LANES:])

    for parity, (cur_set, nxt_set) in enumerate(((set_a, set_b), (set_b, set_a))):
        @pl.when(step % 2 == parity)
        def _(cur_set=cur_set, nxt_set=nxt_set):
            _software_pipeline((scores, softmax, values), block_descs(cur_set), ATTN_GROUP,
                               extras=prepare((qn_ref, kn_ref, vn_ref, cosn_ref, sinn_ref), nxt_set))

    def merge(c, carry):
        d_start = pl.multiple_of(c * prep_rows, prep_rows)
        r = d_start // run
        nat_rows = pl.ds(dmid * (d_start - r * run) + r, prep_rows, stride=dmid)
        d_rows = pl.ds(d_start, prep_rows)
        rows = (nat_rows, d_rows, d_rows)
        ms = [mb[g, rows[g], :] for g in range(len(DILATIONS))]
        m = functools.reduce(jnp.maximum, ms)
        ws = [jnp.exp2(mg - m) for mg in ms]
        num = functools.reduce(jnp.add, [w * ob[g, rows[g], :] for g, w in enumerate(ws)])
        den = functools.reduce(jnp.add, [w * lb[g, rows[g], :] for g, w in enumerate(ws)])
        ob[0, nat_rows, :] = num / den
        return carry

    lax.fori_loop(0, s_len // prep_rows, merge, 0)

    def emit(c, carry):
        rows = pl.ds(pl.multiple_of(c * prep_rows, prep_rows), prep_rows)
        o_ref[rows, :] = ob[0, rows, :].astype(o_ref.dtype)
        return carry

    lax.fori_loop(0, s_len // prep_rows, emit, 0)


def dilated_attention(qkv, tabs, gq2, gk2, w_up, w_down, bsz, s_len, n_pairs):
    cos_t, sin_t = tabs
    d1, d2, d3 = DILATIONS
    assert d1 == 1 and d3 == d2 * d2 and s_len == d3 * ATTN_BLOCK, "layout assumes dilations (1, d, d*d), one block per largest class"
    lane = jnp.arange(LANES)
    seg = (lane[:, None] // ATTN_HEAD_DIM == lane[None, :] // ATTN_HEAD_DIM).astype(BF16)
    n_steps = bsz * n_pairs
    wu_rows, wd_rows = w_up.shape[0] // n_steps, w_down.shape[0] // n_steps

    def nxt(b, p):
        s = jnp.minimum(b * n_pairs + p + 1, n_steps - 1)
        return s // n_pairs, s % n_pairs

    once = dict(pipeline_mode=pl.Buffered(1))
    first = lambda off: pl.BlockSpec((s_len, LANES), lambda b, p, off=off: (0, off), **once)
    nblk = lambda off: pl.BlockSpec((s_len, LANES), lambda b, p, off=off: (nxt(b, p)[0], off + nxt(b, p)[1]))
    tab0 = pl.BlockSpec((s_len, LANES), lambda b, p: (0, 0), **once)
    tabn = pl.BlockSpec((s_len, LANES), lambda b, p: (nxt(b, p)[0], 0))
    row = pl.BlockSpec((1, LANES), lambda b, p: (0, 0))
    sq = pl.BlockSpec((LANES, LANES), lambda b, p: (0, 0))
    wu_spec = pl.BlockSpec((wu_rows, w_up.shape[1]), lambda b, p: (b * n_pairs + p, 0))
    wd_spec = pl.BlockSpec((wd_rows, w_down.shape[1]), lambda b, p: (b * n_pairs + p, 0))
    return pl.pallas_call(
        _attn_kernel,
        grid=(bsz, n_pairs),
        in_specs=[first(0), first(n_pairs), first(2 * n_pairs), tab0, tab0,
                  nblk(0), nblk(n_pairs), nblk(2 * n_pairs), tabn, tabn,
                  row, row, sq, sq, wu_spec, wd_spec],
        out_specs=[pl.BlockSpec((s_len, LANES), lambda b, p: (b, p)), wu_spec, wd_spec],
        out_shape=[jax.ShapeDtypeStruct((bsz * s_len, n_pairs * LANES), BF16),
                   jax.ShapeDtypeStruct(w_up.shape, BF16), jax.ShapeDtypeStruct(w_down.shape, BF16)],
        scratch_shapes=[[pltpu.VMEM((s_len, LANES), F32)] * 6] * 2
                       + [pltpu.VMEM((len(DILATIONS), s_len, LANES), F32)] * 3
                       + [pltpu.VMEM((2 * ATTN_GROUP, 2 * ATTN_BLOCK, 2 * ATTN_BLOCK), F32),
                          pltpu.VMEM((2 * ATTN_GROUP, 2 * ATTN_BLOCK, 2 * ATTN_BLOCK), BF16)],
        compiler_params=_cparams(("arbitrary", "arbitrary")),
        name="dilated_attn",
    )(qkv, qkv, qkv, cos_t, sin_t, qkv, qkv, qkv, cos_t, sin_t, gq2, gk2, seg, _rotate_half_matrix(), w_up, w_down)


def _split3(x):
    hi = x.astype(BF16)
    r1 = x - hi.astype(F32)
    mid = r1.astype(BF16)
    lo = (r1 - mid.astype(F32)).astype(BF16)
    return hi, mid, lo


def _silu(x):
    h = 0.5 * x
    return h * jnp.tanh(h) + h


LOG2E = 1.4426950408889634
SSD_STEP_CHUNKS = 4


def _ssd_kernel(xbc_ref, z_ref, dt_ref, cw_ref, cb_ref, dtb_ref, alog_ref, dskip_ref, g_ref,
                expand_ref, tril_ref, o_ref, xpad, xc, st):
    q = SSD_CHUNK
    n_rows = xbc_ref.shape[0]
    d_ssm = z_ref.shape[1]
    d_conv = xbc_ref.shape[1]
    gw = d_ssm // SSM_GROUPS
    heads_per_group = gw // SSM_HEAD_DIM
    pad = 8

    @pl.when(pl.program_id(1) == 0)
    def _():
        xpad[:, 0:pad, :] = jnp.zeros((d_conv // LANES, pad, LANES), F32)
        st[...] = jnp.zeros_like(st)

    for c0 in range(0, d_conv, LANES):
        cols = slice(c0, c0 + LANES)
        slab = c0 // LANES
        xpad[slab, pad:pad + n_rows, :] = xbc_ref[:, cols].astype(F32)
        acc = cb_ref[:, cols] + cw_ref[CONV_WIDTH - 1:CONV_WIDTH, cols] * xpad[slab, pad:pad + n_rows, :]
        for w in range(CONV_WIDTH - 1):
            off = pad - (CONV_WIDTH - 1) + w
            acc = acc + cw_ref[w:w + 1, cols] * xpad[slab, pl.ds(off, n_rows, stride=1), :]
        xc[:, cols] = _silu(acc)
    xpad[:, 0:pad, :] = xpad[:, n_rows:n_rows + pad, :]

    li = lax.broadcasted_iota(jnp.int32, (q, q), 0)
    si = lax.broadcasted_iota(jnp.int32, (q, q), 1)
    causal = li >= si
    lane_g = lax.broadcasted_iota(jnp.int32, (q, gw), 1)
    tril = tril_ref[...]
    expand = expand_ref[...]

    for r0 in range(0, n_rows, q):
        rows = slice(r0, r0 + q)
        x_dt = dt_ref[rows, :] + dtb_ref[...]
        dt = jnp.maximum(x_dt, 0.0) + jnp.log1p(jnp.exp(-jnp.abs(x_dt)))
        dta = dt * (-LOG2E * jnp.exp(alog_ref[...]))
        acs = sum(jnp.dot(tril, part, preferred_element_type=F32) for part in _split3(dta))
        acs_t = acs.T
        last = acs[q - 1:q, :]
        dt_e = jnp.dot(dt.astype(BF16), expand, preferred_element_type=F32)
        dec_e = jnp.dot(jnp.exp2(acs).astype(BF16), expand, preferred_element_type=F32)
        w_e = jnp.dot((jnp.exp2(last - acs) * dt).astype(BF16), expand, preferred_element_type=F32)

        for g in range(SSM_GROUPS):
            cols = slice(g * gw, (g + 1) * gw)
            b_f = xc[rows, d_ssm + g * SSM_STATE:d_ssm + (g + 1) * SSM_STATE]
            c_b = xc[rows, d_ssm + (SSM_GROUPS + g) * SSM_STATE:d_ssm + (SSM_GROUPS + g + 1) * SSM_STATE].astype(BF16)
            xs = xc[rows, cols]
            cb = lax.dot_general(c_b, b_f.astype(BF16), (((1,), (1,)), ((), ())), preferred_element_type=F32)
            xdt = (xs * dt_e[:, cols]).astype(BF16)
            ws, rs = [], []
            for hh in range(heads_per_group):
                h = g * heads_per_group + hh
                seg = acs[:, h:h + 1] - acs_t[h:h + 1, :]
                l_mat = jnp.exp2(jnp.where(causal, seg, -jnp.inf))
                ws.append((cb * l_mat).astype(BF16))
                in_head = (lane_g >= hh * SSM_HEAD_DIM) & (lane_g < (hh + 1) * SSM_HEAD_DIM)
                rs.append(jnp.where(in_head, xdt, jnp.zeros_like(xdt)))
            y = jnp.dot(jnp.concatenate(ws, axis=1), jnp.concatenate(rs, axis=0), preferred_element_type=F32)
            st_g = st[:, cols]
            y = y + jnp.dot(c_b, st_g.astype(BF16), preferred_element_type=F32) * dec_e[:, cols]
            xw = (xs * w_e[:, cols]).astype(BF16)
            st[:, cols] = st_g * dec_e[q - 1:q, cols] + jnp.dot(b_f.T.astype(BF16), xw, preferred_element_type=F32)
            y = y + dskip_ref[:, cols] * xs
            y = y * _silu(z_ref[rows, cols].astype(F32))
            o_ref[rows, cols] = _rms_rows(y, g_ref[:, cols]).astype(o_ref.dtype)


def ssd_mixer(packed, dt_raw, conv_w, conv_b, dt_bias, a_log, d_skip, g_out, bsz, s_len, d_ssm, xbc_col, z_col):
    n_heads = d_ssm // SSM_HEAD_DIM
    d_conv = d_ssm + 2 * SSM_GROUPS * SSM_STATE
    step_rows = SSD_STEP_CHUNKS * SSD_CHUNK
    nc = s_len // step_rows
    padl = lambda v: jnp.pad(v.astype(F32), (0, LANES - n_heads))[None, :]
    expand = (jnp.arange(LANES)[:, None] == (jnp.arange(d_ssm) // SSM_HEAD_DIM)[None, :]).astype(BF16)
    tril = (jnp.arange(SSD_CHUNK)[:, None] >= jnp.arange(SSD_CHUNK)[None, :]).astype(BF16)
    full = lambda shape: pl.BlockSpec(shape, lambda b, c: (0, 0))
    return pl.pallas_call(
        _ssd_kernel,
        grid=(bsz, nc),
        in_specs=[pl.BlockSpec((step_rows, d_conv), lambda b, c: (b * nc + c, xbc_col // d_conv)),
                  pl.BlockSpec((step_rows, d_ssm), lambda b, c: (b * nc + c, z_col // d_ssm)),
                  pl.BlockSpec((step_rows, LANES), lambda b, c: (b * nc + c, 0)),
                  full((CONV_WIDTH, d_conv)), full((1, d_conv)), full((1, LANES)), full((1, LANES)),
                  full((1, d_ssm)), full((1, d_ssm)), full((LANES, d_ssm)), full((SSD_CHUNK, SSD_CHUNK))],
        out_specs=pl.BlockSpec((step_rows, d_ssm), lambda b, c: (b * nc + c, 0)),
        out_shape=jax.ShapeDtypeStruct((bsz * s_len, d_ssm), BF16),
        scratch_shapes=[pltpu.VMEM((d_conv // LANES, step_rows + 8, LANES), F32),
                        pltpu.VMEM((step_rows, d_conv), F32),
                        pltpu.VMEM((SSM_STATE, d_ssm), F32)],
        compiler_params=_cparams(("parallel", "arbitrary")),
        name="ssd",
    )(packed, packed, dt_raw, conv_w.astype(F32), conv_b.astype(F32)[None, :], padl(dt_bias), padl(a_log),
      jnp.repeat(d_skip.astype(F32), SSM_HEAD_DIM)[None, :], g_out.astype(F32)[None, :], expand, tril)


def _out_proj_kernel(attn_ref, ssm_ref, x_ref, g_ref, w_ref, o_ref, wb_ref, an_ref):
    d_attn = attn_ref.shape[1]

    @pl.when(pl.program_id(0) == 0)
    def _():
        wb_ref[...] = w_ref[...].astype(BF16)

    rows = 32

    def body(c, carry):
        rs = pl.ds(pl.multiple_of(c * rows, rows), rows)
        an_ref[rs, :] = _rms_rows(attn_ref[rs, :].astype(F32), g_ref[...]).astype(BF16)
        return carry
    lax.fori_loop(0, attn_ref.shape[0] // rows, body, 0, unroll=4)
    o_ref[...] = (x_ref[...] + jnp.dot(an_ref[...], wb_ref[:d_attn, :], preferred_element_type=F32)
                  + jnp.dot(ssm_ref[...], wb_ref[d_attn:, :], preferred_element_type=F32))


def out_proj(attn, ssm, x2d, g, w, tm=512):
    m, d = x2d.shape
    da, ds_ = attn.shape[1], ssm.shape[1]
    return pl.pallas_call(
        _out_proj_kernel,
        grid=(m // tm,),
        in_specs=[pl.BlockSpec((tm, da), lambda i: (i, 0)),
                  pl.BlockSpec((tm, ds_), lambda i: (i, 0)),
                  pl.BlockSpec((tm, d), lambda i: (i, 0)),
                  pl.BlockSpec((1, da), lambda i: (0, 0)),
                  pl.BlockSpec((da + ds_, d), lambda i: (0, 0), pipeline_mode=pl.Buffered(1))],
        out_specs=pl.BlockSpec((tm, d), lambda i: (i, 0)),
        out_shape=jax.ShapeDtypeStruct((m, d), F32),
        scratch_shapes=[pltpu.VMEM((da + ds_, d), BF16), pltpu.VMEM((tm, da), BF16)],
        compiler_params=_cparams(("arbitrary",)),
        name="out_proj",
    )(attn, ssm, x2d, g, w)


def _mem_kv_kernel(mem_ref, g_ref, w_ref, gk_ref, kv_ref, h_ref):
    d_cross = N_CROSS_HEADS * CROSS_HEAD_DIM
    _norm_rows_to(mem_ref, g_ref, h_ref)
    kv = jnp.dot(h_ref[...], w_ref[...], preferred_element_type=F32)
    for h in range(N_CROSS_HEADS):
        cols = slice(h * CROSS_HEAD_DIM, (h + 1) * CROSS_HEAD_DIM)
        kv_ref[:, cols] = _rms_rows(kv[:, cols], gk_ref[...]).astype(kv_ref.dtype)
    kv_ref[:, d_cross:] = kv[:, d_cross:].astype(kv_ref.dtype)


def mem_kv(mem2d, g, w, gk, tm=256):
    m, d = mem2d.shape
    n = w.shape[1]
    return pl.pallas_call(
        _mem_kv_kernel,
        grid=(m // tm,),
        in_specs=[pl.BlockSpec((tm, d), lambda i: (i, 0)),
                  pl.BlockSpec((1, d), lambda i: (0, 0)),
                  pl.BlockSpec((d, n), lambda i: (0, 0)),
                  pl.BlockSpec((1, CROSS_HEAD_DIM), lambda i: (0, 0))],
        out_specs=pl.BlockSpec((tm, n), lambda i: (i, 0)),
        out_shape=jax.ShapeDtypeStruct((m, n), BF16),
        scratch_shapes=[pltpu.VMEM((tm, d), BF16)],
        compiler_params=_cparams(("parallel",)),
        name="mem_kv",
    )(mem2d, g, w, gk)


def _cross_kernel(x_ref, xn_ref, g_ref, wq_ref, gq_ref, k_ref, v_ref, wo_ref, o_ref, h_ref, q_ref, a_ref, s_scr, p_scr):
    i = pl.program_id(0)
    tm = x_ref.shape[0]
    rows = CROSS_ROWS
    cur, nxt = i % 2, (i + 1) % 2

    @pl.when(i == 0)
    def _():
        _norm_rows_to(x_ref, g_ref, h_ref.at[0])

    q_ref[...] = jnp.dot(h_ref[cur], wq_ref[...], preferred_element_type=F32)

    def scores(item, slot):
        r0, h = item
        cols = slice(h * CROSS_HEAD_DIM, (h + 1) * CROSS_HEAD_DIM)
        qh = (_rms_rows(q_ref[r0:r0 + rows, cols], gq_ref[...]) * (CROSS_HEAD_DIM ** -0.5)).astype(BF16)
        s_scr[slot] = lax.dot_general(qh, k_ref[:, cols], (((1,), (1,)), ((), ())), preferred_element_type=F32)

    def softmax(item, slot):
        s = s_scr[slot]
        p_scr[slot] = jnp.exp(s - jnp.max(s, axis=-1, keepdims=True)).astype(BF16)

    def values(item, slot):
        r0, h = item
        cols = slice(h * CROSS_HEAD_DIM, (h + 1) * CROSS_HEAD_DIM)
        vh = v_ref[:, cols]
        r = jnp.dot(p_scr[slot], jnp.concatenate([vh, jnp.ones_like(vh)], axis=1), preferred_element_type=F32)
        a_ref[r0:r0 + rows, cols] = (r[:, :CROSS_HEAD_DIM] / r[:, CROSS_HEAD_DIM:]).astype(BF16)

    items = [(r0, h) for r0 in range(0, tm, rows) for h in range(N_CROSS_HEADS)]
    _software_pipeline((scores, softmax, values), items, CROSS_GROUP)
    _norm_chunk_to(xn_ref, g_ref, h_ref.at[nxt], 0, tm)
    o_ref[...] = x_ref[...] + jnp.dot(a_ref[...], wo_ref[...], preferred_element_type=F32)


def cross_attention(x2d, g, wq, gq, kv, wo, s_len, n_mem, tm=512):
    m, d = x2d.shape
    dc = wq.shape[1]
    per_batch = s_len // tm
    full = lambda shape: pl.BlockSpec(shape, lambda i: (0, 0))
    return pl.pallas_call(
        _cross_kernel,
        grid=(m // tm,),
        in_specs=[pl.BlockSpec((tm, d), lambda i: (i, 0)),
                  pl.BlockSpec((tm, d), lambda i: (jnp.minimum(i + 1, m // tm - 1), 0)),
                  full((1, d)), full((d, dc)), full((1, CROSS_HEAD_DIM)),
                  pl.BlockSpec((n_mem, dc), lambda i: (i // per_batch, 0)),
                  pl.BlockSpec((n_mem, dc), lambda i: (i // per_batch, 1)),
                  full((dc, d))],
        out_specs=pl.BlockSpec((tm, d), lambda i: (i, 0)),
        out_shape=jax.ShapeDtypeStruct((m, d), F32),
        scratch_shapes=[pltpu.VMEM((2, tm, d), BF16), pltpu.VMEM((tm, dc), F32), pltpu.VMEM((tm, dc), BF16),
                        pltpu.VMEM((2 * CROSS_GROUP, CROSS_ROWS, n_mem), F32),
                        pltpu.VMEM((2 * CROSS_GROUP, CROSS_ROWS, n_mem), BF16)],
        compiler_params=_cparams(("arbitrary",)),
        name="cross_attn",
    )(x2d, x2d, g, wq, gq, kv, kv, wo)


def _mlp_kernel(x_ref, g_ref, wu_ref, wd_ref, o_ref, h_ref):
    @pl.when(pl.program_id(1) == 0)
    def _():
        _norm_rows_to(x_ref, g_ref, h_ref)
        o_ref[...] = x_ref[...]
    u = jnp.dot(h_ref[...], wu_ref[...], preferred_element_type=F32)
    u = jnp.square(jnp.maximum(u, 0.0)).astype(BF16)
    o_ref[...] += jnp.dot(u, wd_ref[...], preferred_element_type=F32)


def mlp(x2d, g, wu, wd, tm=512, tf=1024):
    m, d = x2d.shape
    f = wu.shape[1]
    return pl.pallas_call(
        _mlp_kernel,
        grid=(m // tm, f // tf),
        in_specs=[pl.BlockSpec((tm, d), lambda i, j: (i, 0)),
                  pl.BlockSpec((1, d), lambda i, j: (0, 0)),
                  pl.BlockSpec((d, tf), lambda i, j: (0, j)),
                  pl.BlockSpec((tf, d), lambda i, j: (j, 0))],
        out_specs=pl.BlockSpec((tm, d), lambda i, j: (i, 0)),
        out_shape=jax.ShapeDtypeStruct((m, d), F32),
        scratch_shapes=[pltpu.VMEM((tm, d), BF16)],
        compiler_params=_cparams(("parallel", "arbitrary")),
        name="mlp",
    )(x2d, g, wu, wd)


def _layer(x2d, mem2d, pos_col, bsz, s_len, n_mem, g_mix, w_in, g_q, g_k, g_attn_out, conv_w, conv_b, dt_bias,
           a_log, d_skip, g_ssm_out, w_out, g_cross, g_mem, w_cq, w_ckv, g_cq, g_ck, w_co, g_mlp, w_up, w_down):
    d_model = x2d.shape[1]
    d_attn = d_model // 2
    d_ssm = d_model // 2
    n_pairs = d_attn // LANES
    n_ssm_heads = d_ssm // SSM_HEAD_DIM
    d_conv = d_ssm + 2 * SSM_GROUPS * SSM_STATE
    d_packed = 3 * d_attn + d_ssm + d_conv
    row = lambda v: v.astype(F32)[None, :]

    w_in_t = w_in.T
    w_dt_t = jnp.pad(w_in_t[d_packed:], ((0, LANES - n_ssm_heads), (0, 0)))
    packed, dt_raw = in_proj(x2d, row(g_mix), w_in_t, w_dt_t, d_packed)

    tabs = rope_tables(pos_col)
    attn, w_up_bf, w_down_bf = dilated_attention(
        packed, tabs, row(jnp.tile(g_q, 2)) * (ATTN_HEAD_DIM ** -0.5 * LOG2E), row(jnp.tile(g_k, 2)), w_up, w_down,
        bsz, s_len, n_pairs)
    ssm = ssd_mixer(packed, dt_raw, conv_w, conv_b, dt_bias, a_log, d_skip, g_ssm_out, bsz, s_len, d_ssm,
                    xbc_col=3 * d_attn + d_ssm, z_col=3 * d_attn)
    x2d = out_proj(attn, ssm, x2d, row(g_attn_out), w_out)

    kv = mem_kv(mem2d, row(g_mem), w_ckv.astype(BF16), row(g_ck))
    x2d = cross_attention(x2d, row(g_cross), w_cq.astype(BF16), row(g_cq), kv, w_co.astype(BF16), s_len, n_mem)

    return mlp(x2d, row(g_mlp), w_up_bf, w_down_bf)


def kernel(x, mem, positions, g_mix, w_in, g_q, g_k, g_attn_out, conv_w, conv_b, dt_bias, a_log, d_skip, g_ssm_out,
           w_out, g_cross, g_mem, w_cq, w_ckv, g_cq, g_ck, w_co, g_mlp, w_up, w_down):
    bsz, s_len, d_model = x.shape
    n_mem = mem.shape[1]
    x2d = x.reshape(bsz * s_len, d_model)
    mem2d = mem.reshape(bsz * n_mem, d_model)
    pos_col = positions.reshape(bsz * s_len, 1)
    for i in range(g_mix.shape[0]):
        x2d = _layer(x2d, mem2d, pos_col, bsz, s_len, n_mem, g_mix[i], w_in[i], g_q[i], g_k[i], g_attn_out[i],
                     conv_w[i], conv_b[i], dt_bias[i], a_log[i], d_skip[i], g_ssm_out[i], w_out[i], g_cross[i],
                     g_mem[i], w_cq[i], w_ckv[i], g_cq[i], g_ck[i], w_co[i], g_mlp[i], w_up[i], w_down[i])
    return x2d.reshape(bsz, s_len, d_model)
```

```python
import functools
import math

import jax
import jax.numpy as jnp
from jax import lax
from jax.experimental import pallas as pl
from jax.experimental.pallas import tpu as pltpu

F32 = jnp.float32
BF16 = jnp.bfloat16
EPS = 1e-6

LANES = 128
ATTN_HEAD_DIM = 64
ROT_DIM = ATTN_HEAD_DIM // 4
ROPE_THETA = 500000.0
ATTN_BLOCK = 128
DILATIONS = (1, 4, 16)
SSM_HEAD_DIM = 64
SSM_GROUPS = 4
SSM_STATE = 128
CONV_WIDTH = 4
SSD_CHUNK = 128
N_CROSS_HEADS = 4
CROSS_HEAD_DIM = 128
CROSS_ROWS = 256
CROSS_GROUP = 2
VMEM_LIMIT = 52 * 1024 * 1024


def _cparams(sem):
    return pltpu.CompilerParams(dimension_semantics=sem, vmem_limit_bytes=VMEM_LIMIT)


def _rms_rows(x, g):
    ms = jnp.mean(x * x, axis=-1, keepdims=True)
    return x * lax.rsqrt(ms + EPS) * g


def _norm_rows_to(x_ref, g_ref, h_ref, rows=16):
    def body(c, carry):
        r0 = pl.multiple_of(c * rows, rows)
        h_ref[pl.ds(r0, rows), :] = _rms_rows(x_ref[pl.ds(r0, rows), :], g_ref[...]).astype(h_ref.dtype)
        return carry
    lax.fori_loop(0, x_ref.shape[0] // rows, body, 0, unroll=4)


def _norm_chunk_to(x_ref, g_ref, h_ref, start, n_rows, rows=16):
    for r in range(0, n_rows, rows):
        rs = pl.ds(start + r, rows)
        h_ref[rs, :] = _rms_rows(x_ref[rs, :], g_ref[...]).astype(h_ref.dtype)


def _in_proj_kernel(x0_ref, xn_ref, g_ref, w_ref, wdt_ref, o_ref, dt_ref, h_ref):
    i, j = pl.program_id(0), pl.program_id(1)
    tm = xn_ref.shape[0]
    cur, nxt = i % 2, (i + 1) % 2

    @pl.when((i == 0) & (j == 0))
    def _():
        _norm_rows_to(x0_ref, g_ref, h_ref.at[0])

    @pl.when(j == 0)
    def _():
        dt_ref[...] = lax.dot_general(h_ref[cur], wdt_ref[...].astype(BF16), (((1,), (1,)), ((), ())),
                                      preferred_element_type=F32)
    o_ref[...] = lax.dot_general(h_ref[cur], w_ref[...].astype(BF16), (((1,), (1,)), ((), ())),
                                 preferred_element_type=F32).astype(o_ref.dtype)
    chunk = tm // IN_PROJ_NORM_STEPS
    start = pl.multiple_of(jnp.minimum(j, IN_PROJ_NORM_STEPS - 1) * chunk, chunk)
    _norm_chunk_to(xn_ref, g_ref, h_ref.at[nxt], start, chunk)


IN_PROJ_NORM_STEPS = 8


def in_proj(x2d, g, w_t, w_dt, n, tm=1024, tn=512):
    m, d = x2d.shape
    n_i = m // tm
    assert n // tn >= IN_PROJ_NORM_STEPS
    return pl.pallas_call(
        _in_proj_kernel,
        grid=(n_i, n // tn),
        in_specs=[pl.BlockSpec((tm, d), lambda i, j: (0, 0), pipeline_mode=pl.Buffered(1)),
                  pl.BlockSpec((tm, d), lambda i, j: (jnp.minimum(i + 1, n_i - 1), 0)),
                  pl.BlockSpec((1, d), lambda i, j: (0, 0)),
                  pl.BlockSpec((tn, d), lambda i, j: (j, 0)),
                  pl.BlockSpec((LANES, d), lambda i, j: (0, 0))],
        out_specs=[pl.BlockSpec((tm, tn), lambda i, j: (i, j)),
                   pl.BlockSpec((tm, LANES), lambda i, j: (i, 0))],
        out_shape=[jax.ShapeDtypeStruct((m, n), BF16), jax.ShapeDtypeStruct((m, LANES), F32)],
        scratch_shapes=[pltpu.VMEM((2, tm, d), BF16)],
        compiler_params=_cparams(("arbitrary", "arbitrary")),
        name="in_proj",
    )(x2d, x2d, g, w_t, w_dt)


def _rope_kernel(pos_ref, expo_ref, cmask_ref, cos_ref, sin_ref):
    inv_freq = jnp.power(jnp.float32(ROPE_THETA), expo_ref[...])
    ang = pos_ref[...].astype(F32) * inv_freq
    cm = cmask_ref[...]
    cos_ref[...] = jnp.cos(ang) * cm + (1.0 - cm)
    sin_ref[...] = jnp.sin(ang) * cm


def rope_tables(pos_col, tm=512):
    m = pos_col.shape[0]
    half = ROT_DIM // 2
    d = jnp.arange(LANES) % ATTN_HEAD_DIM
    expo = (-2.0 * (d % half).astype(F32) / ROT_DIM)[None, :]
    cmask = (d < ROT_DIM).astype(F32)[None, :]
    row = pl.BlockSpec((1, LANES), lambda i: (0, 0))
    tab = pl.BlockSpec((tm, LANES), lambda i: (i, 0))
    return pl.pallas_call(
        _rope_kernel,
        grid=(m // tm,),
        in_specs=[pl.BlockSpec((tm, 1), lambda i: (i, 0)), row, row],
        out_specs=[tab, tab],
        out_shape=[jax.ShapeDtypeStruct((m, LANES), F32)] * 2,
        compiler_params=_cparams(("parallel",)),
        name="rope_tab",
    )(pos_col, expo, cmask)


def _rotate_half_matrix():
    half = ROT_DIM // 2
    src = jnp.arange(LANES)[:, None]
    dst = jnp.arange(LANES)[None, :]
    d = dst % ATTN_HEAD_DIM
    first = (d < half) & (src == dst + half)
    second = (d >= half) & (d < ROT_DIM) & (src == dst - half)
    return (second.astype(F32) - first.astype(F32)).astype(BF16)


def _rows(start, size, dil):
    return pl.ds(start, size) if dil == 1 else pl.ds(start, size, stride=dil)


ATTN_GROUP = 2


def _software_pipeline(stages, items, group, extras=()):
    groups = [items[i:i + group] for i in range(0, len(items), group)]
    n_trips = len(groups) + len(stages) - 1
    for t in range(n_trips):
        for lag, stage in enumerate(stages):
            if 0 <= t - lag < len(groups):
                for g, item in enumerate(groups[t - lag]):
                    stage(item, ((t - lag) % 2) * group + g)
        for k, extra in enumerate(extras):
            if k * n_trips // len(extras) == t:
                extra()


def _attn_kernel(q0_ref, k0_ref, v0_ref, cos0_ref, sin0_ref, qn_ref, kn_ref, vn_ref, cosn_ref, sinn_ref,
                 gq_ref, gk_ref, seg_ref, rot_ref, wu_ref, wd_ref, o_ref, wu_out, wd_out,
                 set_a, set_b, ob, mb, lb, s_scr, p_scr):
    s_len = qn_ref.shape[0]
    blk = ATTN_BLOCK
    step = pl.program_id(0) * pl.num_programs(1) + pl.program_id(1)
    lane = lax.broadcasted_iota(jnp.int32, (blk, LANES), 1)
    head0 = lane < ATTN_HEAD_DIM
    qi = lax.broadcasted_iota(jnp.int32, (2 * blk, 2 * blk), 0) % blk
    kj = lax.broadcasted_iota(jnp.int32, (2 * blk, 2 * blk), 1)
    band_mask = (kj >= qi) & (kj <= qi + blk)
    first_mask = (lax.broadcasted_iota(jnp.int32, (2 * blk, blk), 1)
                  <= lax.broadcasted_iota(jnp.int32, (2 * blk, blk), 0) % blk)

    wu_out[...] = wu_ref[...].astype(BF16)
    wd_out[...] = wd_ref[...].astype(BF16)

    prep_rows = 256

    def prep(srcs, dst, r0):
        q_ref, k_ref, v_ref, cos_ref, sin_ref = srcs
        qf, kf, vf = dst[:3]
        rows = slice(r0, r0 + prep_rows)
        cs, sn = cos_ref[rows, :], sin_ref[rows, :]

        def norm_rope(x_ref, g_ref):
            x = x_ref[rows, :].astype(F32)
            ss = jnp.dot((x * x).astype(BF16), seg_ref[...], preferred_element_type=F32)
            y = x * lax.rsqrt(ss * (1.0 / ATTN_HEAD_DIM) + EPS) * g_ref[...]
            return y * cs + jnp.dot(y.astype(BF16), rot_ref[...], preferred_element_type=F32) * sn

        qf[rows, :] = norm_rope(q_ref, gq_ref)
        kf[rows, :] = norm_rope(k_ref, gk_ref)
        vf[rows, :] = v_ref[rows, :].astype(F32)

    dmid = DILATIONS[1]
    run = s_len // dmid

    def deinterleave(bufs, r):
        for src, dst in zip(bufs[:3], bufs[3:]):
            dst[r * run:(r + 1) * run, :] = src[pl.ds(r, run, stride=dmid), :]

    def prepare(srcs, bufs):
        return ([functools.partial(prep, srcs, bufs, r0) for r0 in range(0, s_len, prep_rows)]
                + [functools.partial(deinterleave, bufs, r) for r in range(dmid)])

    @pl.when(step == 0)
    def _():
        for piece in prepare((q0_ref, k0_ref, v0_ref, cos0_ref, sin0_ref), set_a):
            piece()

    def block_descs(bufs):
        natural, deint = bufs[:3], bufs[3:]
        descs = [(deint, 1, 1, r * run, r * run, blk, first_mask) for r in range(dmid)]
        descs += [(deint, 1, 1, r * run + n * blk, r * run + (n - 1) * blk, 2 * blk, band_mask)
                  for n in range(1, run // blk) for r in range(dmid)]
        descs += [(deint, 2, dmid, r_hi * run + r_lo, r_hi * run + r_lo, blk, first_mask)
                  for r_hi in range(dmid) for r_lo in range(dmid)]
        descs += [(natural, 0, 1, 0, 0, blk, first_mask)]
        descs += [(natural, 0, 1, n * blk, (n - 1) * blk, 2 * blk, band_mask) for n in range(1, s_len // blk)]
        return descs

    def scores(desc, slot):
        (q_src, k_src, _), _, dil, q_start, k_start, n_keys, _ = desc
        qt = q_src[_rows(q_start, blk, dil), :]
        zero = jnp.zeros_like(qt)
        q2 = jnp.concatenate([jnp.where(head0, qt, zero), jnp.where(head0, zero, qt)], axis=0).astype(BF16)
        kb = k_src[_rows(k_start, n_keys, dil), :].astype(BF16)
        s_scr[slot, :, :n_keys] = lax.dot_general(q2, kb, (((1,), (1,)), ((), ())), preferred_element_type=F32)

    def softmax(desc, slot):
        _, br, dil, q_start, _, n_keys, mask = desc
        s = jnp.where(mask, s_scr[slot, :, :n_keys], -jnp.inf)
        m = jnp.max(s, axis=-1, keepdims=True)
        p_scr[slot, :, :n_keys] = jnp.exp2(s - m).astype(BF16)
        mb[br, _rows(q_start, blk, dil), :] = jnp.where(head0, m[:blk], m[blk:])

    def values(desc, slot):
        (_, _, v_src), br, dil, q_start, k_start, n_keys, _ = desc
        vb = v_src[_rows(k_start, n_keys, dil), :].astype(BF16)
        v1 = jnp.concatenate([vb, jnp.ones_like(vb)], axis=1)
        r = jnp.dot(p_scr[slot, :, :n_keys], v1, preferred_element_type=F32)
        out_rows = _rows(q_start, blk, dil)
        ob[br, out_rows, :] = jnp.where(head0, r[:blk, :LANES], r[blk:, :LANES])
        lb[br, out_rows, :] = jnp.where(head0, r[:blk, LANES:], r[blk:, LANES:])
        if br == 0:
            merge(q_start)

    def merge(n0):
        sub = blk // dmid
        for r in range(dmid):
            nat_rows = pl.ds(n0 + r, sub, stride=dmid)
            d_rows = slice(r * run + n0 // dmid, r * run + n0 // dmid + sub)
            rows = (nat_rows, d_rows, d_rows)
            ms = [mb[g, rows[g], :] for g in range(len(DILATIONS))]
            m = functools.reduce(jnp.maximum, ms)
            ws = [jnp.exp2(mg - m) for mg in ms]
            num = functools.reduce(jnp.add, [w * ob[g, rows[g], :] for g, w in enumerate(ws)])
            den = functools.reduce(jnp.add, [w * lb[g, rows[g], :] for g, w in enumerate(ws)])
            ob[0, nat_rows, :] = num / den
        o_ref[n0:n0 + blk, :] = ob[0, n0:n0 + blk, :].astype(o_ref.dtype)

    for parity, (cur_set, nxt_set) in enumerate(((set_a, set_b), (set_b, set_a))):
        @pl.when(step % 2 == parity)
        def _(cur_set=cur_set, nxt_set=nxt_set):
            _software_pipeline((scores, softmax, values), block_descs(cur_set), ATTN_GROUP,
                               extras=prepare((qn_ref, kn_ref, vn_ref, cosn_ref, sinn_ref), nxt_set))


def dilated_attention(qkv, tabs, gq2, gk2, w_up, w_down, bsz, s_len, n_pairs):
    cos_t, sin_t = tabs
    d1, d2, d3 = DILATIONS
    assert d1 == 1 and d3 == d2 * d2 and s_len == d3 * ATTN_BLOCK, "layout assumes dilations (1, d, d*d), one block per largest class"
    lane = jnp.arange(LANES)
    seg = (lane[:, None] // ATTN_HEAD_DIM == lane[None, :] // ATTN_HEAD_DIM).astype(BF16)
    n_steps = bsz * n_pairs
    wu_rows, wd_rows = w_up.shape[0] // n_steps, w_down.shape[0] // n_steps

    def nxt(b, p):
        s = jnp.minimum(b * n_pairs + p + 1, n_steps - 1)
        return s // n_pairs, s % n_pairs

    once = dict(pipeline_mode=pl.Buffered(1))
    first = lambda off: pl.BlockSpec((s_len, LANES), lambda b, p, off=off: (0, off), **once)
    nblk = lambda off: pl.BlockSpec((s_len, LANES), lambda b, p, off=off: (nxt(b, p)[0], off + nxt(b, p)[1]))
    tab0 = pl.BlockSpec((s_len, LANES), lambda b, p: (0, 0), **once)
    tabn = pl.BlockSpec((s_len, LANES), lambda b, p: (nxt(b, p)[0], 0))
    row = pl.BlockSpec((1, LANES), lambda b, p: (0, 0))
    sq = pl.BlockSpec((LANES, LANES), lambda b, p: (0, 0))
    wu_spec = pl.BlockSpec((wu_rows, w_up.shape[1]), lambda b, p: (b * n_pairs + p, 0))
    wd_spec = pl.BlockSpec((wd_rows, w_down.shape[1]), lambda b, p: (b * n_pairs + p, 0))
    return pl.pallas_call(
        _attn_kernel,
        grid=(bsz, n_pairs),
        in_specs=[first(0), first(n_pairs), first(2 * n_pairs), tab0, tab0,
                  nblk(0), nblk(n_pairs), nblk(2 * n_pairs), tabn, tabn,
                  row, row, sq, sq, wu_spec, wd_spec],
        out_specs=[pl.BlockSpec((s_len, LANES), lambda b, p: (b, p)), wu_spec, wd_spec],
        out_shape=[jax.ShapeDtypeStruct((bsz * s_len, n_pairs * LANES), BF16),
                   jax.ShapeDtypeStruct(w_up.shape, BF16), jax.ShapeDtypeStruct(w_down.shape, BF16)],
        scratch_shapes=[[pltpu.VMEM((s_len, LANES), F32)] * 6] * 2
                       + [pltpu.VMEM((len(DILATIONS), s_len, LANES), F32)] * 3
                       + [pltpu.VMEM((2 * ATTN_GROUP, 2 * ATTN_BLOCK, 2 * ATTN_BLOCK), F32),
                          pltpu.VMEM((2 * ATTN_GROUP, 2 * ATTN_BLOCK, 2 * ATTN_BLOCK), BF16)],
        compiler_params=_cparams(("arbitrary", "arbitrary")),
        name="dilated_attn",
    )(qkv, qkv, qkv, cos_t, sin_t, qkv, qkv, qkv, cos_t, sin_t, gq2, gk2, seg, _rotate_half_matrix(), w_up, w_down)


def _split3(x):
    hi = x.astype(BF16)
    r1 = x - hi.astype(F32)
    mid = r1.astype(BF16)
    lo = (r1 - mid.astype(F32)).astype(BF16)
    return hi, mid, lo


def _silu(x):
    h = 0.5 * x
    return h * jnp.tanh(h) + h


LOG2E = 1.4426950408889634
SSD_STEP_CHUNKS = 4


def _ssd_kernel(xbc_ref, z_ref, dt_ref, cw_ref, cb_ref, dtb_ref, alog_ref, dskip_ref, g_ref,
                expand_ref, tril_ref, o_ref, xpad, xc, st):
    q = SSD_CHUNK
    n_rows = xbc_ref.shape[0]
    d_ssm = z_ref.shape[1]
    d_conv = xbc_ref.shape[1]
    gw = d_ssm // SSM_GROUPS
    heads_per_group = gw // SSM_HEAD_DIM
    pad = 8

    @pl.when(pl.program_id(1) == 0)
    def _():
        xpad[:, 0:pad, :] = jnp.zeros((d_conv // LANES, pad, LANES), F32)
        st[...] = jnp.zeros_like(st)

    for c0 in range(0, d_conv, LANES):
        cols = slice(c0, c0 + LANES)
        slab = c0 // LANES
        xpad[slab, pad:pad + n_rows, :] = xbc_ref[:, cols].astype(F32)
        acc = cb_ref[:, cols] + cw_ref[CONV_WIDTH - 1:CONV_WIDTH, cols] * xpad[slab, pad:pad + n_rows, :]
        for w in range(CONV_WIDTH - 1):
            off = pad - (CONV_WIDTH - 1) + w
            acc = acc + cw_ref[w:w + 1, cols] * xpad[slab, pl.ds(off, n_rows, stride=1), :]
        xc[:, cols] = _silu(acc)
    xpad[:, 0:pad, :] = xpad[:, n_rows:n_rows + pad, :]

    li = lax.broadcasted_iota(jnp.int32, (q, q), 0)
    si = lax.broadcasted_iota(jnp.int32, (q, q), 1)
    causal = li >= si
    lane_g = lax.broadcasted_iota(jnp.int32, (q, gw), 1)
    tril = tril_ref[...]
    expand = expand_ref[...]

    for r0 in range(0, n_rows, q):
        rows = slice(r0, r0 + q)
        x_dt = dt_ref[rows, :] + dtb_ref[...]
        dt = jnp.maximum(x_dt, 0.0) + jnp.log1p(jnp.exp(-jnp.abs(x_dt)))
        dta = dt * (-LOG2E * jnp.exp(alog_ref[...]))
        acs = sum(jnp.dot(tril, part, preferred_element_type=F32) for part in _split3(dta))
        acs_t = acs.T
        last = acs[q - 1:q, :]
        dt_e = jnp.dot(dt.astype(BF16), expand, preferred_element_type=F32)
        dec_e = jnp.dot(jnp.exp2(acs).astype(BF16), expand, preferred_element_type=F32)
        w_e = jnp.dot((jnp.exp2(last - acs) * dt).astype(BF16), expand, preferred_element_type=F32)

        for g in range(SSM_GROUPS):
            cols = slice(g * gw, (g + 1) * gw)
            b_f = xc[rows, d_ssm + g * SSM_STATE:d_ssm + (g + 1) * SSM_STATE]
            c_b = xc[rows, d_ssm + (SSM_GROUPS + g) * SSM_STATE:d_ssm + (SSM_GROUPS + g + 1) * SSM_STATE].astype(BF16)
            xs = xc[rows, cols]
            cb = lax.dot_general(c_b, b_f.astype(BF16), (((1,), (1,)), ((), ())), preferred_element_type=F32)
            xdt = (xs * dt_e[:, cols]).astype(BF16)
            ws, rs = [], []
            for hh in range(heads_per_group):
                h = g * heads_per_group + hh
                seg = acs[:, h:h + 1] - acs_t[h:h + 1, :]
                l_mat = jnp.exp2(jnp.where(causal, seg, -jnp.inf))
                ws.append((cb * l_mat).astype(BF16))
                in_head = (lane_g >= hh * SSM_HEAD_DIM) & (lane_g < (hh + 1) * SSM_HEAD_DIM)
                rs.append(jnp.where(in_head, xdt, jnp.zeros_like(xdt)))
            y = jnp.dot(jnp.concatenate(ws, axis=1), jnp.concatenate(rs, axis=0), preferred_element_type=F32)
            st_g = st[:, cols]
            y = y + jnp.dot(c_b, st_g.astype(BF16), preferred_element_type=F32) * dec_e[:, cols]
            xw = (xs * w_e[:, cols]).astype(BF16)
            st[:, cols] = st_g * dec_e[q - 1:q, cols] + jnp.dot(b_f.T.astype(BF16), xw, preferred_element_type=F32)
            y = y + dskip_ref[:, cols] * xs
            y = y * _silu(z_ref[rows, cols].astype(F32))
            o_ref[rows, cols] = _rms_rows(y, g_ref[:, cols]).astype(o_ref.dtype)


def ssd_mixer(packed, dt_raw, conv_w, conv_b, dt_bias, a_log, d_skip, g_out, bsz, s_len, d_ssm, xbc_col, z_col):
    n_heads = d_ssm // SSM_HEAD_DIM
    d_conv = d_ssm + 2 * SSM_GROUPS * SSM_STATE
    step_rows = SSD_STEP_CHUNKS * SSD_CHUNK
    nc = s_len // step_rows
    padl = lambda v: jnp.pad(v.astype(F32), (0, LANES - n_heads))[None, :]
    expand = (jnp.arange(LANES)[:, None] == (jnp.arange(d_ssm) // SSM_HEAD_DIM)[None, :]).astype(BF16)
    tril = (jnp.arange(SSD_CHUNK)[:, None] >= jnp.arange(SSD_CHUNK)[None, :]).astype(BF16)
    full = lambda shape: pl.BlockSpec(shape, lambda b, c: (0, 0))
    return pl.pallas_call(
        _ssd_kernel,
        grid=(bsz, nc),
        in_specs=[pl.BlockSpec((step_rows, d_conv), lambda b, c: (b * nc + c, xbc_col // d_conv)),
                  pl.BlockSpec((step_rows, d_ssm), lambda b, c: (b * nc + c, z_col // d_ssm)),
                  pl.BlockSpec((step_rows, LANES), lambda b, c: (b * nc + c, 0)),
                  full((CONV_WIDTH, d_conv)), full((1, d_conv)), full((1, LANES)), full((1, LANES)),
                  full((1, d_ssm)), full((1, d_ssm)), full((LANES, d_ssm)), full((SSD_CHUNK, SSD_CHUNK))],
        out_specs=pl.BlockSpec((step_rows, d_ssm), lambda b, c: (b * nc + c, 0)),
        out_shape=jax.ShapeDtypeStruct((bsz * s_len, d_ssm), BF16),
        scratch_shapes=[pltpu.VMEM((d_conv // LANES, step_rows + 8, LANES), F32),
                        pltpu.VMEM((step_rows, d_conv), F32),
                        pltpu.VMEM((SSM_STATE, d_ssm), F32)],
        compiler_params=_cparams(("parallel", "arbitrary")),
        name="ssd",
    )(packed, packed, dt_raw, conv_w.astype(F32), conv_b.astype(F32)[None, :], padl(dt_bias), padl(a_log),
      jnp.repeat(d_skip.astype(F32), SSM_HEAD_DIM)[None, :], g_out.astype(F32)[None, :], expand, tril)


def _out_proj_kernel(attn_ref, ssm_ref, x_ref, g_ref, w_ref, o_ref, wb_ref, an_ref):
    d_attn = attn_ref.shape[1]

    @pl.when(pl.program_id(0) == 0)
    def _():
        wb_ref[...] = w_ref[...].astype(BF16)

    rows = 32

    def body(c, carry):
        rs = pl.ds(pl.multiple_of(c * rows, rows), rows)
        an_ref[rs, :] = _rms_rows(attn_ref[rs, :].astype(F32), g_ref[...]).astype(BF16)
        return carry
    lax.fori_loop(0, attn_ref.shape[0] // rows, body, 0, unroll=4)
    o_ref[...] = (x_ref[...] + jnp.dot(an_ref[...], wb_ref[:d_attn, :], preferred_element_type=F32)
                  + jnp.dot(ssm_ref[...], wb_ref[d_attn:, :], preferred_element_type=F32))


def out_proj(attn, ssm, x2d, g, w, tm=512):
    m, d = x2d.shape
    da, ds_ = attn.shape[1], ssm.shape[1]
    return pl.pallas_call(
        _out_proj_kernel,
        grid=(m // tm,),
        in_specs=[pl.BlockSpec((tm, da), lambda i: (i, 0)),
                  pl.BlockSpec((tm, ds_), lambda i: (i, 0)),
                  pl.BlockSpec((tm, d), lambda i: (i, 0)),
                  pl.BlockSpec((1, da), lambda i: (0, 0)),
                  pl.BlockSpec((da + ds_, d), lambda i: (0, 0), pipeline_mode=pl.Buffered(1))],
        out_specs=pl.BlockSpec((tm, d), lambda i: (i, 0)),
        out_shape=jax.ShapeDtypeStruct((m, d), F32),
        scratch_shapes=[pltpu.VMEM((da + ds_, d), BF16), pltpu.VMEM((tm, da), BF16)],
        compiler_params=_cparams(("arbitrary",)),
        name="out_proj",
    )(attn, ssm, x2d, g, w)


def _mem_kv_kernel(mem_ref, g_ref, w_ref, gk_ref, kv_ref, h_ref):
    d_cross = N_CROSS_HEADS * CROSS_HEAD_DIM
    _norm_rows_to(mem_ref, g_ref, h_ref)
    kv = jnp.dot(h_ref[...], w_ref[...], preferred_element_type=F32)
    for h in range(N_CROSS_HEADS):
        cols = slice(h * CROSS_HEAD_DIM, (h + 1) * CROSS_HEAD_DIM)
        kv_ref[:, cols] = _rms_rows(kv[:, cols], gk_ref[...]).astype(kv_ref.dtype)
    kv_ref[:, d_cross:] = kv[:, d_cross:].astype(kv_ref.dtype)


def mem_kv(mem2d, g, w, gk, tm=256):
    m, d = mem2d.shape
    n = w.shape[1]
    return pl.pallas_call(
        _mem_kv_kernel,
        grid=(m // tm,),
        in_specs=[pl.BlockSpec((tm, d), lambda i: (i, 0)),
                  pl.BlockSpec((1, d), lambda i: (0, 0)),
                  pl.BlockSpec((d, n), lambda i: (0, 0)),
                  pl.BlockSpec((1, CROSS_HEAD_DIM), lambda i: (0, 0))],
        out_specs=pl.BlockSpec((tm, n), lambda i: (i, 0)),
        out_shape=jax.ShapeDtypeStruct((m, n), BF16),
        scratch_shapes=[pltpu.VMEM((tm, d), BF16)],
        compiler_params=_cparams(("parallel",)),
        name="mem_kv",
    )(mem2d, g, w, gk)


def _cross_kernel(x_ref, xn_ref, g_ref, wq_ref, gq_ref, k_ref, v_ref, wo_ref, o_ref, h_ref, q_ref, a_ref, s_scr, p_scr):
    i = pl.program_id(0)
    tm = x_ref.shape[0]
    rows = CROSS_ROWS
    cur, nxt = i % 2, (i + 1) % 2

    @pl.when(i == 0)
    def _():
        _norm_rows_to(x_ref, g_ref, h_ref.at[0])

    q_ref[...] = jnp.dot(h_ref[cur], wq_ref[...], preferred_element_type=F32)

    def scores(item, slot):
        r0, h = item
        cols = slice(h * CROSS_HEAD_DIM, (h + 1) * CROSS_HEAD_DIM)
        qh = (_rms_rows(q_ref[r0:r0 + rows, cols], gq_ref[...]) * (CROSS_HEAD_DIM ** -0.5)).astype(BF16)
        s_scr[slot] = lax.dot_general(qh, k_ref[:, cols], (((1,), (1,)), ((), ())), preferred_element_type=F32)

    def softmax(item, slot):
        s = s_scr[slot]
        p_scr[slot] = jnp.exp(s - jnp.max(s, axis=-1, keepdims=True)).astype(BF16)

    def values(item, slot):
        r0, h = item
        cols = slice(h * CROSS_HEAD_DIM, (h + 1) * CROSS_HEAD_DIM)
        vh = v_ref[:, cols]
        r = jnp.dot(p_scr[slot], jnp.concatenate([vh, jnp.ones_like(vh)], axis=1), preferred_element_type=F32)
        a_ref[r0:r0 + rows, cols] = (r[:, :CROSS_HEAD_DIM] / r[:, CROSS_HEAD_DIM:]).astype(BF16)

    items = [(r0, h) for r0 in range(0, tm, rows) for h in range(N_CROSS_HEADS)]
    _software_pipeline((scores, softmax, values), items, CROSS_GROUP)
    _norm_chunk_to(xn_ref, g_ref, h_ref.at[nxt], 0, tm)
    o_ref[...] = x_ref[...] + jnp.dot(a_ref[...], wo_ref[...], preferred_element_type=F32)


def cross_attention(x2d, g, wq, gq, kv, wo, s_len, n_mem, tm=512):
    m, d = x2d.shape
    dc = wq.shape[1]
    per_batch = s_len // tm
    full = lambda shape: pl.BlockSpec(shape, lambda i: (0, 0))
    return pl.pallas_call(
        _cross_kernel,
        grid=(m // tm,),
        in_specs=[pl.BlockSpec((tm, d), lambda i: (i, 0)),
                  pl.BlockSpec((tm, d), lambda i: (jnp.minimum(i + 1, m // tm - 1), 0)),
                  full((1, d)), full((d, dc)), full((1, CROSS_HEAD_DIM)),
                  pl.BlockSpec((n_mem, dc), lambda i: (i // per_batch, 0)),
                  pl.BlockSpec((n_mem, dc), lambda i: (i // per_batch, 1)),
                  full((dc, d))],
        out_specs=pl.BlockSpec((tm, d), lambda i: (i, 0)),
        out_shape=jax.ShapeDtypeStruct((m, d), F32),
        scratch_shapes=[pltpu.VMEM((2, tm, d), BF16), pltpu.VMEM((tm, dc), F32), pltpu.VMEM((tm, dc), BF16),
                        pltpu.VMEM((2 * CROSS_GROUP, CROSS_ROWS, n_mem), F32),
                        pltpu.VMEM((2 * CROSS_GROUP, CROSS_ROWS, n_mem), BF16)],
        compiler_params=_cparams(("arbitrary",)),
        name="cross_attn",
    )(x2d, x2d, g, wq, gq, kv, kv, wo)


def _mlp_kernel(x_ref, g_ref, wu_ref, wd_ref, o_ref, h_ref):
    @pl.when(pl.program_id(1) == 0)
    def _():
        _norm_rows_to(x_ref, g_ref, h_ref)
        o_ref[...] = x_ref[...]
    u = jnp.dot(h_ref[...], wu_ref[...], preferred_element_type=F32)
    u = jnp.square(jnp.maximum(u, 0.0)).astype(BF16)
    o_ref[...] += jnp.dot(u, wd_ref[...], preferred_element_type=F32)


def mlp(x2d, g, wu, wd, tm=512, tf=1024):
    m, d = x2d.shape
    f = wu.shape[1]
    return pl.pallas_call(
        _mlp_kernel,
        grid=(m // tm, f // tf),
        in_specs=[pl.BlockSpec((tm, d), lambda i, j: (i, 0)),
                  pl.BlockSpec((1, d), lambda i, j: (0, 0)),
                  pl.BlockSpec((d, tf), lambda i, j: (0, j)),
                  pl.BlockSpec((tf, d), lambda i, j: (j, 0))],
        out_specs=pl.BlockSpec((tm, d), lambda i, j: (i, 0)),
        out_shape=jax.ShapeDtypeStruct((m, d), F32),
        scratch_shapes=[pltpu.VMEM((tm, d), BF16)],
        compiler_params=_cparams(("parallel", "arbitrary")),
        name="mlp",
    )(x2d, g, wu, wd)


def _layer(x2d, mem2d, pos_col, bsz, s_len, n_mem, g_mix, w_in, g_q, g_k, g_attn_out, conv_w, conv_b, dt_bias,
           a_log, d_skip, g_ssm_out, w_out, g_cross, g_mem, w_cq, w_ckv, g_cq, g_ck, w_co, g_mlp, w_up, w_down):
    d_model = x2d.shape[1]
    d_attn = d_model // 2
    d_ssm = d_model // 2
    n_pairs = d_attn // LANES
    n_ssm_heads = d_ssm // SSM_HEAD_DIM
    d_conv = d_ssm + 2 * SSM_GROUPS * SSM_STATE
    d_packed = 3 * d_attn + d_ssm + d_conv
    row = lambda v: v.astype(F32)[None, :]

    w_in_t = w_in.T
    w_dt_t = jnp.pad(w_in_t[d_packed:], ((0, LANES - n_ssm_heads), (0, 0)))
    packed, dt_raw = in_proj(x2d, row(g_mix), w_in_t[:d_packed].astype(BF16), w_dt_t, d_packed)

    tabs = rope_tables(pos_col)
    attn, w_up_bf, w_down_bf = dilated_attention(
        packed, tabs, row(jnp.tile(g_q, 2)) * (ATTN_HEAD_DIM ** -0.5 * LOG2E), row(jnp.tile(g_k, 2)), w_up, w_down,
        bsz, s_len, n_pairs)
    ssm = ssd_mixer(packed, dt_raw, conv_w, conv_b, dt_bias, a_log, d_skip, g_ssm_out, bsz, s_len, d_ssm,
                    xbc_col=3 * d_attn + d_ssm, z_col=3 * d_attn)
    x2d = out_proj(attn, ssm, x2d, row(g_attn_out), w_out)

    kv = mem_kv(mem2d, row(g_mem), w_ckv.astype(BF16), row(g_ck))
    x2d = cross_attention(x2d, row(g_cross), w_cq.astype(BF16), row(g_cq), kv, w_co.astype(BF16), s_len, n_mem)

    return mlp(x2d, row(g_mlp), w_up_bf, w_down_bf)


def kernel(x, mem, positions, g_mix, w_in, g_q, g_k, g_attn_out, conv_w, conv_b, dt_bias, a_log, d_skip, g_ssm_out,
           w_out, g_cross, g_mem, w_cq, w_ckv, g_cq, g_ck, w_co, g_mlp, w_up, w_down):
    bsz, s_len, d_model = x.shape
    n_mem = mem.shape[1]
    x2d = x.reshape(bsz * s_len, d_model)
    mem2d = mem.reshape(bsz * n_mem, d_model)
    pos_col = positions.reshape(bsz * s_len, 1)
    for i in range(g_mix.shape[0]):
        x2d = _layer(x2d, mem2d, pos_col, bsz, s_len, n_mem, g_mix[i], w_in[i], g_q[i], g_k[i], g_attn_out[i],
                     conv_w[i], conv_b[i], dt_bias[i], a_log[i], d_skip[i], g_ssm_out[i], w_out[i], g_cross[i],
                     g_mem[i], w_cq[i], w_ckv[i], g_cq[i], g_ck[i], w_co[i], g_mlp[i], w_up[i], w_down[i])
    return x2d.reshape(bsz, s_len, d_model)
```

```python
import functools
import math

import jax
import jax.numpy as jnp
from jax import lax
from jax.experimental import pallas as pl
from jax.experimental.pallas import tpu as pltpu

F32 = jnp.float32
BF16 = jnp.bfloat16
EPS = 1e-6

LANES = 128
ATTN_HEAD_DIM = 64
ROT_DIM = ATTN_HEAD_DIM // 4
ROPE_THETA = 500000.0
ATTN_BLOCK = 128
DILATIONS = (1, 4, 16)
SSM_HEAD_DIM = 64
SSM_GROUPS = 4
SSM_STATE = 128
CONV_WIDTH = 4
SSD_CHUNK = 128
N_CROSS_HEADS = 4
CROSS_HEAD_DIM = 128
CROSS_ROWS = 256
CROSS_GROUP = 2
VMEM_LIMIT = 58 * 1024 * 1024


def _cparams(sem):
    return pltpu.CompilerParams(dimension_semantics=sem, vmem_limit_bytes=VMEM_LIMIT)


def _rms_rows(x, g):
    ms = jnp.mean(x * x, axis=-1, keepdims=True)
    return x * lax.rsqrt(ms + EPS) * g


def _norm_rows_to(x_ref, g_ref, h_ref, rows=16):
    def body(c, carry):
        r0 = pl.multiple_of(c * rows, rows)
        h_ref[pl.ds(r0, rows), :] = _rms_rows(x_ref[pl.ds(r0, rows), :], g_ref[...]).astype(h_ref.dtype)
        return carry
    lax.fori_loop(0, x_ref.shape[0] // rows, body, 0, unroll=4)


def _norm_chunk_to(x_ref, g_ref, h_ref, start, n_rows, rows=16):
    for r in range(0, n_rows, rows):
        rs = pl.ds(start + r, rows)
        h_ref[rs, :] = _rms_rows(x_ref[rs, :], g_ref[...]).astype(h_ref.dtype)


def _in_proj_kernel(x0_ref, xn_ref, g_ref, w_ref, wdt_ref, o_ref, dt_ref, h_ref):
    i, j = pl.program_id(0), pl.program_id(1)
    tm = xn_ref.shape[0]
    cur, nxt = i % 2, (i + 1) % 2

    @pl.when((i == 0) & (j == 0))
    def _():
        _norm_rows_to(x0_ref, g_ref, h_ref.at[0])

    @pl.when(j == 0)
    def _():
        dt_ref[...] = lax.dot_general(h_ref[cur], wdt_ref[...].astype(BF16), (((1,), (1,)), ((), ())),
                                      preferred_element_type=F32)
    o_ref[...] = lax.dot_general(h_ref[cur], w_ref[...].astype(BF16), (((1,), (1,)), ((), ())),
                                 preferred_element_type=F32).astype(o_ref.dtype)
    chunk = tm // IN_PROJ_NORM_STEPS
    start = pl.multiple_of(jnp.minimum(j, IN_PROJ_NORM_STEPS - 1) * chunk, chunk)
    _norm_chunk_to(xn_ref, g_ref, h_ref.at[nxt], start, chunk)


IN_PROJ_NORM_STEPS = 8


def in_proj(x2d, g, w_t, w_dt, n, tm=1024, tn=768):
    m, d = x2d.shape
    n_i = m // tm
    assert n // tn >= IN_PROJ_NORM_STEPS
    return pl.pallas_call(
        _in_proj_kernel,
        grid=(n_i, n // tn),
        in_specs=[pl.BlockSpec((tm, d), lambda i, j: (0, 0), pipeline_mode=pl.Buffered(1)),
                  pl.BlockSpec((tm, d), lambda i, j: (jnp.minimum(i + 1, n_i - 1), 0)),
                  pl.BlockSpec((1, d), lambda i, j: (0, 0)),
                  pl.BlockSpec((tn, d), lambda i, j: (j, 0)),
                  pl.BlockSpec((LANES, d), lambda i, j: (0, 0), pipeline_mode=pl.Buffered(1))],
        out_specs=[pl.BlockSpec((tm, tn), lambda i, j: (i, j)),
                   pl.BlockSpec((tm, LANES), lambda i, j: (i, 0))],
        out_shape=[jax.ShapeDtypeStruct((m, n), BF16), jax.ShapeDtypeStruct((m, LANES), F32)],
        scratch_shapes=[pltpu.VMEM((2, tm, d), BF16)],
        compiler_params=_cparams(("arbitrary", "arbitrary")),
        name="in_proj",
    )(x2d, x2d, g, w_t, w_dt)


def _rope_kernel(pos_ref, expo_ref, cmask_ref, cos_ref, sin_ref):
    inv_freq = jnp.power(jnp.float32(ROPE_THETA), expo_ref[...])
    ang = pos_ref[...].astype(F32) * inv_freq
    cm = cmask_ref[...]
    cos_ref[...] = jnp.cos(ang) * cm + (1.0 - cm)
    sin_ref[...] = jnp.sin(ang) * cm


def rope_tables(pos_col, tm=2048):
    m = pos_col.shape[0]
    half = ROT_DIM // 2
    d = jnp.arange(LANES) % ATTN_HEAD_DIM
    expo = (-2.0 * (d % half).astype(F32) / ROT_DIM)[None, :]
    cmask = (d < ROT_DIM).astype(F32)[None, :]
    row = pl.BlockSpec((1, LANES), lambda i: (0, 0))
    tab = pl.BlockSpec((tm, LANES), lambda i: (i, 0))
    return pl.pallas_call(
        _rope_kernel,
        grid=(m // tm,),
        in_specs=[pl.BlockSpec((tm, 1), lambda i: (i, 0)), row, row],
        out_specs=[tab, tab],
        out_shape=[jax.ShapeDtypeStruct((m, LANES), F32)] * 2,
        compiler_params=_cparams(("parallel",)),
        name="rope_tab",
    )(pos_col, expo, cmask)


def _rotate_half_matrix():
    half = ROT_DIM // 2
    src = jnp.arange(LANES)[:, None]
    dst = jnp.arange(LANES)[None, :]
    d = dst % ATTN_HEAD_DIM
    first = (d < half) & (src == dst + half)
    second = (d >= half) & (d < ROT_DIM) & (src == dst - half)
    return (second.astype(F32) - first.astype(F32)).astype(BF16)


def _rows(start, size, dil):
    return pl.ds(start, size) if dil == 1 else pl.ds(start, size, stride=dil)


ATTN_GROUP = 2


def _software_pipeline(stages, items, group, extras=()):
    groups = [items[i:i + group] for i in range(0, len(items), group)]
    n_trips = len(groups) + len(stages) - 1
    for t in range(n_trips):
        for lag, stage in enumerate(stages):
            if 0 <= t - lag < len(groups):
                for g, item in enumerate(groups[t - lag]):
                    stage(item, ((t - lag) % 2) * group + g)
        for k, extra in enumerate(extras):
            if k * n_trips // len(extras) == t:
                extra()


def _attn_kernel(q0_ref, k0_ref, v0_ref, cos0_ref, sin0_ref, qn_ref, kn_ref, vn_ref, cosn_ref, sinn_ref,
                 gq_ref, gk_ref, seg_ref, rot_ref, wu_ref, wd_ref, o_ref, wu_out, wd_out,
                 set_a, set_b, ob, mb, lb, s_scr, p_scr):
    s_len = qn_ref.shape[0]
    blk = ATTN_BLOCK
    step = pl.program_id(0) * pl.num_programs(1) + pl.program_id(1)
    lane = lax.broadcasted_iota(jnp.int32, (blk, LANES), 1)
    head0 = lane < ATTN_HEAD_DIM
    qi = lax.broadcasted_iota(jnp.int32, (2 * blk, 2 * blk), 0) % blk
    kj = lax.broadcasted_iota(jnp.int32, (2 * blk, 2 * blk), 1)
    band_mask = (kj >= qi) & (kj <= qi + blk)
    first_mask = (lax.broadcasted_iota(jnp.int32, (2 * blk, blk), 1)
                  <= lax.broadcasted_iota(jnp.int32, (2 * blk, blk), 0) % blk)

    wu_out[...] = wu_ref[...].astype(BF16)
    wd_out[...] = wd_ref[...].astype(BF16)

    prep_rows = 256

    def prep(srcs, dst, r0):
        q_ref, k_ref, v_ref, cos_ref, sin_ref = srcs
        qf, kf, vf = dst[:3]
        rows = slice(r0, r0 + prep_rows)
        cs, sn = cos_ref[rows, :], sin_ref[rows, :]

        def norm_rope(x_ref, g_ref):
            x = x_ref[rows, :].astype(F32)
            ss = jnp.dot((x * x).astype(BF16), seg_ref[...], preferred_element_type=F32)
            y = x * lax.rsqrt(ss * (1.0 / ATTN_HEAD_DIM) + EPS) * g_ref[...]
            return y * cs + jnp.dot(y.astype(BF16), rot_ref[...], preferred_element_type=F32) * sn

        qf[rows, :] = norm_rope(q_ref, gq_ref)
        kf[rows, :] = norm_rope(k_ref, gk_ref)
        vf[rows, :] = v_ref[rows, :].astype(F32)

    dmid = DILATIONS[1]
    run = s_len // dmid

    def deinterleave(bufs, r):
        for src, dst in zip(bufs[:3], bufs[3:]):
            dst[r * run:(r + 1) * run, :] = src[pl.ds(r, run, stride=dmid), :]

    def prepare(srcs, bufs):
        return ([functools.partial(prep, srcs, bufs, r0) for r0 in range(0, s_len, prep_rows)]
                + [functools.partial(deinterleave, bufs, r) for r in range(dmid)])

    @pl.when(step == 0)
    def _():
        for piece in prepare((q0_ref, k0_ref, v0_ref, cos0_ref, sin0_ref), set_a):
            piece()

    def block_descs(bufs):
        natural, deint = bufs[:3], bufs[3:]
        descs = [(deint, 1, 1, r * run, r * run, blk, first_mask) for r in range(dmid)]
        descs += [(deint, 1, 1, r * run + n * blk, r * run + (n - 1) * blk, 2 * blk, band_mask)
                  for n in range(1, run // blk) for r in range(dmid)]
        descs += [(deint, 2, dmid, r_hi * run + r_lo, r_hi * run + r_lo, blk, first_mask)
                  for r_hi in range(dmid) for r_lo in range(dmid)]
        descs += [(natural, 0, 1, 0, 0, blk, first_mask)]
        descs += [(natural, 0, 1, n * blk, (n - 1) * blk, 2 * blk, band_mask) for n in range(1, s_len // blk)]
        return descs

    def scores(desc, slot):
        (q_src, k_src, _), _, dil, q_start, k_start, n_keys, _ = desc
        qt = q_src[_rows(q_start, blk, dil), :]
        zero = jnp.zeros_like(qt)
        q2 = jnp.concatenate([jnp.where(head0, qt, zero), jnp.where(head0, zero, qt)], axis=0).astype(BF16)
        kb = k_src[_rows(k_start, n_keys, dil), :].astype(BF16)
        s_scr[slot, :, :n_keys] = lax.dot_general(q2, kb, (((1,), (1,)), ((), ())), preferred_element_type=F32)

    def softmax(desc, slot):
        _, br, dil, q_start, _, n_keys, mask = desc
        s = jnp.where(mask, s_scr[slot, :, :n_keys], -jnp.inf)
        m = jnp.max(s, axis=-1, keepdims=True)
        p_scr[slot, :, :n_keys] = jnp.exp2(s - m).astype(BF16)
        mb[br, _rows(q_start, blk, dil), :] = jnp.where(head0, m[:blk], m[blk:])

    def values(desc, slot):
        (_, _, v_src), br, dil, q_start, k_start, n_keys, _ = desc
        vb = v_src[_rows(k_start, n_keys, dil), :].astype(BF16)
        v1 = jnp.concatenate([vb, jnp.ones_like(vb)], axis=1)
        r = jnp.dot(p_scr[slot, :, :n_keys], v1, preferred_element_type=F32)
        out_rows = _rows(q_start, blk, dil)
        ob[br, out_rows, :] = jnp.where(head0, r[:blk, :LANES], r[blk:, :LANES])
        lb[br, out_rows, :] = jnp.where(head0, r[:blk, LANES:], r[blk:, LANES:])
        if br == 0:
            merge(q_start)

    def merge(n0):
        sub = blk // dmid
        for r in range(dmid):
            nat_rows = pl.ds(n0 + r, sub, stride=dmid)
            d_rows = slice(r * run + n0 // dmid, r * run + n0 // dmid + sub)
            rows = (nat_rows, d_rows, d_rows)
            ms = [mb[g, rows[g], :] for g in range(len(DILATIONS))]
            m = functools.reduce(jnp.maximum, ms)
            ws = [jnp.exp2(mg - m) for mg in ms]
            num = functools.reduce(jnp.add, [w * ob[g, rows[g], :] for g, w in enumerate(ws)])
            den = functools.reduce(jnp.add, [w * lb[g, rows[g], :] for g, w in enumerate(ws)])
            ob[0, nat_rows, :] = num / den
        o_ref[n0:n0 + blk, :] = ob[0, n0:n0 + blk, :].astype(o_ref.dtype)

    for parity, (cur_set, nxt_set) in enumerate(((set_a, set_b), (set_b, set_a))):
        @pl.when(step % 2 == parity)
        def _(cur_set=cur_set, nxt_set=nxt_set):
            _software_pipeline((scores, softmax, values), block_descs(cur_set), ATTN_GROUP,
                               extras=prepare((qn_ref, kn_ref, vn_ref, cosn_ref, sinn_ref), nxt_set))


def dilated_attention(qkv, tabs, gq2, gk2, w_up, w_down, bsz, s_len, n_pairs):
    cos_t, sin_t = tabs
    d1, d2, d3 = DILATIONS
    assert d1 == 1 and d3 == d2 * d2 and s_len == d3 * ATTN_BLOCK, "layout assumes dilations (1, d, d*d), one block per largest class"
    lane = jnp.arange(LANES)
    seg = (lane[:, None] // ATTN_HEAD_DIM == lane[None, :] // ATTN_HEAD_DIM).astype(BF16)
    n_steps = bsz * n_pairs
    wu_rows, wd_rows = w_up.shape[0] // n_steps, w_down.shape[0] // n_steps

    def nxt(b, p):
        s = jnp.minimum(b * n_pairs + p + 1, n_steps - 1)
        return s // n_pairs, s % n_pairs

    once = dict(pipeline_mode=pl.Buffered(1))
    first = lambda off: pl.BlockSpec((s_len, LANES), lambda b, p, off=off: (0, off), **once)
    nblk = lambda off: pl.BlockSpec((s_len, LANES), lambda b, p, off=off: (nxt(b, p)[0], off + nxt(b, p)[1]))
    tab0 = pl.BlockSpec((s_len, LANES), lambda b, p: (0, 0), **once)
    tabn = pl.BlockSpec((s_len, LANES), lambda b, p: (nxt(b, p)[0], 0))
    row = pl.BlockSpec((1, LANES), lambda b, p: (0, 0))
    sq = pl.BlockSpec((LANES, LANES), lambda b, p: (0, 0))
    wu_spec = pl.BlockSpec((wu_rows, w_up.shape[1]), lambda b, p: (b * n_pairs + p, 0))
    wd_spec = pl.BlockSpec((wd_rows, w_down.shape[1]), lambda b, p: (b * n_pairs + p, 0))
    return pl.pallas_call(
        _attn_kernel,
        grid=(bsz, n_pairs),
        in_specs=[first(0), first(n_pairs), first(2 * n_pairs), tab0, tab0,
                  nblk(0), nblk(n_pairs), nblk(2 * n_pairs), tabn, tabn,
                  row, row, sq, sq, wu_spec, wd_spec],
        out_specs=[pl.BlockSpec((s_len, LANES), lambda b, p: (b, p)), wu_spec, wd_spec],
        out_shape=[jax.ShapeDtypeStruct((bsz * s_len, n_pairs * LANES), BF16),
                   jax.ShapeDtypeStruct(w_up.shape, BF16), jax.ShapeDtypeStruct(w_down.shape, BF16)],
        scratch_shapes=[[pltpu.VMEM((s_len, LANES), F32)] * 6] * 2
                       + [pltpu.VMEM((len(DILATIONS), s_len, LANES), F32)] * 3
                       + [pltpu.VMEM((2 * ATTN_GROUP, 2 * ATTN_BLOCK, 2 * ATTN_BLOCK), F32),
                          pltpu.VMEM((2 * ATTN_GROUP, 2 * ATTN_BLOCK, 2 * ATTN_BLOCK), BF16)],
        compiler_params=_cparams(("arbitrary", "arbitrary")),
        name="dilated_attn",
    )(qkv, qkv, qkv, cos_t, sin_t, qkv, qkv, qkv, cos_t, sin_t, gq2, gk2, seg, _rotate_half_matrix(), w_up, w_down)


def _split3(x):
    hi = x.astype(BF16)
    r1 = x - hi.astype(F32)
    mid = r1.astype(BF16)
    lo = (r1 - mid.astype(F32)).astype(BF16)
    return hi, mid, lo


def _silu(x):
    h = 0.5 * x
    return h * jnp.tanh(h) + h


LOG2E = 1.4426950408889634
SSD_STEP_CHUNKS = 4


def _ssd_kernel(xbc_ref, z_ref, dt_ref, cw_ref, cb_ref, dtb_ref, alog_ref, dskip_ref, g_ref,
                expand_ref, tril_ref, o_ref, xpad, xc, st):
    q = SSD_CHUNK
    n_rows = xbc_ref.shape[0]
    d_ssm = z_ref.shape[1]
    d_conv = xbc_ref.shape[1]
    gw = d_ssm // SSM_GROUPS
    heads_per_group = gw // SSM_HEAD_DIM
    pad = 8

    @pl.when(pl.program_id(1) == 0)
    def _():
        xpad[:, 0:pad, :] = jnp.zeros((d_conv // LANES, pad, LANES), F32)
        st[...] = jnp.zeros_like(st)

    for c0 in range(0, d_conv, LANES):
        cols = slice(c0, c0 + LANES)
        slab = c0 // LANES
        xpad[slab, pad:pad + n_rows, :] = xbc_ref[:, cols].astype(F32)
        acc = cb_ref[:, cols] + cw_ref[CONV_WIDTH - 1:CONV_WIDTH, cols] * xpad[slab, pad:pad + n_rows, :]
        for w in range(CONV_WIDTH - 1):
            off = pad - (CONV_WIDTH - 1) + w
            acc = acc + cw_ref[w:w + 1, cols] * xpad[slab, pl.ds(off, n_rows, stride=1), :]
        xc[:, cols] = _silu(acc)
    xpad[:, 0:pad, :] = xpad[:, n_rows:n_rows + pad, :]

    li = lax.broadcasted_iota(jnp.int32, (q, q), 0)
    si = lax.broadcasted_iota(jnp.int32, (q, q), 1)
    causal = li >= si
    lane_g = lax.broadcasted_iota(jnp.int32, (q, gw), 1)
    tril = tril_ref[...]
    expand = expand_ref[...]

    for r0 in range(0, n_rows, q):
        rows = slice(r0, r0 + q)
        x_dt = dt_ref[rows, :] + dtb_ref[...]
        dt = jnp.maximum(x_dt, 0.0) + jnp.log1p(jnp.exp(-jnp.abs(x_dt)))
        dta = dt * (-LOG2E * jnp.exp(alog_ref[...]))
        acs = sum(jnp.dot(tril, part, preferred_element_type=F32) for part in _split3(dta))
        acs_t = acs.T
        last = acs[q - 1:q, :]
        dt_e = jnp.dot(dt.astype(BF16), expand, preferred_element_type=F32)
        dec_e = jnp.dot(jnp.exp2(acs).astype(BF16), expand, preferred_element_type=F32)
        w_e = jnp.dot((jnp.exp2(last - acs) * dt).astype(BF16), expand, preferred_element_type=F32)

        for g in range(SSM_GROUPS):
            cols = slice(g * gw, (g + 1) * gw)
            b_f = xc[rows, d_ssm + g * SSM_STATE:d_ssm + (g + 1) * SSM_STATE]
            c_b = xc[rows, d_ssm + (SSM_GROUPS + g) * SSM_STATE:d_ssm + (SSM_GROUPS + g + 1) * SSM_STATE].astype(BF16)
            xs = xc[rows, cols]
            cb = lax.dot_general(c_b, b_f.astype(BF16), (((1,), (1,)), ((), ())), preferred_element_type=F32)
            xdt = (xs * dt_e[:, cols]).astype(BF16)
            ws, rs = [], []
            for hh in range(heads_per_group):
                h = g * heads_per_group + hh
                seg = acs[:, h:h + 1] - acs_t[h:h + 1, :]
                l_mat = jnp.exp2(jnp.where(causal, seg, -jnp.inf))
                ws.append((cb * l_mat).astype(BF16))
                in_head = (lane_g >= hh * SSM_HEAD_DIM) & (lane_g < (hh + 1) * SSM_HEAD_DIM)
                rs.append(jnp.where(in_head, xdt, jnp.zeros_like(xdt)))
            y = jnp.dot(jnp.concatenate(ws, axis=1), jnp.concatenate(rs, axis=0), preferred_element_type=F32)
            st_g = st[:, cols]
            y = y + jnp.dot(c_b, st_g.astype(BF16), preferred_element_type=F32) * dec_e[:, cols]
            xw = (xs * w_e[:, cols]).astype(BF16)
            st[:, cols] = st_g * dec_e[q - 1:q, cols] + jnp.dot(b_f.T.astype(BF16), xw, preferred_element_type=F32)
            y = y + dskip_ref[:, cols] * xs
            y = y * _silu(z_ref[rows, cols].astype(F32))
            o_ref[rows, cols] = _rms_rows(y, g_ref[:, cols]).astype(o_ref.dtype)


def ssd_mixer(packed, dt_raw, conv_w, conv_b, dt_bias, a_log, d_skip, g_out, bsz, s_len, d_ssm, xbc_col, z_col):
    n_heads = d_ssm // SSM_HEAD_DIM
    d_conv = d_ssm + 2 * SSM_GROUPS * SSM_STATE
    step_rows = SSD_STEP_CHUNKS * SSD_CHUNK
    nc = s_len // step_rows
    padl = lambda v: jnp.pad(v.astype(F32), (0, LANES - n_heads))[None, :]
    expand = (jnp.arange(LANES)[:, None] == (jnp.arange(d_ssm) // SSM_HEAD_DIM)[None, :]).astype(BF16)
    tril = (jnp.arange(SSD_CHUNK)[:, None] >= jnp.arange(SSD_CHUNK)[None, :]).astype(BF16)
    full = lambda shape: pl.BlockSpec(shape, lambda b, c: (0, 0))
    return pl.pallas_call(
        _ssd_kernel,
        grid=(bsz, nc),
        in_specs=[pl.BlockSpec((step_rows, d_conv), lambda b, c: (b * nc + c, xbc_col // d_conv)),
                  pl.BlockSpec((step_rows, d_ssm), lambda b, c: (b * nc + c, z_col // d_ssm)),
                  pl.BlockSpec((step_rows, LANES), lambda b, c: (b * nc + c, 0)),
                  full((CONV_WIDTH, d_conv)), full((1, d_conv)), full((1, LANES)), full((1, LANES)),
                  full((1, d_ssm)), full((1, d_ssm)), full((LANES, d_ssm)), full((SSD_CHUNK, SSD_CHUNK))],
        out_specs=pl.BlockSpec((step_rows, d_ssm), lambda b, c: (b * nc + c, 0)),
        out_shape=jax.ShapeDtypeStruct((bsz * s_len, d_ssm), BF16),
        scratch_shapes=[pltpu.VMEM((d_conv // LANES, step_rows + 8, LANES), F32),
                        pltpu.VMEM((step_rows, d_conv), F32),
                        pltpu.VMEM((SSM_STATE, d_ssm), F32)],
        compiler_params=_cparams(("parallel", "arbitrary")),
        name="ssd",
    )(packed, packed, dt_raw, conv_w.astype(F32), conv_b.astype(F32)[None, :], padl(dt_bias), padl(a_log),
      jnp.repeat(d_skip.astype(F32), SSM_HEAD_DIM)[None, :], g_out.astype(F32)[None, :], expand, tril)


def _out_proj_kernel(attn_ref, ssm_ref, x_ref, g_ref, w_ref, o_ref, wb_ref, an_ref):
    d_attn = attn_ref.shape[1]

    @pl.when(pl.program_id(0) == 0)
    def _():
        wb_ref[...] = w_ref[...].astype(BF16)

    rows = 32

    def body(c, carry):
        rs = pl.ds(pl.multiple_of(c * rows, rows), rows)
        an_ref[rs, :] = _rms_rows(attn_ref[rs, :].astype(F32), g_ref[...]).astype(BF16)
        return carry
    lax.fori_loop(0, attn_ref.shape[0] // rows, body, 0, unroll=4)
    o_ref[...] = (x_ref[...] + jnp.dot(an_ref[...], wb_ref[:d_attn, :], preferred_element_type=F32)
                  + jnp.dot(ssm_ref[...], wb_ref[d_attn:, :], preferred_element_type=F32))


def out_proj(attn, ssm, x2d, g, w, tm=512):
    m, d = x2d.shape
    da, ds_ = attn.shape[1], ssm.shape[1]
    return pl.pallas_call(
        _out_proj_kernel,
        grid=(m // tm,),
        in_specs=[pl.BlockSpec((tm, da), lambda i: (i, 0)),
                  pl.BlockSpec((tm, ds_), lambda i: (i, 0)),
                  pl.BlockSpec((tm, d), lambda i: (i, 0)),
                  pl.BlockSpec((1, da), lambda i: (0, 0)),
                  pl.BlockSpec((da + ds_, d), lambda i: (0, 0), pipeline_mode=pl.Buffered(1))],
        out_specs=pl.BlockSpec((tm, d), lambda i: (i, 0)),
        out_shape=jax.ShapeDtypeStruct((m, d), F32),
        scratch_shapes=[pltpu.VMEM((da + ds_, d), BF16), pltpu.VMEM((tm, da), BF16)],
        compiler_params=_cparams(("arbitrary",)),
        name="out_proj",
    )(attn, ssm, x2d, g, w)


def _mem_kv_kernel(mem_ref, g_ref, w_ref, gk_ref, kv_ref, h_ref):
    d_cross = N_CROSS_HEADS * CROSS_HEAD_DIM
    _norm_rows_to(mem_ref, g_ref, h_ref)
    kv = jnp.dot(h_ref[...], w_ref[...], preferred_element_type=F32)
    for h in range(N_CROSS_HEADS):
        cols = slice(h * CROSS_HEAD_DIM, (h + 1) * CROSS_HEAD_DIM)
        kv_ref[:, cols] = _rms_rows(kv[:, cols], gk_ref[...]).astype(kv_ref.dtype)
    kv_ref[:, d_cross:] = kv[:, d_cross:].astype(kv_ref.dtype)


def mem_kv(mem2d, g, w, gk, tm=256):
    m, d = mem2d.shape
    n = w.shape[1]
    return pl.pallas_call(
        _mem_kv_kernel,
        grid=(m // tm,),
        in_specs=[pl.BlockSpec((tm, d), lambda i: (i, 0)),
                  pl.BlockSpec((1, d), lambda i: (0, 0)),
                  pl.BlockSpec((d, n), lambda i: (0, 0)),
                  pl.BlockSpec((1, CROSS_HEAD_DIM), lambda i: (0, 0))],
        out_specs=pl.BlockSpec((tm, n), lambda i: (i, 0)),
        out_shape=jax.ShapeDtypeStruct((m, n), BF16),
        scratch_shapes=[pltpu.VMEM((tm, d), BF16)],
        compiler_params=_cparams(("parallel",)),
        name="mem_kv",
    )(mem2d, g, w, gk)


def _cross_kernel(x_ref, xn_ref, g_ref, wq_ref, gq_ref, k_ref, v_ref, wo_ref, o_ref, h_ref, q_ref, a_ref, s_scr, p_scr):
    i = pl.program_id(0)
    tm = x_ref.shape[0]
    rows = CROSS_ROWS
    cur, nxt = i % 2, (i + 1) % 2

    @pl.when(i == 0)
    def _():
        _norm_rows_to(x_ref, g_ref, h_ref.at[0])

    q_ref[...] = jnp.dot(h_ref[cur], wq_ref[...], preferred_element_type=F32)

    def scores(item, slot):
        r0, h = item
        cols = slice(h * CROSS_HEAD_DIM, (h + 1) * CROSS_HEAD_DIM)
        qh = (_rms_rows(q_ref[r0:r0 + rows, cols], gq_ref[...]) * (CROSS_HEAD_DIM ** -0.5)).astype(BF16)
        s_scr[slot] = lax.dot_general(qh, k_ref[:, cols], (((1,), (1,)), ((), ())), preferred_element_type=F32)

    def softmax(item, slot):
        s = s_scr[slot]
        p_scr[slot] = jnp.exp(s - jnp.max(s, axis=-1, keepdims=True)).astype(BF16)

    def values(item, slot):
        r0, h = item
        cols = slice(h * CROSS_HEAD_DIM, (h + 1) * CROSS_HEAD_DIM)
        vh = v_ref[:, cols]
        r = jnp.dot(p_scr[slot], jnp.concatenate([vh, jnp.ones_like(vh)], axis=1), preferred_element_type=F32)
        a_ref[r0:r0 + rows, cols] = (r[:, :CROSS_HEAD_DIM] / r[:, CROSS_HEAD_DIM:]).astype(BF16)

    items = [(r0, h) for r0 in range(0, tm, rows) for h in range(N_CROSS_HEADS)]
    _software_pipeline((scores, softmax, values), items, CROSS_GROUP)
    _norm_chunk_to(xn_ref, g_ref, h_ref.at[nxt], 0, tm)
    o_ref[...] = x_ref[...] + jnp.dot(a_ref[...], wo_ref[...], preferred_element_type=F32)


def cross_attention(x2d, g, wq, gq, kv, wo, s_len, n_mem, tm=512):
    m, d = x2d.shape
    dc = wq.shape[1]
    per_batch = s_len // tm
    full = lambda shape: pl.BlockSpec(shape, lambda i: (0, 0))
    return pl.pallas_call(
        _cross_kernel,
        grid=(m // tm,),
        in_specs=[pl.BlockSpec((tm, d), lambda i: (i, 0)),
                  pl.BlockSpec((tm, d), lambda i: (jnp.minimum(i + 1, m // tm - 1), 0)),
                  full((1, d)), full((d, dc)), full((1, CROSS_HEAD_DIM)),
                  pl.BlockSpec((n_mem, dc), lambda i: (i // per_batch, 0)),
                  pl.BlockSpec((n_mem, dc), lambda i: (i // per_batch, 1)),
                  full((dc, d))],
        out_specs=pl.BlockSpec((tm, d), lambda i: (i, 0)),
        out_shape=jax.ShapeDtypeStruct((m, d), F32),
        scratch_shapes=[pltpu.VMEM((2, tm, d), BF16), pltpu.VMEM((tm, dc), F32), pltpu.VMEM((tm, dc), BF16),
                        pltpu.VMEM((2 * CROSS_GROUP, CROSS_ROWS, n_mem), F32),
                        pltpu.VMEM((2 * CROSS_GROUP, CROSS_ROWS, n_mem), BF16)],
        compiler_params=_cparams(("arbitrary",)),
        name="cross_attn",
    )(x2d, x2d, g, wq, gq, kv, kv, wo)


def _mlp_kernel(x_ref, g_ref, wu_ref, wd_ref, o_ref, h_ref):
    @pl.when(pl.program_id(1) == 0)
    def _():
        _norm_rows_to(x_ref, g_ref, h_ref)
        o_ref[...] = x_ref[...]
    u = jnp.dot(h_ref[...], wu_ref[...], preferred_element_type=F32)
    u = jnp.square(jnp.maximum(u, 0.0)).astype(BF16)
    o_ref[...] += jnp.dot(u, wd_ref[...], preferred_element_type=F32)


def mlp(x2d, g, wu, wd, tm=512, tf=2048):
    m, d = x2d.shape
    f = wu.shape[1]
    return pl.pallas_call(
        _mlp_kernel,
        grid=(m // tm, f // tf),
        in_specs=[pl.BlockSpec((tm, d), lambda i, j: (i, 0)),
                  pl.BlockSpec((1, d), lambda i, j: (0, 0)),
                  pl.BlockSpec((d, tf), lambda i, j: (0, j)),
                  pl.BlockSpec((tf, d), lambda i, j: (j, 0))],
        out_specs=pl.BlockSpec((tm, d), lambda i, j: (i, 0)),
        out_shape=jax.ShapeDtypeStruct((m, d), F32),
        scratch_shapes=[pltpu.VMEM((tm, d), BF16)],
        compiler_params=_cparams(("parallel", "arbitrary")),
        name="mlp",
    )(x2d, g, wu, wd)


def _layer(x2d, mem2d, pos_col, bsz, s_len, n_mem, g_mix, w_in, g_q, g_k, g_attn_out, conv_w, conv_b, dt_bias,
           a_log, d_skip, g_ssm_out, w_out, g_cross, g_mem, w_cq, w_ckv, g_cq, g_ck, w_co, g_mlp, w_up, w_down):
    d_model = x2d.shape[1]
    d_attn = d_model // 2
    d_ssm = d_model // 2
    n_pairs = d_attn // LANES
    n_ssm_heads = d_ssm // SSM_HEAD_DIM
    d_conv = d_ssm + 2 * SSM_GROUPS * SSM_STATE
    d_packed = 3 * d_attn + d_ssm + d_conv
    row = lambda v: v.astype(F32)[None, :]

    w_in_t = w_in.T
    w_dt_t = jnp.pad(w_in_t[d_packed:], ((0, LANES - n_ssm_heads), (0, 0)))
    packed, dt_raw = in_proj(x2d, row(g_mix), w_in_t, w_dt_t, d_packed)

    tabs = rope_tables(pos_col)
    attn, w_up_bf, w_down_bf = dilated_attention(
        packed, tabs, row(jnp.tile(g_q, 2)) * (ATTN_HEAD_DIM ** -0.5 * LOG2E), row(jnp.tile(g_k, 2)), w_up, w_down,
        bsz, s_len, n_pairs)
    ssm = ssd_mixer(packed, dt_raw, conv_w, conv_b, dt_bias, a_log, d_skip, g_ssm_out, bsz, s_len, d_ssm,
                    xbc_col=3 * d_attn + d_ssm, z_col=3 * d_attn)
    x2d = out_proj(attn, ssm, x2d, row(g_attn_out), w_out)

    kv = mem_kv(mem2d, row(g_mem), w_ckv.astype(BF16), row(g_ck))
    x2d = cross_attention(x2d, row(g_cross), w_cq.astype(BF16), row(g_cq), kv, w_co.astype(BF16), s_len, n_mem)

    return mlp(x2d, row(g_mlp), w_up_bf, w_down_bf)


def kernel(x, mem, positions, g_mix, w_in, g_q, g_k, g_attn_out, conv_w, conv_b, dt_bias, a_log, d_skip, g_ssm_out,
           w_out, g_cross, g_mem, w_cq, w_ckv, g_cq, g_ck, w_co, g_mlp, w_up, w_down):
    bsz, s_len, d_model = x.shape
    n_mem = mem.shape[1]
    x2d = x.reshape(bsz * s_len, d_model)
    mem2d = mem.reshape(bsz * n_mem, d_model)
    pos_col = positions.reshape(bsz * s_len, 1)
    for i in range(g_mix.shape[0]):
        x2d = _layer(x2d, mem2d, pos_col, bsz, s_len, n_mem, g_mix[i], w_in[i], g_q[i], g_k[i], g_attn_out[i],
                     conv_w[i], conv_b[i], dt_bias[i], a_log[i], d_skip[i], g_ssm_out[i], w_out[i], g_cross[i],
                     g_mem[i], w_cq[i], w_ckv[i], g_cq[i], g_ck[i], w_co[i], g_mlp[i], w_up[i], w_down[i])
    return x2d.reshape(bsz, s_len, d_model)
```

```python
import functools
import math

import jax
import jax.numpy as jnp
from jax import lax
from jax.experimental import pallas as pl
from jax.experimental.pallas import tpu as pltpu

F32 = jnp.float32
BF16 = jnp.bfloat16
EPS = 1e-6

LANES = 128
ATTN_HEAD_DIM = 64
ROT_DIM = ATTN_HEAD_DIM // 4
ROPE_THETA = 500000.0
ATTN_BLOCK = 128
DILATIONS = (1, 4, 16)
SSM_HEAD_DIM = 64
SSM_GROUPS = 4
SSM_STATE = 128
CONV_WIDTH = 4
SSD_CHUNK = 128
N_CROSS_HEADS = 4
CROSS_HEAD_DIM = 128
CROSS_ROWS = 256
CROSS_GROUP = 2
VMEM_LIMIT = 58 * 1024 * 1024


def _cparams(sem):
    return pltpu.CompilerParams(dimension_semantics=sem, vmem_limit_bytes=VMEM_LIMIT)


def _rms_rows(x, g):
    ms = jnp.mean(x * x, axis=-1, keepdims=True)
    return x * lax.rsqrt(ms + EPS) * g


def _norm_rows_to(x_ref, g_ref, h_ref, rows=16):
    def body(c, carry):
        r0 = pl.multiple_of(c * rows, rows)
        h_ref[pl.ds(r0, rows), :] = _rms_rows(x_ref[pl.ds(r0, rows), :], g_ref[...]).astype(h_ref.dtype)
        return carry
    lax.fori_loop(0, x_ref.shape[0] // rows, body, 0, unroll=4)


def _norm_chunk_to(x_ref, g_ref, h_ref, start, n_rows, rows=16):
    for r in range(0, n_rows, rows):
        rs = pl.ds(start + r, rows)
        h_ref[rs, :] = _rms_rows(x_ref[rs, :], g_ref[...]).astype(h_ref.dtype)


def _in_proj_kernel(x0_ref, xn_ref, g_ref, w_ref, wdt_ref, o_ref, dt_ref, h_ref):
    i, j = pl.program_id(0), pl.program_id(1)
    tm = xn_ref.shape[0]
    cur, nxt = i % 2, (i + 1) % 2

    @pl.when((i == 0) & (j == 0))
    def _():
        _norm_rows_to(x0_ref, g_ref, h_ref.at[0])

    @pl.when(j == 0)
    def _():
        dt_ref[...] = lax.dot_general(h_ref[cur], wdt_ref[...].astype(BF16), (((1,), (1,)), ((), ())),
                                      preferred_element_type=F32)
    o_ref[...] = lax.dot_general(h_ref[cur], w_ref[...].astype(BF16), (((1,), (1,)), ((), ())),
                                 preferred_element_type=F32).astype(o_ref.dtype)
    chunk = tm // IN_PROJ_NORM_STEPS
    start = pl.multiple_of(jnp.minimum(j, IN_PROJ_NORM_STEPS - 1) * chunk, chunk)
    _norm_chunk_to(xn_ref, g_ref, h_ref.at[nxt], start, chunk)


IN_PROJ_NORM_STEPS = 8


def in_proj(x2d, g, w_t, w_dt, n, tm=1024, tn=768):
    m, d = x2d.shape
    n_i = m // tm
    assert n // tn >= IN_PROJ_NORM_STEPS
    return pl.pallas_call(
        _in_proj_kernel,
        grid=(n_i, n // tn),
        in_specs=[pl.BlockSpec((tm, d), lambda i, j: (0, 0), pipeline_mode=pl.Buffered(1)),
                  pl.BlockSpec((tm, d), lambda i, j: (jnp.minimum(i + 1, n_i - 1), 0)),
                  pl.BlockSpec((1, d), lambda i, j: (0, 0)),
                  pl.BlockSpec((tn, d), lambda i, j: (j, 0)),
                  pl.BlockSpec((LANES, d), lambda i, j: (0, 0), pipeline_mode=pl.Buffered(1))],
        out_specs=[pl.BlockSpec((tm, tn), lambda i, j: (i, j)),
                   pl.BlockSpec((tm, LANES), lambda i, j: (i, 0))],
        out_shape=[jax.ShapeDtypeStruct((m, n), BF16), jax.ShapeDtypeStruct((m, LANES), F32)],
        scratch_shapes=[pltpu.VMEM((2, tm, d), BF16)],
        compiler_params=_cparams(("arbitrary", "arbitrary")),
        name="in_proj",
    )(x2d, x2d, g, w_t, w_dt)


def _rope_kernel(pos_ref, expo_ref, cmask_ref, cos_ref, sin_ref):
    inv_freq = jnp.power(jnp.float32(ROPE_THETA), expo_ref[...])
    ang = pos_ref[...].astype(F32) * inv_freq
    cm = cmask_ref[...]
    cos_ref[...] = jnp.cos(ang) * cm + (1.0 - cm)
    sin_ref[...] = jnp.sin(ang) * cm


def rope_tables(pos_col, tm=2048):
    m = pos_col.shape[0]
    half = ROT_DIM // 2
    d = jnp.arange(LANES) % ATTN_HEAD_DIM
    expo = (-2.0 * (d % half).astype(F32) / ROT_DIM)[None, :]
    cmask = (d < ROT_DIM).astype(F32)[None, :]
    row = pl.BlockSpec((1, LANES), lambda i: (0, 0))
    tab = pl.BlockSpec((tm, LANES), lambda i: (i, 0))
    return pl.pallas_call(
        _rope_kernel,
        grid=(m // tm,),
        in_specs=[pl.BlockSpec((tm, 1), lambda i: (i, 0)), row, row],
        out_specs=[tab, tab],
        out_shape=[jax.ShapeDtypeStruct((m, LANES), F32)] * 2,
        compiler_params=_cparams(("parallel",)),
        name="rope_tab",
    )(pos_col, expo, cmask)


def _rotate_half_matrix():
    half = ROT_DIM // 2
    src = jnp.arange(LANES)[:, None]
    dst = jnp.arange(LANES)[None, :]
    d = dst % ATTN_HEAD_DIM
    first = (d < half) & (src == dst + half)
    second = (d >= half) & (d < ROT_DIM) & (src == dst - half)
    return (second.astype(F32) - first.astype(F32)).astype(BF16)


def _rows(start, size, dil):
    return pl.ds(start, size) if dil == 1 else pl.ds(start, size, stride=dil)


ATTN_GROUP = 2


def _software_pipeline(stages, items, group, extras=()):
    groups = [items[i:i + group] for i in range(0, len(items), group)]
    n_trips = len(groups) + len(stages) - 1
    for t in range(n_trips):
        for lag, stage in enumerate(stages):
            if 0 <= t - lag < len(groups):
                for g, item in enumerate(groups[t - lag]):
                    stage(item, ((t - lag) % 2) * group + g)
        for k, extra in enumerate(extras):
            if k * n_trips // len(extras) == t:
                extra()


def _attn_kernel(q0_ref, k0_ref, v0_ref, cos0_ref, sin0_ref, qn_ref, kn_ref, vn_ref, cosn_ref, sinn_ref,
                 gq_ref, gk_ref, seg_ref, rot_ref, wu_ref, wd_ref, wo_ref, o_ref, wu_out, wd_out, wo_out,
                 set_a, set_b, ob, mb, lb, s_scr, p_scr):
    s_len = qn_ref.shape[0]
    blk = ATTN_BLOCK
    step = pl.program_id(0) * pl.num_programs(1) + pl.program_id(1)
    lane = lax.broadcasted_iota(jnp.int32, (blk, LANES), 1)
    head0 = lane < ATTN_HEAD_DIM
    qi = lax.broadcasted_iota(jnp.int32, (2 * blk, 2 * blk), 0) % blk
    kj = lax.broadcasted_iota(jnp.int32, (2 * blk, 2 * blk), 1)
    band_mask = (kj >= qi) & (kj <= qi + blk)
    first_mask = (lax.broadcasted_iota(jnp.int32, (2 * blk, blk), 1)
                  <= lax.broadcasted_iota(jnp.int32, (2 * blk, blk), 0) % blk)

    wu_out[...] = wu_ref[...].astype(BF16)
    wd_out[...] = wd_ref[...].astype(BF16)
    wo_out[...] = wo_ref[...].astype(BF16)

    prep_rows = 256

    def prep(srcs, dst, r0):
        q_ref, k_ref, v_ref, cos_ref, sin_ref = srcs
        qf, kf, vf = dst[:3]
        rows = slice(r0, r0 + prep_rows)
        cs, sn = cos_ref[rows, :], sin_ref[rows, :]

        def norm_rope(x_ref, g_ref):
            x = x_ref[rows, :].astype(F32)
            ss = jnp.dot((x * x).astype(BF16), seg_ref[...], preferred_element_type=F32)
            y = x * lax.rsqrt(ss * (1.0 / ATTN_HEAD_DIM) + EPS) * g_ref[...]
            return y * cs + jnp.dot(y.astype(BF16), rot_ref[...], preferred_element_type=F32) * sn

        qf[rows, :] = norm_rope(q_ref, gq_ref)
        kf[rows, :] = norm_rope(k_ref, gk_ref)
        vf[rows, :] = v_ref[rows, :].astype(F32)

    dmid = DILATIONS[1]
    run = s_len // dmid

    def deinterleave(bufs, r):
        for src, dst in zip(bufs[:3], bufs[3:]):
            dst[r * run:(r + 1) * run, :] = src[pl.ds(r, run, stride=dmid), :]

    def prepare(srcs, bufs):
        return ([functools.partial(prep, srcs, bufs, r0) for r0 in range(0, s_len, prep_rows)]
                + [functools.partial(deinterleave, bufs, r) for r in range(dmid)])

    @pl.when(step == 0)
    def _():
        for piece in prepare((q0_ref, k0_ref, v0_ref, cos0_ref, sin0_ref), set_a):
            piece()

    def block_descs(bufs):
        natural, deint = bufs[:3], bufs[3:]
        descs = [(deint, 1, 1, r * run, r * run, blk, first_mask) for r in range(dmid)]
        descs += [(deint, 1, 1, r * run + n * blk, r * run + (n - 1) * blk, 2 * blk, band_mask)
                  for n in range(1, run // blk) for r in range(dmid)]
        descs += [(deint, 2, dmid, r_hi * run + r_lo, r_hi * run + r_lo, blk, first_mask)
                  for r_hi in range(dmid) for r_lo in range(dmid)]
        descs += [(natural, 0, 1, 0, 0, blk, first_mask)]
        descs += [(natural, 0, 1, n * blk, (n - 1) * blk, 2 * blk, band_mask) for n in range(1, s_len // blk)]
        return descs

    def scores(desc, slot):
        (q_src, k_src, _), _, dil, q_start, k_start, n_keys, _ = desc
        qt = q_src[_rows(q_start, blk, dil), :]
        zero = jnp.zeros_like(qt)
        q2 = jnp.concatenate([jnp.where(head0, qt, zero), jnp.where(head0, zero, qt)], axis=0).astype(BF16)
        kb = k_src[_rows(k_start, n_keys, dil), :].astype(BF16)
        s_scr[slot, :, :n_keys] = lax.dot_general(q2, kb, (((1,), (1,)), ((), ())), preferred_element_type=F32)

    def softmax(desc, slot):
        _, br, dil, q_start, _, n_keys, mask = desc
        s = jnp.where(mask, s_scr[slot, :, :n_keys], -jnp.inf)
        m = jnp.max(s, axis=-1, keepdims=True)
        p_scr[slot, :, :n_keys] = jnp.exp2(s - m).astype(BF16)
        mb[br, _rows(q_start, blk, dil), :] = jnp.where(head0, m[:blk], m[blk:])

    def values(desc, slot):
        (_, _, v_src), br, dil, q_start, k_start, n_keys, _ = desc
        vb = v_src[_rows(k_start, n_keys, dil), :].astype(BF16)
        v1 = jnp.concatenate([vb, jnp.ones_like(vb)], axis=1)
        r = jnp.dot(p_scr[slot, :, :n_keys], v1, preferred_element_type=F32)
        out_rows = _rows(q_start, blk, dil)
        ob[br, out_rows, :] = jnp.where(head0, r[:blk, :LANES], r[blk:, :LANES])
        lb[br, out_rows, :] = jnp.where(head0, r[:blk, LANES:], r[blk:, LANES:])
        if br == 0:
            merge(q_start)

    def merge(n0):
        sub = blk // dmid
        for r in range(dmid):
            nat_rows = pl.ds(n0 + r, sub, stride=dmid)
            d_rows = slice(r * run + n0 // dmid, r * run + n0 // dmid + sub)
            rows = (nat_rows, d_rows, d_rows)
            ms = [mb[g, rows[g], :] for g in range(len(DILATIONS))]
            m = functools.reduce(jnp.maximum, ms)
            ws = [jnp.exp2(mg - m) for mg in ms]
            num = functools.reduce(jnp.add, [w * ob[g, rows[g], :] for g, w in enumerate(ws)])
            den = functools.reduce(jnp.add, [w * lb[g, rows[g], :] for g, w in enumerate(ws)])
            ob[0, nat_rows, :] = num / den
        o_ref[n0:n0 + blk, :] = ob[0, n0:n0 + blk, :].astype(o_ref.dtype)

    for parity, (cur_set, nxt_set) in enumerate(((set_a, set_b), (set_b, set_a))):
        @pl.when(step % 2 == parity)
        def _(cur_set=cur_set, nxt_set=nxt_set):
            _software_pipeline((scores, softmax, values), block_descs(cur_set), ATTN_GROUP,
                               extras=prepare((qn_ref, kn_ref, vn_ref, cosn_ref, sinn_ref), nxt_set))


def dilated_attention(qkv, tabs, gq2, gk2, w_up, w_down, w_out, bsz, s_len, n_pairs):
    cos_t, sin_t = tabs
    d1, d2, d3 = DILATIONS
    assert d1 == 1 and d3 == d2 * d2 and s_len == d3 * ATTN_BLOCK, "layout assumes dilations (1, d, d*d), one block per largest class"
    lane = jnp.arange(LANES)
    seg = (lane[:, None] // ATTN_HEAD_DIM == lane[None, :] // ATTN_HEAD_DIM).astype(BF16)
    n_steps = bsz * n_pairs
    wu_rows, wd_rows, wo_rows = (w.shape[0] // n_steps for w in (w_up, w_down, w_out))

    def nxt(b, p):
        s = jnp.minimum(b * n_pairs + p + 1, n_steps - 1)
        return s // n_pairs, s % n_pairs

    once = dict(pipeline_mode=pl.Buffered(1))
    first = lambda off: pl.BlockSpec((s_len, LANES), lambda b, p, off=off: (0, off), **once)
    nblk = lambda off: pl.BlockSpec((s_len, LANES), lambda b, p, off=off: (nxt(b, p)[0], off + nxt(b, p)[1]))
    tab0 = pl.BlockSpec((s_len, LANES), lambda b, p: (0, 0), **once)
    tabn = pl.BlockSpec((s_len, LANES), lambda b, p: (nxt(b, p)[0], 0))
    row = pl.BlockSpec((1, LANES), lambda b, p: (0, 0))
    sq = pl.BlockSpec((LANES, LANES), lambda b, p: (0, 0))
    wu_spec = pl.BlockSpec((wu_rows, w_up.shape[1]), lambda b, p: (b * n_pairs + p, 0))
    wd_spec = pl.BlockSpec((wd_rows, w_down.shape[1]), lambda b, p: (b * n_pairs + p, 0))
    wo_spec = pl.BlockSpec((wo_rows, w_out.shape[1]), lambda b, p: (b * n_pairs + p, 0))
    return pl.pallas_call(
        _attn_kernel,
        grid=(bsz, n_pairs),
        in_specs=[first(0), first(n_pairs), first(2 * n_pairs), tab0, tab0,
                  nblk(0), nblk(n_pairs), nblk(2 * n_pairs), tabn, tabn,
                  row, row, sq, sq, wu_spec, wd_spec, wo_spec],
        out_specs=[pl.BlockSpec((s_len, LANES), lambda b, p: (b, p)), wu_spec, wd_spec, wo_spec],
        out_shape=[jax.ShapeDtypeStruct((bsz * s_len, n_pairs * LANES), BF16),
                   jax.ShapeDtypeStruct(w_up.shape, BF16), jax.ShapeDtypeStruct(w_down.shape, BF16),
                   jax.ShapeDtypeStruct(w_out.shape, BF16)],
        scratch_shapes=[[pltpu.VMEM((s_len, LANES), F32)] * 6] * 2
                       + [pltpu.VMEM((len(DILATIONS), s_len, LANES), F32)] * 3
                       + [pltpu.VMEM((2 * ATTN_GROUP, 2 * ATTN_BLOCK, 2 * ATTN_BLOCK), F32),
                          pltpu.VMEM((2 * ATTN_GROUP, 2 * ATTN_BLOCK, 2 * ATTN_BLOCK), BF16)],
        compiler_params=_cparams(("arbitrary", "arbitrary")),
        name="dilated_attn",
    )(qkv, qkv, qkv, cos_t, sin_t, qkv, qkv, qkv, cos_t, sin_t, gq2, gk2, seg, _rotate_half_matrix(), w_up, w_down, w_out)


def _split3(x):
    hi = x.astype(BF16)
    r1 = x - hi.astype(F32)
    mid = r1.astype(BF16)
    lo = (r1 - mid.astype(F32)).astype(BF16)
    return hi, mid, lo


def _silu(x):
    h = 0.5 * x
    return h * jnp.tanh(h) + h


LOG2E = 1.4426950408889634
SSD_STEP_CHUNKS = 4


def _ssd_kernel(xbc_ref, z_ref, dt_ref, cw_ref, cb_ref, dtb_ref, alog_ref, dskip_ref, g_ref,
                expand_ref, tril_ref, o_ref, xpad, xc, st):
    q = SSD_CHUNK
    n_rows = xbc_ref.shape[0]
    d_ssm = z_ref.shape[1]
    d_conv = xbc_ref.shape[1]
    gw = d_ssm // SSM_GROUPS
    heads_per_group = gw // SSM_HEAD_DIM
    pad = 8

    @pl.when(pl.program_id(1) == 0)
    def _():
        xpad[:, 0:pad, :] = jnp.zeros((d_conv // LANES, pad, LANES), F32)
        st[...] = jnp.zeros_like(st)

    for c0 in range(0, d_conv, LANES):
        cols = slice(c0, c0 + LANES)
        slab = c0 // LANES
        xpad[slab, pad:pad + n_rows, :] = xbc_ref[:, cols].astype(F32)
        acc = cb_ref[:, cols] + cw_ref[CONV_WIDTH - 1:CONV_WIDTH, cols] * xpad[slab, pad:pad + n_rows, :]
        for w in range(CONV_WIDTH - 1):
            off = pad - (CONV_WIDTH - 1) + w
            acc = acc + cw_ref[w:w + 1, cols] * xpad[slab, pl.ds(off, n_rows, stride=1), :]
        xc[:, cols] = _silu(acc)
    xpad[:, 0:pad, :] = xpad[:, n_rows:n_rows + pad, :]

    li = lax.broadcasted_iota(jnp.int32, (q, q), 0)
    si = lax.broadcasted_iota(jnp.int32, (q, q), 1)
    causal = li >= si
    lane_g = lax.broadcasted_iota(jnp.int32, (q, gw), 1)
    tril = tril_ref[...]
    expand = expand_ref[...]

    for r0 in range(0, n_rows, q):
        rows = slice(r0, r0 + q)
        x_dt = dt_ref[rows, :] + dtb_ref[...]
        dt = jnp.maximum(x_dt, 0.0) + jnp.log1p(jnp.exp(-jnp.abs(x_dt)))
        dta = dt * (-LOG2E * jnp.exp(alog_ref[...]))
        acs = sum(jnp.dot(tril, part, preferred_element_type=F32) for part in _split3(dta))
        acs_t = acs.T
        last = acs[q - 1:q, :]
        dt_e = jnp.dot(dt.astype(BF16), expand, preferred_element_type=F32)
        dec_e = jnp.dot(jnp.exp2(acs).astype(BF16), expand, preferred_element_type=F32)
        w_e = jnp.dot((jnp.exp2(last - acs) * dt).astype(BF16), expand, preferred_element_type=F32)

        for g in range(SSM_GROUPS):
            cols = slice(g * gw, (g + 1) * gw)
            b_f = xc[rows, d_ssm + g * SSM_STATE:d_ssm + (g + 1) * SSM_STATE]
            c_b = xc[rows, d_ssm + (SSM_GROUPS + g) * SSM_STATE:d_ssm + (SSM_GROUPS + g + 1) * SSM_STATE].astype(BF16)
            xs = xc[rows, cols]
            cb = lax.dot_general(c_b, b_f.astype(BF16), (((1,), (1,)), ((), ())), preferred_element_type=F32)
            xdt = (xs * dt_e[:, cols]).astype(BF16)
            ws, rs = [], []
            for hh in range(heads_per_group):
                h = g * heads_per_group + hh
                seg = acs[:, h:h + 1] - acs_t[h:h + 1, :]
                l_mat = jnp.exp2(jnp.where(causal, seg, -jnp.inf))
                ws.append((cb * l_mat).astype(BF16))
                in_head = (lane_g >= hh * SSM_HEAD_DIM) & (lane_g < (hh + 1) * SSM_HEAD_DIM)
                rs.append(jnp.where(in_head, xdt, jnp.zeros_like(xdt)))
            y = jnp.dot(jnp.concatenate(ws, axis=1), jnp.concatenate(rs, axis=0), preferred_element_type=F32)
            st_g = st[:, cols]
            y = y + jnp.dot(c_b, st_g.astype(BF16), preferred_element_type=F32) * dec_e[:, cols]
            xw = (xs * w_e[:, cols]).astype(BF16)
            st[:, cols] = st_g * dec_e[q - 1:q, cols] + jnp.dot(b_f.T.astype(BF16), xw, preferred_element_type=F32)
            y = y + dskip_ref[:, cols] * xs
            y = y * _silu(z_ref[rows, cols].astype(F32))
            o_ref[rows, cols] = _rms_rows(y, g_ref[:, cols]).astype(o_ref.dtype)


def ssd_mixer(packed, dt_raw, conv_w, conv_b, dt_bias, a_log, d_skip, g_out, bsz, s_len, d_ssm, xbc_col, z_col):
    n_heads = d_ssm // SSM_HEAD_DIM
    d_conv = d_ssm + 2 * SSM_GROUPS * SSM_STATE
    step_rows = SSD_STEP_CHUNKS * SSD_CHUNK
    nc = s_len // step_rows
    padl = lambda v: jnp.pad(v.astype(F32), (0, LANES - n_heads))[None, :]
    expand = (jnp.arange(LANES)[:, None] == (jnp.arange(d_ssm) // SSM_HEAD_DIM)[None, :]).astype(BF16)
    tril = (jnp.arange(SSD_CHUNK)[:, None] >= jnp.arange(SSD_CHUNK)[None, :]).astype(BF16)
    full = lambda shape: pl.BlockSpec(shape, lambda b, c: (0, 0))
    return pl.pallas_call(
        _ssd_kernel,
        grid=(bsz, nc),
        in_specs=[pl.BlockSpec((step_rows, d_conv), lambda b, c: (b * nc + c, xbc_col // d_conv)),
                  pl.BlockSpec((step_rows, d_ssm), lambda b, c: (b * nc + c, z_col // d_ssm)),
                  pl.BlockSpec((step_rows, LANES), lambda b, c: (b * nc + c, 0)),
                  full((CONV_WIDTH, d_conv)), full((1, d_conv)), full((1, LANES)), full((1, LANES)),
                  full((1, d_ssm)), full((1, d_ssm)), full((LANES, d_ssm)), full((SSD_CHUNK, SSD_CHUNK))],
        out_specs=pl.BlockSpec((step_rows, d_ssm), lambda b, c: (b * nc + c, 0)),
        out_shape=jax.ShapeDtypeStruct((bsz * s_len, d_ssm), BF16),
        scratch_shapes=[pltpu.VMEM((d_conv // LANES, step_rows + 8, LANES), F32),
                        pltpu.VMEM((step_rows, d_conv), F32),
                        pltpu.VMEM((SSM_STATE, d_ssm), F32)],
        compiler_params=_cparams(("parallel", "arbitrary")),
        name="ssd",
    )(packed, packed, dt_raw, conv_w.astype(F32), conv_b.astype(F32)[None, :], padl(dt_bias), padl(a_log),
      jnp.repeat(d_skip.astype(F32), SSM_HEAD_DIM)[None, :], g_out.astype(F32)[None, :], expand, tril)


def _mem_kv_kernel(mem_ref, g_ref, w_ref, gk_ref, kv_ref, h_ref):
    d_cross = N_CROSS_HEADS * CROSS_HEAD_DIM
    _norm_rows_to(mem_ref, g_ref, h_ref)
    kv = jnp.dot(h_ref[...], w_ref[...], preferred_element_type=F32)
    for h in range(N_CROSS_HEADS):
        cols = slice(h * CROSS_HEAD_DIM, (h + 1) * CROSS_HEAD_DIM)
        kv_ref[:, cols] = _rms_rows(kv[:, cols], gk_ref[...]).astype(kv_ref.dtype)
    kv_ref[:, d_cross:] = kv[:, d_cross:].astype(kv_ref.dtype)


def mem_kv(mem2d, g, w, gk, tm=256):
    m, d = mem2d.shape
    n = w.shape[1]
    return pl.pallas_call(
        _mem_kv_kernel,
        grid=(m // tm,),
        in_specs=[pl.BlockSpec((tm, d), lambda i: (i, 0)),
                  pl.BlockSpec((1, d), lambda i: (0, 0)),
                  pl.BlockSpec((d, n), lambda i: (0, 0)),
                  pl.BlockSpec((1, CROSS_HEAD_DIM), lambda i: (0, 0))],
        out_specs=pl.BlockSpec((tm, n), lambda i: (i, 0)),
        out_shape=jax.ShapeDtypeStruct((m, n), BF16),
        scratch_shapes=[pltpu.VMEM((tm, d), BF16)],
        compiler_params=_cparams(("parallel",)),
        name="mem_kv",
    )(mem2d, g, w, gk)


def _mix_cross_kernel(attn_ref, ssm_ref, x_ref, ga_ref, w_ref, gc_ref, wq_ref, gq_ref, k_ref, v_ref, wo_ref, o_ref,
                      h_ref, xk_ref, an_ref, q_ref, a_ref, s_scr, p_scr):
    s = pl.program_id(0)
    n_tiles = pl.num_programs(0) - 1
    tm, d_attn = x_ref.shape[0], attn_ref.shape[1]
    rows = CROSS_ROWS
    prev, cur = (s + 1) % 2, s % 2

    @pl.when(s > 0)
    def _():
        q_ref[...] = jnp.dot(h_ref[prev], wq_ref[...], preferred_element_type=F32)

        def scores(item, slot):
            r0, h = item
            cols = slice(h * CROSS_HEAD_DIM, (h + 1) * CROSS_HEAD_DIM)
            qh = (_rms_rows(q_ref[r0:r0 + rows, cols], gq_ref[...]) * (CROSS_HEAD_DIM ** -0.5)).astype(BF16)
            s_scr[slot] = lax.dot_general(qh, k_ref[:, cols], (((1,), (1,)), ((), ())), preferred_element_type=F32)

        def softmax(item, slot):
            sc = s_scr[slot]
            p_scr[slot] = jnp.exp(sc - jnp.max(sc, axis=-1, keepdims=True)).astype(BF16)

        def values(item, slot):
            r0, h = item
            cols = slice(h * CROSS_HEAD_DIM, (h + 1) * CROSS_HEAD_DIM)
            vh = v_ref[:, cols]
            r = jnp.dot(p_scr[slot], jnp.concatenate([vh, jnp.ones_like(vh)], axis=1), preferred_element_type=F32)
            a_ref[r0:r0 + rows, cols] = (r[:, :CROSS_HEAD_DIM] / r[:, CROSS_HEAD_DIM:]).astype(BF16)

        items = [(r0, h) for r0 in range(0, tm, rows) for h in range(N_CROSS_HEADS)]
        _software_pipeline((scores, softmax, values), items, CROSS_GROUP)
        o_ref[...] = xk_ref[prev] + jnp.dot(a_ref[...], wo_ref[...], preferred_element_type=F32)

    @pl.when(s < n_tiles)
    def _():
        chunk = 32

        def body(c, carry):
            rs = pl.ds(pl.multiple_of(c * chunk, chunk), chunk)
            an_ref[rs, :] = _rms_rows(attn_ref[rs, :].astype(F32), ga_ref[...]).astype(BF16)
            return carry
        lax.fori_loop(0, tm // chunk, body, 0, unroll=4)
        xk_ref[cur] = (x_ref[...] + jnp.dot(an_ref[...], w_ref[:d_attn, :], preferred_element_type=F32)
                       + jnp.dot(ssm_ref[...], w_ref[d_attn:, :], preferred_element_type=F32))
        _norm_rows_to(xk_ref.at[cur], gc_ref, h_ref.at[cur])


def mix_cross(attn, ssm, x2d, ga, w, gc, wq, gq, kv, wo, s_len, n_mem, tm=512):
    m, d = x2d.shape
    da, ds_ = attn.shape[1], ssm.shape[1]
    dc = wq.shape[1]
    n_tiles = m // tm
    per_batch = s_len // tm
    ahead = lambda s: (jnp.minimum(s, n_tiles - 1), 0)
    behind = lambda s: jnp.maximum(s - 1, 0)
    full = lambda shape, **kw: pl.BlockSpec(shape, lambda s: (0, 0), **kw)
    once = dict(pipeline_mode=pl.Buffered(1))
    return pl.pallas_call(
        _mix_cross_kernel,
        grid=(n_tiles + 1,),
        in_specs=[pl.BlockSpec((tm, da), ahead), pl.BlockSpec((tm, ds_), ahead), pl.BlockSpec((tm, d), ahead),
                  full((1, da)), full((da + ds_, d), **once), full((1, d)), full((d, dc), **once),
                  full((1, CROSS_HEAD_DIM)),
                  pl.BlockSpec((n_mem, dc), lambda s: (behind(s) // per_batch, 0)),
                  pl.BlockSpec((n_mem, dc), lambda s: (behind(s) // per_batch, 1)),
                  full((dc, d), **once)],
        out_specs=pl.BlockSpec((tm, d), lambda s: (behind(s), 0)),
        out_shape=jax.ShapeDtypeStruct((m, d), F32),
        scratch_shapes=[pltpu.VMEM((2, tm, d), BF16), pltpu.VMEM((2, tm, d), F32), pltpu.VMEM((tm, da), BF16),
                        pltpu.VMEM((tm, dc), F32), pltpu.VMEM((tm, dc), BF16),
                        pltpu.VMEM((2 * CROSS_GROUP, CROSS_ROWS, n_mem), F32),
                        pltpu.VMEM((2 * CROSS_GROUP, CROSS_ROWS, n_mem), BF16)],
        compiler_params=_cparams(("arbitrary",)),
        name="mix_cross",
    )(attn, ssm, x2d, ga, w, gc, wq, gq, kv, kv, wo)


def _mlp_kernel(x_ref, g_ref, wu_ref, wd_ref, o_ref, h_ref):
    @pl.when(pl.program_id(1) == 0)
    def _():
        _norm_rows_to(x_ref, g_ref, h_ref)
        o_ref[...] = x_ref[...]
    u = jnp.dot(h_ref[...], wu_ref[...], preferred_element_type=F32)
    u = jnp.square(jnp.maximum(u, 0.0)).astype(BF16)
    o_ref[...] += jnp.dot(u, wd_ref[...], preferred_element_type=F32)


def mlp(x2d, g, wu, wd, tm=512, tf=2048):
    m, d = x2d.shape
    f = wu.shape[1]
    return pl.pallas_call(
        _mlp_kernel,
        grid=(m // tm, f // tf),
        in_specs=[pl.BlockSpec((tm, d), lambda i, j: (i, 0)),
                  pl.BlockSpec((1, d), lambda i, j: (0, 0)),
                  pl.BlockSpec((d, tf), lambda i, j: (0, j)),
                  pl.BlockSpec((tf, d), lambda i, j: (j, 0))],
        out_specs=pl.BlockSpec((tm, d), lambda i, j: (i, 0)),
        out_shape=jax.ShapeDtypeStruct((m, d), F32),
        scratch_shapes=[pltpu.VMEM((tm, d), BF16)],
        compiler_params=_cparams(("parallel", "arbitrary")),
        name="mlp",
    )(x2d, g, wu, wd)


def _layer(x2d, mem2d, pos_col, bsz, s_len, n_mem, g_mix, w_in, g_q, g_k, g_attn_out, conv_w, conv_b, dt_bias,
           a_log, d_skip, g_ssm_out, w_out, g_cross, g_mem, w_cq, w_ckv, g_cq, g_ck, w_co, g_mlp, w_up, w_down):
    d_model = x2d.shape[1]
    d_attn = d_model // 2
    d_ssm = d_model // 2
    n_pairs = d_attn // LANES
    n_ssm_heads = d_ssm // SSM_HEAD_DIM
    d_conv = d_ssm + 2 * SSM_GROUPS * SSM_STATE
    d_packed = 3 * d_attn + d_ssm + d_conv
    row = lambda v: v.astype(F32)[None, :]

    w_in_t = w_in.T
    w_dt_t = jnp.pad(w_in_t[d_packed:], ((0, LANES - n_ssm_heads), (0, 0)))
    packed, dt_raw = in_proj(x2d, row(g_mix), w_in_t, w_dt_t, d_packed)

    tabs = rope_tables(pos_col)
    attn, w_up_bf, w_down_bf, w_out_bf = dilated_attention(
        packed, tabs, row(jnp.tile(g_q, 2)) * (ATTN_HEAD_DIM ** -0.5 * LOG2E), row(jnp.tile(g_k, 2)), w_up, w_down,
        w_out, bsz, s_len, n_pairs)
    ssm = ssd_mixer(packed, dt_raw, conv_w, conv_b, dt_bias, a_log, d_skip, g_ssm_out, bsz, s_len, d_ssm,
                    xbc_col=3 * d_attn + d_ssm, z_col=3 * d_attn)
    kv = mem_kv(mem2d, row(g_mem), w_ckv.astype(BF16), row(g_ck))
    x2d = mix_cross(attn, ssm, x2d, row(g_attn_out), w_out_bf, row(g_cross), w_cq.astype(BF16), row(g_cq), kv,
                    w_co.astype(BF16), s_len, n_mem)

    return mlp(x2d, row(g_mlp), w_up_bf, w_down_bf)


def kernel(x, mem, positions, g_mix, w_in, g_q, g_k, g_attn_out, conv_w, conv_b, dt_bias, a_log, d_skip, g_ssm_out,
           w_out, g_cross, g_mem, w_cq, w_ckv, g_cq, g_ck, w_co, g_mlp, w_up, w_down):
    bsz, s_len, d_model = x.shape
    n_mem = mem.shape[1]
    x2d = x.reshape(bsz * s_len, d_model)
    mem2d = mem.reshape(bsz * n_mem, d_model)
    pos_col = positions.reshape(bsz * s_len, 1)
    for i in range(g_mix.shape[0]):
        x2d = _layer(x2d, mem2d, pos_col, bsz, s_len, n_mem, g_mix[i], w_in[i], g_q[i], g_k[i], g_attn_out[i],
                     conv_w[i], conv_b[i], dt_bias[i], a_log[i], d_skip[i], g_ssm_out[i], w_out[i], g_cross[i],
                     g_mem[i], w_cq[i], w_ckv[i], g_cq[i], g_ck[i], w_co[i], g_mlp[i], w_up[i], w_down[i])
    return x2d.reshape(bsz, s_len, d_model)
```

```python
import functools
import math

import jax
import jax.numpy as jnp
from jax import lax
from jax.experimental import pallas as pl
from jax.experimental.pallas import tpu as pltpu

F32 = jnp.float32
BF16 = jnp.bfloat16
EPS = 1e-6

LANES = 128
ATTN_HEAD_DIM = 64
ROT_DIM = ATTN_HEAD_DIM // 4
ROPE_THETA = 500000.0
ATTN_BLOCK = 128
DILATIONS = (1, 4, 16)
SSM_HEAD_DIM = 64
SSM_GROUPS = 4
SSM_STATE = 128
CONV_WIDTH = 4
SSD_CHUNK = 128
N_CROSS_HEADS = 4
CROSS_HEAD_DIM = 128
CROSS_ROWS = 256
CROSS_GROUP = 2
VMEM_LIMIT = 58 * 1024 * 1024


def _cparams(sem):
    return pltpu.CompilerParams(dimension_semantics=sem, vmem_limit_bytes=VMEM_LIMIT)


def _rms_rows(x, g):
    ms = jnp.mean(x * x, axis=-1, keepdims=True)
    return x * lax.rsqrt(ms + EPS) * g


def _norm_rows_to(x_ref, g_ref, h_ref, rows=16):
    def body(c, carry):
        r0 = pl.multiple_of(c * rows, rows)
        h_ref[pl.ds(r0, rows), :] = _rms_rows(x_ref[pl.ds(r0, rows), :], g_ref[...]).astype(h_ref.dtype)
        return carry
    lax.fori_loop(0, x_ref.shape[0] // rows, body, 0, unroll=4)


def _norm_chunk_to(x_ref, g_ref, h_ref, start, n_rows, rows=16):
    for r in range(0, n_rows, rows):
        rs = pl.ds(start + r, rows)
        h_ref[rs, :] = _rms_rows(x_ref[rs, :], g_ref[...]).astype(h_ref.dtype)


def _in_proj_kernel(x0_ref, xn_ref, g_ref, w_ref, wdt_ref, o_ref, dt_ref, h_ref):
    i, j = pl.program_id(0), pl.program_id(1)
    tm = xn_ref.shape[0]
    cur, nxt = i % 2, (i + 1) % 2

    @pl.when((i == 0) & (j == 0))
    def _():
        _norm_rows_to(x0_ref, g_ref, h_ref.at[0])

    @pl.when(j == 0)
    def _():
        dt_ref[...] = lax.dot_general(h_ref[cur], wdt_ref[...].astype(BF16), (((1,), (1,)), ((), ())),
                                      preferred_element_type=F32)
    o_ref[...] = lax.dot_general(h_ref[cur], w_ref[...].astype(BF16), (((1,), (1,)), ((), ())),
                                 preferred_element_type=F32).astype(o_ref.dtype)
    chunk = tm // IN_PROJ_NORM_STEPS
    start = pl.multiple_of(jnp.minimum(j, IN_PROJ_NORM_STEPS - 1) * chunk, chunk)
    _norm_chunk_to(xn_ref, g_ref, h_ref.at[nxt], start, chunk)


IN_PROJ_NORM_STEPS = 8


def in_proj(x2d, g, w_t, w_dt, n, tm=1024, tn=768):
    m, d = x2d.shape
    n_i = m // tm
    assert n // tn >= IN_PROJ_NORM_STEPS
    return pl.pallas_call(
        _in_proj_kernel,
        grid=(n_i, n // tn),
        in_specs=[pl.BlockSpec((tm, d), lambda i, j: (0, 0), pipeline_mode=pl.Buffered(1)),
                  pl.BlockSpec((tm, d), lambda i, j: (jnp.minimum(i + 1, n_i - 1), 0)),
                  pl.BlockSpec((1, d), lambda i, j: (0, 0)),
                  pl.BlockSpec((tn, d), lambda i, j: (j, 0)),
                  pl.BlockSpec((LANES, d), lambda i, j: (0, 0), pipeline_mode=pl.Buffered(1))],
        out_specs=[pl.BlockSpec((tm, tn), lambda i, j: (i, j)),
                   pl.BlockSpec((tm, LANES), lambda i, j: (i, 0))],
        out_shape=[jax.ShapeDtypeStruct((m, n), BF16), jax.ShapeDtypeStruct((m, LANES), F32)],
        scratch_shapes=[pltpu.VMEM((2, tm, d), BF16)],
        compiler_params=_cparams(("arbitrary", "arbitrary")),
        name="in_proj",
    )(x2d, x2d, g, w_t, w_dt)


def _rope_kernel(pos_ref, expo_ref, cmask_ref, cos_ref, sin_ref):
    inv_freq = jnp.power(jnp.float32(ROPE_THETA), expo_ref[...])
    ang = pos_ref[...].astype(F32) * inv_freq
    cm = cmask_ref[...]
    cos_ref[...] = jnp.cos(ang) * cm + (1.0 - cm)
    sin_ref[...] = jnp.sin(ang) * cm


def rope_tables(pos_col, tm=2048):
    m = pos_col.shape[0]
    half = ROT_DIM // 2
    d = jnp.arange(LANES) % ATTN_HEAD_DIM
    expo = (-2.0 * (d % half).astype(F32) / ROT_DIM)[None, :]
    cmask = (d < ROT_DIM).astype(F32)[None, :]
    row = pl.BlockSpec((1, LANES), lambda i: (0, 0))
    tab = pl.BlockSpec((tm, LANES), lambda i: (i, 0))
    return pl.pallas_call(
        _rope_kernel,
        grid=(m // tm,),
        in_specs=[pl.BlockSpec((tm, 1), lambda i: (i, 0)), row, row],
        out_specs=[tab, tab],
        out_shape=[jax.ShapeDtypeStruct((m, LANES), F32)] * 2,
        compiler_params=_cparams(("parallel",)),
        name="rope_tab",
    )(pos_col, expo, cmask)


def _rotate_half_matrix():
    half = ROT_DIM // 2
    src = jnp.arange(LANES)[:, None]
    dst = jnp.arange(LANES)[None, :]
    d = dst % ATTN_HEAD_DIM
    first = (d < half) & (src == dst + half)
    second = (d >= half) & (d < ROT_DIM) & (src == dst - half)
    return (second.astype(F32) - first.astype(F32)).astype(BF16)


def _rows(start, size, dil):
    return pl.ds(start, size) if dil == 1 else pl.ds(start, size, stride=dil)


ATTN_GROUP = 2


def _software_pipeline(stages, items, group, extras=()):
    groups = [items[i:i + group] for i in range(0, len(items), group)]
    n_trips = len(groups) + len(stages) - 1
    for t in range(n_trips):
        for lag, stage in enumerate(stages):
            if 0 <= t - lag < len(groups):
                for g, item in enumerate(groups[t - lag]):
                    stage(item, ((t - lag) % 2) * group + g)
        for k, extra in enumerate(extras):
            if k * n_trips // len(extras) == t:
                extra()


def _attn_kernel(q0_ref, k0_ref, v0_ref, cos0_ref, sin0_ref, qn_ref, kn_ref, vn_ref, cosn_ref, sinn_ref,
                 gq_ref, gk_ref, seg_ref, rot_ref, wu_ref, wd_ref, wo_ref, o_ref, wu_out, wd_out, wo_out,
                 set_a, set_b, ob, mb, lb, s_scr, p_scr):
    s_len = qn_ref.shape[0]
    blk = ATTN_BLOCK
    step = pl.program_id(0) * pl.num_programs(1) + pl.program_id(1)
    lane = lax.broadcasted_iota(jnp.int32, (blk, LANES), 1)
    head0 = lane < ATTN_HEAD_DIM
    qi = lax.broadcasted_iota(jnp.int32, (2 * blk, 2 * blk), 0) % blk
    kj = lax.broadcasted_iota(jnp.int32, (2 * blk, 2 * blk), 1)
    band_mask = (kj >= qi) & (kj <= qi + blk)
    first_mask = (lax.broadcasted_iota(jnp.int32, (2 * blk, blk), 1)
                  <= lax.broadcasted_iota(jnp.int32, (2 * blk, blk), 0) % blk)

    wu_out[...] = wu_ref[...].astype(BF16)
    wd_out[...] = wd_ref[...].astype(BF16)
    wo_out[...] = wo_ref[...].astype(BF16)

    prep_rows = 256

    def prep(srcs, dst, r0):
        q_ref, k_ref, v_ref, cos_ref, sin_ref = srcs
        qf, kf, vf = dst[:3]
        rows = slice(r0, r0 + prep_rows)
        cs, sn = cos_ref[rows, :], sin_ref[rows, :]

        def norm_rope(x_ref, g_ref):
            x = x_ref[rows, :].astype(F32)
            ss = jnp.dot((x * x).astype(BF16), seg_ref[...], preferred_element_type=F32)
            y = x * lax.rsqrt(ss * (1.0 / ATTN_HEAD_DIM) + EPS) * g_ref[...]
            return y * cs + jnp.dot(y.astype(BF16), rot_ref[...], preferred_element_type=F32) * sn

        qf[rows, :] = norm_rope(q_ref, gq_ref)
        kf[rows, :] = norm_rope(k_ref, gk_ref)
        vf[rows, :] = v_ref[rows, :].astype(F32)

    dmid = DILATIONS[1]
    run = s_len // dmid

    def deinterleave(bufs, r):
        for src, dst in zip(bufs[:3], bufs[3:]):
            dst[r * run:(r + 1) * run, :] = src[pl.ds(r, run, stride=dmid), :]

    def prepare(srcs, bufs):
        return ([functools.partial(prep, srcs, bufs, r0) for r0 in range(0, s_len, prep_rows)]
                + [functools.partial(deinterleave, bufs, r) for r in range(dmid)])

    @pl.when(step == 0)
    def _():
        for piece in prepare((q0_ref, k0_ref, v0_ref, cos0_ref, sin0_ref), set_a):
            piece()

    def block_descs(bufs):
        natural, deint = bufs[:3], bufs[3:]
        descs = [(deint, 1, 1, r * run, r * run, blk, first_mask) for r in range(dmid)]
        descs += [(deint, 1, 1, r * run + n * blk, r * run + (n - 1) * blk, 2 * blk, band_mask)
                  for n in range(1, run // blk) for r in range(dmid)]
        descs += [(deint, 2, dmid, r_hi * run + r_lo, r_hi * run + r_lo, blk, first_mask)
                  for r_hi in range(dmid) for r_lo in range(dmid)]
        descs += [(natural, 0, 1, 0, 0, blk, first_mask)]
        descs += [(natural, 0, 1, n * blk, (n - 1) * blk, 2 * blk, band_mask) for n in range(1, s_len // blk)]
        return descs

    def scores(desc, slot):
        (q_src, k_src, _), _, dil, q_start, k_start, n_keys, _ = desc
        qt = q_src[_rows(q_start, blk, dil), :]
        zero = jnp.zeros_like(qt)
        q2 = jnp.concatenate([jnp.where(head0, qt, zero), jnp.where(head0, zero, qt)], axis=0).astype(BF16)
        kb = k_src[_rows(k_start, n_keys, dil), :].astype(BF16)
        s_scr[slot, :, :n_keys] = lax.dot_general(q2, kb, (((1,), (1,)), ((), ())), preferred_element_type=F32)

    def softmax(desc, slot):
        _, br, dil, q_start, _, n_keys, mask = desc
        s = jnp.where(mask, s_scr[slot, :, :n_keys], -jnp.inf)
        m = jnp.max(s, axis=-1, keepdims=True)
        p_scr[slot, :, :n_keys] = jnp.exp2(s - m).astype(BF16)
        mb[br, _rows(q_start, blk, dil), :] = jnp.where(head0, m[:blk], m[blk:])

    def values(desc, slot):
        (_, _, v_src), br, dil, q_start, k_start, n_keys, _ = desc
        vb = v_src[_rows(k_start, n_keys, dil), :].astype(BF16)
        v1 = jnp.concatenate([vb, jnp.ones_like(vb)], axis=1)
        r = jnp.dot(p_scr[slot, :, :n_keys], v1, preferred_element_type=F32)
        out_rows = _rows(q_start, blk, dil)
        ob[br, out_rows, :] = jnp.where(head0, r[:blk, :LANES], r[blk:, :LANES])
        lb[br, out_rows, :] = jnp.where(head0, r[:blk, LANES:], r[blk:, LANES:])
        if br == 0:
            merge(q_start)

    def merge(n0):
        sub = blk // dmid
        for r in range(dmid):
            nat_rows = pl.ds(n0 + r, sub, stride=dmid)
            d_rows = slice(r * run + n0 // dmid, r * run + n0 // dmid + sub)
            rows = (nat_rows, d_rows, d_rows)
            ms = [mb[g, rows[g], :] for g in range(len(DILATIONS))]
            m = functools.reduce(jnp.maximum, ms)
            ws = [jnp.exp2(mg - m) for mg in ms]
            num = functools.reduce(jnp.add, [w * ob[g, rows[g], :] for g, w in enumerate(ws)])
            den = functools.reduce(jnp.add, [w * lb[g, rows[g], :] for g, w in enumerate(ws)])
            ob[0, nat_rows, :] = num / den
        o_ref[n0:n0 + blk, :] = ob[0, n0:n0 + blk, :].astype(o_ref.dtype)

    for parity, (cur_set, nxt_set) in enumerate(((set_a, set_b), (set_b, set_a))):
        @pl.when(step % 2 == parity)
        def _(cur_set=cur_set, nxt_set=nxt_set):
            _software_pipeline((scores, softmax, values), block_descs(cur_set), ATTN_GROUP,
                               extras=prepare((qn_ref, kn_ref, vn_ref, cosn_ref, sinn_ref), nxt_set))


def dilated_attention(qkv, tabs, gq2, gk2, w_up, w_down, w_out, bsz, s_len, n_pairs):
    cos_t, sin_t = tabs
    d1, d2, d3 = DILATIONS
    assert d1 == 1 and d3 == d2 * d2 and s_len == d3 * ATTN_BLOCK, "layout assumes dilations (1, d, d*d), one block per largest class"
    lane = jnp.arange(LANES)
    seg = (lane[:, None] // ATTN_HEAD_DIM == lane[None, :] // ATTN_HEAD_DIM).astype(BF16)
    n_steps = bsz * n_pairs
    wu_rows, wd_rows, wo_rows = (w.shape[0] // n_steps for w in (w_up, w_down, w_out))

    def nxt(b, p):
        s = jnp.minimum(b * n_pairs + p + 1, n_steps - 1)
        return s // n_pairs, s % n_pairs

    once = dict(pipeline_mode=pl.Buffered(1))
    first = lambda off: pl.BlockSpec((s_len, LANES), lambda b, p, off=off: (0, off), **once)
    nblk = lambda off: pl.BlockSpec((s_len, LANES), lambda b, p, off=off: (nxt(b, p)[0], off + nxt(b, p)[1]))
    tab0 = pl.BlockSpec((s_len, LANES), lambda b, p: (0, 0), **once)
    tabn = pl.BlockSpec((s_len, LANES), lambda b, p: (nxt(b, p)[0], 0))
    row = pl.BlockSpec((1, LANES), lambda b, p: (0, 0))
    sq = pl.BlockSpec((LANES, LANES), lambda b, p: (0, 0))
    wu_spec = pl.BlockSpec((wu_rows, w_up.shape[1]), lambda b, p: (b * n_pairs + p, 0))
    wd_spec = pl.BlockSpec((wd_rows, w_down.shape[1]), lambda b, p: (b * n_pairs + p, 0))
    wo_spec = pl.BlockSpec((wo_rows, w_out.shape[1]), lambda b, p: (b * n_pairs + p, 0))
    return pl.pallas_call(
        _attn_kernel,
        grid=(bsz, n_pairs),
        in_specs=[first(0), first(n_pairs), first(2 * n_pairs), tab0, tab0,
                  nblk(0), nblk(n_pairs), nblk(2 * n_pairs), tabn, tabn,
                  row, row, sq, sq, wu_spec, wd_spec, wo_spec],
        out_specs=[pl.BlockSpec((s_len, LANES), lambda b, p: (b, p)), wu_spec, wd_spec, wo_spec],
        out_shape=[jax.ShapeDtypeStruct((bsz * s_len, n_pairs * LANES), BF16),
                   jax.ShapeDtypeStruct(w_up.shape, BF16), jax.ShapeDtypeStruct(w_down.shape, BF16),
                   jax.ShapeDtypeStruct(w_out.shape, BF16)],
        scratch_shapes=[[pltpu.VMEM((s_len, LANES), F32)] * 6] * 2
                       + [pltpu.VMEM((len(DILATIONS), s_len, LANES), F32)] * 3
                       + [pltpu.VMEM((2 * ATTN_GROUP, 2 * ATTN_BLOCK, 2 * ATTN_BLOCK), F32),
                          pltpu.VMEM((2 * ATTN_GROUP, 2 * ATTN_BLOCK, 2 * ATTN_BLOCK), BF16)],
        compiler_params=_cparams(("arbitrary", "arbitrary")),
        name="dilated_attn",
    )(qkv, qkv, qkv, cos_t, sin_t, qkv, qkv, qkv, cos_t, sin_t, gq2, gk2, seg, _rotate_half_matrix(), w_up, w_down, w_out)


def _split3(x):
    hi = x.astype(BF16)
    r1 = x - hi.astype(F32)
    mid = r1.astype(BF16)
    lo = (r1 - mid.astype(F32)).astype(BF16)
    return hi, mid, lo


def _silu(x):
    h = 0.5 * x
    return h * jnp.tanh(h) + h


LOG2E = 1.4426950408889634
SSD_STEP_CHUNKS = 4


def _ssd_kernel(xbc_ref, z_ref, dt_ref, cw_ref, cb_ref, dtb_ref, alog_ref, dskip_ref, g_ref,
                expand_ref, tril_ref, o_ref, xpad, xc, st):
    q = SSD_CHUNK
    n_rows = xbc_ref.shape[0]
    d_ssm = z_ref.shape[1]
    d_conv = xbc_ref.shape[1]
    gw = d_ssm // SSM_GROUPS
    heads_per_group = gw // SSM_HEAD_DIM
    pad = 8

    @pl.when(pl.program_id(1) == 0)
    def _():
        xpad[:, 0:pad, :] = jnp.zeros((d_conv // LANES, pad, LANES), F32)
        st[...] = jnp.zeros_like(st)

    for c0 in range(0, d_conv, LANES):
        cols = slice(c0, c0 + LANES)
        slab = c0 // LANES
        xpad[slab, pad:pad + n_rows, :] = xbc_ref[:, cols].astype(F32)
        acc = cb_ref[:, cols] + cw_ref[CONV_WIDTH - 1:CONV_WIDTH, cols] * xpad[slab, pad:pad + n_rows, :]
        for w in range(CONV_WIDTH - 1):
            off = pad - (CONV_WIDTH - 1) + w
            acc = acc + cw_ref[w:w + 1, cols] * xpad[slab, pl.ds(off, n_rows, stride=1), :]
        xc[:, cols] = _silu(acc)
    xpad[:, 0:pad, :] = xpad[:, n_rows:n_rows + pad, :]

    li = lax.broadcasted_iota(jnp.int32, (q, q), 0)
    si = lax.broadcasted_iota(jnp.int32, (q, q), 1)
    causal = li >= si
    lane_g = lax.broadcasted_iota(jnp.int32, (q, gw), 1)
    tril = tril_ref[...]
    expand = expand_ref[...]

    for r0 in range(0, n_rows, q):
        rows = slice(r0, r0 + q)
        x_dt = dt_ref[rows, :] + dtb_ref[...]
        dt = jnp.maximum(x_dt, 0.0) + jnp.log1p(jnp.exp(-jnp.abs(x_dt)))
        dta = dt * (-LOG2E * jnp.exp(alog_ref[...]))
        acs = sum(jnp.dot(tril, part, preferred_element_type=F32) for part in _split3(dta))
        acs_t = acs.T
        last = acs[q - 1:q, :]
        dt_e = jnp.dot(dt.astype(BF16), expand, preferred_element_type=F32)
        dec_e = jnp.dot(jnp.exp2(acs).astype(BF16), expand, preferred_element_type=F32)
        w_e = jnp.dot((jnp.exp2(last - acs) * dt).astype(BF16), expand, preferred_element_type=F32)

        for g in range(SSM_GROUPS):
            cols = slice(g * gw, (g + 1) * gw)
            b_f = xc[rows, d_ssm + g * SSM_STATE:d_ssm + (g + 1) * SSM_STATE]
            c_b = xc[rows, d_ssm + (SSM_GROUPS + g) * SSM_STATE:d_ssm + (SSM_GROUPS + g + 1) * SSM_STATE].astype(BF16)
            xs = xc[rows, cols]
            cb = lax.dot_general(c_b, b_f.astype(BF16), (((1,), (1,)), ((), ())), preferred_element_type=F32)
            xdt = (xs * dt_e[:, cols]).astype(BF16)
            ws, rs = [], []
            for hh in range(heads_per_group):
                h = g * heads_per_group + hh
                seg = acs[:, h:h + 1] - acs_t[h:h + 1, :]
                l_mat = jnp.exp2(jnp.where(causal, seg, -jnp.inf))
                ws.append((cb * l_mat).astype(BF16))
                in_head = (lane_g >= hh * SSM_HEAD_DIM) & (lane_g < (hh + 1) * SSM_HEAD_DIM)
                rs.append(jnp.where(in_head, xdt, jnp.zeros_like(xdt)))
            y = jnp.dot(jnp.concatenate(ws, axis=1), jnp.concatenate(rs, axis=0), preferred_element_type=F32)
            st_g = st[:, cols]
            y = y + jnp.dot(c_b, st_g.astype(BF16), preferred_element_type=F32) * dec_e[:, cols]
            xw = (xs * w_e[:, cols]).astype(BF16)
            st[:, cols] = st_g * dec_e[q - 1:q, cols] + jnp.dot(b_f.T.astype(BF16), xw, preferred_element_type=F32)
            y = y + dskip_ref[:, cols] * xs
            y = y * _silu(z_ref[rows, cols].astype(F32))
            o_ref[rows, cols] = _rms_rows(y, g_ref[:, cols]).astype(o_ref.dtype)


def ssd_mixer(packed, dt_raw, conv_w, conv_b, dt_bias, a_log, d_skip, g_out, bsz, s_len, d_ssm, xbc_col, z_col):
    n_heads = d_ssm // SSM_HEAD_DIM
    d_conv = d_ssm + 2 * SSM_GROUPS * SSM_STATE
    step_rows = SSD_STEP_CHUNKS * SSD_CHUNK
    nc = s_len // step_rows
    padl = lambda v: jnp.pad(v.astype(F32), (0, LANES - n_heads))[None, :]
    expand = (jnp.arange(LANES)[:, None] == (jnp.arange(d_ssm) // SSM_HEAD_DIM)[None, :]).astype(BF16)
    tril = (jnp.arange(SSD_CHUNK)[:, None] >= jnp.arange(SSD_CHUNK)[None, :]).astype(BF16)
    full = lambda shape: pl.BlockSpec(shape, lambda b, c: (0, 0))
    return pl.pallas_call(
        _ssd_kernel,
        grid=(bsz, nc),
        in_specs=[pl.BlockSpec((step_rows, d_conv), lambda b, c: (b * nc + c, xbc_col // d_conv)),
                  pl.BlockSpec((step_rows, d_ssm), lambda b, c: (b * nc + c, z_col // d_ssm)),
                  pl.BlockSpec((step_rows, LANES), lambda b, c: (b * nc + c, 0)),
                  full((CONV_WIDTH, d_conv)), full((1, d_conv)), full((1, LANES)), full((1, LANES)),
                  full((1, d_ssm)), full((1, d_ssm)), full((LANES, d_ssm)), full((SSD_CHUNK, SSD_CHUNK))],
        out_specs=pl.BlockSpec((step_rows, d_ssm), lambda b, c: (b * nc + c, 0)),
        out_shape=jax.ShapeDtypeStruct((bsz * s_len, d_ssm), BF16),
        scratch_shapes=[pltpu.VMEM((d_conv // LANES, step_rows + 8, LANES), F32),
                        pltpu.VMEM((step_rows, d_conv), F32),
                        pltpu.VMEM((SSM_STATE, d_ssm), F32)],
        compiler_params=_cparams(("parallel", "arbitrary")),
        name="ssd",
    )(packed, packed, dt_raw, conv_w.astype(F32), conv_b.astype(F32)[None, :], padl(dt_bias), padl(a_log),
      jnp.repeat(d_skip.astype(F32), SSM_HEAD_DIM)[None, :], g_out.astype(F32)[None, :], expand, tril)


def _mem_kv_kernel(mem_ref, g_ref, w_ref, gk_ref, kv_ref, h_ref):
    d_cross = N_CROSS_HEADS * CROSS_HEAD_DIM
    _norm_rows_to(mem_ref, g_ref, h_ref)
    kv = jnp.dot(h_ref[...], w_ref[...], preferred_element_type=F32)
    for h in range(N_CROSS_HEADS):
        cols = slice(h * CROSS_HEAD_DIM, (h + 1) * CROSS_HEAD_DIM)
        kv_ref[:, cols] = _rms_rows(kv[:, cols], gk_ref[...]).astype(kv_ref.dtype)
    kv_ref[:, d_cross:] = kv[:, d_cross:].astype(kv_ref.dtype)


def mem_kv(mem2d, g, w, gk, tm=256):
    m, d = mem2d.shape
    n = w.shape[1]
    return pl.pallas_call(
        _mem_kv_kernel,
        grid=(m // tm,),
        in_specs=[pl.BlockSpec((tm, d), lambda i: (i, 0)),
                  pl.BlockSpec((1, d), lambda i: (0, 0)),
                  pl.BlockSpec((d, n), lambda i: (0, 0)),
                  pl.BlockSpec((1, CROSS_HEAD_DIM), lambda i: (0, 0))],
        out_specs=pl.BlockSpec((tm, n), lambda i: (i, 0)),
        out_shape=jax.ShapeDtypeStruct((m, n), BF16),
        scratch_shapes=[pltpu.VMEM((tm, d), BF16)],
        compiler_params=_cparams(("parallel",)),
        name="mem_kv",
    )(mem2d, g, w, gk)


def _mix_cross_kernel(attn_ref, ssm_ref, x_ref, ga_ref, w_ref, gc_ref, wq_ref, gq_ref, k_ref, v_ref, wo_ref, o_ref,
                      h_ref, xk_ref, an_ref, q_ref, a_ref, s_scr, p_scr):
    s = pl.program_id(0)
    n_tiles = pl.num_programs(0) - 1
    tm, d_attn = x_ref.shape[0], attn_ref.shape[1]
    rows = CROSS_ROWS
    prev, cur = (s + 1) % 2, s % 2

    @pl.when(s > 0)
    def _():
        q_ref[...] = jnp.dot(h_ref[prev], wq_ref[...], preferred_element_type=F32)

        def scores(item, slot):
            r0, h = item
            cols = slice(h * CROSS_HEAD_DIM, (h + 1) * CROSS_HEAD_DIM)
            qh = (_rms_rows(q_ref[r0:r0 + rows, cols], gq_ref[...]) * (CROSS_HEAD_DIM ** -0.5)).astype(BF16)
            s_scr[slot] = lax.dot_general(qh, k_ref[:, cols], (((1,), (1,)), ((), ())), preferred_element_type=F32)

        def softmax(item, slot):
            sc = s_scr[slot]
            p_scr[slot] = jnp.exp(sc - jnp.max(sc, axis=-1, keepdims=True)).astype(BF16)

        def values(item, slot):
            r0, h = item
            cols = slice(h * CROSS_HEAD_DIM, (h + 1) * CROSS_HEAD_DIM)
            vh = v_ref[:, cols]
            r = jnp.dot(p_scr[slot], jnp.concatenate([vh, jnp.ones_like(vh)], axis=1), preferred_element_type=F32)
            a_ref[r0:r0 + rows, cols] = (r[:, :CROSS_HEAD_DIM] / r[:, CROSS_HEAD_DIM:]).astype(BF16)

        items = [(r0, h) for r0 in range(0, tm, rows) for h in range(N_CROSS_HEADS)]
        _software_pipeline((scores, softmax, values), items, CROSS_GROUP)
        o_ref[...] = xk_ref[prev] + jnp.dot(a_ref[...], wo_ref[...], preferred_element_type=F32)

    @pl.when(s < n_tiles)
    def _():
        n_split = 2
        part = tm // n_split
        for r0 in range(0, tm, part):
            for r in range(r0, r0 + part, 32):
                an_ref[r:r + 32, :] = _rms_rows(attn_ref[r:r + 32, :].astype(F32), ga_ref[...]).astype(BF16)
            rs = slice(r0, r0 + part)
            xk_ref[cur, rs, :] = (x_ref[rs, :] + jnp.dot(an_ref[rs, :], w_ref[:d_attn, :], preferred_element_type=F32)
                                  + jnp.dot(ssm_ref[rs, :], w_ref[d_attn:, :], preferred_element_type=F32))
            _norm_chunk_to(xk_ref.at[cur], gc_ref, h_ref.at[cur], r0, part)


def mix_cross(attn, ssm, x2d, ga, w, gc, wq, gq, kv, wo, s_len, n_mem, tm=512):
    m, d = x2d.shape
    da, ds_ = attn.shape[1], ssm.shape[1]
    dc = wq.shape[1]
    n_tiles = m // tm
    per_batch = s_len // tm
    ahead = lambda s: (jnp.minimum(s, n_tiles - 1), 0)
    behind = lambda s: jnp.maximum(s - 1, 0)
    full = lambda shape, **kw: pl.BlockSpec(shape, lambda s: (0, 0), **kw)
    once = dict(pipeline_mode=pl.Buffered(1))
    return pl.pallas_call(
        _mix_cross_kernel,
        grid=(n_tiles + 1,),
        in_specs=[pl.BlockSpec((tm, da), ahead), pl.BlockSpec((tm, ds_), ahead), pl.BlockSpec((tm, d), ahead),
                  full((1, da)), full((da + ds_, d), **once), full((1, d)), full((d, dc), **once),
                  full((1, CROSS_HEAD_DIM)),
                  pl.BlockSpec((n_mem, dc), lambda s: (behind(s) // per_batch, 0)),
                  pl.BlockSpec((n_mem, dc), lambda s: (behind(s) // per_batch, 1)),
                  full((dc, d), **once)],
        out_specs=pl.BlockSpec((tm, d), lambda s: (behind(s), 0)),
        out_shape=jax.ShapeDtypeStruct((m, d), F32),
        scratch_shapes=[pltpu.VMEM((2, tm, d), BF16), pltpu.VMEM((2, tm, d), F32), pltpu.VMEM((tm, da), BF16),
                        pltpu.VMEM((tm, dc), F32), pltpu.VMEM((tm, dc), BF16),
                        pltpu.VMEM((2 * CROSS_GROUP, CROSS_ROWS, n_mem), F32),
                        pltpu.VMEM((2 * CROSS_GROUP, CROSS_ROWS, n_mem), BF16)],
        compiler_params=_cparams(("arbitrary",)),
        name="mix_cross",
    )(attn, ssm, x2d, ga, w, gc, wq, gq, kv, kv, wo)


def _mlp_kernel(x_ref, g_ref, wu_ref, wd_ref, o_ref, h_ref):
    tm = x_ref.shape[0]

    def expand(rows):
        u = jnp.dot(h_ref[rows, :], wu_ref[...], preferred_element_type=F32)
        u = jnp.square(jnp.maximum(u, 0.0)).astype(BF16)
        return jnp.dot(u, wd_ref[...], preferred_element_type=F32)

    @pl.when(pl.program_id(1) == 0)
    def _():
        part = tm // 2
        for r0 in range(0, tm, part):
            _norm_chunk_to(x_ref, g_ref, h_ref, r0, part)
            rows = slice(r0, r0 + part)
            o_ref[rows, :] = x_ref[rows, :] + expand(rows)

    @pl.when(pl.program_id(1) > 0)
    def _():
        o_ref[...] += expand(slice(None))


def mlp(x2d, g, wu, wd, tm=512, tf=2048):
    m, d = x2d.shape
    f = wu.shape[1]
    return pl.pallas_call(
        _mlp_kernel,
        grid=(m // tm, f // tf),
        in_specs=[pl.BlockSpec((tm, d), lambda i, j: (i, 0)),
                  pl.BlockSpec((1, d), lambda i, j: (0, 0)),
                  pl.BlockSpec((d, tf), lambda i, j: (0, j)),
                  pl.BlockSpec((tf, d), lambda i, j: (j, 0))],
        out_specs=pl.BlockSpec((tm, d), lambda i, j: (i, 0)),
        out_shape=jax.ShapeDtypeStruct((m, d), F32),
        scratch_shapes=[pltpu.VMEM((tm, d), BF16)],
        compiler_params=_cparams(("parallel", "arbitrary")),
        name="mlp",
    )(x2d, g, wu, wd)


def _layer(x2d, mem2d, pos_col, bsz, s_len, n_mem, g_mix, w_in, g_q, g_k, g_attn_out, conv_w, conv_b, dt_bias,
           a_log, d_skip, g_ssm_out, w_out, g_cross, g_mem, w_cq, w_ckv, g_cq, g_ck, w_co, g_mlp, w_up, w_down):
    d_model = x2d.shape[1]
    d_attn = d_model // 2
    d_ssm = d_model // 2
    n_pairs = d_attn // LANES
    n_ssm_heads = d_ssm // SSM_HEAD_DIM
    d_conv = d_ssm + 2 * SSM_GROUPS * SSM_STATE
    d_packed = 3 * d_attn + d_ssm + d_conv
    row = lambda v: v.astype(F32)[None, :]

    w_in_t = w_in.T
    w_dt_t = jnp.pad(w_in_t[d_packed:], ((0, LANES - n_ssm_heads), (0, 0)))
    packed, dt_raw = in_proj(x2d, row(g_mix), w_in_t, w_dt_t, d_packed)

    tabs = rope_tables(pos_col)
    attn, w_up_bf, w_down_bf, w_out_bf = dilated_attention(
        packed, tabs, row(jnp.tile(g_q, 2)) * (ATTN_HEAD_DIM ** -0.5 * LOG2E), row(jnp.tile(g_k, 2)), w_up, w_down,
        w_out, bsz, s_len, n_pairs)
    ssm = ssd_mixer(packed, dt_raw, conv_w, conv_b, dt_bias, a_log, d_skip, g_ssm_out, bsz, s_len, d_ssm,
                    xbc_col=3 * d_attn + d_ssm, z_col=3 * d_attn)
    kv = mem_kv(mem2d, row(g_mem), w_ckv.astype(BF16), row(g_ck))
    x2d = mix_cross(attn, ssm, x2d, row(g_attn_out), w_out_bf, row(g_cross), w_cq.astype(BF16), row(g_cq), kv,
                    w_co.astype(BF16), s_len, n_mem)

    return mlp(x2d, row(g_mlp), w_up_bf, w_down_bf)


def kernel(x, mem, positions, g_mix, w_in, g_q, g_k, g_attn_out, conv_w, conv_b, dt_bias, a_log, d_skip, g_ssm_out,
           w_out, g_cross, g_mem, w_cq, w_ckv, g_cq, g_ck, w_co, g_mlp, w_up, w_down):
    bsz, s_len, d_model = x.shape
    n_mem = mem.shape[1]
    x2d = x.reshape(bsz * s_len, d_model)
    mem2d = mem.reshape(bsz * n_mem, d_model)
    pos_col = positions.reshape(bsz * s_len, 1)
    for i in range(g_mix.shape[0]):
        x2d = _layer(x2d, mem2d, pos_col, bsz, s_len, n_mem, g_mix[i], w_in[i], g_q[i], g_k[i], g_attn_out[i],
                     conv_w[i], conv_b[i], dt_bias[i], a_log[i], d_skip[i], g_ssm_out[i], w_out[i], g_cross[i],
                     g_mem[i], w_cq[i], w_ckv[i], g_cq[i], g_ck[i], w_co[i], g_mlp[i], w_up[i], w_down[i])
    return x2d.reshape(bsz, s_len, d_model)
```

```python
import functools
import math

import jax
import jax.numpy as jnp
from jax import lax
from jax.experimental import pallas as pl
from jax.experimental.pallas import tpu as pltpu

F32 = jnp.float32
BF16 = jnp.bfloat16
EPS = 1e-6

LANES = 128
ATTN_HEAD_DIM = 64
ROT_DIM = ATTN_HEAD_DIM // 4
ROPE_THETA = 500000.0
ATTN_BLOCK = 128
DILATIONS = (1, 4, 16)
SSM_HEAD_DIM = 64
SSM_GROUPS = 4
SSM_STATE = 128
CONV_WIDTH = 4
SSD_CHUNK = 128
N_CROSS_HEADS = 4
CROSS_HEAD_DIM = 128
CROSS_ROWS = 256
CROSS_GROUP = 2
VMEM_LIMIT = 58 * 1024 * 1024


def _cparams(sem):
    return pltpu.CompilerParams(dimension_semantics=sem, vmem_limit_bytes=VMEM_LIMIT)


def _rms_rows(x, g):
    ms = jnp.mean(x * x, axis=-1, keepdims=True)
    return x * lax.rsqrt(ms + EPS) * g


def _norm_rows_to(x_ref, g_ref, h_ref, rows=16):
    def body(c, carry):
        r0 = pl.multiple_of(c * rows, rows)
        h_ref[pl.ds(r0, rows), :] = _rms_rows(x_ref[pl.ds(r0, rows), :], g_ref[...]).astype(h_ref.dtype)
        return carry
    lax.fori_loop(0, x_ref.shape[0] // rows, body, 0, unroll=4)


def _norm_chunk_to(x_ref, g_ref, h_ref, start, n_rows, rows=16):
    for r in range(0, n_rows, rows):
        rs = pl.ds(start + r, rows)
        h_ref[rs, :] = _rms_rows(x_ref[rs, :], g_ref[...]).astype(h_ref.dtype)


def _in_proj_kernel(x0_ref, xn_ref, g_ref, w_ref, wdt_ref, o_ref, dt_ref, h_ref):
    i, j = pl.program_id(0), pl.program_id(1)
    cur, nxt = i % 2, (i + 1) % 2

    @pl.when((i == 0) & (j == 0))
    def _():
        _norm_rows_to(x0_ref, g_ref, h_ref.at[0])

    @pl.when(j == 0)
    def _():
        dt_ref[...] = lax.dot_general(h_ref[cur], wdt_ref[...].astype(BF16), (((1,), (1,)), ((), ())),
                                      preferred_element_type=F32)
    o_ref[...] = lax.dot_general(h_ref[cur], w_ref[...].astype(BF16), (((1,), (1,)), ((), ())),
                                 preferred_element_type=F32).astype(o_ref.dtype)
    chunk = xn_ref.shape[0]
    start = pl.multiple_of(jnp.minimum(j, IN_PROJ_NORM_STEPS - 1) * chunk, chunk)
    for r in range(0, chunk, 16):
        h_ref[nxt, pl.ds(start + r, 16), :] = _rms_rows(xn_ref[r:r + 16, :], g_ref[...]).astype(BF16)


IN_PROJ_NORM_STEPS = 4


def in_proj(x2d, g, w_t, w_dt, n, tm=1024, tn=1024):
    m, d = x2d.shape
    n_i = m // tm
    assert n // tn >= IN_PROJ_NORM_STEPS
    return pl.pallas_call(
        _in_proj_kernel,
        grid=(n_i, n // tn),
        in_specs=[pl.BlockSpec((tm, d), lambda i, j: (0, 0), pipeline_mode=pl.Buffered(1)),
                  pl.BlockSpec((tm // IN_PROJ_NORM_STEPS, d),
                               lambda i, j: (jnp.minimum(i + 1, n_i - 1) * IN_PROJ_NORM_STEPS
                                             + jnp.minimum(j, IN_PROJ_NORM_STEPS - 1), 0)),
                  pl.BlockSpec((1, d), lambda i, j: (0, 0)),
                  pl.BlockSpec((tn, d), lambda i, j: (j, 0)),
                  pl.BlockSpec((LANES, d), lambda i, j: (0, 0), pipeline_mode=pl.Buffered(1))],
        out_specs=[pl.BlockSpec((tm, tn), lambda i, j: (i, j)),
                   pl.BlockSpec((tm, LANES), lambda i, j: (i, 0))],
        out_shape=[jax.ShapeDtypeStruct((m, n), BF16), jax.ShapeDtypeStruct((m, LANES), F32)],
        scratch_shapes=[pltpu.VMEM((2, tm, d), BF16)],
        compiler_params=_cparams(("arbitrary", "arbitrary")),
        name="in_proj",
    )(x2d, x2d, g, w_t, w_dt)


def _rope_kernel(pos_ref, expo_ref, cmask_ref, cos_ref, sin_ref):
    inv_freq = jnp.power(jnp.float32(ROPE_THETA), expo_ref[...])
    ang = pos_ref[...].astype(F32) * inv_freq
    cm = cmask_ref[...]
    cos_ref[...] = jnp.cos(ang) * cm + (1.0 - cm)
    sin_ref[...] = jnp.sin(ang) * cm


def rope_tables(pos_col, tm=2048):
    m = pos_col.shape[0]
    half = ROT_DIM // 2
    d = jnp.arange(LANES) % ATTN_HEAD_DIM
    expo = (-2.0 * (d % half).astype(F32) / ROT_DIM)[None, :]
    cmask = (d < ROT_DIM).astype(F32)[None, :]
    row = pl.BlockSpec((1, LANES), lambda i: (0, 0))
    tab = pl.BlockSpec((tm, LANES), lambda i: (i, 0))
    return pl.pallas_call(
        _rope_kernel,
        grid=(m // tm,),
        in_specs=[pl.BlockSpec((tm, 1), lambda i: (i, 0)), row, row],
        out_specs=[tab, tab],
        out_shape=[jax.ShapeDtypeStruct((m, LANES), F32)] * 2,
        compiler_params=_cparams(("parallel",)),
        name="rope_tab",
    )(pos_col, expo, cmask)


def _rotate_half_matrix():
    half = ROT_DIM // 2
    src = jnp.arange(LANES)[:, None]
    dst = jnp.arange(LANES)[None, :]
    d = dst % ATTN_HEAD_DIM
    first = (d < half) & (src == dst + half)
    second = (d >= half) & (d < ROT_DIM) & (src == dst - half)
    return (second.astype(F32) - first.astype(F32)).astype(BF16)


def _rows(start, size, dil):
    return pl.ds(start, size) if dil == 1 else pl.ds(start, size, stride=dil)


ATTN_GROUP = 2


def _software_pipeline(stages, items, group, extras=()):
    groups = [items[i:i + group] for i in range(0, len(items), group)]
    n_trips = len(groups) + len(stages) - 1
    for t in range(n_trips):
        for lag, stage in enumerate(stages):
            if 0 <= t - lag < len(groups):
                for g, item in enumerate(groups[t - lag]):
                    stage(item, ((t - lag) % 2) * group + g)
        for k, extra in enumerate(extras):
            if k * n_trips // len(extras) == t:
                extra()


def _attn_kernel(n_cast, q0_ref, k0_ref, v0_ref, cos0_ref, sin0_ref, qn_ref, kn_ref, vn_ref, cosn_ref, sinn_ref,
                 gq_ref, gk_ref, seg_ref, rot_ref, *refs):
    cast_in, o_ref, cast_out = refs[:n_cast], refs[n_cast], refs[n_cast + 1:2 * n_cast + 1]
    set_a, set_b, ob, mb, lb, s_scr, p_scr = refs[2 * n_cast + 1:]
    s_len = qn_ref.shape[0]
    blk = ATTN_BLOCK
    step = pl.program_id(0) * pl.num_programs(1) + pl.program_id(1)
    lane = lax.broadcasted_iota(jnp.int32, (blk, LANES), 1)
    head0 = lane < ATTN_HEAD_DIM
    qi = lax.broadcasted_iota(jnp.int32, (2 * blk, 2 * blk), 0) % blk
    kj = lax.broadcasted_iota(jnp.int32, (2 * blk, 2 * blk), 1)
    band_mask = (kj >= qi) & (kj <= qi + blk)
    first_mask = (lax.broadcasted_iota(jnp.int32, (2 * blk, blk), 1)
                  <= lax.broadcasted_iota(jnp.int32, (2 * blk, blk), 0) % blk)

    for w_ref, w_out in zip(cast_in, cast_out):
        w_out[...] = w_ref[...].astype(BF16)

    prep_rows = 256

    def prep(srcs, dst, r0):
        q_ref, k_ref, v_ref, cos_ref, sin_ref = srcs
        qf, kf, vf = dst[:3]
        rows = slice(r0, r0 + prep_rows)
        cs, sn = cos_ref[rows, :], sin_ref[rows, :]

        def norm_rope(x_ref, g_ref):
            x = x_ref[rows, :].astype(F32)
            ss = jnp.dot((x * x).astype(BF16), seg_ref[...], preferred_element_type=F32)
            y = x * lax.rsqrt(ss * (1.0 / ATTN_HEAD_DIM) + EPS) * g_ref[...]
            return y * cs + jnp.dot(y.astype(BF16), rot_ref[...], preferred_element_type=F32) * sn

        qf[rows, :] = norm_rope(q_ref, gq_ref)
        kf[rows, :] = norm_rope(k_ref, gk_ref)
        vf[rows, :] = v_ref[rows, :].astype(F32)

    dmid = DILATIONS[1]
    run = s_len // dmid

    def deinterleave(bufs, r):
        for src, dst in zip(bufs[:3], bufs[3:]):
            dst[r * run:(r + 1) * run, :] = src[pl.ds(r, run, stride=dmid), :]

    def prepare(srcs, bufs):
        return ([functools.partial(prep, srcs, bufs, r0) for r0 in range(0, s_len, prep_rows)]
                + [functools.partial(deinterleave, bufs, r) for r in range(dmid)])

    @pl.when(step == 0)
    def _():
        for piece in prepare((q0_ref, k0_ref, v0_ref, cos0_ref, sin0_ref), set_a):
            piece()

    def block_descs(bufs):
        natural, deint = bufs[:3], bufs[3:]
        descs = [(deint, 1, 1, r * run, r * run, blk, first_mask) for r in range(dmid)]
        descs += [(deint, 1, 1, r * run + n * blk, r * run + (n - 1) * blk, 2 * blk, band_mask)
                  for n in range(1, run // blk) for r in range(dmid)]
        descs += [(deint, 2, dmid, r_hi * run + r_lo, r_hi * run + r_lo, blk, first_mask)
                  for r_hi in range(dmid) for r_lo in range(dmid)]
        descs += [(natural, 0, 1, 0, 0, blk, first_mask)]
        descs += [(natural, 0, 1, n * blk, (n - 1) * blk, 2 * blk, band_mask) for n in range(1, s_len // blk)]
        return descs

    def scores(desc, slot):
        (q_src, k_src, _), _, dil, q_start, k_start, n_keys, _ = desc
        qt = q_src[_rows(q_start, blk, dil), :]
        zero = jnp.zeros_like(qt)
        q2 = jnp.concatenate([jnp.where(head0, qt, zero), jnp.where(head0, zero, qt)], axis=0).astype(BF16)
        kb = k_src[_rows(k_start, n_keys, dil), :].astype(BF16)
        s_scr[slot, :, :n_keys] = lax.dot_general(q2, kb, (((1,), (1,)), ((), ())), preferred_element_type=F32)

    def softmax(desc, slot):
        _, br, dil, q_start, _, n_keys, mask = desc
        s = jnp.where(mask, s_scr[slot, :, :n_keys], -jnp.inf)
        m = jnp.max(s, axis=-1, keepdims=True)
        p_scr[slot, :, :n_keys] = jnp.exp2(s - m).astype(BF16)
        mb[br, _rows(q_start, blk, dil), :] = jnp.where(head0, m[:blk], m[blk:])

    def values(desc, slot):
        (_, _, v_src), br, dil, q_start, k_start, n_keys, _ = desc
        vb = v_src[_rows(k_start, n_keys, dil), :].astype(BF16)
        v1 = jnp.concatenate([vb, jnp.ones_like(vb)], axis=1)
        r = jnp.dot(p_scr[slot, :, :n_keys], v1, preferred_element_type=F32)
        out_rows = _rows(q_start, blk, dil)
        ob[br, out_rows, :] = jnp.where(head0, r[:blk, :LANES], r[blk:, :LANES])
        lb[br, out_rows, :] = jnp.where(head0, r[:blk, LANES:], r[blk:, LANES:])
        if br == 0:
            merge(q_start)

    def merge(n0):
        sub = blk // dmid
        for r in range(dmid):
            nat_rows = pl.ds(n0 + r, sub, stride=dmid)
            d_rows = slice(r * run + n0 // dmid, r * run + n0 // dmid + sub)
            rows = (nat_rows, d_rows, d_rows)
            ms = [mb[g, rows[g], :] for g in range(len(DILATIONS))]
            m = functools.reduce(jnp.maximum, ms)
            ws = [jnp.exp2(mg - m) for mg in ms]
            num = functools.reduce(jnp.add, [w * ob[g, rows[g], :] for g, w in enumerate(ws)])
            den = functools.reduce(jnp.add, [w * lb[g, rows[g], :] for g, w in enumerate(ws)])
            ob[0, nat_rows, :] = num / den
        o_ref[n0:n0 + blk, :] = ob[0, n0:n0 + blk, :].astype(o_ref.dtype)

    for parity, (cur_set, nxt_set) in enumerate(((set_a, set_b), (set_b, set_a))):
        @pl.when(step % 2 == parity)
        def _(cur_set=cur_set, nxt_set=nxt_set):
            _software_pipeline((scores, softmax, values), block_descs(cur_set), ATTN_GROUP,
                               extras=prepare((qn_ref, kn_ref, vn_ref, cosn_ref, sinn_ref), nxt_set))


def dilated_attention(qkv, tabs, gq2, gk2, weights, bsz, s_len, n_pairs):
    cos_t, sin_t = tabs
    d1, d2, d3 = DILATIONS
    assert d1 == 1 and d3 == d2 * d2 and s_len == d3 * ATTN_BLOCK, "layout assumes dilations (1, d, d*d), one block per largest class"
    lane = jnp.arange(LANES)
    seg = (lane[:, None] // ATTN_HEAD_DIM == lane[None, :] // ATTN_HEAD_DIM).astype(BF16)
    n_steps = bsz * n_pairs

    def nxt(b, p):
        s = jnp.minimum(b * n_pairs + p + 1, n_steps - 1)
        return s // n_pairs, s % n_pairs

    once = dict(pipeline_mode=pl.Buffered(1))
    first = lambda off: pl.BlockSpec((s_len, LANES), lambda b, p, off=off: (0, off), **once)
    nblk = lambda off: pl.BlockSpec((s_len, LANES), lambda b, p, off=off: (nxt(b, p)[0], off + nxt(b, p)[1]))
    tab0 = pl.BlockSpec((s_len, LANES), lambda b, p: (0, 0), **once)
    tabn = pl.BlockSpec((s_len, LANES), lambda b, p: (nxt(b, p)[0], 0))
    row = pl.BlockSpec((1, LANES), lambda b, p: (0, 0))
    sq = pl.BlockSpec((LANES, LANES), lambda b, p: (0, 0))
    w_specs = [pl.BlockSpec((w.shape[0] // n_steps, w.shape[1]), lambda b, p: (b * n_pairs + p, 0)) for w in weights]
    return pl.pallas_call(
        functools.partial(_attn_kernel, len(weights)),
        grid=(bsz, n_pairs),
        in_specs=[first(0), first(n_pairs), first(2 * n_pairs), tab0, tab0,
                  nblk(0), nblk(n_pairs), nblk(2 * n_pairs), tabn, tabn,
                  row, row, sq, sq] + w_specs,
        out_specs=[pl.BlockSpec((s_len, LANES), lambda b, p: (b, p))] + w_specs,
        out_shape=[jax.ShapeDtypeStruct((bsz * s_len, n_pairs * LANES), BF16)]
                  + [jax.ShapeDtypeStruct(w.shape, BF16) for w in weights],
        scratch_shapes=[[pltpu.VMEM((s_len, LANES), F32)] * 6] * 2
                       + [pltpu.VMEM((len(DILATIONS), s_len, LANES), F32)] * 3
                       + [pltpu.VMEM((2 * ATTN_GROUP, 2 * ATTN_BLOCK, 2 * ATTN_BLOCK), F32),
                          pltpu.VMEM((2 * ATTN_GROUP, 2 * ATTN_BLOCK, 2 * ATTN_BLOCK), BF16)],
        compiler_params=_cparams(("arbitrary", "arbitrary")),
        name="dilated_attn",
    )(qkv, qkv, qkv, cos_t, sin_t, qkv, qkv, qkv, cos_t, sin_t, gq2, gk2, seg, _rotate_half_matrix(), *weights)


def _split3(x):
    hi = x.astype(BF16)
    r1 = x - hi.astype(F32)
    mid = r1.astype(BF16)
    lo = (r1 - mid.astype(F32)).astype(BF16)
    return hi, mid, lo


def _silu(x):
    h = 0.5 * x
    return h * jnp.tanh(h) + h


LOG2E = 1.4426950408889634
SSD_STEP_CHUNKS = 4


def _ssd_kernel(xbc_ref, z_ref, dt_ref, cw_ref, cb_ref, dtb_ref, alog_ref, dskip_ref, g_ref,
                expand_ref, tril_ref, o_ref, xpad, xc, st):
    q = SSD_CHUNK
    n_rows = xbc_ref.shape[0]
    d_ssm = z_ref.shape[1]
    d_conv = xbc_ref.shape[1]
    gw = d_ssm // SSM_GROUPS
    heads_per_group = gw // SSM_HEAD_DIM
    pad = 8

    @pl.when(pl.program_id(1) == 0)
    def _():
        xpad[:, 0:pad, :] = jnp.zeros((d_conv // LANES, pad, LANES), F32)
        st[...] = jnp.zeros_like(st)

    for c0 in range(0, d_conv, LANES):
        cols = slice(c0, c0 + LANES)
        slab = c0 // LANES
        xpad[slab, pad:pad + n_rows, :] = xbc_ref[:, cols].astype(F32)
        acc = cb_ref[:, cols] + cw_ref[CONV_WIDTH - 1:CONV_WIDTH, cols] * xpad[slab, pad:pad + n_rows, :]
        for w in range(CONV_WIDTH - 1):
            off = pad - (CONV_WIDTH - 1) + w
            acc = acc + cw_ref[w:w + 1, cols] * xpad[slab, pl.ds(off, n_rows, stride=1), :]
        xc[:, cols] = _silu(acc)
    xpad[:, 0:pad, :] = xpad[:, n_rows:n_rows + pad, :]

    li = lax.broadcasted_iota(jnp.int32, (q, q), 0)
    si = lax.broadcasted_iota(jnp.int32, (q, q), 1)
    causal = li >= si
    lane_g = lax.broadcasted_iota(jnp.int32, (q, gw), 1)
    tril = tril_ref[...]
    expand = expand_ref[...]

    for r0 in range(0, n_rows, q):
        rows = slice(r0, r0 + q)
        x_dt = dt_ref[rows, :] + dtb_ref[...]
        dt = jnp.maximum(x_dt, 0.0) + jnp.log1p(jnp.exp(-jnp.abs(x_dt)))
        dta = dt * (-LOG2E * jnp.exp(alog_ref[...]))
        acs = sum(jnp.dot(tril, part, preferred_element_type=F32) for part in _split3(dta))
        acs_t = acs.T
        last = acs[q - 1:q, :]
        dt_e = jnp.dot(dt.astype(BF16), expand, preferred_element_type=F32)
        dec_e = jnp.dot(jnp.exp2(acs).astype(BF16), expand, preferred_element_type=F32)
        w_e = jnp.dot((jnp.exp2(last - acs) * dt).astype(BF16), expand, preferred_element_type=F32)

        for g in range(SSM_GROUPS):
            cols = slice(g * gw, (g + 1) * gw)
            b_f = xc[rows, d_ssm + g * SSM_STATE:d_ssm + (g + 1) * SSM_STATE]
            c_b = xc[rows, d_ssm + (SSM_GROUPS + g) * SSM_STATE:d_ssm + (SSM_GROUPS + g + 1) * SSM_STATE].astype(BF16)
            xs = xc[rows, cols]
            cb = lax.dot_general(c_b, b_f.astype(BF16), (((1,), (1,)), ((), ())), preferred_element_type=F32)
            xdt = (xs * dt_e[:, cols]).astype(BF16)
            ws, rs = [], []
            for hh in range(heads_per_group):
                h = g * heads_per_group + hh
                seg = acs[:, h:h + 1] - acs_t[h:h + 1, :]
                l_mat = jnp.exp2(jnp.where(causal, seg, -jnp.inf))
                ws.append((cb * l_mat).astype(BF16))
                in_head = (lane_g >= hh * SSM_HEAD_DIM) & (lane_g < (hh + 1) * SSM_HEAD_DIM)
                rs.append(jnp.where(in_head, xdt, jnp.zeros_like(xdt)))
            y = jnp.dot(jnp.concatenate(ws, axis=1), jnp.concatenate(rs, axis=0), preferred_element_type=F32)
            st_g = st[:, cols]
            y = y + jnp.dot(c_b, st_g.astype(BF16), preferred_element_type=F32) * dec_e[:, cols]
            xw = (xs * w_e[:, cols]).astype(BF16)
            st[:, cols] = st_g * dec_e[q - 1:q, cols] + jnp.dot(b_f.T.astype(BF16), xw, preferred_element_type=F32)
            y = y + dskip_ref[:, cols] * xs
            y = y * _silu(z_ref[rows, cols].astype(F32))
            o_ref[rows, cols] = _rms_rows(y, g_ref[:, cols]).astype(o_ref.dtype)


def ssd_mixer(packed, dt_raw, conv_w, conv_b, dt_bias, a_log, d_skip, g_out, bsz, s_len, d_ssm, xbc_col, z_col):
    n_heads = d_ssm // SSM_HEAD_DIM
    d_conv = d_ssm + 2 * SSM_GROUPS * SSM_STATE
    step_rows = SSD_STEP_CHUNKS * SSD_CHUNK
    nc = s_len // step_rows
    padl = lambda v: jnp.pad(v.astype(F32), (0, LANES - n_heads))[None, :]
    expand = (jnp.arange(LANES)[:, None] == (jnp.arange(d_ssm) // SSM_HEAD_DIM)[None, :]).astype(BF16)
    tril = (jnp.arange(SSD_CHUNK)[:, None] >= jnp.arange(SSD_CHUNK)[None, :]).astype(BF16)
    full = lambda shape: pl.BlockSpec(shape, lambda b, c: (0, 0))
    return pl.pallas_call(
        _ssd_kernel,
        grid=(bsz, nc),
        in_specs=[pl.BlockSpec((step_rows, d_conv), lambda b, c: (b * nc + c, xbc_col // d_conv)),
                  pl.BlockSpec((step_rows, d_ssm), lambda b, c: (b * nc + c, z_col // d_ssm)),
                  pl.BlockSpec((step_rows, LANES), lambda b, c: (b * nc + c, 0)),
                  full((CONV_WIDTH, d_conv)), full((1, d_conv)), full((1, LANES)), full((1, LANES)),
                  full((1, d_ssm)), full((1, d_ssm)), full((LANES, d_ssm)), full((SSD_CHUNK, SSD_CHUNK))],
        out_specs=pl.BlockSpec((step_rows, d_ssm), lambda b, c: (b * nc + c, 0)),
        out_shape=jax.ShapeDtypeStruct((bsz * s_len, d_ssm), BF16),
        scratch_shapes=[pltpu.VMEM((d_conv // LANES, step_rows + 8, LANES), F32),
                        pltpu.VMEM((step_rows, d_conv), F32),
                        pltpu.VMEM((SSM_STATE, d_ssm), F32)],
        compiler_params=_cparams(("parallel", "arbitrary")),
        name="ssd",
    )(packed, packed, dt_raw, conv_w.astype(F32), conv_b.astype(F32)[None, :], padl(dt_bias), padl(a_log),
      jnp.repeat(d_skip.astype(F32), SSM_HEAD_DIM)[None, :], g_out.astype(F32)[None, :], expand, tril)


def _mem_kv_kernel(mem_ref, g_ref, w_ref, gk_ref, kv_ref, h_ref):
    d_cross = N_CROSS_HEADS * CROSS_HEAD_DIM
    part = 256
    for r0 in range(0, mem_ref.shape[0], part):
        rows = slice(r0, r0 + part)
        _norm_chunk_to(mem_ref, g_ref, h_ref, r0, part)
        kv = jnp.dot(h_ref[rows, :], w_ref[...], preferred_element_type=F32)
        for h in range(N_CROSS_HEADS):
            cols = slice(h * CROSS_HEAD_DIM, (h + 1) * CROSS_HEAD_DIM)
            kv_ref[rows, cols] = _rms_rows(kv[:, cols], gk_ref[...]).astype(kv_ref.dtype)
        kv_ref[rows, d_cross:] = kv[:, d_cross:].astype(kv_ref.dtype)


def mem_kv(mem2d, g, w, gk, tm=1024):
    m, d = mem2d.shape
    tm = min(tm, m)
    n = w.shape[1]
    return pl.pallas_call(
        _mem_kv_kernel,
        grid=(m // tm,),
        in_specs=[pl.BlockSpec((tm, d), lambda i: (i, 0)),
                  pl.BlockSpec((1, d), lambda i: (0, 0)),
                  pl.BlockSpec((d, n), lambda i: (0, 0)),
                  pl.BlockSpec((1, CROSS_HEAD_DIM), lambda i: (0, 0))],
        out_specs=pl.BlockSpec((tm, n), lambda i: (i, 0)),
        out_shape=jax.ShapeDtypeStruct((m, n), BF16),
        scratch_shapes=[pltpu.VMEM((tm, d), BF16)],
        compiler_params=_cparams(("parallel",)),
        name="mem_kv",
    )(mem2d, g, w, gk)


def _mix_cross_kernel(attn_ref, ssm_ref, x_ref, ga_ref, w_ref, gc_ref, wq_ref, gq_ref, k_ref, v_ref, wo_ref, o_ref,
                      h_ref, xk_ref, an_ref, q_ref, a_ref, s_scr, p_scr):
    s = pl.program_id(0)
    n_tiles = pl.num_programs(0) - 1
    tm, d_attn = x_ref.shape[0], attn_ref.shape[1]
    rows = CROSS_ROWS
    prev, cur = (s + 1) % 2, s % 2

    @pl.when(s > 0)
    def _():
        q_ref[...] = jnp.dot(h_ref[prev], wq_ref[...], preferred_element_type=F32)

        def scores(item, slot):
            r0, h = item
            cols = slice(h * CROSS_HEAD_DIM, (h + 1) * CROSS_HEAD_DIM)
            qh = (_rms_rows(q_ref[r0:r0 + rows, cols], gq_ref[...]) * (CROSS_HEAD_DIM ** -0.5)).astype(BF16)
            s_scr[slot] = lax.dot_general(qh, k_ref[:, cols], (((1,), (1,)), ((), ())), preferred_element_type=F32)

        def softmax(item, slot):
            sc = s_scr[slot]
            p_scr[slot] = jnp.exp(sc - jnp.max(sc, axis=-1, keepdims=True)).astype(BF16)

        def values(item, slot):
            r0, h = item
            cols = slice(h * CROSS_HEAD_DIM, (h + 1) * CROSS_HEAD_DIM)
            vh = v_ref[:, cols]
            r = jnp.dot(p_scr[slot], jnp.concatenate([vh, jnp.ones_like(vh)], axis=1), preferred_element_type=F32)
            a_ref[r0:r0 + rows, cols] = (r[:, :CROSS_HEAD_DIM] / r[:, CROSS_HEAD_DIM:]).astype(BF16)

        items = [(r0, h) for r0 in range(0, tm, rows) for h in range(N_CROSS_HEADS)]
        _software_pipeline((scores, softmax, values), items, CROSS_GROUP)
        o_ref[...] = xk_ref[prev] + jnp.dot(a_ref[...], wo_ref[...], preferred_element_type=F32)

    @pl.when(s < n_tiles)
    def _():
        n_split = 2
        part = tm // n_split
        for r0 in range(0, tm, part):
            for r in range(r0, r0 + part, 32):
                an_ref[r:r + 32, :] = _rms_rows(attn_ref[r:r + 32, :].astype(F32), ga_ref[...]).astype(BF16)
            rs = slice(r0, r0 + part)
            xk_ref[cur, rs, :] = (x_ref[rs, :] + jnp.dot(an_ref[rs, :], w_ref[:d_attn, :], preferred_element_type=F32)
                                  + jnp.dot(ssm_ref[rs, :], w_ref[d_attn:, :], preferred_element_type=F32))
            _norm_chunk_to(xk_ref.at[cur], gc_ref, h_ref.at[cur], r0, part)


def mix_cross(attn, ssm, x2d, ga, w, gc, wq, gq, kv, wo, s_len, n_mem, tm=512):
    m, d = x2d.shape
    da, ds_ = attn.shape[1], ssm.shape[1]
    dc = wq.shape[1]
    n_tiles = m // tm
    per_batch = s_len // tm
    ahead = lambda s: (jnp.minimum(s, n_tiles - 1), 0)
    behind = lambda s: jnp.maximum(s - 1, 0)
    full = lambda shape, **kw: pl.BlockSpec(shape, lambda s: (0, 0), **kw)
    once = dict(pipeline_mode=pl.Buffered(1))
    return pl.pallas_call(
        _mix_cross_kernel,
        grid=(n_tiles + 1,),
        in_specs=[pl.BlockSpec((tm, da), ahead), pl.BlockSpec((tm, ds_), ahead), pl.BlockSpec((tm, d), ahead),
                  full((1, da)), full((da + ds_, d), **once), full((1, d)), full((d, dc), **once),
                  full((1, CROSS_HEAD_DIM)),
                  pl.BlockSpec((n_mem, dc), lambda s: (behind(s) // per_batch, 0)),
                  pl.BlockSpec((n_mem, dc), lambda s: (behind(s) // per_batch, 1)),
                  full((dc, d), **once)],
        out_specs=pl.BlockSpec((tm, d), lambda s: (behind(s), 0)),
        out_shape=jax.ShapeDtypeStruct((m, d), F32),
        scratch_shapes=[pltpu.VMEM((2, tm, d), BF16), pltpu.VMEM((2, tm, d), F32), pltpu.VMEM((tm, da), BF16),
                        pltpu.VMEM((tm, dc), F32), pltpu.VMEM((tm, dc), BF16),
                        pltpu.VMEM((2 * CROSS_GROUP, CROSS_ROWS, n_mem), F32),
                        pltpu.VMEM((2 * CROSS_GROUP, CROSS_ROWS, n_mem), BF16)],
        compiler_params=_cparams(("arbitrary",)),
        name="mix_cross",
    )(attn, ssm, x2d, ga, w, gc, wq, gq, kv, kv, wo)


def _mlp_kernel(x_ref, g_ref, wu_ref, wd_ref, o_ref, h_ref):
    tm = x_ref.shape[0]

    def expand(rows):
        u = jnp.dot(h_ref[rows, :], wu_ref[...], preferred_element_type=F32)
        u = jnp.square(jnp.maximum(u, 0.0)).astype(BF16)
        return jnp.dot(u, wd_ref[...], preferred_element_type=F32)

    @pl.when(pl.program_id(1) == 0)
    def _():
        part = tm // 2
        for r0 in range(0, tm, part):
            _norm_chunk_to(x_ref, g_ref, h_ref, r0, part)
            rows = slice(r0, r0 + part)
            o_ref[rows, :] = x_ref[rows, :] + expand(rows)

    @pl.when(pl.program_id(1) > 0)
    def _():
        o_ref[...] += expand(slice(None))


def mlp(x2d, g, wu, wd, tm=512, tf=2048):
    m, d = x2d.shape
    f = wu.shape[1]
    return pl.pallas_call(
        _mlp_kernel,
        grid=(m // tm, f // tf),
        in_specs=[pl.BlockSpec((tm, d), lambda i, j: (i, 0)),
                  pl.BlockSpec((1, d), lambda i, j: (0, 0)),
                  pl.BlockSpec((d, tf), lambda i, j: (0, j)),
                  pl.BlockSpec((tf, d), lambda i, j: (j, 0))],
        out_specs=pl.BlockSpec((tm, d), lambda i, j: (i, 0)),
        out_shape=jax.ShapeDtypeStruct((m, d), F32),
        scratch_shapes=[pltpu.VMEM((tm, d), BF16)],
        compiler_params=_cparams(("parallel", "arbitrary")),
        name="mlp",
    )(x2d, g, wu, wd)


def _layer(x2d, mem2d, pos_col, bsz, s_len, n_mem, g_mix, w_in, g_q, g_k, g_attn_out, conv_w, conv_b, dt_bias,
           a_log, d_skip, g_ssm_out, w_out, g_cross, g_mem, w_cq, w_ckv, g_cq, g_ck, w_co, g_mlp, w_up, w_down):
    d_model = x2d.shape[1]
    d_attn = d_model // 2
    d_ssm = d_model // 2
    n_pairs = d_attn // LANES
    n_ssm_heads = d_ssm // SSM_HEAD_DIM
    d_conv = d_ssm + 2 * SSM_GROUPS * SSM_STATE
    d_packed = 3 * d_attn + d_ssm + d_conv
    row = lambda v: v.astype(F32)[None, :]

    w_in_t = w_in.T
    w_dt_t = jnp.pad(w_in_t[d_packed:], ((0, LANES - n_ssm_heads), (0, 0)))
    packed, dt_raw = in_proj(x2d, row(g_mix), w_in_t, w_dt_t, d_packed)

    tabs = rope_tables(pos_col)
    attn, w_up_bf, w_down_bf, w_out_bf, w_ckv_bf, w_cq_bf, w_co_bf = dilated_attention(
        packed, tabs, row(jnp.tile(g_q, 2)) * (ATTN_HEAD_DIM ** -0.5 * LOG2E), row(jnp.tile(g_k, 2)),
        (w_up, w_down, w_out, w_ckv, w_cq, w_co), bsz, s_len, n_pairs)
    ssm = ssd_mixer(packed, dt_raw, conv_w, conv_b, dt_bias, a_log, d_skip, g_ssm_out, bsz, s_len, d_ssm,
                    xbc_col=3 * d_attn + d_ssm, z_col=3 * d_attn)
    kv = mem_kv(mem2d, row(g_mem), w_ckv_bf, row(g_ck))
    x2d = mix_cross(attn, ssm, x2d, row(g_attn_out), w_out_bf, row(g_cross), w_cq_bf, row(g_cq), kv, w_co_bf,
                    s_len, n_mem)

    return mlp(x2d, row(g_mlp), w_up_bf, w_down_bf)


def kernel(x, mem, positions, g_mix, w_in, g_q, g_k, g_attn_out, conv_w, conv_b, dt_bias, a_log, d_skip, g_ssm_out,
           w_out, g_cross, g_mem, w_cq, w_ckv, g_cq, g_ck, w_co, g_mlp, w_up, w_down):
    bsz, s_len, d_model = x.shape
    n_mem = mem.shape[1]
    x2d = x.reshape(bsz * s_len, d_model)
    mem2d = mem.reshape(bsz * n_mem, d_model)
    pos_col = positions.reshape(bsz * s_len, 1)
    for i in range(g_mix.shape[0]):
        x2d = _layer(x2d, mem2d, pos_col, bsz, s_len, n_mem, g_mix[i], w_in[i], g_q[i], g_k[i], g_attn_out[i],
                     conv_w[i], conv_b[i], dt_bias[i], a_log[i], d_skip[i], g_ssm_out[i], w_out[i], g_cross[i],
                     g_mem[i], w_cq[i], w_ckv[i], g_cq[i], g_ck[i], w_co[i], g_mlp[i], w_up[i], w_down[i])
    return x2d.reshape(bsz, s_len, d_model)
```

```python
import functools
import math

import jax
import jax.numpy as jnp
from jax import lax
from jax.experimental import pallas as pl
from jax.experimental.pallas import tpu as pltpu

F32 = jnp.float32
BF16 = jnp.bfloat16
EPS = 1e-6

LANES = 128
ATTN_HEAD_DIM = 64
ROT_DIM = ATTN_HEAD_DIM // 4
ROPE_THETA = 500000.0
ATTN_BLOCK = 128
DILATIONS = (1, 4, 16)
SSM_HEAD_DIM = 64
SSM_GROUPS = 4
SSM_STATE = 128
CONV_WIDTH = 4
SSD_CHUNK = 128
N_CROSS_HEADS = 4
CROSS_HEAD_DIM = 128
CROSS_ROWS = 256
CROSS_GROUP = 2
VMEM_LIMIT = 58 * 1024 * 1024


def _cparams(sem):
    return pltpu.CompilerParams(dimension_semantics=sem, vmem_limit_bytes=VMEM_LIMIT)


def _rms_rows(x, g):
    ms = jnp.mean(x * x, axis=-1, keepdims=True)
    return x * lax.rsqrt(ms + EPS) * g


def _norm_rows_to(x_ref, g_ref, h_ref, rows=16):
    def body(c, carry):
        r0 = pl.multiple_of(c * rows, rows)
        h_ref[pl.ds(r0, rows), :] = _rms_rows(x_ref[pl.ds(r0, rows), :], g_ref[...]).astype(h_ref.dtype)
        return carry
    lax.fori_loop(0, x_ref.shape[0] // rows, body, 0, unroll=4)


def _norm_chunk_to(x_ref, g_ref, h_ref, start, n_rows, rows=16):
    for r in range(0, n_rows, rows):
        rs = pl.ds(start + r, rows)
        h_ref[rs, :] = _rms_rows(x_ref[rs, :], g_ref[...]).astype(h_ref.dtype)


def _in_proj_kernel(x0_ref, xn_ref, g_ref, w_ref, wdt_ref, o_ref, dt_ref, h_ref):
    i, j = pl.program_id(0), pl.program_id(1)
    cur, nxt = i % 2, (i + 1) % 2

    @pl.when((i == 0) & (j == 0))
    def _():
        _norm_rows_to(x0_ref, g_ref, h_ref.at[0])

    @pl.when(j == 0)
    def _():
        dt_ref[...] = lax.dot_general(h_ref[cur], wdt_ref[...].astype(BF16), (((1,), (1,)), ((), ())),
                                      preferred_element_type=F32)
    o_ref[...] = lax.dot_general(h_ref[cur], w_ref[...].astype(BF16), (((1,), (1,)), ((), ())),
                                 preferred_element_type=F32).astype(o_ref.dtype)
    chunk = xn_ref.shape[0]
    start = pl.multiple_of(jnp.minimum(j, IN_PROJ_NORM_STEPS - 1) * chunk, chunk)
    for r in range(0, chunk, 16):
        h_ref[nxt, pl.ds(start + r, 16), :] = _rms_rows(xn_ref[r:r + 16, :], g_ref[...]).astype(BF16)


IN_PROJ_NORM_STEPS = 4


def in_proj(x2d, g, w_t, w_dt, n, tm=1024, tn=1024):
    m, d = x2d.shape
    n_i = m // tm
    assert n // tn >= IN_PROJ_NORM_STEPS
    return pl.pallas_call(
        _in_proj_kernel,
        grid=(n_i, n // tn),
        in_specs=[pl.BlockSpec((tm, d), lambda i, j: (0, 0), pipeline_mode=pl.Buffered(1)),
                  pl.BlockSpec((tm // IN_PROJ_NORM_STEPS, d),
                               lambda i, j: (jnp.minimum(i + 1, n_i - 1) * IN_PROJ_NORM_STEPS
                                             + jnp.minimum(j, IN_PROJ_NORM_STEPS - 1), 0)),
                  pl.BlockSpec((1, d), lambda i, j: (0, 0)),
                  pl.BlockSpec((tn, d), lambda i, j: (j, 0)),
                  pl.BlockSpec((LANES, d), lambda i, j: (0, 0), pipeline_mode=pl.Buffered(1))],
        out_specs=[pl.BlockSpec((tm, tn), lambda i, j: (i, j)),
                   pl.BlockSpec((tm, LANES), lambda i, j: (i, 0))],
        out_shape=[jax.ShapeDtypeStruct((m, n), BF16), jax.ShapeDtypeStruct((m, LANES), F32)],
        scratch_shapes=[pltpu.VMEM((2, tm, d), BF16)],
        compiler_params=_cparams(("arbitrary", "arbitrary")),
        name="in_proj",
    )(x2d, x2d, g, w_t, w_dt)


def _rope_kernel(pos_ref, expo_ref, cmask_ref, cos_ref, sin_ref):
    inv_freq = jnp.power(jnp.float32(ROPE_THETA), expo_ref[...])
    ang = pos_ref[...].astype(F32) * inv_freq
    cm = cmask_ref[...]
    cos_ref[...] = jnp.cos(ang) * cm + (1.0 - cm)
    sin_ref[...] = jnp.sin(ang) * cm


def rope_tables(pos_col, tm=2048):
    m = pos_col.shape[0]
    half = ROT_DIM // 2
    d = jnp.arange(LANES) % ATTN_HEAD_DIM
    expo = (-2.0 * (d % half).astype(F32) / ROT_DIM)[None, :]
    cmask = (d < ROT_DIM).astype(F32)[None, :]
    row = pl.BlockSpec((1, LANES), lambda i: (0, 0))
    tab = pl.BlockSpec((tm, LANES), lambda i: (i, 0))
    return pl.pallas_call(
        _rope_kernel,
        grid=(m // tm,),
        in_specs=[pl.BlockSpec((tm, 1), lambda i: (i, 0)), row, row],
        out_specs=[tab, tab],
        out_shape=[jax.ShapeDtypeStruct((m, LANES), F32)] * 2,
        compiler_params=_cparams(("parallel",)),
        name="rope_tab",
    )(pos_col, expo, cmask)


def _rotate_half_matrix():
    half = ROT_DIM // 2
    src = jnp.arange(LANES)[:, None]
    dst = jnp.arange(LANES)[None, :]
    d = dst % ATTN_HEAD_DIM
    first = (d < half) & (src == dst + half)
    second = (d >= half) & (d < ROT_DIM) & (src == dst - half)
    return (second.astype(F32) - first.astype(F32)).astype(BF16)


def _rows(start, size, dil):
    return pl.ds(start, size) if dil == 1 else pl.ds(start, size, stride=dil)


ATTN_GROUP = 2


def _software_pipeline(stages, items, group, extras=()):
    groups = [items[i:i + group] for i in range(0, len(items), group)]
    n_trips = len(groups) + len(stages) - 1
    for t in range(n_trips):
        for lag, stage in enumerate(stages):
            if 0 <= t - lag < len(groups):
                for g, item in enumerate(groups[t - lag]):
                    stage(item, ((t - lag) % 2) * group + g)
        for k, extra in enumerate(extras):
            if k * n_trips // len(extras) == t:
                extra()


def _attn_kernel(n_cast, q0_ref, k0_ref, v0_ref, cos0_ref, sin0_ref, qn_ref, kn_ref, vn_ref, cosn_ref, sinn_ref,
                 gq_ref, gk_ref, seg_ref, rot_ref, *refs):
    cast_in, o_ref, cast_out = refs[:n_cast], refs[n_cast], refs[n_cast + 1:2 * n_cast + 1]
    set_a, set_b, ob, mb, lb, s_scr, p_scr = refs[2 * n_cast + 1:]
    s_len = qn_ref.shape[0]
    blk = ATTN_BLOCK
    step = pl.program_id(0) * pl.num_programs(1) + pl.program_id(1)
    lane = lax.broadcasted_iota(jnp.int32, (blk, LANES), 1)
    head0 = lane < ATTN_HEAD_DIM
    qi = lax.broadcasted_iota(jnp.int32, (2 * blk, 2 * blk), 0) % blk
    kj = lax.broadcasted_iota(jnp.int32, (2 * blk, 2 * blk), 1)
    band_mask = (kj >= qi) & (kj <= qi + blk)
    first_mask = (lax.broadcasted_iota(jnp.int32, (2 * blk, blk), 1)
                  <= lax.broadcasted_iota(jnp.int32, (2 * blk, blk), 0) % blk)

    for w_ref, w_out in zip(cast_in, cast_out):
        w_out[...] = w_ref[...].astype(BF16)

    prep_rows = 256

    def prep(srcs, dst, r0):
        q_ref, k_ref, v_ref, cos_ref, sin_ref = srcs
        qf, kf, vf = dst[:3]
        rows = slice(r0, r0 + prep_rows)
        cs, sn = cos_ref[rows, :], sin_ref[rows, :]

        def norm_rope(x_ref, g_ref):
            x = x_ref[rows, :].astype(F32)
            ss = jnp.dot((x * x).astype(BF16), seg_ref[...], preferred_element_type=F32)
            y = x * lax.rsqrt(ss * (1.0 / ATTN_HEAD_DIM) + EPS) * g_ref[...]
            return y * cs + jnp.dot(y.astype(BF16), rot_ref[...], preferred_element_type=F32) * sn

        qf[rows, :] = norm_rope(q_ref, gq_ref)
        kf[rows, :] = norm_rope(k_ref, gk_ref)
        vf[rows, :] = v_ref[rows, :].astype(F32)

    dmid = DILATIONS[1]
    run = s_len // dmid

    def deinterleave(bufs, r):
        for src, dst in zip(bufs[:3], bufs[3:]):
            dst[r * run:(r + 1) * run, :] = src[pl.ds(r, run, stride=dmid), :]

    def prepare(srcs, bufs):
        return ([functools.partial(prep, srcs, bufs, r0) for r0 in range(0, s_len, prep_rows)]
                + [functools.partial(deinterleave, bufs, r) for r in range(dmid)])

    @pl.when(step == 0)
    def _():
        for piece in prepare((q0_ref, k0_ref, v0_ref, cos0_ref, sin0_ref), set_a):
            piece()

    def block_descs(bufs):
        natural, deint = bufs[:3], bufs[3:]
        descs = [(deint, 1, 1, r * run, r * run, blk, first_mask) for r in range(dmid)]
        descs += [(deint, 1, 1, r * run + n * blk, r * run + (n - 1) * blk, 2 * blk, band_mask)
                  for n in range(1, run // blk) for r in range(dmid)]
        descs += [(deint, 2, dmid, r_hi * run + r_lo, r_hi * run + r_lo, blk, first_mask)
                  for r_hi in range(dmid) for r_lo in range(dmid)]
        descs += [(natural, 0, 1, 0, 0, blk, first_mask)]
        descs += [(natural, 0, 1, n * blk, (n - 1) * blk, 2 * blk, band_mask) for n in range(1, s_len // blk)]
        return descs

    def scores(desc, slot):
        (q_src, k_src, _), _, dil, q_start, k_start, n_keys, _ = desc
        qt = q_src[_rows(q_start, blk, dil), :]
        zero = jnp.zeros_like(qt)
        q2 = jnp.concatenate([jnp.where(head0, qt, zero), jnp.where(head0, zero, qt)], axis=0).astype(BF16)
        kb = k_src[_rows(k_start, n_keys, dil), :].astype(BF16)
        s_scr[slot, :, :n_keys] = lax.dot_general(q2, kb, (((1,), (1,)), ((), ())), preferred_element_type=F32)

    def softmax(desc, slot):
        _, br, dil, q_start, _, n_keys, mask = desc
        s = jnp.where(mask, s_scr[slot, :, :n_keys], -jnp.inf)
        m = jnp.max(s, axis=-1, keepdims=True)
        p_scr[slot, :, :n_keys] = jnp.exp2(s - m).astype(BF16)
        mb[br, _rows(q_start, blk, dil), :] = jnp.where(head0, m[:blk], m[blk:])

    def values(desc, slot):
        (_, _, v_src), br, dil, q_start, k_start, n_keys, _ = desc
        vb = v_src[_rows(k_start, n_keys, dil), :].astype(BF16)
        v1 = jnp.concatenate([vb, jnp.ones_like(vb)], axis=1)
        r = jnp.dot(p_scr[slot, :, :n_keys], v1, preferred_element_type=F32)
        out_rows = _rows(q_start, blk, dil)
        ob[br, out_rows, :] = jnp.where(head0, r[:blk, :LANES], r[blk:, :LANES])
        lb[br, out_rows, :] = jnp.where(head0, r[:blk, LANES:], r[blk:, LANES:])
        if br == 0:
            merge(q_start)

    def merge(n0):
        sub = blk // dmid
        for r in range(dmid):
            nat_rows = pl.ds(n0 + r, sub, stride=dmid)
            d_rows = slice(r * run + n0 // dmid, r * run + n0 // dmid + sub)
            rows = (nat_rows, d_rows, d_rows)
            ms = [mb[g, rows[g], :] for g in range(len(DILATIONS))]
            m = functools.reduce(jnp.maximum, ms)
            ws = [jnp.exp2(mg - m) for mg in ms]
            num = functools.reduce(jnp.add, [w * ob[g, rows[g], :] for g, w in enumerate(ws)])
            den = functools.reduce(jnp.add, [w * lb[g, rows[g], :] for g, w in enumerate(ws)])
            ob[0, nat_rows, :] = num / den
        o_ref[n0:n0 + blk, :] = ob[0, n0:n0 + blk, :].astype(o_ref.dtype)

    for parity, (cur_set, nxt_set) in enumerate(((set_a, set_b), (set_b, set_a))):
        @pl.when(step % 2 == parity)
        def _(cur_set=cur_set, nxt_set=nxt_set):
            _software_pipeline((scores, softmax, values), block_descs(cur_set), ATTN_GROUP,
                               extras=prepare((qn_ref, kn_ref, vn_ref, cosn_ref, sinn_ref), nxt_set))


def dilated_attention(qkv, tabs, gq2, gk2, weights, bsz, s_len, n_pairs):
    cos_t, sin_t = tabs
    d1, d2, d3 = DILATIONS
    assert d1 == 1 and d3 == d2 * d2 and s_len == d3 * ATTN_BLOCK, "layout assumes dilations (1, d, d*d), one block per largest class"
    lane = jnp.arange(LANES)
    seg = (lane[:, None] // ATTN_HEAD_DIM == lane[None, :] // ATTN_HEAD_DIM).astype(BF16)
    n_steps = bsz * n_pairs

    def nxt(b, p):
        s = jnp.minimum(b * n_pairs + p + 1, n_steps - 1)
        return s // n_pairs, s % n_pairs

    once = dict(pipeline_mode=pl.Buffered(1))
    first = lambda off: pl.BlockSpec((s_len, LANES), lambda b, p, off=off: (0, off), **once)
    nblk = lambda off: pl.BlockSpec((s_len, LANES), lambda b, p, off=off: (nxt(b, p)[0], off + nxt(b, p)[1]))
    tab0 = pl.BlockSpec((s_len, LANES), lambda b, p: (0, 0), **once)
    tabn = pl.BlockSpec((s_len, LANES), lambda b, p: (nxt(b, p)[0], 0))
    row = pl.BlockSpec((1, LANES), lambda b, p: (0, 0))
    sq = pl.BlockSpec((LANES, LANES), lambda b, p: (0, 0))
    w_specs = [pl.BlockSpec((w.shape[0] // n_steps, w.shape[1]), lambda b, p: (b * n_pairs + p, 0)) for w in weights]
    return pl.pallas_call(
        functools.partial(_attn_kernel, len(weights)),
        grid=(bsz, n_pairs),
        in_specs=[first(0), first(n_pairs), first(2 * n_pairs), tab0, tab0,
                  nblk(0), nblk(n_pairs), nblk(2 * n_pairs), tabn, tabn,
                  row, row, sq, sq] + w_specs,
        out_specs=[pl.BlockSpec((s_len, LANES), lambda b, p: (b, p))] + w_specs,
        out_shape=[jax.ShapeDtypeStruct((bsz * s_len, n_pairs * LANES), BF16)]
                  + [jax.ShapeDtypeStruct(w.shape, BF16) for w in weights],
        scratch_shapes=[[pltpu.VMEM((s_len, LANES), F32)] * 6] * 2
                       + [pltpu.VMEM((len(DILATIONS), s_len, LANES), F32)] * 3
                       + [pltpu.VMEM((2 * ATTN_GROUP, 2 * ATTN_BLOCK, 2 * ATTN_BLOCK), F32),
                          pltpu.VMEM((2 * ATTN_GROUP, 2 * ATTN_BLOCK, 2 * ATTN_BLOCK), BF16)],
        compiler_params=_cparams(("arbitrary", "arbitrary")),
        name="dilated_attn",
    )(qkv, qkv, qkv, cos_t, sin_t, qkv, qkv, qkv, cos_t, sin_t, gq2, gk2, seg, _rotate_half_matrix(), *weights)


def _split3(x):
    hi = x.astype(BF16)
    r1 = x - hi.astype(F32)
    mid = r1.astype(BF16)
    lo = (r1 - mid.astype(F32)).astype(BF16)
    return hi, mid, lo


def _silu(x):
    h = 0.5 * x
    return h * jnp.tanh(h) + h


LOG2E = 1.4426950408889634
SSD_STEP_CHUNKS = 4


def _ssd_kernel(xbc_ref, z_ref, dt_ref, cw_ref, cb_ref, dtb_ref, alog_ref, dskip_ref, g_ref,
                expand_ref, tril_ref, o_ref, xpad, xc, st):
    q = SSD_CHUNK
    n_rows = xbc_ref.shape[0]
    d_ssm = z_ref.shape[1]
    d_conv = xbc_ref.shape[1]
    gw = d_ssm // SSM_GROUPS
    heads_per_group = gw // SSM_HEAD_DIM
    pad = 8

    @pl.when(pl.program_id(1) == 0)
    def _():
        xpad[:, 0:pad, :] = jnp.zeros((d_conv // LANES, pad, LANES), F32)
        st[...] = jnp.zeros_like(st)

    for c0 in range(0, d_conv, LANES):
        cols = slice(c0, c0 + LANES)
        slab = c0 // LANES
        xpad[slab, pad:pad + n_rows, :] = xbc_ref[:, cols].astype(F32)
        acc = cb_ref[:, cols] + cw_ref[CONV_WIDTH - 1:CONV_WIDTH, cols] * xpad[slab, pad:pad + n_rows, :]
        for w in range(CONV_WIDTH - 1):
            off = pad - (CONV_WIDTH - 1) + w
            acc = acc + cw_ref[w:w + 1, cols] * xpad[slab, pl.ds(off, n_rows, stride=1), :]
        xc[:, cols] = _silu(acc)
    xpad[:, 0:pad, :] = xpad[:, n_rows:n_rows + pad, :]

    li = lax.broadcasted_iota(jnp.int32, (q, q), 0)
    si = lax.broadcasted_iota(jnp.int32, (q, q), 1)
    causal = li >= si
    lane_g = lax.broadcasted_iota(jnp.int32, (q, gw), 1)
    tril = tril_ref[...]
    expand = expand_ref[...]

    for r0 in range(0, n_rows, q):
        rows = slice(r0, r0 + q)
        x_dt = dt_ref[rows, :] + dtb_ref[...]
        dt = jnp.maximum(x_dt, 0.0) + jnp.log1p(jnp.exp(-jnp.abs(x_dt)))
        dta = dt * (-LOG2E * jnp.exp(alog_ref[...]))
        acs = sum(jnp.dot(tril, part, preferred_element_type=F32) for part in _split3(dta))
        acs_t = acs.T
        last = acs[q - 1:q, :]
        dt_e = jnp.dot(dt.astype(BF16), expand, preferred_element_type=F32)
        dec_e = jnp.dot(jnp.exp2(acs).astype(BF16), expand, preferred_element_type=F32)
        w_e = jnp.dot((jnp.exp2(last - acs) * dt).astype(BF16), expand, preferred_element_type=F32)

        for g in range(SSM_GROUPS):
            cols = slice(g * gw, (g + 1) * gw)
            b_f = xc[rows, d_ssm + g * SSM_STATE:d_ssm + (g + 1) * SSM_STATE]
            c_b = xc[rows, d_ssm + (SSM_GROUPS + g) * SSM_STATE:d_ssm + (SSM_GROUPS + g + 1) * SSM_STATE].astype(BF16)
            xs = xc[rows, cols]
            cb = lax.dot_general(c_b, b_f.astype(BF16), (((1,), (1,)), ((), ())), preferred_element_type=F32)
            xdt = (xs * dt_e[:, cols]).astype(BF16)
            ws, rs = [], []
            for hh in range(heads_per_group):
                h = g * heads_per_group + hh
                seg = acs[:, h:h + 1] - acs_t[h:h + 1, :]
                l_mat = jnp.exp2(jnp.where(causal, seg, -jnp.inf))
                ws.append((cb * l_mat).astype(BF16))
                in_head = (lane_g >= hh * SSM_HEAD_DIM) & (lane_g < (hh + 1) * SSM_HEAD_DIM)
                rs.append(jnp.where(in_head, xdt, jnp.zeros_like(xdt)))
            y = jnp.dot(jnp.concatenate(ws, axis=1), jnp.concatenate(rs, axis=0), preferred_element_type=F32)
            st_g = st[:, cols]
            y = y + jnp.dot(c_b, st_g.astype(BF16), preferred_element_type=F32) * dec_e[:, cols]
            xw = (xs * w_e[:, cols]).astype(BF16)
            st[:, cols] = st_g * dec_e[q - 1:q, cols] + jnp.dot(b_f.T.astype(BF16), xw, preferred_element_type=F32)
            y = y + dskip_ref[:, cols] * xs
            y = y * _silu(z_ref[rows, cols].astype(F32))
            o_ref[rows, cols] = _rms_rows(y, g_ref[:, cols]).astype(o_ref.dtype)


def ssd_mixer(packed, dt_raw, conv_w, conv_b, dt_bias, a_log, d_skip, g_out, bsz, s_len, d_ssm, xbc_col, z_col):
    n_heads = d_ssm // SSM_HEAD_DIM
    d_conv = d_ssm + 2 * SSM_GROUPS * SSM_STATE
    step_rows = SSD_STEP_CHUNKS * SSD_CHUNK
    nc = s_len // step_rows
    padl = lambda v: jnp.pad(v.astype(F32), (0, LANES - n_heads))[None, :]
    expand = (jnp.arange(LANES)[:, None] == (jnp.arange(d_ssm) // SSM_HEAD_DIM)[None, :]).astype(BF16)
    tril = (jnp.arange(SSD_CHUNK)[:, None] >= jnp.arange(SSD_CHUNK)[None, :]).astype(BF16)
    full = lambda shape: pl.BlockSpec(shape, lambda b, c: (0, 0))
    return pl.pallas_call(
        _ssd_kernel,
        grid=(bsz, nc),
        in_specs=[pl.BlockSpec((step_rows, d_conv), lambda b, c: (b * nc + c, xbc_col // d_conv)),
                  pl.BlockSpec((step_rows, d_ssm), lambda b, c: (b * nc + c, z_col // d_ssm)),
                  pl.BlockSpec((step_rows, LANES), lambda b, c: (b * nc + c, 0)),
                  full((CONV_WIDTH, d_conv)), full((1, d_conv)), full((1, LANES)), full((1, LANES)),
                  full((1, d_ssm)), full((1, d_ssm)), full((LANES, d_ssm)), full((SSD_CHUNK, SSD_CHUNK))],
        out_specs=pl.BlockSpec((step_rows, d_ssm), lambda b, c: (b * nc + c, 0)),
        out_shape=jax.ShapeDtypeStruct((bsz * s_len, d_ssm), BF16),
        scratch_shapes=[pltpu.VMEM((d_conv // LANES, step_rows + 8, LANES), F32),
                        pltpu.VMEM((step_rows, d_conv), F32),
                        pltpu.VMEM((SSM_STATE, d_ssm), F32)],
        compiler_params=_cparams(("parallel", "arbitrary")),
        name="ssd",
    )(packed, packed, dt_raw, conv_w.astype(F32), conv_b.astype(F32)[None, :], padl(dt_bias), padl(a_log),
      jnp.repeat(d_skip.astype(F32), SSM_HEAD_DIM)[None, :], g_out.astype(F32)[None, :], expand, tril)


def _mem_kv_kernel(mem_ref, g_ref, w_ref, gk_ref, kv_ref, h_ref):
    d_cross = N_CROSS_HEADS * CROSS_HEAD_DIM
    part = 256
    for r0 in range(0, mem_ref.shape[0], part):
        rows = slice(r0, r0 + part)
        _norm_chunk_to(mem_ref, g_ref, h_ref, r0, part)
        kv = jnp.dot(h_ref[rows, :], w_ref[...], preferred_element_type=F32)
        for h in range(N_CROSS_HEADS):
            cols = slice(h * CROSS_HEAD_DIM, (h + 1) * CROSS_HEAD_DIM)
            kv_ref[rows, cols] = _rms_rows(kv[:, cols], gk_ref[...]).astype(kv_ref.dtype)
        kv_ref[rows, d_cross:] = kv[:, d_cross:].astype(kv_ref.dtype)


def mem_kv(mem2d, g, w, gk, tm=1024):
    m, d = mem2d.shape
    tm = min(tm, m)
    n = w.shape[1]
    return pl.pallas_call(
        _mem_kv_kernel,
        grid=(m // tm,),
        in_specs=[pl.BlockSpec((tm, d), lambda i: (i, 0)),
                  pl.BlockSpec((1, d), lambda i: (0, 0)),
                  pl.BlockSpec((d, n), lambda i: (0, 0)),
                  pl.BlockSpec((1, CROSS_HEAD_DIM), lambda i: (0, 0))],
        out_specs=pl.BlockSpec((tm, n), lambda i: (i, 0)),
        out_shape=jax.ShapeDtypeStruct((m, n), BF16),
        scratch_shapes=[pltpu.VMEM((tm, d), BF16)],
        compiler_params=_cparams(("parallel",)),
        name="mem_kv",
    )(mem2d, g, w, gk)


def _mix_cross_kernel(attn_ref, ssm_ref, x_ref, ga_ref, w_ref, gc_ref, wq_ref, gq_ref, k_ref, v_ref, wo_ref, o_ref,
                      h_ref, xk_ref, an_ref, q_ref, a_ref, s_scr, p_scr):
    s = pl.program_id(0)
    n_tiles = pl.num_programs(0) - 1
    tm, d_attn = x_ref.shape[0], attn_ref.shape[1]
    rows = CROSS_ROWS
    prev, cur = (s + 1) % 2, s % 2

    @pl.when(s > 0)
    def _():
        q_ref[...] = jnp.dot(h_ref[prev], wq_ref[...], preferred_element_type=F32)

        def scores(item, slot):
            r0, h = item
            cols = slice(h * CROSS_HEAD_DIM, (h + 1) * CROSS_HEAD_DIM)
            qh = (_rms_rows(q_ref[r0:r0 + rows, cols], gq_ref[...]) * (CROSS_HEAD_DIM ** -0.5)).astype(BF16)
            s_scr[slot] = lax.dot_general(qh, k_ref[:, cols], (((1,), (1,)), ((), ())), preferred_element_type=F32)

        def softmax(item, slot):
            sc = s_scr[slot]
            p_scr[slot] = jnp.exp(sc - jnp.max(sc, axis=-1, keepdims=True)).astype(BF16)

        def values(item, slot):
            r0, h = item
            cols = slice(h * CROSS_HEAD_DIM, (h + 1) * CROSS_HEAD_DIM)
            vh = v_ref[:, cols]
            r = jnp.dot(p_scr[slot], jnp.concatenate([vh, jnp.ones_like(vh)], axis=1), preferred_element_type=F32)
            a_ref[r0:r0 + rows, cols] = (r[:, :CROSS_HEAD_DIM] / r[:, CROSS_HEAD_DIM:]).astype(BF16)

        items = [(r0, h) for r0 in range(0, tm, rows) for h in range(N_CROSS_HEADS)]
        _software_pipeline((scores, softmax, values), items, CROSS_GROUP)
        o_ref[...] = xk_ref[prev] + jnp.dot(a_ref[...], wo_ref[...], preferred_element_type=F32)

    @pl.when(s < n_tiles)
    def _():
        n_split = 2
        part = tm // n_split
        for r0 in range(0, tm, part):
            for r in range(r0, r0 + part, 32):
                an_ref[r:r + 32, :] = _rms_rows(attn_ref[r:r + 32, :].astype(F32), ga_ref[...]).astype(BF16)
            rs = slice(r0, r0 + part)
            xk_ref[cur, rs, :] = (x_ref[rs, :] + jnp.dot(an_ref[rs, :], w_ref[:d_attn, :], preferred_element_type=F32)
                                  + jnp.dot(ssm_ref[rs, :], w_ref[d_attn:, :], preferred_element_type=F32))
            _norm_chunk_to(xk_ref.at[cur], gc_ref, h_ref.at[cur], r0, part)


def mix_cross(attn, ssm, x2d, ga, w, gc, wq, gq, kv, wo, s_len, n_mem, tm=512):
    m, d = x2d.shape
    da, ds_ = attn.shape[1], ssm.shape[1]
    dc = wq.shape[1]
    n_tiles = m // tm
    per_batch = s_len // tm
    ahead = lambda s: (jnp.minimum(s, n_tiles - 1), 0)
    behind = lambda s: jnp.maximum(s - 1, 0)
    full = lambda shape, **kw: pl.BlockSpec(shape, lambda s: (0, 0), **kw)
    once = dict(pipeline_mode=pl.Buffered(1))
    return pl.pallas_call(
        _mix_cross_kernel,
        grid=(n_tiles + 1,),
        in_specs=[pl.BlockSpec((tm, da), ahead), pl.BlockSpec((tm, ds_), ahead), pl.BlockSpec((tm, d), ahead),
                  full((1, da)), full((da + ds_, d), **once), full((1, d)), full((d, dc), **once),
                  full((1, CROSS_HEAD_DIM)),
                  pl.BlockSpec((n_mem, dc), lambda s: (behind(s) // per_batch, 0)),
                  pl.BlockSpec((n_mem, dc), lambda s: (behind(s) // per_batch, 1)),
                  full((dc, d), **once)],
        out_specs=pl.BlockSpec((tm, d), lambda s: (behind(s), 0)),
        out_shape=jax.ShapeDtypeStruct((m, d), F32),
        scratch_shapes=[pltpu.VMEM((2, tm, d), BF16), pltpu.VMEM((2, tm, d), F32), pltpu.VMEM((tm, da), BF16),
                        pltpu.VMEM((tm, dc), F32), pltpu.VMEM((tm, dc), BF16),
                        pltpu.VMEM((2 * CROSS_GROUP, CROSS_ROWS, n_mem), F32),
                        pltpu.VMEM((2 * CROSS_GROUP, CROSS_ROWS, n_mem), BF16)],
        compiler_params=_cparams(("arbitrary",)),
        name="mix_cross",
    )(attn, ssm, x2d, ga, w, gc, wq, gq, kv, kv, wo)


def _mlp_kernel(x_ref, g_ref, wu_ref, wd_ref, o_ref, h_ref):
    tm = x_ref.shape[0]

    def expand(rows):
        u = jnp.dot(h_ref[rows, :], wu_ref[...], preferred_element_type=F32)
        u = jnp.square(jnp.maximum(u, 0.0)).astype(BF16)
        return jnp.dot(u, wd_ref[...], preferred_element_type=F32)

    @pl.when(pl.program_id(1) == 0)
    def _():
        part = tm // 2
        for r0 in range(0, tm, part):
            _norm_chunk_to(x_ref, g_ref, h_ref, r0, part)
            rows = slice(r0, r0 + part)
            o_ref[rows, :] = x_ref[rows, :] + expand(rows)

    @pl.when(pl.program_id(1) > 0)
    def _():
        o_ref[...] += expand(slice(None))


def mlp(x2d, g, wu, wd, tm=512, tf=2048):
    m, d = x2d.shape
    f = wu.shape[1]
    return pl.pallas_call(
        _mlp_kernel,
        grid=(m // tm, f // tf),
        in_specs=[pl.BlockSpec((tm, d), lambda i, j: (i, 0)),
                  pl.BlockSpec((1, d), lambda i, j: (0, 0)),
                  pl.BlockSpec((d, tf), lambda i, j: (0, j)),
                  pl.BlockSpec((tf, d), lambda i, j: (j, 0))],
        out_specs=pl.BlockSpec((tm, d), lambda i, j: (i, 0)),
        out_shape=jax.ShapeDtypeStruct((m, d), F32),
        scratch_shapes=[pltpu.VMEM((tm, d), BF16)],
        compiler_params=_cparams(("parallel", "arbitrary")),
        name="mlp",
    )(x2d, g, wu, wd)


def _layer(x2d, mem2d, pos_col, bsz, s_len, n_mem, g_mix, w_in, g_q, g_k, g_attn_out, conv_w, conv_b, dt_bias,
           a_log, d_skip, g_ssm_out, w_out, g_cross, g_mem, w_cq, w_ckv, g_cq, g_ck, w_co, g_mlp, w_up, w_down):
    d_model = x2d.shape[1]
    d_attn = d_model // 2
    d_ssm = d_model // 2
    n_pairs = d_attn // LANES
    n_ssm_heads = d_ssm // SSM_HEAD_DIM
    d_conv = d_ssm + 2 * SSM_GROUPS * SSM_STATE
    d_packed = 3 * d_attn + d_ssm + d_conv
    row = lambda v: v.astype(F32)[None, :]

    w_in_t = w_in.T
    w_dt_t = jnp.pad(w_in_t[d_packed:], ((0, LANES - n_ssm_heads), (0, 0)))
    packed, dt_raw = in_proj(x2d, row(g_mix), w_in_t, w_dt_t, d_packed)

    tabs = rope_tables(pos_col)
    attn, w_up_bf, w_down_bf, w_out_bf = dilated_attention(
        packed, tabs, row(jnp.tile(g_q, 2)) * (ATTN_HEAD_DIM ** -0.5 * LOG2E), row(jnp.tile(g_k, 2)),
        (w_up, w_down, w_out), bsz, s_len, n_pairs)
    ssm = ssd_mixer(packed, dt_raw, conv_w, conv_b, dt_bias, a_log, d_skip, g_ssm_out, bsz, s_len, d_ssm,
                    xbc_col=3 * d_attn + d_ssm, z_col=3 * d_attn)
    kv = mem_kv(mem2d, row(g_mem), w_ckv.astype(BF16), row(g_ck))
    x2d = mix_cross(attn, ssm, x2d, row(g_attn_out), w_out_bf, row(g_cross), w_cq.astype(BF16), row(g_cq), kv,
                    w_co.astype(BF16), s_len, n_mem)

    return mlp(x2d, row(g_mlp), w_up_bf, w_down_bf)


def kernel(x, mem, positions, g_mix, w_in, g_q, g_k, g_attn_out, conv_w, conv_b, dt_bias, a_log, d_skip, g_ssm_out,
           w_out, g_cross, g_mem, w_cq, w_ckv, g_cq, g_ck, w_co, g_mlp, w_up, w_down):
    bsz, s_len, d_model = x.shape
    n_mem = mem.shape[1]
    x2d = x.reshape(bsz * s_len, d_model)
    mem2d = mem.reshape(bsz * n_mem, d_model)
    pos_col = positions.reshape(bsz * s_len, 1)
    for i in range(g_mix.shape[0]):
        x2d = _layer(x2d, mem2d, pos_col, bsz, s_len, n_mem, g_mix[i], w_in[i], g_q[i], g_k[i], g_attn_out[i],
                     conv_w[i], conv_b[i], dt_bias[i], a_log[i], d_skip[i], g_ssm_out[i], w_out[i], g_cross[i],
                     g_mem[i], w_cq[i], w_ckv[i], g_cq[i], g_ck[i], w_co[i], g_mlp[i], w_up[i], w_down[i])
    return x2d.reshape(bsz, s_len, d_model)
```

```python
import functools
import math

import jax
import jax.numpy as jnp
from jax import lax
from jax.experimental import pallas as pl
from jax.experimental.pallas import tpu as pltpu

F32 = jnp.float32
BF16 = jnp.bfloat16
EPS = 1e-6

LANES = 128
ATTN_HEAD_DIM = 64
ROT_DIM = ATTN_HEAD_DIM // 4
ROPE_THETA = 500000.0
ATTN_BLOCK = 128
DILATIONS = (1, 4, 16)
SSM_HEAD_DIM = 64
SSM_GROUPS = 4
SSM_STATE = 128
CONV_WIDTH = 4
SSD_CHUNK = 128
N_CROSS_HEADS = 4
CROSS_HEAD_DIM = 128
CROSS_ROWS = 256
CROSS_GROUP = 2
VMEM_LIMIT = 58 * 1024 * 1024


def _cparams(sem):
    return pltpu.CompilerParams(dimension_semantics=sem, vmem_limit_bytes=VMEM_LIMIT)


def _rms_rows(x, g):
    ms = jnp.mean(x * x, axis=-1, keepdims=True)
    return x * lax.rsqrt(ms + EPS) * g


def _split3(x):
    hi = x.astype(BF16)
    r1 = x - hi.astype(F32)
    mid = r1.astype(BF16)
    lo = (r1 - mid.astype(F32)).astype(BF16)
    return hi, mid, lo


def _norm_rows_to(x_ref, g_ref, h_ref, rows=16):
    def body(c, carry):
        r0 = pl.multiple_of(c * rows, rows)
        h_ref[pl.ds(r0, rows), :] = _rms_rows(x_ref[pl.ds(r0, rows), :], g_ref[...]).astype(h_ref.dtype)
        return carry
    lax.fori_loop(0, x_ref.shape[0] // rows, body, 0, unroll=4)


def _norm_chunk_to(x_ref, g_ref, h_ref, start, n_rows, rows=16):
    for r in range(0, n_rows, rows):
        rs = pl.ds(start + r, rows)
        h_ref[rs, :] = _rms_rows(x_ref[rs, :], g_ref[...]).astype(h_ref.dtype)


def _in_proj_kernel(x0_ref, xn_ref, g_ref, w_ref, wdt_ref, o_ref, dt_ref, h_ref):
    i, j = pl.program_id(0), pl.program_id(1)
    cur, nxt = i % 2, (i + 1) % 2

    @pl.when((i == 0) & (j == 0))
    def _():
        _norm_rows_to(x0_ref, g_ref, h_ref.at[0])

    @pl.when(j == 0)
    def _():
        dt_ref[...] = lax.dot_general(h_ref[cur], wdt_ref[...].astype(BF16), (((1,), (1,)), ((), ())),
                                      preferred_element_type=F32)
    o_ref[...] = lax.dot_general(h_ref[cur], w_ref[...].astype(BF16), (((1,), (1,)), ((), ())),
                                 preferred_element_type=F32).astype(o_ref.dtype)
    chunk = xn_ref.shape[0]
    start = pl.multiple_of(jnp.minimum(j, IN_PROJ_NORM_STEPS - 1) * chunk, chunk)
    for r in range(0, chunk, 16):
        h_ref[nxt, pl.ds(start + r, 16), :] = _rms_rows(xn_ref[r:r + 16, :], g_ref[...]).astype(BF16)


IN_PROJ_NORM_STEPS = 4


def in_proj(x2d, g, w_t, w_dt, n, tm=1024, tn=1024):
    m, d = x2d.shape
    n_i = m // tm
    assert n // tn >= IN_PROJ_NORM_STEPS
    return pl.pallas_call(
        _in_proj_kernel,
        grid=(n_i, n // tn),
        in_specs=[pl.BlockSpec((tm, d), lambda i, j: (0, 0), pipeline_mode=pl.Buffered(1)),
                  pl.BlockSpec((tm // IN_PROJ_NORM_STEPS, d),
                               lambda i, j: (jnp.minimum(i + 1, n_i - 1) * IN_PROJ_NORM_STEPS
                                             + jnp.minimum(j, IN_PROJ_NORM_STEPS - 1), 0)),
                  pl.BlockSpec((1, d), lambda i, j: (0, 0)),
                  pl.BlockSpec((tn, d), lambda i, j: (j, 0)),
                  pl.BlockSpec((LANES, d), lambda i, j: (0, 0), pipeline_mode=pl.Buffered(1))],
        out_specs=[pl.BlockSpec((tm, tn), lambda i, j: (i, j)),
                   pl.BlockSpec((tm, LANES), lambda i, j: (i, 0))],
        out_shape=[jax.ShapeDtypeStruct((m, n), BF16), jax.ShapeDtypeStruct((m, LANES), F32)],
        scratch_shapes=[pltpu.VMEM((2, tm, d), BF16)],
        compiler_params=_cparams(("arbitrary", "arbitrary")),
        name="in_proj",
    )(x2d, x2d, g, w_t, w_dt)


ROPE_PACK = LANES // ROT_DIM


def _rope_kernel(pos_ref, expo_ref, spread_ref, cmask_ref, cos_ref, sin_ref):
    inv_freq = jnp.power(jnp.float32(ROPE_THETA), expo_ref[...])
    ang = pos_ref[...].astype(F32) * inv_freq
    packed = (jnp.cos(ang), jnp.sin(ang))
    fill = (1.0 - cmask_ref[...], jnp.zeros_like(cmask_ref[...]))
    n_rows = pos_ref.shape[0]
    for out_ref, tab, off in zip((cos_ref, sin_ref), packed, fill):
        parts = _split3(tab)
        for k in range(ROPE_PACK):
            rows = sum(jnp.dot(part, spread_ref[k], preferred_element_type=F32) for part in parts)
            out_ref[pl.ds(k, n_rows, stride=ROPE_PACK), :] = rows + off


def rope_tables(positions, tm=2048):
    m = positions.size
    half = ROT_DIM // 2
    lane = jnp.arange(LANES)
    expo = (-2.0 * ((lane % ROT_DIM) % half).astype(F32) / ROT_DIM)[None, :]
    cmask = ((lane % ATTN_HEAD_DIM) < ROT_DIM).astype(F32)[None, :]
    d_dst = lane % ATTN_HEAD_DIM
    spread = ((d_dst[None, None, :] < ROT_DIM)
              & (lane[None, :, None] == ROT_DIM * jnp.arange(ROPE_PACK)[:, None, None] + d_dst[None, None, :])).astype(BF16)
    pos_packed = jnp.repeat(positions.reshape(m // ROPE_PACK, ROPE_PACK), ROT_DIM, axis=1)
    row = pl.BlockSpec((1, LANES), lambda i: (0, 0))
    tab = pl.BlockSpec((tm, LANES), lambda i: (i, 0))
    return pl.pallas_call(
        _rope_kernel,
        grid=(m // tm,),
        in_specs=[pl.BlockSpec((tm // ROPE_PACK, LANES), lambda i: (i, 0)), row,
                  pl.BlockSpec((ROPE_PACK, LANES, LANES), lambda i: (0, 0, 0)), row],
        out_specs=[tab, tab],
        out_shape=[jax.ShapeDtypeStruct((m, LANES), F32)] * 2,
        compiler_params=_cparams(("parallel",)),
        name="rope_tab",
    )(pos_packed, expo, spread, cmask)


def _rotate_half_matrix():
    half = ROT_DIM // 2
    src = jnp.arange(LANES)[:, None]
    dst = jnp.arange(LANES)[None, :]
    d = dst % ATTN_HEAD_DIM
    first = (d < half) & (src == dst + half)
    second = (d >= half) & (d < ROT_DIM) & (src == dst - half)
    return (second.astype(F32) - first.astype(F32)).astype(BF16)


def _rows(start, size, dil):
    return pl.ds(start, size) if dil == 1 else pl.ds(start, size, stride=dil)


ATTN_GROUP = 2


def _software_pipeline(stages, items, group, extras=()):
    groups = [items[i:i + group] for i in range(0, len(items), group)]
    n_trips = len(groups) + len(stages) - 1
    for t in range(n_trips):
        for lag, stage in enumerate(stages):
            if 0 <= t - lag < len(groups):
                for g, item in enumerate(groups[t - lag]):
                    stage(item, ((t - lag) % 2) * group + g)
        for k, extra in enumerate(extras):
            if k * n_trips // len(extras) == t:
                extra()


def _attn_kernel(n_cast, q0_ref, k0_ref, v0_ref, cos0_ref, sin0_ref, qn_ref, kn_ref, vn_ref, cosn_ref, sinn_ref,
                 gq_ref, gk_ref, seg_ref, rot_ref, *refs):
    cast_in, o_ref, cast_out = refs[:n_cast], refs[n_cast], refs[n_cast + 1:2 * n_cast + 1]
    set_a, set_b, ob, mb, lb, s_scr, p_scr = refs[2 * n_cast + 1:]
    s_len = qn_ref.shape[0]
    blk = ATTN_BLOCK
    step = pl.program_id(0) * pl.num_programs(1) + pl.program_id(1)
    lane = lax.broadcasted_iota(jnp.int32, (blk, LANES), 1)
    head0 = lane < ATTN_HEAD_DIM
    qi = lax.broadcasted_iota(jnp.int32, (2 * blk, 2 * blk), 0) % blk
    kj = lax.broadcasted_iota(jnp.int32, (2 * blk, 2 * blk), 1)
    band_mask = (kj >= qi) & (kj <= qi + blk)
    first_mask = (lax.broadcasted_iota(jnp.int32, (2 * blk, blk), 1)
                  <= lax.broadcasted_iota(jnp.int32, (2 * blk, blk), 0) % blk)

    for w_ref, w_out in zip(cast_in, cast_out):
        w_out[...] = w_ref[...].astype(BF16)

    prep_rows = 256

    def prep(srcs, dst, r0):
        q_ref, k_ref, v_ref, cos_ref, sin_ref = srcs
        qf, kf, vf = dst[:3]
        rows = slice(r0, r0 + prep_rows)
        cs, sn = cos_ref[rows, :], sin_ref[rows, :]

        def norm_rope(x_ref, g_ref):
            x = x_ref[rows, :].astype(F32)
            ss = jnp.dot((x * x).astype(BF16), seg_ref[...], preferred_element_type=F32)
            y = x * lax.rsqrt(ss * (1.0 / ATTN_HEAD_DIM) + EPS) * g_ref[...]
            return y * cs + jnp.dot(y.astype(BF16), rot_ref[...], preferred_element_type=F32) * sn

        qf[rows, :] = norm_rope(q_ref, gq_ref)
        kf[rows, :] = norm_rope(k_ref, gk_ref)
        vf[rows, :] = v_ref[rows, :].astype(F32)

    dmid = DILATIONS[1]
    run = s_len // dmid

    def deinterleave(bufs, r):
        for src, dst in zip(bufs[:3], bufs[3:]):
            dst[r * run:(r + 1) * run, :] = src[pl.ds(r, run, stride=dmid), :]

    def prepare(srcs, bufs):
        return ([functools.partial(prep, srcs, bufs, r0) for r0 in range(0, s_len, prep_rows)]
                + [functools.partial(deinterleave, bufs, r) for r in range(dmid)])

    @pl.when(step == 0)
    def _():
        for piece in prepare((q0_ref, k0_ref, v0_ref, cos0_ref, sin0_ref), set_a):
            piece()

    def block_descs(bufs):
        natural, deint = bufs[:3], bufs[3:]
        descs = [(deint, 1, 1, r * run, r * run, blk, first_mask) for r in range(dmid)]
        descs += [(deint, 1, 1, r * run + n * blk, r * run + (n - 1) * blk, 2 * blk, band_mask)
                  for n in range(1, run // blk) for r in range(dmid)]
        descs += [(deint, 2, dmid, r_hi * run + r_lo, r_hi * run + r_lo, blk, first_mask)
                  for r_hi in range(dmid) for r_lo in range(dmid)]
        descs += [(natural, 0, 1, 0, 0, blk, first_mask)]
        descs += [(natural, 0, 1, n * blk, (n - 1) * blk, 2 * blk, band_mask) for n in range(1, s_len // blk)]
        return descs

    def scores(desc, slot):
        (q_src, k_src, _), _, dil, q_start, k_start, n_keys, _ = desc
        qt = q_src[_rows(q_start, blk, dil), :]
        zero = jnp.zeros_like(qt)
        q2 = jnp.concatenate([jnp.where(head0, qt, zero), jnp.where(head0, zero, qt)], axis=0).astype(BF16)
        kb = k_src[_rows(k_start, n_keys, dil), :].astype(BF16)
        s_scr[slot, :, :n_keys] = lax.dot_general(q2, kb, (((1,), (1,)), ((), ())), preferred_element_type=F32)

    def softmax(desc, slot):
        _, br, dil, q_start, _, n_keys, mask = desc
        s = jnp.where(mask, s_scr[slot, :, :n_keys], -jnp.inf)
        m = jnp.max(s, axis=-1, keepdims=True)
        p_scr[slot, :, :n_keys] = jnp.exp2(s - m).astype(BF16)
        mb[br, _rows(q_start, blk, dil), :] = jnp.where(head0, m[:blk], m[blk:])

    def values(desc, slot):
        (_, _, v_src), br, dil, q_start, k_start, n_keys, _ = desc
        vb = v_src[_rows(k_start, n_keys, dil), :].astype(BF16)
        v1 = jnp.concatenate([vb, jnp.ones_like(vb)], axis=1)
        r = jnp.dot(p_scr[slot, :, :n_keys], v1, preferred_element_type=F32)
        out_rows = _rows(q_start, blk, dil)
        ob[br, out_rows, :] = jnp.where(head0, r[:blk, :LANES], r[blk:, :LANES])
        lb[br, out_rows, :] = jnp.where(head0, r[:blk, LANES:], r[blk:, LANES:])
        if br == 0:
            merge(q_start)

    def merge(n0):
        sub = blk // dmid
        for r in range(dmid):
            nat_rows = pl.ds(n0 + r, sub, stride=dmid)
            d_rows = slice(r * run + n0 // dmid, r * run + n0 // dmid + sub)
            rows = (nat_rows, d_rows, d_rows)
            ms = [mb[g, rows[g], :] for g in range(len(DILATIONS))]
            m = functools.reduce(jnp.maximum, ms)
            ws = [jnp.exp2(mg - m) for mg in ms]
            num = functools.reduce(jnp.add, [w * ob[g, rows[g], :] for g, w in enumerate(ws)])
            den = functools.reduce(jnp.add, [w * lb[g, rows[g], :] for g, w in enumerate(ws)])
            ob[0, nat_rows, :] = num / den
        o_ref[n0:n0 + blk, :] = ob[0, n0:n0 + blk, :].astype(o_ref.dtype)

    for parity, (cur_set, nxt_set) in enumerate(((set_a, set_b), (set_b, set_a))):
        @pl.when(step % 2 == parity)
        def _(cur_set=cur_set, nxt_set=nxt_set):
            _software_pipeline((scores, softmax, values), block_descs(cur_set), ATTN_GROUP,
                               extras=prepare((qn_ref, kn_ref, vn_ref, cosn_ref, sinn_ref), nxt_set))


def dilated_attention(qkv, tabs, gq2, gk2, weights, bsz, s_len, n_pairs):
    cos_t, sin_t = tabs
    d1, d2, d3 = DILATIONS
    assert d1 == 1 and d3 == d2 * d2 and s_len == d3 * ATTN_BLOCK, "layout assumes dilations (1, d, d*d), one block per largest class"
    lane = jnp.arange(LANES)
    seg = (lane[:, None] // ATTN_HEAD_DIM == lane[None, :] // ATTN_HEAD_DIM).astype(BF16)
    n_steps = bsz * n_pairs

    def nxt(b, p):
        s = jnp.minimum(b * n_pairs + p + 1, n_steps - 1)
        return s // n_pairs, s % n_pairs

    once = dict(pipeline_mode=pl.Buffered(1))
    first = lambda off: pl.BlockSpec((s_len, LANES), lambda b, p, off=off: (0, off), **once)
    nblk = lambda off: pl.BlockSpec((s_len, LANES), lambda b, p, off=off: (nxt(b, p)[0], off + nxt(b, p)[1]))
    tab0 = pl.BlockSpec((s_len, LANES), lambda b, p: (0, 0), **once)
    tabn = pl.BlockSpec((s_len, LANES), lambda b, p: (nxt(b, p)[0], 0))
    row = pl.BlockSpec((1, LANES), lambda b, p: (0, 0))
    sq = pl.BlockSpec((LANES, LANES), lambda b, p: (0, 0))
    w_specs = [pl.BlockSpec((w.shape[0] // n_steps, w.shape[1]), lambda b, p: (b * n_pairs + p, 0)) for w in weights]
    return pl.pallas_call(
        functools.partial(_attn_kernel, len(weights)),
        grid=(bsz, n_pairs),
        in_specs=[first(0), first(n_pairs), first(2 * n_pairs), tab0, tab0,
                  nblk(0), nblk(n_pairs), nblk(2 * n_pairs), tabn, tabn,
                  row, row, sq, sq] + w_specs,
        out_specs=[pl.BlockSpec((s_len, LANES), lambda b, p: (b, p))] + w_specs,
        out_shape=[jax.ShapeDtypeStruct((bsz * s_len, n_pairs * LANES), BF16)]
                  + [jax.ShapeDtypeStruct(w.shape, BF16) for w in weights],
        scratch_shapes=[[pltpu.VMEM((s_len, LANES), F32)] * 6] * 2
                       + [pltpu.VMEM((len(DILATIONS), s_len, LANES), F32)] * 3
                       + [pltpu.VMEM((2 * ATTN_GROUP, 2 * ATTN_BLOCK, 2 * ATTN_BLOCK), F32),
                          pltpu.VMEM((2 * ATTN_GROUP, 2 * ATTN_BLOCK, 2 * ATTN_BLOCK), BF16)],
        compiler_params=_cparams(("arbitrary", "arbitrary")),
        name="dilated_attn",
    )(qkv, qkv, qkv, cos_t, sin_t, qkv, qkv, qkv, cos_t, sin_t, gq2, gk2, seg, _rotate_half_matrix(), *weights)


def _silu(x):
    h = 0.5 * x
    return h * jnp.tanh(h) + h


LOG2E = 1.4426950408889634
SSD_STEP_CHUNKS = 4


def _ssd_kernel(xbc_ref, z_ref, dt_ref, cw_ref, cb_ref, dtb_ref, alog_ref, dskip_ref, g_ref,
                expand_ref, tril_ref, o_ref, xpad, xc, st):
    q = SSD_CHUNK
    n_rows = xbc_ref.shape[0]
    d_ssm = z_ref.shape[1]
    d_conv = xbc_ref.shape[1]
    gw = d_ssm // SSM_GROUPS
    heads_per_group = gw // SSM_HEAD_DIM
    pad = 8

    @pl.when(pl.program_id(1) == 0)
    def _():
        xpad[:, 0:pad, :] = jnp.zeros((d_conv // LANES, pad, LANES), F32)
        st[...] = jnp.zeros_like(st)

    for c0 in range(0, d_conv, LANES):
        cols = slice(c0, c0 + LANES)
        slab = c0 // LANES
        xpad[slab, pad:pad + n_rows, :] = xbc_ref[:, cols].astype(F32)
        acc = cb_ref[:, cols] + cw_ref[CONV_WIDTH - 1:CONV_WIDTH, cols] * xpad[slab, pad:pad + n_rows, :]
        for w in range(CONV_WIDTH - 1):
            off = pad - (CONV_WIDTH - 1) + w
            acc = acc + cw_ref[w:w + 1, cols] * xpad[slab, pl.ds(off, n_rows, stride=1), :]
        xc[:, cols] = _silu(acc)
    xpad[:, 0:pad, :] = xpad[:, n_rows:n_rows + pad, :]

    li = lax.broadcasted_iota(jnp.int32, (q, q), 0)
    si = lax.broadcasted_iota(jnp.int32, (q, q), 1)
    causal = li >= si
    lane_g = lax.broadcasted_iota(jnp.int32, (q, gw), 1)
    tril = tril_ref[...]
    expand = expand_ref[...]

    for r0 in range(0, n_rows, q):
        rows = slice(r0, r0 + q)
        x_dt = dt_ref[rows, :] + dtb_ref[...]
        dt = jnp.maximum(x_dt, 0.0) + jnp.log1p(jnp.exp(-jnp.abs(x_dt)))
        dta = dt * (-LOG2E * jnp.exp(alog_ref[...]))
        acs = sum(jnp.dot(tril, part, preferred_element_type=F32) for part in _split3(dta))
        acs_t = acs.T
        last = acs[q - 1:q, :]
        dt_e = jnp.dot(dt.astype(BF16), expand, preferred_element_type=F32)
        dec_e = jnp.dot(jnp.exp2(acs).astype(BF16), expand, preferred_element_type=F32)
        w_e = jnp.dot((jnp.exp2(last - acs) * dt).astype(BF16), expand, preferred_element_type=F32)

        for g in range(SSM_GROUPS):
            cols = slice(g * gw, (g + 1) * gw)
            b_f = xc[rows, d_ssm + g * SSM_STATE:d_ssm + (g + 1) * SSM_STATE]
            c_b = xc[rows, d_ssm + (SSM_GROUPS + g) * SSM_STATE:d_ssm + (SSM_GROUPS + g + 1) * SSM_STATE].astype(BF16)
            xs = xc[rows, cols]
            cb = lax.dot_general(c_b, b_f.astype(BF16), (((1,), (1,)), ((), ())), preferred_element_type=F32)
            xdt = (xs * dt_e[:, cols]).astype(BF16)
            ws, rs = [], []
            for hh in range(heads_per_group):
                h = g * heads_per_group + hh
                seg = acs[:, h:h + 1] - acs_t[h:h + 1, :]
                l_mat = jnp.exp2(jnp.where(causal, seg, -jnp.inf))
                ws.append((cb * l_mat).astype(BF16))
                in_head = (lane_g >= hh * SSM_HEAD_DIM) & (lane_g < (hh + 1) * SSM_HEAD_DIM)
                rs.append(jnp.where(in_head, xdt, jnp.zeros_like(xdt)))
            y = jnp.dot(jnp.concatenate(ws, axis=1), jnp.concatenate(rs, axis=0), preferred_element_type=F32)
            st_g = st[:, cols]
            y = y + jnp.dot(c_b, st_g.astype(BF16), preferred_element_type=F32) * dec_e[:, cols]
            xw = (xs * w_e[:, cols]).astype(BF16)
            st[:, cols] = st_g * dec_e[q - 1:q, cols] + jnp.dot(b_f.T.astype(BF16), xw, preferred_element_type=F32)
            y = y + dskip_ref[:, cols] * xs
            y = y * _silu(z_ref[rows, cols].astype(F32))
            o_ref[rows, cols] = _rms_rows(y, g_ref[:, cols]).astype(o_ref.dtype)


def ssd_mixer(packed, dt_raw, conv_w, conv_b, dt_bias, a_log, d_skip, g_out, bsz, s_len, d_ssm, xbc_col, z_col):
    n_heads = d_ssm // SSM_HEAD_DIM
    d_conv = d_ssm + 2 * SSM_GROUPS * SSM_STATE
    step_rows = SSD_STEP_CHUNKS * SSD_CHUNK
    nc = s_len // step_rows
    padl = lambda v: jnp.pad(v.astype(F32), (0, LANES - n_heads))[None, :]
    expand = (jnp.arange(LANES)[:, None] == (jnp.arange(d_ssm) // SSM_HEAD_DIM)[None, :]).astype(BF16)
    tril = (jnp.arange(SSD_CHUNK)[:, None] >= jnp.arange(SSD_CHUNK)[None, :]).astype(BF16)
    full = lambda shape: pl.BlockSpec(shape, lambda b, c: (0, 0))
    return pl.pallas_call(
        _ssd_kernel,
        grid=(bsz, nc),
        in_specs=[pl.BlockSpec((step_rows, d_conv), lambda b, c: (b * nc + c, xbc_col // d_conv)),
                  pl.BlockSpec((step_rows, d_ssm), lambda b, c: (b * nc + c, z_col // d_ssm)),
                  pl.BlockSpec((step_rows, LANES), lambda b, c: (b * nc + c, 0)),
                  full((CONV_WIDTH, d_conv)), full((1, d_conv)), full((1, LANES)), full((1, LANES)),
                  full((1, d_ssm)), full((1, d_ssm)), full((LANES, d_ssm)), full((SSD_CHUNK, SSD_CHUNK))],
        out_specs=pl.BlockSpec((step_rows, d_ssm), lambda b, c: (b * nc + c, 0)),
        out_shape=jax.ShapeDtypeStruct((bsz * s_len, d_ssm), BF16),
        scratch_shapes=[pltpu.VMEM((d_conv // LANES, step_rows + 8, LANES), F32),
                        pltpu.VMEM((step_rows, d_conv), F32),
                        pltpu.VMEM((SSM_STATE, d_ssm), F32)],
        compiler_params=_cparams(("parallel", "arbitrary")),
        name="ssd",
    )(packed, packed, dt_raw, conv_w.astype(F32), conv_b.astype(F32)[None, :], padl(dt_bias), padl(a_log),
      jnp.repeat(d_skip.astype(F32), SSM_HEAD_DIM)[None, :], g_out.astype(F32)[None, :], expand, tril)


def _mem_kv_kernel(mem_ref, g_ref, w_ref, gk_ref, kv_ref, h_ref):
    d_cross = N_CROSS_HEADS * CROSS_HEAD_DIM
    part = 256
    for r0 in range(0, mem_ref.shape[0], part):
        rows = slice(r0, r0 + part)
        _norm_chunk_to(mem_ref, g_ref, h_ref, r0, part)
        kv = jnp.dot(h_ref[rows, :], w_ref[...], preferred_element_type=F32)
        for h in range(N_CROSS_HEADS):
            cols = slice(h * CROSS_HEAD_DIM, (h + 1) * CROSS_HEAD_DIM)
            kv_ref[rows, cols] = _rms_rows(kv[:, cols], gk_ref[...]).astype(kv_ref.dtype)
        kv_ref[rows, d_cross:] = kv[:, d_cross:].astype(kv_ref.dtype)


def mem_kv(mem2d, g, w, gk, tm=1024):
    m, d = mem2d.shape
    tm = min(tm, m)
    n = w.shape[1]
    return pl.pallas_call(
        _mem_kv_kernel,
        grid=(m // tm,),
        in_specs=[pl.BlockSpec((tm, d), lambda i: (i, 0)),
                  pl.BlockSpec((1, d), lambda i: (0, 0)),
                  pl.BlockSpec((d, n), lambda i: (0, 0)),
                  pl.BlockSpec((1, CROSS_HEAD_DIM), lambda i: (0, 0))],
        out_specs=pl.BlockSpec((tm, n), lambda i: (i, 0)),
        out_shape=jax.ShapeDtypeStruct((m, n), BF16),
        scratch_shapes=[pltpu.VMEM((tm, d), BF16)],
        compiler_params=_cparams(("parallel",)),
        name="mem_kv",
    )(mem2d, g, w, gk)


def _mix_cross_kernel(attn_ref, ssm_ref, x_ref, ga_ref, w_ref, gc_ref, wq_ref, gq_ref, k_ref, v_ref, wo_ref, o_ref,
                      h_ref, xk_ref, an_ref, q_ref, a_ref, s_scr, p_scr):
    s = pl.program_id(0)
    n_tiles = pl.num_programs(0) - 1
    tm, d_attn = x_ref.shape[0], attn_ref.shape[1]
    rows = CROSS_ROWS
    prev, cur = (s + 1) % 2, s % 2

    @pl.when(s > 0)
    def _():
        q_ref[...] = jnp.dot(h_ref[prev], wq_ref[...], preferred_element_type=F32)

        def scores(item, slot):
            r0, h = item
            cols = slice(h * CROSS_HEAD_DIM, (h + 1) * CROSS_HEAD_DIM)
            qh = (_rms_rows(q_ref[r0:r0 + rows, cols], gq_ref[...]) * (CROSS_HEAD_DIM ** -0.5)).astype(BF16)
            s_scr[slot] = lax.dot_general(qh, k_ref[:, cols], (((1,), (1,)), ((), ())), preferred_element_type=F32)

        def softmax(item, slot):
            sc = s_scr[slot]
            p_scr[slot] = jnp.exp(sc - jnp.max(sc, axis=-1, keepdims=True)).astype(BF16)

        def values(item, slot):
            r0, h = item
            cols = slice(h * CROSS_HEAD_DIM, (h + 1) * CROSS_HEAD_DIM)
            vh = v_ref[:, cols]
            r = jnp.dot(p_scr[slot], jnp.concatenate([vh, jnp.ones_like(vh)], axis=1), preferred_element_type=F32)
            a_ref[r0:r0 + rows, cols] = (r[:, :CROSS_HEAD_DIM] / r[:, CROSS_HEAD_DIM:]).astype(BF16)

        items = [(r0, h) for r0 in range(0, tm, rows) for h in range(N_CROSS_HEADS)]
        _software_pipeline((scores, softmax, values), items, CROSS_GROUP)
        o_ref[...] = xk_ref[prev] + jnp.dot(a_ref[...], wo_ref[...], preferred_element_type=F32)

    @pl.when(s < n_tiles)
    def _():
        n_split = 2
        part = tm // n_split
        for r0 in range(0, tm, part):
            for r in range(r0, r0 + part, 32):
                an_ref[r:r + 32, :] = _rms_rows(attn_ref[r:r + 32, :].astype(F32), ga_ref[...]).astype(BF16)
            rs = slice(r0, r0 + part)
            xk_ref[cur, rs, :] = (x_ref[rs, :] + jnp.dot(an_ref[rs, :], w_ref[:d_attn, :], preferred_element_type=F32)
                                  + jnp.dot(ssm_ref[rs, :], w_ref[d_attn:, :], preferred_element_type=F32))
            _norm_chunk_to(xk_ref.at[cur], gc_ref, h_ref.at[cur], r0, part)


def mix_cross(attn, ssm, x2d, ga, w, gc, wq, gq, kv, wo, s_len, n_mem, tm=512):
    m, d = x2d.shape
    da, ds_ = attn.shape[1], ssm.shape[1]
    dc = wq.shape[1]
    n_tiles = m // tm
    per_batch = s_len // tm
    ahead = lambda s: (jnp.minimum(s, n_tiles - 1), 0)
    behind = lambda s: jnp.maximum(s - 1, 0)
    full = lambda shape, **kw: pl.BlockSpec(shape, lambda s: (0, 0), **kw)
    once = dict(pipeline_mode=pl.Buffered(1))
    return pl.pallas_call(
        _mix_cross_kernel,
        grid=(n_tiles + 1,),
        in_specs=[pl.BlockSpec((tm, da), ahead), pl.BlockSpec((tm, ds_), ahead), pl.BlockSpec((tm, d), ahead),
                  full((1, da)), full((da + ds_, d), **once), full((1, d)), full((d, dc), **once),
                  full((1, CROSS_HEAD_DIM)),
                  pl.BlockSpec((n_mem, dc), lambda s: (behind(s) // per_batch, 0)),
                  pl.BlockSpec((n_mem, dc), lambda s: (behind(s) // per_batch, 1)),
                  full((dc, d), **once)],
        out_specs=pl.BlockSpec((tm, d), lambda s: (behind(s), 0)),
        out_shape=jax.ShapeDtypeStruct((m, d), F32),
        scratch_shapes=[pltpu.VMEM((2, tm, d), BF16), pltpu.VMEM((2, tm, d), F32), pltpu.VMEM((tm, da), BF16),
                        pltpu.VMEM((tm, dc), F32), pltpu.VMEM((tm, dc), BF16),
                        pltpu.VMEM((2 * CROSS_GROUP, CROSS_ROWS, n_mem), F32),
                        pltpu.VMEM((2 * CROSS_GROUP, CROSS_ROWS, n_mem), BF16)],
        compiler_params=_cparams(("arbitrary",)),
        name="mix_cross",
    )(attn, ssm, x2d, ga, w, gc, wq, gq, kv, kv, wo)


def _mlp_kernel(x_ref, g_ref, wu_ref, wd_ref, o_ref, h_ref):
    tm = x_ref.shape[0]

    def expand(rows):
        u = jnp.dot(h_ref[rows, :], wu_ref[...], preferred_element_type=F32)
        u = jnp.square(jnp.maximum(u, 0.0)).astype(BF16)
        return jnp.dot(u, wd_ref[...], preferred_element_type=F32)

    @pl.when(pl.program_id(1) == 0)
    def _():
        part = tm // 2
        for r0 in range(0, tm, part):
            _norm_chunk_to(x_ref, g_ref, h_ref, r0, part)
            rows = slice(r0, r0 + part)
            o_ref[rows, :] = x_ref[rows, :] + expand(rows)

    @pl.when(pl.program_id(1) > 0)
    def _():
        o_ref[...] += expand(slice(None))


def mlp(x2d, g, wu, wd, tm=512, tf=2048):
    m, d = x2d.shape
    f = wu.shape[1]
    return pl.pallas_call(
        _mlp_kernel,
        grid=(m // tm, f // tf),
        in_specs=[pl.BlockSpec((tm, d), lambda i, j: (i, 0)),
                  pl.BlockSpec((1, d), lambda i, j: (0, 0)),
                  pl.BlockSpec((d, tf), lambda i, j: (0, j)),
                  pl.BlockSpec((tf, d), lambda i, j: (j, 0))],
        out_specs=pl.BlockSpec((tm, d), lambda i, j: (i, 0)),
        out_shape=jax.ShapeDtypeStruct((m, d), F32),
        scratch_shapes=[pltpu.VMEM((tm, d), BF16)],
        compiler_params=_cparams(("parallel", "arbitrary")),
        name="mlp",
    )(x2d, g, wu, wd)


def _layer(x2d, mem2d, positions, bsz, s_len, n_mem, g_mix, w_in, g_q, g_k, g_attn_out, conv_w, conv_b, dt_bias,
           a_log, d_skip, g_ssm_out, w_out, g_cross, g_mem, w_cq, w_ckv, g_cq, g_ck, w_co, g_mlp, w_up, w_down):
    d_model = x2d.shape[1]
    d_attn = d_model // 2
    d_ssm = d_model // 2
    n_pairs = d_attn // LANES
    n_ssm_heads = d_ssm // SSM_HEAD_DIM
    d_conv = d_ssm + 2 * SSM_GROUPS * SSM_STATE
    d_packed = 3 * d_attn + d_ssm + d_conv
    row = lambda v: v.astype(F32)[None, :]

    w_in_t = w_in.T
    w_dt_t = jnp.pad(w_in_t[d_packed:], ((0, LANES - n_ssm_heads), (0, 0)))
    packed, dt_raw = in_proj(x2d, row(g_mix), w_in_t, w_dt_t, d_packed)

    tabs = rope_tables(positions)
    attn, w_up_bf, w_down_bf, w_out_bf, w_ckv_bf, w_cq_bf, w_co_bf = dilated_attention(
        packed, tabs, row(jnp.tile(g_q, 2)) * (ATTN_HEAD_DIM ** -0.5 * LOG2E), row(jnp.tile(g_k, 2)),
        (w_up, w_down, w_out, w_ckv, w_cq, w_co), bsz, s_len, n_pairs)
    ssm = ssd_mixer(packed, dt_raw, conv_w, conv_b, dt_bias, a_log, d_skip, g_ssm_out, bsz, s_len, d_ssm,
                    xbc_col=3 * d_attn + d_ssm, z_col=3 * d_attn)
    kv = mem_kv(mem2d, row(g_mem), w_ckv_bf, row(g_ck))
    x2d = mix_cross(attn, ssm, x2d, row(g_attn_out), w_out_bf, row(g_cross), w_cq_bf, row(g_cq), kv, w_co_bf,
                    s_len, n_mem)

    return mlp(x2d, row(g_mlp), w_up_bf, w_down_bf)


def kernel(x, mem, positions, g_mix, w_in, g_q, g_k, g_attn_out, conv_w, conv_b, dt_bias, a_log, d_skip, g_ssm_out,
           w_out, g_cross, g_mem, w_cq, w_ckv, g_cq, g_ck, w_co, g_mlp, w_up, w_down):
    bsz, s_len, d_model = x.shape
    n_mem = mem.shape[1]
    x2d = x.reshape(bsz * s_len, d_model)
    mem2d = mem.reshape(bsz * n_mem, d_model)
    for i in range(g_mix.shape[0]):
        x2d = _layer(x2d, mem2d, positions, bsz, s_len, n_mem, g_mix[i], w_in[i], g_q[i], g_k[i], g_attn_out[i],
                     conv_w[i], conv_b[i], dt_bias[i], a_log[i], d_skip[i], g_ssm_out[i], w_out[i], g_cross[i],
                     g_mem[i], w_cq[i], w_ckv[i], g_cq[i], g_ck[i], w_co[i], g_mlp[i], w_up[i], w_down[i])
    return x2d.reshape(bsz, s_len, d_model)
```

```python
import functools
import math

import jax
import jax.numpy as jnp
from jax import lax
from jax.experimental import pallas as pl
from jax.experimental.pallas import tpu as pltpu

F32 = jnp.float32
BF16 = jnp.bfloat16
EPS = 1e-6

LANES = 128
ATTN_HEAD_DIM = 64
ROT_DIM = ATTN_HEAD_DIM // 4
ROPE_THETA = 500000.0
ATTN_BLOCK = 128
DILATIONS = (1, 4, 16)
SSM_HEAD_DIM = 64
SSM_GROUPS = 4
SSM_STATE = 128
CONV_WIDTH = 4
SSD_CHUNK = 128
N_CROSS_HEADS = 4
CROSS_HEAD_DIM = 128
CROSS_ROWS = 256
CROSS_GROUP = 2
VMEM_LIMIT = 58 * 1024 * 1024


def _cparams(sem):
    return pltpu.CompilerParams(dimension_semantics=sem, vmem_limit_bytes=VMEM_LIMIT)


def _rms_rows(x, g):
    ms = jnp.mean(x * x, axis=-1, keepdims=True)
    return x * lax.rsqrt(ms + EPS) * g


def _split3(x):
    hi = x.astype(BF16)
    r1 = x - hi.astype(F32)
    mid = r1.astype(BF16)
    lo = (r1 - mid.astype(F32)).astype(BF16)
    return hi, mid, lo


def _norm_rows_to(x_ref, g_ref, h_ref, rows=16):
    def body(c, carry):
        r0 = pl.multiple_of(c * rows, rows)
        h_ref[pl.ds(r0, rows), :] = _rms_rows(x_ref[pl.ds(r0, rows), :], g_ref[...]).astype(h_ref.dtype)
        return carry
    lax.fori_loop(0, x_ref.shape[0] // rows, body, 0, unroll=4)


def _norm_chunk_to(x_ref, g_ref, h_ref, start, n_rows, rows=16):
    for r in range(0, n_rows, rows):
        rs = pl.ds(start + r, rows)
        h_ref[rs, :] = _rms_rows(x_ref[rs, :], g_ref[...]).astype(h_ref.dtype)


def _in_proj_kernel(x0_ref, xn_ref, g_ref, w_ref, wdt_ref, o_ref, dt_ref, h_ref):
    i, j = pl.program_id(0), pl.program_id(1)
    cur, nxt = i % 2, (i + 1) % 2

    @pl.when((i == 0) & (j == 0))
    def _():
        _norm_rows_to(x0_ref, g_ref, h_ref.at[0])

    @pl.when(j == 0)
    def _():
        dt_ref[...] = lax.dot_general(h_ref[cur], wdt_ref[...].astype(BF16), (((1,), (1,)), ((), ())),
                                      preferred_element_type=F32)
    o_ref[...] = lax.dot_general(h_ref[cur], w_ref[...].astype(BF16), (((1,), (1,)), ((), ())),
                                 preferred_element_type=F32).astype(o_ref.dtype)
    chunk = xn_ref.shape[0]
    start = pl.multiple_of(jnp.minimum(j, IN_PROJ_NORM_STEPS - 1) * chunk, chunk)
    for r in range(0, chunk, 16):
        h_ref[nxt, pl.ds(start + r, 16), :] = _rms_rows(xn_ref[r:r + 16, :], g_ref[...]).astype(BF16)


IN_PROJ_NORM_STEPS = 4


def in_proj(x2d, g, w_t, w_dt, n, tm=1024, tn=1024):
    m, d = x2d.shape
    n_i = m // tm
    assert n // tn >= IN_PROJ_NORM_STEPS
    return pl.pallas_call(
        _in_proj_kernel,
        grid=(n_i, n // tn),
        in_specs=[pl.BlockSpec((tm, d), lambda i, j: (0, 0), pipeline_mode=pl.Buffered(1)),
                  pl.BlockSpec((tm // IN_PROJ_NORM_STEPS, d),
                               lambda i, j: (jnp.minimum(i + 1, n_i - 1) * IN_PROJ_NORM_STEPS
                                             + jnp.minimum(j, IN_PROJ_NORM_STEPS - 1), 0)),
                  pl.BlockSpec((1, d), lambda i, j: (0, 0)),
                  pl.BlockSpec((tn, d), lambda i, j: (j, 0)),
                  pl.BlockSpec((LANES, d), lambda i, j: (0, 0), pipeline_mode=pl.Buffered(1))],
        out_specs=[pl.BlockSpec((tm, tn), lambda i, j: (i, j)),
                   pl.BlockSpec((tm, LANES), lambda i, j: (i, 0))],
        out_shape=[jax.ShapeDtypeStruct((m, n), BF16), jax.ShapeDtypeStruct((m, LANES), F32)],
        scratch_shapes=[pltpu.VMEM((2, tm, d), BF16)],
        compiler_params=_cparams(("arbitrary", "arbitrary")),
        name="in_proj",
    )(x2d, x2d, g, w_t, w_dt)


ROPE_PACK = LANES // ROT_DIM


def _rope_kernel(pos_ref, expo_ref, spread_ref, cmask_ref, cos_ref, sin_ref):
    inv_freq = jnp.power(jnp.float32(ROPE_THETA), expo_ref[...])
    ang = pos_ref[...].astype(F32) * inv_freq
    packed = (jnp.cos(ang), jnp.sin(ang))
    fill = (1.0 - cmask_ref[...], jnp.zeros_like(cmask_ref[...]))
    n_rows = pos_ref.shape[0]
    for out_ref, tab, off in zip((cos_ref, sin_ref), packed, fill):
        parts = _split3(tab)
        for k in range(ROPE_PACK):
            rows = sum(jnp.dot(part, spread_ref[k], preferred_element_type=F32) for part in parts)
            out_ref[pl.ds(k, n_rows, stride=ROPE_PACK), :] = rows + off


def rope_tables(positions, tm=2048):
    m = positions.size
    half = ROT_DIM // 2
    lane = jnp.arange(LANES)
    expo = (-2.0 * ((lane % ROT_DIM) % half).astype(F32) / ROT_DIM)[None, :]
    cmask = ((lane % ATTN_HEAD_DIM) < ROT_DIM).astype(F32)[None, :]
    d_dst = lane % ATTN_HEAD_DIM
    spread = ((d_dst[None, None, :] < ROT_DIM)
              & (lane[None, :, None] == ROT_DIM * jnp.arange(ROPE_PACK)[:, None, None] + d_dst[None, None, :])).astype(BF16)
    pos_packed = jnp.repeat(positions.reshape(m // ROPE_PACK, ROPE_PACK), ROT_DIM, axis=1)
    row = pl.BlockSpec((1, LANES), lambda i: (0, 0))
    tab = pl.BlockSpec((tm, LANES), lambda i: (i, 0))
    return pl.pallas_call(
        _rope_kernel,
        grid=(m // tm,),
        in_specs=[pl.BlockSpec((tm // ROPE_PACK, LANES), lambda i: (i, 0)), row,
                  pl.BlockSpec((ROPE_PACK, LANES, LANES), lambda i: (0, 0, 0)), row],
        out_specs=[tab, tab],
        out_shape=[jax.ShapeDtypeStruct((m, LANES), F32)] * 2,
        compiler_params=_cparams(("parallel",)),
        name="rope_tab",
    )(pos_packed, expo, spread, cmask)


def _rotate_half_matrix():
    half = ROT_DIM // 2
    src = jnp.arange(LANES)[:, None]
    dst = jnp.arange(LANES)[None, :]
    d = dst % ATTN_HEAD_DIM
    first = (d < half) & (src == dst + half)
    second = (d >= half) & (d < ROT_DIM) & (src == dst - half)
    return (second.astype(F32) - first.astype(F32)).astype(BF16)


def _rows(start, size, dil):
    return pl.ds(start, size) if dil == 1 else pl.ds(start, size, stride=dil)


ATTN_GROUP = 2


def _software_pipeline(stages, items, group, extras=()):
    groups = [items[i:i + group] for i in range(0, len(items), group)]
    n_trips = len(groups) + len(stages) - 1
    for t in range(n_trips):
        for lag, stage in enumerate(stages):
            if 0 <= t - lag < len(groups):
                for g, item in enumerate(groups[t - lag]):
                    stage(item, ((t - lag) % 2) * group + g)
        for k, extra in enumerate(extras):
            if k * n_trips // len(extras) == t:
                extra()


def _attn_kernel(n_cast, q0_ref, k0_ref, v0_ref, cos0_ref, sin0_ref, qn_ref, kn_ref, vn_ref, cosn_ref, sinn_ref,
                 gq_ref, gk_ref, seg_ref, rot_ref, *refs):
    cast_in, o_ref, cast_out = refs[:n_cast], refs[n_cast], refs[n_cast + 1:2 * n_cast + 1]
    set_a, set_b, ob, mb, lb, s_scr, p_scr = refs[2 * n_cast + 1:]
    s_len = qn_ref.shape[0]
    blk = ATTN_BLOCK
    step = pl.program_id(0) * pl.num_programs(1) + pl.program_id(1)
    lane = lax.broadcasted_iota(jnp.int32, (blk, LANES), 1)
    head0 = lane < ATTN_HEAD_DIM
    qi = lax.broadcasted_iota(jnp.int32, (2 * blk, 2 * blk), 0) % blk
    kj = lax.broadcasted_iota(jnp.int32, (2 * blk, 2 * blk), 1)
    band_mask = (kj >= qi) & (kj <= qi + blk)
    first_mask = (lax.broadcasted_iota(jnp.int32, (2 * blk, blk), 1)
                  <= lax.broadcasted_iota(jnp.int32, (2 * blk, blk), 0) % blk)

    for w_ref, w_out in zip(cast_in, cast_out):
        w_out[...] = w_ref[...].astype(BF16)

    prep_rows = 256

    def prep(srcs, dst, r0):
        q_ref, k_ref, v_ref, cos_ref, sin_ref = srcs
        qf, kf, vf = dst[:3]
        rows = slice(r0, r0 + prep_rows)
        cs, sn = cos_ref[rows, :], sin_ref[rows, :]

        def norm_rope(x_ref, g_ref):
            x = x_ref[rows, :].astype(F32)
            ss = jnp.dot((x * x).astype(BF16), seg_ref[...], preferred_element_type=F32)
            y = x * lax.rsqrt(ss * (1.0 / ATTN_HEAD_DIM) + EPS) * g_ref[...]
            return y * cs + jnp.dot(y.astype(BF16), rot_ref[...], preferred_element_type=F32) * sn

        qf[rows, :] = norm_rope(q_ref, gq_ref)
        kf[rows, :] = norm_rope(k_ref, gk_ref)
        vf[rows, :] = v_ref[rows, :].astype(F32)

    dmid = DILATIONS[1]
    run = s_len // dmid

    def deinterleave(bufs, r):
        for src, dst in zip(bufs[:3], bufs[3:]):
            dst[r * run:(r + 1) * run, :] = src[pl.ds(r, run, stride=dmid), :]

    def prepare(srcs, bufs):
        return ([functools.partial(prep, srcs, bufs, r0) for r0 in range(0, s_len, prep_rows)]
                + [functools.partial(deinterleave, bufs, r) for r in range(dmid)])

    @pl.when(step == 0)
    def _():
        for piece in prepare((q0_ref, k0_ref, v0_ref, cos0_ref, sin0_ref), set_a):
            piece()

    def block_descs(bufs):
        natural, deint = bufs[:3], bufs[3:]
        descs = [(deint, 1, 1, r * run, r * run, blk, first_mask) for r in range(dmid)]
        descs += [(deint, 1, 1, r * run + n * blk, r * run + (n - 1) * blk, 2 * blk, band_mask)
                  for n in range(1, run // blk) for r in range(dmid)]
        descs += [(deint, 2, dmid, r_hi * run + r_lo, r_hi * run + r_lo, blk, first_mask)
                  for r_hi in range(dmid) for r_lo in range(dmid)]
        descs += [(natural, 0, 1, 0, 0, blk, first_mask)]
        descs += [(natural, 0, 1, n * blk, (n - 1) * blk, 2 * blk, band_mask) for n in range(1, s_len // blk)]
        return descs

    def scores(desc, slot):
        (q_src, k_src, _), _, dil, q_start, k_start, n_keys, _ = desc
        qt = q_src[_rows(q_start, blk, dil), :]
        zero = jnp.zeros_like(qt)
        q2 = jnp.concatenate([jnp.where(head0, qt, zero), jnp.where(head0, zero, qt)], axis=0).astype(BF16)
        kb = k_src[_rows(k_start, n_keys, dil), :].astype(BF16)
        s_scr[slot, :, :n_keys] = lax.dot_general(q2, kb, (((1,), (1,)), ((), ())), preferred_element_type=F32)

    def softmax(desc, slot):
        _, br, dil, q_start, _, n_keys, mask = desc
        s = jnp.where(mask, s_scr[slot, :, :n_keys], -jnp.inf)
        m = jnp.max(s, axis=-1, keepdims=True)
        p_scr[slot, :, :n_keys] = jnp.exp2(s - m).astype(BF16)
        mb[br, _rows(q_start, blk, dil), :] = jnp.where(head0, m[:blk], m[blk:])

    def values(desc, slot):
        (_, _, v_src), br, dil, q_start, k_start, n_keys, _ = desc
        vb = v_src[_rows(k_start, n_keys, dil), :].astype(BF16)
        v1 = jnp.concatenate([vb, jnp.ones_like(vb)], axis=1)
        r = jnp.dot(p_scr[slot, :, :n_keys], v1, preferred_element_type=F32)
        out_rows = _rows(q_start, blk, dil)
        ob[br, out_rows, :] = jnp.where(head0, r[:blk, :LANES], r[blk:, :LANES])
        lb[br, out_rows, :] = jnp.where(head0, r[:blk, LANES:], r[blk:, LANES:])
        if br == 0:
            merge(q_start)

    def merge(n0):
        sub = blk // dmid
        for r in range(dmid):
            nat_rows = pl.ds(n0 + r, sub, stride=dmid)
            d_rows = slice(r * run + n0 // dmid, r * run + n0 // dmid + sub)
            rows = (nat_rows, d_rows, d_rows)
            ms = [mb[g, rows[g], :] for g in range(len(DILATIONS))]
            m = functools.reduce(jnp.maximum, ms)
            ws = [jnp.exp2(mg - m) for mg in ms]
            num = functools.reduce(jnp.add, [w * ob[g, rows[g], :] for g, w in enumerate(ws)])
            den = functools.reduce(jnp.add, [w * lb[g, rows[g], :] for g, w in enumerate(ws)])
            ob[0, nat_rows, :] = num / den
        o_ref[n0:n0 + blk, :] = ob[0, n0:n0 + blk, :].astype(o_ref.dtype)

    for parity, (cur_set, nxt_set) in enumerate(((set_a, set_b), (set_b, set_a))):
        @pl.when(step % 2 == parity)
        def _(cur_set=cur_set, nxt_set=nxt_set):
            _software_pipeline((scores, softmax, values), block_descs(cur_set), ATTN_GROUP,
                               extras=prepare((qn_ref, kn_ref, vn_ref, cosn_ref, sinn_ref), nxt_set))


def dilated_attention(qkv, tabs, gq2, gk2, weights, bsz, s_len, n_pairs):
    cos_t, sin_t = tabs
    d1, d2, d3 = DILATIONS
    assert d1 == 1 and d3 == d2 * d2 and s_len == d3 * ATTN_BLOCK, "layout assumes dilations (1, d, d*d), one block per largest class"
    lane = jnp.arange(LANES)
    seg = (lane[:, None] // ATTN_HEAD_DIM == lane[None, :] // ATTN_HEAD_DIM).astype(BF16)
    n_steps = bsz * n_pairs

    def nxt(b, p):
        s = jnp.minimum(b * n_pairs + p + 1, n_steps - 1)
        return s // n_pairs, s % n_pairs

    once = dict(pipeline_mode=pl.Buffered(1))
    first = lambda off: pl.BlockSpec((s_len, LANES), lambda b, p, off=off: (0, off), **once)
    nblk = lambda off: pl.BlockSpec((s_len, LANES), lambda b, p, off=off: (nxt(b, p)[0], off + nxt(b, p)[1]))
    tab0 = pl.BlockSpec((s_len, LANES), lambda b, p: (0, 0), **once)
    tabn = pl.BlockSpec((s_len, LANES), lambda b, p: (nxt(b, p)[0], 0))
    row = pl.BlockSpec((1, LANES), lambda b, p: (0, 0))
    sq = pl.BlockSpec((LANES, LANES), lambda b, p: (0, 0))
    w_specs = [pl.BlockSpec((w.shape[0] // n_steps, w.shape[1]), lambda b, p: (b * n_pairs + p, 0)) for w in weights]
    return pl.pallas_call(
        functools.partial(_attn_kernel, len(weights)),
        grid=(bsz, n_pairs),
        in_specs=[first(0), first(n_pairs), first(2 * n_pairs), tab0, tab0,
                  nblk(0), nblk(n_pairs), nblk(2 * n_pairs), tabn, tabn,
                  row, row, sq, sq] + w_specs,
        out_specs=[pl.BlockSpec((s_len, LANES), lambda b, p: (b, p))] + w_specs,
        out_shape=[jax.ShapeDtypeStruct((bsz * s_len, n_pairs * LANES), BF16)]
                  + [jax.ShapeDtypeStruct(w.shape, BF16) for w in weights],
        scratch_shapes=[[pltpu.VMEM((s_len, LANES), F32)] * 6] * 2
                       + [pltpu.VMEM((len(DILATIONS), s_len, LANES), F32)] * 3
                       + [pltpu.VMEM((2 * ATTN_GROUP, 2 * ATTN_BLOCK, 2 * ATTN_BLOCK), F32),
                          pltpu.VMEM((2 * ATTN_GROUP, 2 * ATTN_BLOCK, 2 * ATTN_BLOCK), BF16)],
        compiler_params=_cparams(("arbitrary", "arbitrary")),
        name="dilated_attn",
    )(qkv, qkv, qkv, cos_t, sin_t, qkv, qkv, qkv, cos_t, sin_t, gq2, gk2, seg, _rotate_half_matrix(), *weights)


def _silu(x):
    h = 0.5 * x
    return h * jnp.tanh(h) + h


LOG2E = 1.4426950408889634
SSD_STEP_CHUNKS = 8


def _ssd_kernel(xbc_ref, z_ref, dt_ref, cw_ref, cb_ref, dtb_ref, alog_ref, dskip_ref, g_ref,
                expand_ref, tril_ref, o_ref, xpad, xc, st):
    q = SSD_CHUNK
    n_rows = xbc_ref.shape[0]
    d_ssm = z_ref.shape[1]
    d_conv = xbc_ref.shape[1]
    gw = d_ssm // SSM_GROUPS
    heads_per_group = gw // SSM_HEAD_DIM
    pad = 8

    @pl.when(pl.program_id(1) == 0)
    def _():
        xpad[:, 0:pad, :] = jnp.zeros((d_conv // LANES, pad, LANES), F32)
        st[...] = jnp.zeros_like(st)

    for c0 in range(0, d_conv, LANES):
        cols = slice(c0, c0 + LANES)
        slab = c0 // LANES
        xpad[slab, pad:pad + n_rows, :] = xbc_ref[:, cols].astype(F32)
        acc = cb_ref[:, cols] + cw_ref[CONV_WIDTH - 1:CONV_WIDTH, cols] * xpad[slab, pad:pad + n_rows, :]
        for w in range(CONV_WIDTH - 1):
            off = pad - (CONV_WIDTH - 1) + w
            acc = acc + cw_ref[w:w + 1, cols] * xpad[slab, pl.ds(off, n_rows, stride=1), :]
        xc[:, cols] = _silu(acc)
    xpad[:, 0:pad, :] = xpad[:, n_rows:n_rows + pad, :]

    li = lax.broadcasted_iota(jnp.int32, (q, q), 0)
    si = lax.broadcasted_iota(jnp.int32, (q, q), 1)
    causal = li >= si
    lane_g = lax.broadcasted_iota(jnp.int32, (q, gw), 1)
    tril = tril_ref[...]
    expand = expand_ref[...]

    for r0 in range(0, n_rows, q):
        rows = slice(r0, r0 + q)
        x_dt = dt_ref[rows, :] + dtb_ref[...]
        dt = jnp.maximum(x_dt, 0.0) + jnp.log1p(jnp.exp(-jnp.abs(x_dt)))
        dta = dt * (-LOG2E * jnp.exp(alog_ref[...]))
        acs = sum(jnp.dot(tril, part, preferred_element_type=F32) for part in _split3(dta))
        acs_t = acs.T
        last = acs[q - 1:q, :]
        dt_e = jnp.dot(dt.astype(BF16), expand, preferred_element_type=F32)
        dec_e = jnp.dot(jnp.exp2(acs).astype(BF16), expand, preferred_element_type=F32)
        w_e = jnp.dot((jnp.exp2(last - acs) * dt).astype(BF16), expand, preferred_element_type=F32)

        for g in range(SSM_GROUPS):
            cols = slice(g * gw, (g + 1) * gw)
            b_f = xc[rows, d_ssm + g * SSM_STATE:d_ssm + (g + 1) * SSM_STATE]
            c_b = xc[rows, d_ssm + (SSM_GROUPS + g) * SSM_STATE:d_ssm + (SSM_GROUPS + g + 1) * SSM_STATE].astype(BF16)
            xs = xc[rows, cols]
            cb = lax.dot_general(c_b, b_f.astype(BF16), (((1,), (1,)), ((), ())), preferred_element_type=F32)
            xdt = (xs * dt_e[:, cols]).astype(BF16)
            ws, rs = [], []
            for hh in range(heads_per_group):
                h = g * heads_per_group + hh
                seg = acs[:, h:h + 1] - acs_t[h:h + 1, :]
                l_mat = jnp.exp2(jnp.where(causal, seg, -jnp.inf))
                ws.append((cb * l_mat).astype(BF16))
                in_head = (lane_g >= hh * SSM_HEAD_DIM) & (lane_g < (hh + 1) * SSM_HEAD_DIM)
                rs.append(jnp.where(in_head, xdt, jnp.zeros_like(xdt)))
            y = jnp.dot(jnp.concatenate(ws, axis=1), jnp.concatenate(rs, axis=0), preferred_element_type=F32)
            st_g = st[:, cols]
            y = y + jnp.dot(c_b, st_g.astype(BF16), preferred_element_type=F32) * dec_e[:, cols]
            xw = (xs * w_e[:, cols]).astype(BF16)
            st[:, cols] = st_g * dec_e[q - 1:q, cols] + jnp.dot(b_f.T.astype(BF16), xw, preferred_element_type=F32)
            y = y + dskip_ref[:, cols] * xs
            y = y * _silu(z_ref[rows, cols].astype(F32))
            o_ref[rows, cols] = _rms_rows(y, g_ref[:, cols]).astype(o_ref.dtype)


def ssd_mixer(packed, dt_raw, conv_w, conv_b, dt_bias, a_log, d_skip, g_out, bsz, s_len, d_ssm, xbc_col, z_col):
    n_heads = d_ssm // SSM_HEAD_DIM
    d_conv = d_ssm + 2 * SSM_GROUPS * SSM_STATE
    step_rows = SSD_STEP_CHUNKS * SSD_CHUNK
    nc = s_len // step_rows
    padl = lambda v: jnp.pad(v.astype(F32), (0, LANES - n_heads))[None, :]
    expand = (jnp.arange(LANES)[:, None] == (jnp.arange(d_ssm) // SSM_HEAD_DIM)[None, :]).astype(BF16)
    tril = (jnp.arange(SSD_CHUNK)[:, None] >= jnp.arange(SSD_CHUNK)[None, :]).astype(BF16)
    full = lambda shape: pl.BlockSpec(shape, lambda b, c: (0, 0))
    return pl.pallas_call(
        _ssd_kernel,
        grid=(bsz, nc),
        in_specs=[pl.BlockSpec((step_rows, d_conv), lambda b, c: (b * nc + c, xbc_col // d_conv)),
                  pl.BlockSpec((step_rows, d_ssm), lambda b, c: (b * nc + c, z_col // d_ssm)),
                  pl.BlockSpec((step_rows, LANES), lambda b, c: (b * nc + c, 0)),
                  full((CONV_WIDTH, d_conv)), full((1, d_conv)), full((1, LANES)), full((1, LANES)),
                  full((1, d_ssm)), full((1, d_ssm)), full((LANES, d_ssm)), full((SSD_CHUNK, SSD_CHUNK))],
        out_specs=pl.BlockSpec((step_rows, d_ssm), lambda b, c: (b * nc + c, 0)),
        out_shape=jax.ShapeDtypeStruct((bsz * s_len, d_ssm), BF16),
        scratch_shapes=[pltpu.VMEM((d_conv // LANES, step_rows + 8, LANES), F32),
                        pltpu.VMEM((step_rows, d_conv), F32),
                        pltpu.VMEM((SSM_STATE, d_ssm), F32)],
        compiler_params=_cparams(("parallel", "arbitrary")),
        name="ssd",
    )(packed, packed, dt_raw, conv_w.astype(F32), conv_b.astype(F32)[None, :], padl(dt_bias), padl(a_log),
      jnp.repeat(d_skip.astype(F32), SSM_HEAD_DIM)[None, :], g_out.astype(F32)[None, :], expand, tril)


def _mem_kv_kernel(mem_ref, g_ref, w_ref, gk_ref, kv_ref, h_ref):
    d_cross = N_CROSS_HEADS * CROSS_HEAD_DIM
    part = 256
    for r0 in range(0, mem_ref.shape[0], part):
        rows = slice(r0, r0 + part)
        _norm_chunk_to(mem_ref, g_ref, h_ref, r0, part)
        kv = jnp.dot(h_ref[rows, :], w_ref[...], preferred_element_type=F32)
        for h in range(N_CROSS_HEADS):
            cols = slice(h * CROSS_HEAD_DIM, (h + 1) * CROSS_HEAD_DIM)
            kv_ref[rows, cols] = _rms_rows(kv[:, cols], gk_ref[...]).astype(kv_ref.dtype)
        kv_ref[rows, d_cross:] = kv[:, d_cross:].astype(kv_ref.dtype)


def mem_kv(mem2d, g, w, gk, tm=1024):
    m, d = mem2d.shape
    tm = min(tm, m)
    n = w.shape[1]
    return pl.pallas_call(
        _mem_kv_kernel,
        grid=(m // tm,),
        in_specs=[pl.BlockSpec((tm, d), lambda i: (i, 0)),
                  pl.BlockSpec((1, d), lambda i: (0, 0)),
                  pl.BlockSpec((d, n), lambda i: (0, 0)),
                  pl.BlockSpec((1, CROSS_HEAD_DIM), lambda i: (0, 0))],
        out_specs=pl.BlockSpec((tm, n), lambda i: (i, 0)),
        out_shape=jax.ShapeDtypeStruct((m, n), BF16),
        scratch_shapes=[pltpu.VMEM((tm, d), BF16)],
        compiler_params=_cparams(("parallel",)),
        name="mem_kv",
    )(mem2d, g, w, gk)


def _mix_cross_kernel(attn_ref, ssm_ref, x_ref, ga_ref, w_ref, gc_ref, wq_ref, gq_ref, k_ref, v_ref, wo_ref, o_ref,
                      h_ref, xk_ref, an_ref, q_ref, a_ref, s_scr, p_scr):
    s = pl.program_id(0)
    n_tiles = pl.num_programs(0) - 1
    tm, d_attn = x_ref.shape[0], attn_ref.shape[1]
    rows = CROSS_ROWS
    prev, cur = (s + 1) % 2, s % 2

    @pl.when(s > 0)
    def _():
        q_ref[...] = jnp.dot(h_ref[prev], wq_ref[...], preferred_element_type=F32)

        def scores(item, slot):
            r0, h = item
            cols = slice(h * CROSS_HEAD_DIM, (h + 1) * CROSS_HEAD_DIM)
            qh = (_rms_rows(q_ref[r0:r0 + rows, cols], gq_ref[...]) * (CROSS_HEAD_DIM ** -0.5)).astype(BF16)
            s_scr[slot] = lax.dot_general(qh, k_ref[:, cols], (((1,), (1,)), ((), ())), preferred_element_type=F32)

        def softmax(item, slot):
            sc = s_scr[slot]
            p_scr[slot] = jnp.exp(sc - jnp.max(sc, axis=-1, keepdims=True)).astype(BF16)

        def values(item, slot):
            r0, h = item
            cols = slice(h * CROSS_HEAD_DIM, (h + 1) * CROSS_HEAD_DIM)
            vh = v_ref[:, cols]
            r = jnp.dot(p_scr[slot], jnp.concatenate([vh, jnp.ones_like(vh)], axis=1), preferred_element_type=F32)
            a_ref[r0:r0 + rows, cols] = (r[:, :CROSS_HEAD_DIM] / r[:, CROSS_HEAD_DIM:]).astype(BF16)

        items = [(r0, h) for r0 in range(0, tm, rows) for h in range(N_CROSS_HEADS)]
        _software_pipeline((scores, softmax, values), items, CROSS_GROUP)
        o_ref[...] = xk_ref[prev] + jnp.dot(a_ref[...], wo_ref[...], preferred_element_type=F32)

    @pl.when(s < n_tiles)
    def _():
        n_split = 2
        part = tm // n_split
        for r0 in range(0, tm, part):
            for r in range(r0, r0 + part, 32):
                an_ref[r:r + 32, :] = _rms_rows(attn_ref[r:r + 32, :].astype(F32), ga_ref[...]).astype(BF16)
            rs = slice(r0, r0 + part)
            xk_ref[cur, rs, :] = (x_ref[rs, :] + jnp.dot(an_ref[rs, :], w_ref[:d_attn, :], preferred_element_type=F32)
                                  + jnp.dot(ssm_ref[rs, :], w_ref[d_attn:, :], preferred_element_type=F32))
            _norm_chunk_to(xk_ref.at[cur], gc_ref, h_ref.at[cur], r0, part)


def mix_cross(attn, ssm, x2d, ga, w, gc, wq, gq, kv, wo, s_len, n_mem, tm=512):
    m, d = x2d.shape
    da, ds_ = attn.shape[1], ssm.shape[1]
    dc = wq.shape[1]
    n_tiles = m // tm
    per_batch = s_len // tm
    ahead = lambda s: (jnp.minimum(s, n_tiles - 1), 0)
    behind = lambda s: jnp.maximum(s - 1, 0)
    full = lambda shape, **kw: pl.BlockSpec(shape, lambda s: (0, 0), **kw)
    once = dict(pipeline_mode=pl.Buffered(1))
    return pl.pallas_call(
        _mix_cross_kernel,
        grid=(n_tiles + 1,),
        in_specs=[pl.BlockSpec((tm, da), ahead), pl.BlockSpec((tm, ds_), ahead), pl.BlockSpec((tm, d), ahead),
                  full((1, da)), full((da + ds_, d), **once), full((1, d)), full((d, dc), **once),
                  full((1, CROSS_HEAD_DIM)),
                  pl.BlockSpec((n_mem, dc), lambda s: (behind(s) // per_batch, 0)),
                  pl.BlockSpec((n_mem, dc), lambda s: (behind(s) // per_batch, 1)),
                  full((dc, d), **once)],
        out_specs=pl.BlockSpec((tm, d), lambda s: (behind(s), 0)),
        out_shape=jax.ShapeDtypeStruct((m, d), F32),
        scratch_shapes=[pltpu.VMEM((2, tm, d), BF16), pltpu.VMEM((2, tm, d), F32), pltpu.VMEM((tm, da), BF16),
                        pltpu.VMEM((tm, dc), F32), pltpu.VMEM((tm, dc), BF16),
                        pltpu.VMEM((2 * CROSS_GROUP, CROSS_ROWS, n_mem), F32),
                        pltpu.VMEM((2 * CROSS_GROUP, CROSS_ROWS, n_mem), BF16)],
        compiler_params=_cparams(("arbitrary",)),
        name="mix_cross",
    )(attn, ssm, x2d, ga, w, gc, wq, gq, kv, kv, wo)


def _mlp_kernel(x_ref, g_ref, wu_ref, wd_ref, o_ref, h_ref):
    tm = x_ref.shape[0]

    def expand(rows):
        u = jnp.dot(h_ref[rows, :], wu_ref[...], preferred_element_type=F32)
        u = jnp.square(jnp.maximum(u, 0.0)).astype(BF16)
        return jnp.dot(u, wd_ref[...], preferred_element_type=F32)

    @pl.when(pl.program_id(1) == 0)
    def _():
        part = tm // 2
        for r0 in range(0, tm, part):
            _norm_chunk_to(x_ref, g_ref, h_ref, r0, part)
            rows = slice(r0, r0 + part)
            o_ref[rows, :] = x_ref[rows, :] + expand(rows)

    @pl.when(pl.program_id(1) > 0)
    def _():
        o_ref[...] += expand(slice(None))


def mlp(x2d, g, wu, wd, tm=512, tf=2048):
    m, d = x2d.shape
    f = wu.shape[1]
    return pl.pallas_call(
        _mlp_kernel,
        grid=(m // tm, f // tf),
        in_specs=[pl.BlockSpec((tm, d), lambda i, j: (i, 0)),
                  pl.BlockSpec((1, d), lambda i, j: (0, 0)),
                  pl.BlockSpec((d, tf), lambda i, j: (0, j)),
                  pl.BlockSpec((tf, d), lambda i, j: (j, 0))],
        out_specs=pl.BlockSpec((tm, d), lambda i, j: (i, 0)),
        out_shape=jax.ShapeDtypeStruct((m, d), F32),
        scratch_shapes=[pltpu.VMEM((tm, d), BF16)],
        compiler_params=_cparams(("parallel", "arbitrary")),
        name="mlp",
    )(x2d, g, wu, wd)


def _layer(x2d, mem2d, positions, bsz, s_len, n_mem, g_mix, w_in, g_q, g_k, g_attn_out, conv_w, conv_b, dt_bias,
           a_log, d_skip, g_ssm_out, w_out, g_cross, g_mem, w_cq, w_ckv, g_cq, g_ck, w_co, g_mlp, w_up, w_down):
    d_model = x2d.shape[1]
    d_attn = d_model // 2
    d_ssm = d_model // 2
    n_pairs = d_attn // LANES
    n_ssm_heads = d_ssm // SSM_HEAD_DIM
    d_conv = d_ssm + 2 * SSM_GROUPS * SSM_STATE
    d_packed = 3 * d_attn + d_ssm + d_conv
    row = lambda v: v.astype(F32)[None, :]

    w_in_t = w_in.T
    w_dt_t = jnp.pad(w_in_t[d_packed:], ((0, LANES - n_ssm_heads), (0, 0)))
    packed, dt_raw = in_proj(x2d, row(g_mix), w_in_t, w_dt_t, d_packed)

    tabs = rope_tables(positions)
    attn, w_up_bf, w_down_bf, w_out_bf, w_ckv_bf, w_cq_bf, w_co_bf = dilated_attention(
        packed, tabs, row(jnp.tile(g_q, 2)) * (ATTN_HEAD_DIM ** -0.5 * LOG2E), row(jnp.tile(g_k, 2)),
        (w_up, w_down, w_out, w_ckv, w_cq, w_co), bsz, s_len, n_pairs)
    ssm = ssd_mixer(packed, dt_raw, conv_w, conv_b, dt_bias, a_log, d_skip, g_ssm_out, bsz, s_len, d_ssm,
                    xbc_col=3 * d_attn + d_ssm, z_col=3 * d_attn)
    kv = mem_kv(mem2d, row(g_mem), w_ckv_bf, row(g_ck))
    x2d = mix_cross(attn, ssm, x2d, row(g_attn_out), w_out_bf, row(g_cross), w_cq_bf, row(g_cq), kv, w_co_bf,
                    s_len, n_mem)

    return mlp(x2d, row(g_mlp), w_up_bf, w_down_bf)


def kernel(x, mem, positions, g_mix, w_in, g_q, g_k, g_attn_out, conv_w, conv_b, dt_bias, a_log, d_skip, g_ssm_out,
           w_out, g_cross, g_mem, w_cq, w_ckv, g_cq, g_ck, w_co, g_mlp, w_up, w_down):
    bsz, s_len, d_model = x.shape
    n_mem = mem.shape[1]
    x2d = x.reshape(bsz * s_len, d_model)
    mem2d = mem.reshape(bsz * n_mem, d_model)
    for i in range(g_mix.shape[0]):
        x2d = _layer(x2d, mem2d, positions, bsz, s_len, n_mem, g_mix[i], w_in[i], g_q[i], g_k[i], g_attn_out[i],
                     conv_w[i], conv_b[i], dt_bias[i], a_log[i], d_skip[i], g_ssm_out[i], w_out[i], g_cross[i],
                     g_mem[i], w_cq[i], w_ckv[i], g_cq[i], g_ck[i], w_co[i], g_mlp[i], w_up[i], w_down[i])
    return x2d.reshape(bsz, s_len, d_model)
```

```python
import functools
import math

import jax
import jax.numpy as jnp
from jax import lax
from jax.experimental import pallas as pl
from jax.experimental.pallas import tpu as pltpu

F32 = jnp.float32
BF16 = jnp.bfloat16
EPS = 1e-6

LANES = 128
ATTN_HEAD_DIM = 64
ROT_DIM = ATTN_HEAD_DIM // 4
ROPE_THETA = 500000.0
ATTN_BLOCK = 128
DILATIONS = (1, 4, 16)
SSM_HEAD_DIM = 64
SSM_GROUPS = 4
SSM_STATE = 128
CONV_WIDTH = 4
SSD_CHUNK = 128
N_CROSS_HEADS = 4
CROSS_HEAD_DIM = 128
CROSS_ROWS = 256
CROSS_GROUP = 2
VMEM_LIMIT = 58 * 1024 * 1024


def _cparams(sem):
    return pltpu.CompilerParams(dimension_semantics=sem, vmem_limit_bytes=VMEM_LIMIT)


def _rms_rows(x, g):
    ms = jnp.mean(x * x, axis=-1, keepdims=True)
    return x * lax.rsqrt(ms + EPS) * g


def _split3(x):
    hi = x.astype(BF16)
    r1 = x - hi.astype(F32)
    mid = r1.astype(BF16)
    lo = (r1 - mid.astype(F32)).astype(BF16)
    return hi, mid, lo


def _norm_rows_to(x_ref, g_ref, h_ref, rows=16):
    def body(c, carry):
        r0 = pl.multiple_of(c * rows, rows)
        h_ref[pl.ds(r0, rows), :] = _rms_rows(x_ref[pl.ds(r0, rows), :], g_ref[...]).astype(h_ref.dtype)
        return carry
    lax.fori_loop(0, x_ref.shape[0] // rows, body, 0, unroll=4)


def _norm_chunk_to(x_ref, g_ref, h_ref, start, n_rows, rows=16):
    for r in range(0, n_rows, rows):
        rs = pl.ds(start + r, rows)
        h_ref[rs, :] = _rms_rows(x_ref[rs, :], g_ref[...]).astype(h_ref.dtype)


def _in_proj_kernel(x0_ref, xn_ref, g_ref, w_ref, wdt_ref, o_ref, dt_ref, h_ref):
    i, j = pl.program_id(0), pl.program_id(1)
    cur, nxt = i % 2, (i + 1) % 2

    @pl.when((i == 0) & (j == 0))
    def _():
        _norm_rows_to(x0_ref, g_ref, h_ref.at[0])

    @pl.when(j == 0)
    def _():
        dt_ref[...] = lax.dot_general(h_ref[cur], wdt_ref[...].astype(BF16), (((1,), (1,)), ((), ())),
                                      preferred_element_type=F32)
    o_ref[...] = lax.dot_general(h_ref[cur], w_ref[...].astype(BF16), (((1,), (1,)), ((), ())),
                                 preferred_element_type=F32).astype(o_ref.dtype)
    chunk = xn_ref.shape[0]
    start = pl.multiple_of(jnp.minimum(j, IN_PROJ_NORM_STEPS - 1) * chunk, chunk)
    for r in range(0, chunk, 16):
        h_ref[nxt, pl.ds(start + r, 16), :] = _rms_rows(xn_ref[r:r + 16, :], g_ref[...]).astype(BF16)


IN_PROJ_NORM_STEPS = 4


def in_proj(x2d, g, w_t, w_dt, n, tm=1024, tn=1024):
    m, d = x2d.shape
    n_i = m // tm
    assert n // tn >= IN_PROJ_NORM_STEPS
    return pl.pallas_call(
        _in_proj_kernel,
        grid=(n_i, n // tn),
        in_specs=[pl.BlockSpec((tm, d), lambda i, j: (0, 0), pipeline_mode=pl.Buffered(1)),
                  pl.BlockSpec((tm // IN_PROJ_NORM_STEPS, d),
                               lambda i, j: (jnp.minimum(i + 1, n_i - 1) * IN_PROJ_NORM_STEPS
                                             + jnp.minimum(j, IN_PROJ_NORM_STEPS - 1), 0)),
                  pl.BlockSpec((1, d), lambda i, j: (0, 0)),
                  pl.BlockSpec((tn, d), lambda i, j: (j, 0)),
                  pl.BlockSpec((LANES, d), lambda i, j: (0, 0), pipeline_mode=pl.Buffered(1))],
        out_specs=[pl.BlockSpec((tm, tn), lambda i, j: (i, j)),
                   pl.BlockSpec((tm, LANES), lambda i, j: (i, 0))],
        out_shape=[jax.ShapeDtypeStruct((m, n), BF16), jax.ShapeDtypeStruct((m, LANES), F32)],
        scratch_shapes=[pltpu.VMEM((2, tm, d), BF16)],
        compiler_params=_cparams(("arbitrary", "arbitrary")),
        name="in_proj",
    )(x2d, x2d, g, w_t, w_dt)


ROPE_PACK = LANES // ROT_DIM


def _rope_rows(pos_ref, expo_ref, spread_ref, cmask_ref, cos_ref, sin_ref, p0, n_rows):
    inv_freq = jnp.power(jnp.float32(ROPE_THETA), expo_ref[...])
    ang = pos_ref[p0:p0 + n_rows, :].astype(F32) * inv_freq
    packed = (jnp.cos(ang), jnp.sin(ang))
    fill = (1.0 - cmask_ref[...], jnp.zeros_like(cmask_ref[...]))
    for out_ref, tab, off in zip((cos_ref, sin_ref), packed, fill):
        parts = _split3(tab)
        for k in range(ROPE_PACK):
            rows = sum(jnp.dot(part, spread_ref[k], preferred_element_type=F32) for part in parts)
            out_ref[pl.ds(ROPE_PACK * p0 + k, n_rows, stride=ROPE_PACK), :] = rows + off


def rope_operands(positions):
    m = positions.size
    half = ROT_DIM // 2
    lane = jnp.arange(LANES)
    expo = (-2.0 * ((lane % ROT_DIM) % half).astype(F32) / ROT_DIM)[None, :]
    cmask = ((lane % ATTN_HEAD_DIM) < ROT_DIM).astype(F32)[None, :]
    d_dst = lane % ATTN_HEAD_DIM
    spread = ((d_dst[None, None, :] < ROT_DIM)
              & (lane[None, :, None] == ROT_DIM * jnp.arange(ROPE_PACK)[:, None, None] + d_dst[None, None, :])).astype(BF16)
    pos_packed = jnp.repeat(positions.reshape(m // ROPE_PACK, ROPE_PACK), ROT_DIM, axis=1)
    return pos_packed, expo, spread, cmask


def _rotate_half_matrix():
    half = ROT_DIM // 2
    src = jnp.arange(LANES)[:, None]
    dst = jnp.arange(LANES)[None, :]
    d = dst % ATTN_HEAD_DIM
    first = (d < half) & (src == dst + half)
    second = (d >= half) & (d < ROT_DIM) & (src == dst - half)
    return (second.astype(F32) - first.astype(F32)).astype(BF16)


def _rows(start, size, dil):
    return pl.ds(start, size) if dil == 1 else pl.ds(start, size, stride=dil)


ATTN_GROUP = 2


def _software_pipeline(stages, items, group, extras=()):
    groups = [items[i:i + group] for i in range(0, len(items), group)]
    n_trips = len(groups) + len(stages) - 1
    for t in range(n_trips):
        for lag, stage in enumerate(stages):
            if 0 <= t - lag < len(groups):
                for g, item in enumerate(groups[t - lag]):
                    stage(item, ((t - lag) % 2) * group + g)
        for k, extra in enumerate(extras):
            if k * n_trips // len(extras) == t:
                extra()


def _attn_kernel(n_cast, q0_ref, k0_ref, v0_ref, cos0_ref, sin0_ref, qn_ref, kn_ref, vn_ref, cosn_ref, sinn_ref,
                 gq_ref, gk_ref, seg_ref, rot_ref, *refs):
    cast_in, o_ref, cast_out = refs[:n_cast], refs[n_cast], refs[n_cast + 1:2 * n_cast + 1]
    set_a, set_b, ob, mb, lb, s_scr, p_scr = refs[2 * n_cast + 1:]
    s_len = qn_ref.shape[0]
    blk = ATTN_BLOCK
    step = pl.program_id(0) * pl.num_programs(1) + pl.program_id(1)
    lane = lax.broadcasted_iota(jnp.int32, (blk, LANES), 1)
    head0 = lane < ATTN_HEAD_DIM
    qi = lax.broadcasted_iota(jnp.int32, (2 * blk, 2 * blk), 0) % blk
    kj = lax.broadcasted_iota(jnp.int32, (2 * blk, 2 * blk), 1)
    band_mask = (kj >= qi) & (kj <= qi + blk)
    first_mask = (lax.broadcasted_iota(jnp.int32, (2 * blk, blk), 1)
                  <= lax.broadcasted_iota(jnp.int32, (2 * blk, blk), 0) % blk)

    for w_ref, w_out in zip(cast_in, cast_out):
        w_out[...] = w_ref[...].astype(BF16)

    prep_rows = 256

    def prep(srcs, dst, r0):
        q_ref, k_ref, v_ref, cos_ref, sin_ref = srcs
        qf, kf, vf = dst[:3]
        rows = slice(r0, r0 + prep_rows)
        cs, sn = cos_ref[rows, :], sin_ref[rows, :]

        def norm_rope(x_ref, g_ref):
            x = x_ref[rows, :].astype(F32)
            ss = jnp.dot((x * x).astype(BF16), seg_ref[...], preferred_element_type=F32)
            y = x * lax.rsqrt(ss * (1.0 / ATTN_HEAD_DIM) + EPS) * g_ref[...]
            return y * cs + jnp.dot(y.astype(BF16), rot_ref[...], preferred_element_type=F32) * sn

        qf[rows, :] = norm_rope(q_ref, gq_ref)
        kf[rows, :] = norm_rope(k_ref, gk_ref)
        vf[rows, :] = v_ref[rows, :].astype(F32)

    dmid = DILATIONS[1]
    run = s_len // dmid

    def deinterleave(bufs, r):
        for src, dst in zip(bufs[:3], bufs[3:]):
            dst[r * run:(r + 1) * run, :] = src[pl.ds(r, run, stride=dmid), :]

    def prepare(srcs, bufs):
        return ([functools.partial(prep, srcs, bufs, r0) for r0 in range(0, s_len, prep_rows)]
                + [functools.partial(deinterleave, bufs, r) for r in range(dmid)])

    @pl.when(step == 0)
    def _():
        for piece in prepare((q0_ref, k0_ref, v0_ref, cos0_ref, sin0_ref), set_a):
            piece()

    def block_descs(bufs):
        natural, deint = bufs[:3], bufs[3:]
        descs = [(deint, 1, 1, r * run, r * run, blk, first_mask) for r in range(dmid)]
        descs += [(deint, 1, 1, r * run + n * blk, r * run + (n - 1) * blk, 2 * blk, band_mask)
                  for n in range(1, run // blk) for r in range(dmid)]
        descs += [(deint, 2, dmid, r_hi * run + r_lo, r_hi * run + r_lo, blk, first_mask)
                  for r_hi in range(dmid) for r_lo in range(dmid)]
        descs += [(natural, 0, 1, 0, 0, blk, first_mask)]
        descs += [(natural, 0, 1, n * blk, (n - 1) * blk, 2 * blk, band_mask) for n in range(1, s_len // blk)]
        return descs

    def scores(desc, slot):
        (q_src, k_src, _), _, dil, q_start, k_start, n_keys, _ = desc
        qt = q_src[_rows(q_start, blk, dil), :]
        zero = jnp.zeros_like(qt)
        q2 = jnp.concatenate([jnp.where(head0, qt, zero), jnp.where(head0, zero, qt)], axis=0).astype(BF16)
        kb = k_src[_rows(k_start, n_keys, dil), :].astype(BF16)
        s_scr[slot, :, :n_keys] = lax.dot_general(q2, kb, (((1,), (1,)), ((), ())), preferred_element_type=F32)

    def softmax(desc, slot):
        _, br, dil, q_start, _, n_keys, mask = desc
        s = jnp.where(mask, s_scr[slot, :, :n_keys], -jnp.inf)
        m = jnp.max(s, axis=-1, keepdims=True)
        p_scr[slot, :, :n_keys] = jnp.exp2(s - m).astype(BF16)
        mb[br, _rows(q_start, blk, dil), :] = jnp.where(head0, m[:blk], m[blk:])

    def values(desc, slot):
        (_, _, v_src), br, dil, q_start, k_start, n_keys, _ = desc
        vb = v_src[_rows(k_start, n_keys, dil), :].astype(BF16)
        v1 = jnp.concatenate([vb, jnp.ones_like(vb)], axis=1)
        r = jnp.dot(p_scr[slot, :, :n_keys], v1, preferred_element_type=F32)
        out_rows = _rows(q_start, blk, dil)
        ob[br, out_rows, :] = jnp.where(head0, r[:blk, :LANES], r[blk:, :LANES])
        lb[br, out_rows, :] = jnp.where(head0, r[:blk, LANES:], r[blk:, LANES:])
        if br == 0:
            merge(q_start)

    def merge(n0):
        sub = blk // dmid
        for r in range(dmid):
            nat_rows = pl.ds(n0 + r, sub, stride=dmid)
            d_rows = slice(r * run + n0 // dmid, r * run + n0 // dmid + sub)
            rows = (nat_rows, d_rows, d_rows)
            ms = [mb[g, rows[g], :] for g in range(len(DILATIONS))]
            m = functools.reduce(jnp.maximum, ms)
            ws = [jnp.exp2(mg - m) for mg in ms]
            num = functools.reduce(jnp.add, [w * ob[g, rows[g], :] for g, w in enumerate(ws)])
            den = functools.reduce(jnp.add, [w * lb[g, rows[g], :] for g, w in enumerate(ws)])
            ob[0, nat_rows, :] = num / den
        o_ref[n0:n0 + blk, :] = ob[0, n0:n0 + blk, :].astype(o_ref.dtype)

    for parity, (cur_set, nxt_set) in enumerate(((set_a, set_b), (set_b, set_a))):
        @pl.when(step % 2 == parity)
        def _(cur_set=cur_set, nxt_set=nxt_set):
            _software_pipeline((scores, softmax, values), block_descs(cur_set), ATTN_GROUP,
                               extras=prepare((qn_ref, kn_ref, vn_ref, cosn_ref, sinn_ref), nxt_set))


def dilated_attention(qkv, tabs, gq2, gk2, weights, bsz, s_len, n_pairs):
    cos_t, sin_t = tabs
    d1, d2, d3 = DILATIONS
    assert d1 == 1 and d3 == d2 * d2 and s_len == d3 * ATTN_BLOCK, "layout assumes dilations (1, d, d*d), one block per largest class"
    lane = jnp.arange(LANES)
    seg = (lane[:, None] // ATTN_HEAD_DIM == lane[None, :] // ATTN_HEAD_DIM).astype(BF16)
    n_steps = bsz * n_pairs

    def nxt(b, p):
        s = jnp.minimum(b * n_pairs + p + 1, n_steps - 1)
        return s // n_pairs, s % n_pairs

    once = dict(pipeline_mode=pl.Buffered(1))
    first = lambda off: pl.BlockSpec((s_len, LANES), lambda b, p, off=off: (0, off), **once)
    nblk = lambda off: pl.BlockSpec((s_len, LANES), lambda b, p, off=off: (nxt(b, p)[0], off + nxt(b, p)[1]))
    tab0 = pl.BlockSpec((s_len, LANES), lambda b, p: (0, 0), **once)
    tabn = pl.BlockSpec((s_len, LANES), lambda b, p: (nxt(b, p)[0], 0))
    row = pl.BlockSpec((1, LANES), lambda b, p: (0, 0))
    sq = pl.BlockSpec((LANES, LANES), lambda b, p: (0, 0))
    w_specs = [pl.BlockSpec((w.shape[0] // n_steps, w.shape[1]), lambda b, p: (b * n_pairs + p, 0)) for w in weights]
    return pl.pallas_call(
        functools.partial(_attn_kernel, len(weights)),
        grid=(bsz, n_pairs),
        in_specs=[first(0), first(n_pairs), first(2 * n_pairs), tab0, tab0,
                  nblk(0), nblk(n_pairs), nblk(2 * n_pairs), tabn, tabn,
                  row, row, sq, sq] + w_specs,
        out_specs=[pl.BlockSpec((s_len, LANES), lambda b, p: (b, p))] + w_specs,
        out_shape=[jax.ShapeDtypeStruct((bsz * s_len, n_pairs * LANES), BF16)]
                  + [jax.ShapeDtypeStruct(w.shape, BF16) for w in weights],
        scratch_shapes=[[pltpu.VMEM((s_len, LANES), F32)] * 6] * 2
                       + [pltpu.VMEM((len(DILATIONS), s_len, LANES), F32)] * 3
                       + [pltpu.VMEM((2 * ATTN_GROUP, 2 * ATTN_BLOCK, 2 * ATTN_BLOCK), F32),
                          pltpu.VMEM((2 * ATTN_GROUP, 2 * ATTN_BLOCK, 2 * ATTN_BLOCK), BF16)],
        compiler_params=_cparams(("arbitrary", "arbitrary")),
        name="dilated_attn",
    )(qkv, qkv, qkv, cos_t, sin_t, qkv, qkv, qkv, cos_t, sin_t, gq2, gk2, seg, _rotate_half_matrix(), *weights)


def _silu(x):
    h = 0.5 * x
    return h * jnp.tanh(h) + h


LOG2E = 1.4426950408889634
SSD_STEP_CHUNKS = 4


def _ssd_kernel(xbc_ref, z_ref, dt_ref, cw_ref, cb_ref, dtb_ref, alog_ref, dskip_ref, g_ref,
                expand_ref, tril_ref, o_ref, xpad, xc, st):
    q = SSD_CHUNK
    n_rows = xbc_ref.shape[0]
    d_ssm = z_ref.shape[1]
    d_conv = xbc_ref.shape[1]
    gw = d_ssm // SSM_GROUPS
    heads_per_group = gw // SSM_HEAD_DIM
    pad = 8

    @pl.when(pl.program_id(1) == 0)
    def _():
        xpad[:, 0:pad, :] = jnp.zeros((d_conv // LANES, pad, LANES), F32)
        st[...] = jnp.zeros_like(st)

    for c0 in range(0, d_conv, LANES):
        cols = slice(c0, c0 + LANES)
        slab = c0 // LANES
        xpad[slab, pad:pad + n_rows, :] = xbc_ref[:, cols].astype(F32)
        acc = cb_ref[:, cols] + cw_ref[CONV_WIDTH - 1:CONV_WIDTH, cols] * xpad[slab, pad:pad + n_rows, :]
        for w in range(CONV_WIDTH - 1):
            off = pad - (CONV_WIDTH - 1) + w
            acc = acc + cw_ref[w:w + 1, cols] * xpad[slab, pl.ds(off, n_rows, stride=1), :]
        xc[:, cols] = _silu(acc)
    xpad[:, 0:pad, :] = xpad[:, n_rows:n_rows + pad, :]

    li = lax.broadcasted_iota(jnp.int32, (q, q), 0)
    si = lax.broadcasted_iota(jnp.int32, (q, q), 1)
    causal = li >= si
    lane_g = lax.broadcasted_iota(jnp.int32, (q, gw), 1)
    tril = tril_ref[...]
    expand = expand_ref[...]

    for r0 in range(0, n_rows, q):
        rows = slice(r0, r0 + q)
        x_dt = dt_ref[rows, :] + dtb_ref[...]
        dt = jnp.maximum(x_dt, 0.0) + jnp.log1p(jnp.exp(-jnp.abs(x_dt)))
        dta = dt * (-LOG2E * jnp.exp(alog_ref[...]))
        acs = sum(jnp.dot(tril, part, preferred_element_type=F32) for part in _split3(dta))
        acs_t = acs.T
        last = acs[q - 1:q, :]
        dt_e = jnp.dot(dt.astype(BF16), expand, preferred_element_type=F32)
        dec_e = jnp.dot(jnp.exp2(acs).astype(BF16), expand, preferred_element_type=F32)
        w_e = jnp.dot((jnp.exp2(last - acs) * dt).astype(BF16), expand, preferred_element_type=F32)

        for g in range(SSM_GROUPS):
            cols = slice(g * gw, (g + 1) * gw)
            b_f = xc[rows, d_ssm + g * SSM_STATE:d_ssm + (g + 1) * SSM_STATE]
            c_b = xc[rows, d_ssm + (SSM_GROUPS + g) * SSM_STATE:d_ssm + (SSM_GROUPS + g + 1) * SSM_STATE].astype(BF16)
            xs = xc[rows, cols]
            cb = lax.dot_general(c_b, b_f.astype(BF16), (((1,), (1,)), ((), ())), preferred_element_type=F32)
            xdt = (xs * dt_e[:, cols]).astype(BF16)
            ws, rs = [], []
            for hh in range(heads_per_group):
                h = g * heads_per_group + hh
                seg = acs[:, h:h + 1] - acs_t[h:h + 1, :]
                l_mat = jnp.exp2(jnp.where(causal, seg, -jnp.inf))
                ws.append((cb * l_mat).astype(BF16))
                in_head = (lane_g >= hh * SSM_HEAD_DIM) & (lane_g < (hh + 1) * SSM_HEAD_DIM)
                rs.append(jnp.where(in_head, xdt, jnp.zeros_like(xdt)))
            y = jnp.dot(jnp.concatenate(ws, axis=1), jnp.concatenate(rs, axis=0), preferred_element_type=F32)
            st_g = st[:, cols]
            y = y + jnp.dot(c_b, st_g.astype(BF16), preferred_element_type=F32) * dec_e[:, cols]
            xw = (xs * w_e[:, cols]).astype(BF16)
            st[:, cols] = st_g * dec_e[q - 1:q, cols] + jnp.dot(b_f.T.astype(BF16), xw, preferred_element_type=F32)
            y = y + dskip_ref[:, cols] * xs
            y = y * _silu(z_ref[rows, cols].astype(F32))
            o_ref[rows, cols] = _rms_rows(y, g_ref[:, cols]).astype(o_ref.dtype)


def ssd_mixer(packed, dt_raw, conv_w, conv_b, dt_bias, a_log, d_skip, g_out, bsz, s_len, d_ssm, xbc_col, z_col):
    n_heads = d_ssm // SSM_HEAD_DIM
    d_conv = d_ssm + 2 * SSM_GROUPS * SSM_STATE
    step_rows = SSD_STEP_CHUNKS * SSD_CHUNK
    nc = s_len // step_rows
    padl = lambda v: jnp.pad(v.astype(F32), (0, LANES - n_heads))[None, :]
    expand = (jnp.arange(LANES)[:, None] == (jnp.arange(d_ssm) // SSM_HEAD_DIM)[None, :]).astype(BF16)
    tril = (jnp.arange(SSD_CHUNK)[:, None] >= jnp.arange(SSD_CHUNK)[None, :]).astype(BF16)
    full = lambda shape: pl.BlockSpec(shape, lambda b, c: (0, 0))
    return pl.pallas_call(
        _ssd_kernel,
        grid=(bsz, nc),
        in_specs=[pl.BlockSpec((step_rows, d_conv), lambda b, c: (b * nc + c, xbc_col // d_conv)),
                  pl.BlockSpec((step_rows, d_ssm), lambda b, c: (b * nc + c, z_col // d_ssm)),
                  pl.BlockSpec((step_rows, LANES), lambda b, c: (b * nc + c, 0)),
                  full((CONV_WIDTH, d_conv)), full((1, d_conv)), full((1, LANES)), full((1, LANES)),
                  full((1, d_ssm)), full((1, d_ssm)), full((LANES, d_ssm)), full((SSD_CHUNK, SSD_CHUNK))],
        out_specs=pl.BlockSpec((step_rows, d_ssm), lambda b, c: (b * nc + c, 0)),
        out_shape=jax.ShapeDtypeStruct((bsz * s_len, d_ssm), BF16),
        scratch_shapes=[pltpu.VMEM((d_conv // LANES, step_rows + 8, LANES), F32),
                        pltpu.VMEM((step_rows, d_conv), F32),
                        pltpu.VMEM((SSM_STATE, d_ssm), F32)],
        compiler_params=_cparams(("parallel", "arbitrary")),
        name="ssd",
    )(packed, packed, dt_raw, conv_w.astype(F32), conv_b.astype(F32)[None, :], padl(dt_bias), padl(a_log),
      jnp.repeat(d_skip.astype(F32), SSM_HEAD_DIM)[None, :], g_out.astype(F32)[None, :], expand, tril)


def _mem_kv_rope_kernel(mem_ref, g_ref, w_ref, gk_ref, pos_ref, expo_ref, spread_ref, cmask_ref,
                        kv_ref, cos_ref, sin_ref, h_ref):
    d_cross = N_CROSS_HEADS * CROSS_HEAD_DIM
    part = 256
    n_parts = mem_ref.shape[0] // part
    rope_rows = pos_ref.shape[0] // n_parts
    for i in range(n_parts):
        r0 = i * part
        rows = slice(r0, r0 + part)
        _norm_chunk_to(mem_ref, g_ref, h_ref, r0, part)
        kv = jnp.dot(h_ref[rows, :], w_ref[...], preferred_element_type=F32)
        for h in range(N_CROSS_HEADS):
            cols = slice(h * CROSS_HEAD_DIM, (h + 1) * CROSS_HEAD_DIM)
            kv_ref[rows, cols] = _rms_rows(kv[:, cols], gk_ref[...]).astype(kv_ref.dtype)
        kv_ref[rows, d_cross:] = kv[:, d_cross:].astype(kv_ref.dtype)
        _rope_rows(pos_ref, expo_ref, spread_ref, cmask_ref, cos_ref, sin_ref, i * rope_rows, rope_rows)


def mem_kv_rope(mem2d, g, w, gk, positions):
    m, d = mem2d.shape
    n = w.shape[1]
    n_tok = positions.size
    pos_packed, expo, spread, cmask = rope_operands(positions)
    assert m % 256 == 0 and pos_packed.shape[0] % (m // 256) == 0
    full2 = lambda shape: pl.BlockSpec(shape, lambda i: (0, 0))
    return pl.pallas_call(
        _mem_kv_rope_kernel,
        grid=(1,),
        in_specs=[full2((m, d)), full2((1, d)), full2((d, n)), full2((1, CROSS_HEAD_DIM)),
                  full2(pos_packed.shape), full2((1, LANES)),
                  pl.BlockSpec((ROPE_PACK, LANES, LANES), lambda i: (0, 0, 0)), full2((1, LANES))],
        out_specs=[full2((m, n)), full2((n_tok, LANES)), full2((n_tok, LANES))],
        out_shape=[jax.ShapeDtypeStruct((m, n), BF16)] + [jax.ShapeDtypeStruct((n_tok, LANES), F32)] * 2,
        scratch_shapes=[pltpu.VMEM((m, d), BF16)],
        compiler_params=_cparams(("arbitrary",)),
        name="mem_kv_rope",
    )(mem2d, g, w, gk, pos_packed, expo, spread, cmask)


def _mix_cross_kernel(attn_ref, ssm_ref, x_ref, ga_ref, w_ref, gc_ref, wq_ref, gq_ref, k_ref, v_ref, wo_ref, o_ref,
                      h_ref, xk_ref, an_ref, q_ref, a_ref, s_scr, p_scr):
    s = pl.program_id(0)
    n_tiles = pl.num_programs(0) - 1
    tm, d_attn = x_ref.shape[0], attn_ref.shape[1]
    rows = CROSS_ROWS
    prev, cur = (s + 1) % 2, s % 2

    @pl.when(s > 0)
    def _():
        q_ref[...] = jnp.dot(h_ref[prev], wq_ref[...], preferred_element_type=F32)

        def scores(item, slot):
            r0, h = item
            cols = slice(h * CROSS_HEAD_DIM, (h + 1) * CROSS_HEAD_DIM)
            qh = (_rms_rows(q_ref[r0:r0 + rows, cols], gq_ref[...]) * (CROSS_HEAD_DIM ** -0.5)).astype(BF16)
            s_scr[slot] = lax.dot_general(qh, k_ref[:, cols], (((1,), (1,)), ((), ())), preferred_element_type=F32)

        def softmax(item, slot):
            sc = s_scr[slot]
            p_scr[slot] = jnp.exp(sc - jnp.max(sc, axis=-1, keepdims=True)).astype(BF16)

        def values(item, slot):
            r0, h = item
            cols = slice(h * CROSS_HEAD_DIM, (h + 1) * CROSS_HEAD_DIM)
            vh = v_ref[:, cols]
            r = jnp.dot(p_scr[slot], jnp.concatenate([vh, jnp.ones_like(vh)], axis=1), preferred_element_type=F32)
            a_ref[r0:r0 + rows, cols] = (r[:, :CROSS_HEAD_DIM] / r[:, CROSS_HEAD_DIM:]).astype(BF16)

        items = [(r0, h) for r0 in range(0, tm, rows) for h in range(N_CROSS_HEADS)]
        _software_pipeline((scores, softmax, values), items, CROSS_GROUP)
        o_ref[...] = xk_ref[prev] + jnp.dot(a_ref[...], wo_ref[...], preferred_element_type=F32)

    @pl.when(s < n_tiles)
    def _():
        n_split = 2
        part = tm // n_split
        for r0 in range(0, tm, part):
            for r in range(r0, r0 + part, 32):
                an_ref[r:r + 32, :] = _rms_rows(attn_ref[r:r + 32, :].astype(F32), ga_ref[...]).astype(BF16)
            rs = slice(r0, r0 + part)
            xk_ref[cur, rs, :] = (x_ref[rs, :] + jnp.dot(an_ref[rs, :], w_ref[:d_attn, :], preferred_element_type=F32)
                                  + jnp.dot(ssm_ref[rs, :], w_ref[d_attn:, :], preferred_element_type=F32))
            _norm_chunk_to(xk_ref.at[cur], gc_ref, h_ref.at[cur], r0, part)


def mix_cross(attn, ssm, x2d, ga, w, gc, wq, gq, kv, wo, s_len, n_mem, tm=512):
    m, d = x2d.shape
    da, ds_ = attn.shape[1], ssm.shape[1]
    dc = wq.shape[1]
    n_tiles = m // tm
    per_batch = s_len // tm
    ahead = lambda s: (jnp.minimum(s, n_tiles - 1), 0)
    behind = lambda s: jnp.maximum(s - 1, 0)
    full = lambda shape, **kw: pl.BlockSpec(shape, lambda s: (0, 0), **kw)
    once = dict(pipeline_mode=pl.Buffered(1))
    return pl.pallas_call(
        _mix_cross_kernel,
        grid=(n_tiles + 1,),
        in_specs=[pl.BlockSpec((tm, da), ahead), pl.BlockSpec((tm, ds_), ahead), pl.BlockSpec((tm, d), ahead),
                  full((1, da)), full((da + ds_, d), **once), full((1, d)), full((d, dc), **once),
                  full((1, CROSS_HEAD_DIM)),
                  pl.BlockSpec((n_mem, dc), lambda s: (behind(s) // per_batch, 0)),
                  pl.BlockSpec((n_mem, dc), lambda s: (behind(s) // per_batch, 1)),
                  full((dc, d), **once)],
        out_specs=pl.BlockSpec((tm, d), lambda s: (behind(s), 0)),
        out_shape=jax.ShapeDtypeStruct((m, d), F32),
        scratch_shapes=[pltpu.VMEM((2, tm, d), BF16), pltpu.VMEM((2, tm, d), F32), pltpu.VMEM((tm, da), BF16),
                        pltpu.VMEM((tm, dc), F32), pltpu.VMEM((tm, dc), BF16),
                        pltpu.VMEM((2 * CROSS_GROUP, CROSS_ROWS, n_mem), F32),
                        pltpu.VMEM((2 * CROSS_GROUP, CROSS_ROWS, n_mem), BF16)],
        compiler_params=_cparams(("arbitrary",)),
        name="mix_cross",
    )(attn, ssm, x2d, ga, w, gc, wq, gq, kv, kv, wo)


def _mlp_kernel(x_ref, g_ref, wu_ref, wd_ref, o_ref, h_ref):
    tm = x_ref.shape[0]

    def expand(rows):
        u = jnp.dot(h_ref[rows, :], wu_ref[...], preferred_element_type=F32)
        u = jnp.square(jnp.maximum(u, 0.0)).astype(BF16)
        return jnp.dot(u, wd_ref[...], preferred_element_type=F32)

    @pl.when(pl.program_id(1) == 0)
    def _():
        part = tm // 2
        for r0 in range(0, tm, part):
            _norm_chunk_to(x_ref, g_ref, h_ref, r0, part)
            rows = slice(r0, r0 + part)
            o_ref[rows, :] = x_ref[rows, :] + expand(rows)

    @pl.when(pl.program_id(1) > 0)
    def _():
        o_ref[...] += expand(slice(None))


def mlp(x2d, g, wu, wd, tm=512, tf=2048):
    m, d = x2d.shape
    f = wu.shape[1]
    return pl.pallas_call(
        _mlp_kernel,
        grid=(m // tm, f // tf),
        in_specs=[pl.BlockSpec((tm, d), lambda i, j: (i, 0)),
                  pl.BlockSpec((1, d), lambda i, j: (0, 0)),
                  pl.BlockSpec((d, tf), lambda i, j: (0, j)),
                  pl.BlockSpec((tf, d), lambda i, j: (j, 0))],
        out_specs=pl.BlockSpec((tm, d), lambda i, j: (i, 0)),
        out_shape=jax.ShapeDtypeStruct((m, d), F32),
        scratch_shapes=[pltpu.VMEM((tm, d), BF16)],
        compiler_params=_cparams(("parallel", "arbitrary")),
        name="mlp",
    )(x2d, g, wu, wd)


def _layer(x2d, mem2d, positions, bsz, s_len, n_mem, g_mix, w_in, g_q, g_k, g_attn_out, conv_w, conv_b, dt_bias,
           a_log, d_skip, g_ssm_out, w_out, g_cross, g_mem, w_cq, w_ckv, g_cq, g_ck, w_co, g_mlp, w_up, w_down):
    d_model = x2d.shape[1]
    d_attn = d_model // 2
    d_ssm = d_model // 2
    n_pairs = d_attn // LANES
    n_ssm_heads = d_ssm // SSM_HEAD_DIM
    d_conv = d_ssm + 2 * SSM_GROUPS * SSM_STATE
    d_packed = 3 * d_attn + d_ssm + d_conv
    row = lambda v: v.astype(F32)[None, :]

    w_in_t = w_in.T
    w_dt_t = jnp.pad(w_in_t[d_packed:], ((0, LANES - n_ssm_heads), (0, 0)))
    packed, dt_raw = in_proj(x2d, row(g_mix), w_in_t, w_dt_t, d_packed)

    kv, cos_t, sin_t = mem_kv_rope(mem2d, row(g_mem), w_ckv.astype(BF16), row(g_ck), positions)
    tabs = (cos_t, sin_t)
    attn, w_up_bf, w_down_bf, w_out_bf, w_cq_bf, w_co_bf = dilated_attention(
        packed, tabs, row(jnp.tile(g_q, 2)) * (ATTN_HEAD_DIM ** -0.5 * LOG2E), row(jnp.tile(g_k, 2)),
        (w_up, w_down, w_out, w_cq, w_co), bsz, s_len, n_pairs)
    ssm = ssd_mixer(packed, dt_raw, conv_w, conv_b, dt_bias, a_log, d_skip, g_ssm_out, bsz, s_len, d_ssm,
                    xbc_col=3 * d_attn + d_ssm, z_col=3 * d_attn)
    x2d = mix_cross(attn, ssm, x2d, row(g_attn_out), w_out_bf, row(g_cross), w_cq_bf, row(g_cq), kv, w_co_bf,
                    s_len, n_mem)

    return mlp(x2d, row(g_mlp), w_up_bf, w_down_bf)


def kernel(x, mem, positions, g_mix, w_in, g_q, g_k, g_attn_out, conv_w, conv_b, dt_bias, a_log, d_skip, g_ssm_out,
           w_out, g_cross, g_mem, w_cq, w_ckv, g_cq, g_ck, w_co, g_mlp, w_up, w_down):
    bsz, s_len, d_model = x.shape
    n_mem = mem.shape[1]
    x2d = x.reshape(bsz * s_len, d_model)
    mem2d = mem.reshape(bsz * n_mem, d_model)
    for i in range(g_mix.shape[0]):
        x2d = _layer(x2d, mem2d, positions, bsz, s_len, n_mem, g_mix[i], w_in[i], g_q[i], g_k[i], g_attn_out[i],
                     conv_w[i], conv_b[i], dt_bias[i], a_log[i], d_skip[i], g_ssm_out[i], w_out[i], g_cross[i],
                     g_mem[i], w_cq[i], w_ckv[i], g_cq[i], g_ck[i], w_co[i], g_mlp[i], w_up[i], w_down[i])
    return x2d.reshape(bsz, s_len, d_model)
```

```python
import functools
import math

import jax
import jax.numpy as jnp
from jax import lax
from jax.experimental import pallas as pl
from jax.experimental.pallas import tpu as pltpu

F32 = jnp.float32
BF16 = jnp.bfloat16
EPS = 1e-6

LANES = 128
ATTN_HEAD_DIM = 64
ROT_DIM = ATTN_HEAD_DIM // 4
ROPE_THETA = 500000.0
ATTN_BLOCK = 128
DILATIONS = (1, 4, 16)
SSM_HEAD_DIM = 64
SSM_GROUPS = 4
SSM_STATE = 128
CONV_WIDTH = 4
SSD_CHUNK = 128
N_CROSS_HEADS = 4
CROSS_HEAD_DIM = 128
CROSS_ROWS = 256
CROSS_GROUP = 2
VMEM_LIMIT = 58 * 1024 * 1024


def _cparams(sem):
    return pltpu.CompilerParams(dimension_semantics=sem, vmem_limit_bytes=VMEM_LIMIT)


def _rms_rows(x, g):
    ms = jnp.mean(x * x, axis=-1, keepdims=True)
    return x * lax.rsqrt(ms + EPS) * g


def _split3(x):
    hi = x.astype(BF16)
    r1 = x - hi.astype(F32)
    mid = r1.astype(BF16)
    lo = (r1 - mid.astype(F32)).astype(BF16)
    return hi, mid, lo


def _norm_rows_to(x_ref, g_ref, h_ref, rows=16):
    def body(c, carry):
        r0 = pl.multiple_of(c * rows, rows)
        h_ref[pl.ds(r0, rows), :] = _rms_rows(x_ref[pl.ds(r0, rows), :], g_ref[...]).astype(h_ref.dtype)
        return carry
    lax.fori_loop(0, x_ref.shape[0] // rows, body, 0, unroll=4)


def _norm_chunk_to(x_ref, g_ref, h_ref, start, n_rows, rows=16):
    for r in range(0, n_rows, rows):
        rs = pl.ds(start + r, rows)
        h_ref[rs, :] = _rms_rows(x_ref[rs, :], g_ref[...]).astype(h_ref.dtype)


def _in_proj_kernel(x0_ref, xn_ref, g_ref, w_ref, wdt_ref, o_ref, dt_ref, h_ref):
    i, j = pl.program_id(0), pl.program_id(1)
    cur, nxt = i % 2, (i + 1) % 2

    @pl.when((i == 0) & (j == 0))
    def _():
        _norm_rows_to(x0_ref, g_ref, h_ref.at[0])

    @pl.when(j == 0)
    def _():
        dt_ref[...] = lax.dot_general(h_ref[cur], wdt_ref[...].astype(BF16), (((1,), (1,)), ((), ())),
                                      preferred_element_type=F32)
    o_ref[...] = lax.dot_general(h_ref[cur], w_ref[...].astype(BF16), (((1,), (1,)), ((), ())),
                                 preferred_element_type=F32).astype(o_ref.dtype)
    chunk = xn_ref.shape[0]
    start = pl.multiple_of(jnp.minimum(j, IN_PROJ_NORM_STEPS - 1) * chunk, chunk)
    for r in range(0, chunk, 16):
        h_ref[nxt, pl.ds(start + r, 16), :] = _rms_rows(xn_ref[r:r + 16, :], g_ref[...]).astype(BF16)


IN_PROJ_NORM_STEPS = 4


def in_proj(x2d, g, w_t, w_dt, n, tm=1024, tn=1024):
    m, d = x2d.shape
    n_i = m // tm
    assert n // tn >= IN_PROJ_NORM_STEPS
    return pl.pallas_call(
        _in_proj_kernel,
        grid=(n_i, n // tn),
        in_specs=[pl.BlockSpec((tm, d), lambda i, j: (0, 0), pipeline_mode=pl.Buffered(1)),
                  pl.BlockSpec((tm // IN_PROJ_NORM_STEPS, d),
                               lambda i, j: (jnp.minimum(i + 1, n_i - 1) * IN_PROJ_NORM_STEPS
                                             + jnp.minimum(j, IN_PROJ_NORM_STEPS - 1), 0)),
                  pl.BlockSpec((1, d), lambda i, j: (0, 0)),
                  pl.BlockSpec((tn, d), lambda i, j: (j, 0)),
                  pl.BlockSpec((LANES, d), lambda i, j: (0, 0), pipeline_mode=pl.Buffered(1))],
        out_specs=[pl.BlockSpec((tm, tn), lambda i, j: (i, j)),
                   pl.BlockSpec((tm, LANES), lambda i, j: (i, 0))],
        out_shape=[jax.ShapeDtypeStruct((m, n), BF16), jax.ShapeDtypeStruct((m, LANES), F32)],
        scratch_shapes=[pltpu.VMEM((2, tm, d), BF16)],
        compiler_params=_cparams(("arbitrary", "arbitrary")),
        name="in_proj",
    )(x2d, x2d, g, w_t, w_dt)


ROPE_PACK = LANES // ROT_DIM


def _rope_rows(pos_ref, expo_ref, spread_ref, cmask_ref, cos_ref, sin_ref, p0, n_rows):
    inv_freq = jnp.power(jnp.float32(ROPE_THETA), expo_ref[...])
    ang = pos_ref[p0:p0 + n_rows, :].astype(F32) * inv_freq
    packed = (jnp.cos(ang), jnp.sin(ang))
    fill = (1.0 - cmask_ref[...], jnp.zeros_like(cmask_ref[...]))
    for out_ref, tab, off in zip((cos_ref, sin_ref), packed, fill):
        parts = _split3(tab)
        for k in range(ROPE_PACK):
            rows = sum(jnp.dot(part, spread_ref[k], preferred_element_type=F32) for part in parts)
            out_ref[pl.ds(ROPE_PACK * p0 + k, n_rows, stride=ROPE_PACK), :] = rows + off


def rope_operands(positions):
    m = positions.size
    half = ROT_DIM // 2
    lane = jnp.arange(LANES)
    expo = (-2.0 * ((lane % ROT_DIM) % half).astype(F32) / ROT_DIM)[None, :]
    cmask = ((lane % ATTN_HEAD_DIM) < ROT_DIM).astype(F32)[None, :]
    d_dst = lane % ATTN_HEAD_DIM
    spread = ((d_dst[None, None, :] < ROT_DIM)
              & (lane[None, :, None] == ROT_DIM * jnp.arange(ROPE_PACK)[:, None, None] + d_dst[None, None, :])).astype(BF16)
    pos_packed = jnp.repeat(positions.reshape(m // ROPE_PACK, ROPE_PACK), ROT_DIM, axis=1)
    return pos_packed, expo, spread, cmask


def _rotate_half_matrix():
    half = ROT_DIM // 2
    src = jnp.arange(LANES)[:, None]
    dst = jnp.arange(LANES)[None, :]
    d = dst % ATTN_HEAD_DIM
    first = (d < half) & (src == dst + half)
    second = (d >= half) & (d < ROT_DIM) & (src == dst - half)
    return (second.astype(F32) - first.astype(F32)).astype(BF16)


def _rows(start, size, dil):
    return pl.ds(start, size) if dil == 1 else pl.ds(start, size, stride=dil)


ATTN_GROUP = 2


def _software_pipeline(stages, items, group, extras=()):
    groups = [items[i:i + group] for i in range(0, len(items), group)]
    n_trips = len(groups) + len(stages) - 1
    for t in range(n_trips):
        for lag, stage in enumerate(stages):
            if 0 <= t - lag < len(groups):
                for g, item in enumerate(groups[t - lag]):
                    stage(item, ((t - lag) % 2) * group + g)
        for k, extra in enumerate(extras):
            if k * n_trips // len(extras) == t:
                extra()


def _attn_kernel(n_cast, q0_ref, k0_ref, v0_ref, cos0_ref, sin0_ref, qn_ref, kn_ref, vn_ref, cosn_ref, sinn_ref,
                 gq_ref, gk_ref, seg_ref, rot_ref, *refs):
    cast_in, o_ref, cast_out = refs[:n_cast], refs[n_cast], refs[n_cast + 1:2 * n_cast + 1]
    set_a, set_b, ob, mb, lb, s_scr, p_scr = refs[2 * n_cast + 1:]
    s_len = qn_ref.shape[0]
    blk = ATTN_BLOCK
    step = pl.program_id(0) * pl.num_programs(1) + pl.program_id(1)
    lane = lax.broadcasted_iota(jnp.int32, (blk, LANES), 1)
    head0 = lane < ATTN_HEAD_DIM
    qi = lax.broadcasted_iota(jnp.int32, (2 * blk, 2 * blk), 0) % blk
    kj = lax.broadcasted_iota(jnp.int32, (2 * blk, 2 * blk), 1)
    band_mask = (kj >= qi) & (kj <= qi + blk)
    first_mask = (lax.broadcasted_iota(jnp.int32, (2 * blk, blk), 1)
                  <= lax.broadcasted_iota(jnp.int32, (2 * blk, blk), 0) % blk)

    for w_ref, w_out in zip(cast_in, cast_out):
        w_out[...] = w_ref[...].astype(BF16)

    prep_rows = 256

    def prep(srcs, dst, r0):
        q_ref, k_ref, v_ref, cos_ref, sin_ref = srcs
        qf, kf, vf = dst[:3]
        rows = slice(r0, r0 + prep_rows)
        cs, sn = cos_ref[rows, :], sin_ref[rows, :]

        def norm_rope(x_ref, g_ref):
            x = x_ref[rows, :].astype(F32)
            ss = jnp.dot((x * x).astype(BF16), seg_ref[...], preferred_element_type=F32)
            y = x * lax.rsqrt(ss * (1.0 / ATTN_HEAD_DIM) + EPS) * g_ref[...]
            return y * cs + jnp.dot(y.astype(BF16), rot_ref[...], preferred_element_type=F32) * sn

        qf[rows, :] = norm_rope(q_ref, gq_ref)
        kf[rows, :] = norm_rope(k_ref, gk_ref)
        vf[rows, :] = v_ref[rows, :].astype(F32)

    dmid = DILATIONS[1]
    run = s_len // dmid

    def deinterleave(bufs, r):
        for src, dst in zip(bufs[:3], bufs[3:]):
            dst[r * run:(r + 1) * run, :] = src[pl.ds(r, run, stride=dmid), :]

    def prepare(srcs, bufs):
        return ([functools.partial(prep, srcs, bufs, r0) for r0 in range(0, s_len, prep_rows)]
                + [functools.partial(deinterleave, bufs, r) for r in range(dmid)])

    @pl.when(step == 0)
    def _():
        for piece in prepare((q0_ref, k0_ref, v0_ref, cos0_ref, sin0_ref), set_a):
            piece()

    def block_descs(bufs):
        natural, deint = bufs[:3], bufs[3:]
        descs = [(deint, 1, 1, r * run, r * run, blk, first_mask) for r in range(dmid)]
        descs += [(deint, 1, 1, r * run + n * blk, r * run + (n - 1) * blk, 2 * blk, band_mask)
                  for n in range(1, run // blk) for r in range(dmid)]
        descs += [(deint, 2, dmid, r_hi * run + r_lo, r_hi * run + r_lo, blk, first_mask)
                  for r_hi in range(dmid) for r_lo in range(dmid)]
        descs += [(natural, 0, 1, 0, 0, blk, first_mask)]
        descs += [(natural, 0, 1, n * blk, (n - 1) * blk, 2 * blk, band_mask) for n in range(1, s_len // blk)]
        return descs

    def scores(desc, slot):
        (q_src, k_src, _), _, dil, q_start, k_start, n_keys, _ = desc
        qt = q_src[_rows(q_start, blk, dil), :]
        zero = jnp.zeros_like(qt)
        q2 = jnp.concatenate([jnp.where(head0, qt, zero), jnp.where(head0, zero, qt)], axis=0).astype(BF16)
        kb = k_src[_rows(k_start, n_keys, dil), :].astype(BF16)
        s_scr[slot, :, :n_keys] = lax.dot_general(q2, kb, (((1,), (1,)), ((), ())), preferred_element_type=F32)

    def softmax(desc, slot):
        _, br, dil, q_start, _, n_keys, mask = desc
        s = jnp.where(mask, s_scr[slot, :, :n_keys], -jnp.inf)
        m = jnp.max(s, axis=-1, keepdims=True)
        p_scr[slot, :, :n_keys] = jnp.exp2(s - m).astype(BF16)
        mb[br, _rows(q_start, blk, dil), :] = jnp.where(head0, m[:blk], m[blk:])

    def values(desc, slot):
        (_, _, v_src), br, dil, q_start, k_start, n_keys, _ = desc
        vb = v_src[_rows(k_start, n_keys, dil), :].astype(BF16)
        v1 = jnp.concatenate([vb, jnp.ones_like(vb)], axis=1)
        r = jnp.dot(p_scr[slot, :, :n_keys], v1, preferred_element_type=F32)
        out_rows = _rows(q_start, blk, dil)
        ob[br, out_rows, :] = jnp.where(head0, r[:blk, :LANES], r[blk:, :LANES])
        lb[br, out_rows, :] = jnp.where(head0, r[:blk, LANES:], r[blk:, LANES:])
        if br == 0:
            merge(q_start)

    def merge(n0):
        sub = blk // dmid
        for r in range(dmid):
            nat_rows = pl.ds(n0 + r, sub, stride=dmid)
            d_rows = slice(r * run + n0 // dmid, r * run + n0 // dmid + sub)
            rows = (nat_rows, d_rows, d_rows)
            ms = [mb[g, rows[g], :] for g in range(len(DILATIONS))]
            m = functools.reduce(jnp.maximum, ms)
            ws = [jnp.exp2(mg - m) for mg in ms]
            num = functools.reduce(jnp.add, [w * ob[g, rows[g], :] for g, w in enumerate(ws)])
            den = functools.reduce(jnp.add, [w * lb[g, rows[g], :] for g, w in enumerate(ws)])
            ob[0, nat_rows, :] = num / den
        o_ref[n0:n0 + blk, :] = ob[0, n0:n0 + blk, :].astype(o_ref.dtype)

    for parity, (cur_set, nxt_set) in enumerate(((set_a, set_b), (set_b, set_a))):
        @pl.when(step % 2 == parity)
        def _(cur_set=cur_set, nxt_set=nxt_set):
            _software_pipeline((scores, softmax, values), block_descs(cur_set), ATTN_GROUP,
                               extras=prepare((qn_ref, kn_ref, vn_ref, cosn_ref, sinn_ref), nxt_set))


def dilated_attention(qkv, tabs, gq2, gk2, weights, bsz, s_len, n_pairs):
    cos_t, sin_t = tabs
    d1, d2, d3 = DILATIONS
    assert d1 == 1 and d3 == d2 * d2 and s_len == d3 * ATTN_BLOCK, "layout assumes dilations (1, d, d*d), one block per largest class"
    lane = jnp.arange(LANES)
    seg = (lane[:, None] // ATTN_HEAD_DIM == lane[None, :] // ATTN_HEAD_DIM).astype(BF16)
    n_steps = bsz * n_pairs

    def nxt(b, p):
        s = jnp.minimum(b * n_pairs + p + 1, n_steps - 1)
        return s // n_pairs, s % n_pairs

    once = dict(pipeline_mode=pl.Buffered(1))
    first = lambda off: pl.BlockSpec((s_len, LANES), lambda b, p, off=off: (0, off), **once)
    nblk = lambda off: pl.BlockSpec((s_len, LANES), lambda b, p, off=off: (nxt(b, p)[0], off + nxt(b, p)[1]))
    tab0 = pl.BlockSpec((s_len, LANES), lambda b, p: (0, 0), **once)
    tabn = pl.BlockSpec((s_len, LANES), lambda b, p: (nxt(b, p)[0], 0))
    row = pl.BlockSpec((1, LANES), lambda b, p: (0, 0))
    sq = pl.BlockSpec((LANES, LANES), lambda b, p: (0, 0))
    w_specs = [pl.BlockSpec((w.shape[0] // n_steps, w.shape[1]), lambda b, p: (b * n_pairs + p, 0)) for w in weights]
    return pl.pallas_call(
        functools.partial(_attn_kernel, len(weights)),
        grid=(bsz, n_pairs),
        in_specs=[first(0), first(n_pairs), first(2 * n_pairs), tab0, tab0,
                  nblk(0), nblk(n_pairs), nblk(2 * n_pairs), tabn, tabn,
                  row, row, sq, sq] + w_specs,
        out_specs=[pl.BlockSpec((s_len, LANES), lambda b, p: (b, p))] + w_specs,
        out_shape=[jax.ShapeDtypeStruct((bsz * s_len, n_pairs * LANES), BF16)]
                  + [jax.ShapeDtypeStruct(w.shape, BF16) for w in weights],
        scratch_shapes=[[pltpu.VMEM((s_len, LANES), F32)] * 6] * 2
                       + [pltpu.VMEM((len(DILATIONS), s_len, LANES), F32)] * 3
                       + [pltpu.VMEM((2 * ATTN_GROUP, 2 * ATTN_BLOCK, 2 * ATTN_BLOCK), F32),
                          pltpu.VMEM((2 * ATTN_GROUP, 2 * ATTN_BLOCK, 2 * ATTN_BLOCK), BF16)],
        compiler_params=_cparams(("arbitrary", "arbitrary")),
        name="dilated_attn",
    )(qkv, qkv, qkv, cos_t, sin_t, qkv, qkv, qkv, cos_t, sin_t, gq2, gk2, seg, _rotate_half_matrix(), *weights)


def _silu(x):
    h = 0.5 * x
    return h * jnp.tanh(h) + h


LOG2E = 1.4426950408889634
SSD_STEP_CHUNKS = 4


def _ssd_kernel(xbc_ref, z_ref, dt_ref, cw_ref, cb_ref, dtb_ref, alog_ref, dskip_ref, g_ref,
                expand_ref, tril_ref, o_ref, xpad, xc, st):
    q = SSD_CHUNK
    n_rows = xbc_ref.shape[0]
    d_ssm = z_ref.shape[1]
    d_conv = xbc_ref.shape[1]
    gw = d_ssm // SSM_GROUPS
    heads_per_group = gw // SSM_HEAD_DIM
    pad = 8

    @pl.when(pl.program_id(1) == 0)
    def _():
        xpad[:, 0:pad, :] = jnp.zeros((d_conv // LANES, pad, LANES), F32)
        st[...] = jnp.zeros_like(st)

    for c0 in range(0, d_conv, LANES):
        cols = slice(c0, c0 + LANES)
        slab = c0 // LANES
        xpad[slab, pad:pad + n_rows, :] = xbc_ref[:, cols].astype(F32)
        acc = cb_ref[:, cols] + cw_ref[CONV_WIDTH - 1:CONV_WIDTH, cols] * xpad[slab, pad:pad + n_rows, :]
        for w in range(CONV_WIDTH - 1):
            off = pad - (CONV_WIDTH - 1) + w
            acc = acc + cw_ref[w:w + 1, cols] * xpad[slab, pl.ds(off, n_rows, stride=1), :]
        xc[:, cols] = _silu(acc)
    xpad[:, 0:pad, :] = xpad[:, n_rows:n_rows + pad, :]

    li = lax.broadcasted_iota(jnp.int32, (q, q), 0)
    si = lax.broadcasted_iota(jnp.int32, (q, q), 1)
    causal = li >= si
    lane_g = lax.broadcasted_iota(jnp.int32, (q, gw), 1)
    tril = tril_ref[...]
    expand = expand_ref[...]

    for r0 in range(0, n_rows, q):
        rows = slice(r0, r0 + q)
        x_dt = dt_ref[rows, :] + dtb_ref[...]
        dt = jnp.maximum(x_dt, 0.0) + jnp.log1p(jnp.exp(-jnp.abs(x_dt)))
        dta = dt * (-LOG2E * jnp.exp(alog_ref[...]))
        acs = sum(jnp.dot(tril, part, preferred_element_type=F32) for part in _split3(dta))
        acs_t = acs.T
        last = acs[q - 1:q, :]
        dt_e = jnp.dot(dt.astype(BF16), expand, preferred_element_type=F32)
        dec_e = jnp.dot(jnp.exp2(acs).astype(BF16), expand, preferred_element_type=F32)
        w_e = jnp.dot((jnp.exp2(last - acs) * dt).astype(BF16), expand, preferred_element_type=F32)

        for g in range(SSM_GROUPS):
            cols = slice(g * gw, (g + 1) * gw)
            b_f = xc[rows, d_ssm + g * SSM_STATE:d_ssm + (g + 1) * SSM_STATE]
            c_b = xc[rows, d_ssm + (SSM_GROUPS + g) * SSM_STATE:d_ssm + (SSM_GROUPS + g + 1) * SSM_STATE].astype(BF16)
            xs = xc[rows, cols]
            cb = lax.dot_general(c_b, b_f.astype(BF16), (((1,), (1,)), ((), ())), preferred_element_type=F32)
            xdt = (xs * dt_e[:, cols]).astype(BF16)
            ws, rs = [], []
            for hh in range(heads_per_group):
                h = g * heads_per_group + hh
                seg = acs[:, h:h + 1] - acs_t[h:h + 1, :]
                l_mat = jnp.exp2(jnp.where(causal, seg, -jnp.inf))
                ws.append((cb * l_mat).astype(BF16))
                in_head = (lane_g >= hh * SSM_HEAD_DIM) & (lane_g < (hh + 1) * SSM_HEAD_DIM)
                rs.append(jnp.where(in_head, xdt, jnp.zeros_like(xdt)))
            y = jnp.dot(jnp.concatenate(ws, axis=1), jnp.concatenate(rs, axis=0), preferred_element_type=F32)
            st_g = st[:, cols]
            y = y + jnp.dot(c_b, st_g.astype(BF16), preferred_element_type=F32) * dec_e[:, cols]
            xw = (xs * w_e[:, cols]).astype(BF16)
            st[:, cols] = st_g * dec_e[q - 1:q, cols] + jnp.dot(b_f.T.astype(BF16), xw, preferred_element_type=F32)
            y = y + dskip_ref[:, cols] * xs
            y = y * _silu(z_ref[rows, cols].astype(F32))
            o_ref[rows, cols] = _rms_rows(y, g_ref[:, cols]).astype(o_ref.dtype)


def ssd_mixer(packed, dt_raw, conv_w, conv_b, dt_bias, a_log, d_skip, g_out, bsz, s_len, d_ssm, xbc_col, z_col):
    n_heads = d_ssm // SSM_HEAD_DIM
    d_conv = d_ssm + 2 * SSM_GROUPS * SSM_STATE
    step_rows = SSD_STEP_CHUNKS * SSD_CHUNK
    nc = s_len // step_rows
    padl = lambda v: jnp.pad(v.astype(F32), (0, LANES - n_heads))[None, :]
    expand = (jnp.arange(LANES)[:, None] == (jnp.arange(d_ssm) // SSM_HEAD_DIM)[None, :]).astype(BF16)
    tril = (jnp.arange(SSD_CHUNK)[:, None] >= jnp.arange(SSD_CHUNK)[None, :]).astype(BF16)
    full = lambda shape: pl.BlockSpec(shape, lambda b, c: (0, 0))
    return pl.pallas_call(
        _ssd_kernel,
        grid=(bsz, nc),
        in_specs=[pl.BlockSpec((step_rows, d_conv), lambda b, c: (b * nc + c, xbc_col // d_conv)),
                  pl.BlockSpec((step_rows, d_ssm), lambda b, c: (b * nc + c, z_col // d_ssm)),
                  pl.BlockSpec((step_rows, LANES), lambda b, c: (b * nc + c, 0)),
                  full((CONV_WIDTH, d_conv)), full((1, d_conv)), full((1, LANES)), full((1, LANES)),
                  full((1, d_ssm)), full((1, d_ssm)), full((LANES, d_ssm)), full((SSD_CHUNK, SSD_CHUNK))],
        out_specs=pl.BlockSpec((step_rows, d_ssm), lambda b, c: (b * nc + c, 0)),
        out_shape=jax.ShapeDtypeStruct((bsz * s_len, d_ssm), BF16),
        scratch_shapes=[pltpu.VMEM((d_conv // LANES, step_rows + 8, LANES), F32),
                        pltpu.VMEM((step_rows, d_conv), F32),
                        pltpu.VMEM((SSM_STATE, d_ssm), F32)],
        compiler_params=_cparams(("parallel", "arbitrary")),
        name="ssd",
    )(packed, packed, dt_raw, conv_w.astype(F32), conv_b.astype(F32)[None, :], padl(dt_bias), padl(a_log),
      jnp.repeat(d_skip.astype(F32), SSM_HEAD_DIM)[None, :], g_out.astype(F32)[None, :], expand, tril)


def _mem_kv_rope_kernel(mem_ref, g_ref, w_ref, gk_ref, pos_ref, expo_ref, spread_ref, cmask_ref,
                        kv_ref, cos_ref, sin_ref, h_ref, wb_ref):
    d_cross = N_CROSS_HEADS * CROSS_HEAD_DIM
    part = 256
    n_parts = mem_ref.shape[0] // part
    rope_rows = pos_ref.shape[0] // n_parts
    wb_ref[...] = w_ref[...].astype(BF16)
    for i in range(n_parts):
        r0 = i * part
        rows = slice(r0, r0 + part)
        _norm_chunk_to(mem_ref, g_ref, h_ref, r0, part)
        kv = jnp.dot(h_ref[rows, :], wb_ref[...], preferred_element_type=F32)
        for h in range(N_CROSS_HEADS):
            cols = slice(h * CROSS_HEAD_DIM, (h + 1) * CROSS_HEAD_DIM)
            kv_ref[rows, cols] = _rms_rows(kv[:, cols], gk_ref[...]).astype(kv_ref.dtype)
        kv_ref[rows, d_cross:] = kv[:, d_cross:].astype(kv_ref.dtype)
        _rope_rows(pos_ref, expo_ref, spread_ref, cmask_ref, cos_ref, sin_ref, i * rope_rows, rope_rows)


def mem_kv_rope(mem2d, g, w, gk, positions):
    m, d = mem2d.shape
    n = w.shape[1]
    n_tok = positions.size
    pos_packed, expo, spread, cmask = rope_operands(positions)
    assert m % 256 == 0 and pos_packed.shape[0] % (m // 256) == 0
    full2 = lambda shape: pl.BlockSpec(shape, lambda i: (0, 0))
    return pl.pallas_call(
        _mem_kv_rope_kernel,
        grid=(1,),
        in_specs=[full2((m, d)), full2((1, d)), full2((d, n)), full2((1, CROSS_HEAD_DIM)),
                  full2(pos_packed.shape), full2((1, LANES)),
                  pl.BlockSpec((ROPE_PACK, LANES, LANES), lambda i: (0, 0, 0)), full2((1, LANES))],
        out_specs=[full2((m, n)), full2((n_tok, LANES)), full2((n_tok, LANES))],
        out_shape=[jax.ShapeDtypeStruct((m, n), BF16)] + [jax.ShapeDtypeStruct((n_tok, LANES), F32)] * 2,
        scratch_shapes=[pltpu.VMEM((m, d), BF16), pltpu.VMEM((d, n), BF16)],
        compiler_params=_cparams(("arbitrary",)),
        name="mem_kv_rope",
    )(mem2d, g, w, gk, pos_packed, expo, spread, cmask)


def _mix_cross_kernel(attn_ref, ssm_ref, x_ref, ga_ref, w_ref, gc_ref, wq_ref, gq_ref, k_ref, v_ref, wo_ref, o_ref,
                      h_ref, xk_ref, an_ref, q_ref, a_ref, s_scr, p_scr):
    s = pl.program_id(0)
    n_tiles = pl.num_programs(0) - 1
    tm, d_attn = x_ref.shape[0], attn_ref.shape[1]
    rows = CROSS_ROWS
    prev, cur = (s + 1) % 2, s % 2

    @pl.when(s > 0)
    def _():
        q_ref[...] = jnp.dot(h_ref[prev], wq_ref[...], preferred_element_type=F32)

        def scores(item, slot):
            r0, h = item
            cols = slice(h * CROSS_HEAD_DIM, (h + 1) * CROSS_HEAD_DIM)
            qh = (_rms_rows(q_ref[r0:r0 + rows, cols], gq_ref[...]) * (CROSS_HEAD_DIM ** -0.5)).astype(BF16)
            s_scr[slot] = lax.dot_general(qh, k_ref[:, cols], (((1,), (1,)), ((), ())), preferred_element_type=F32)

        def softmax(item, slot):
            sc = s_scr[slot]
            p_scr[slot] = jnp.exp(sc - jnp.max(sc, axis=-1, keepdims=True)).astype(BF16)

        def values(item, slot):
            r0, h = item
            cols = slice(h * CROSS_HEAD_DIM, (h + 1) * CROSS_HEAD_DIM)
            vh = v_ref[:, cols]
            r = jnp.dot(p_scr[slot], jnp.concatenate([vh, jnp.ones_like(vh)], axis=1), preferred_element_type=F32)
            a_ref[r0:r0 + rows, cols] = (r[:, :CROSS_HEAD_DIM] / r[:, CROSS_HEAD_DIM:]).astype(BF16)

        items = [(r0, h) for r0 in range(0, tm, rows) for h in range(N_CROSS_HEADS)]
        _software_pipeline((scores, softmax, values), items, CROSS_GROUP)
        o_ref[...] = xk_ref[prev] + jnp.dot(a_ref[...], wo_ref[...], preferred_element_type=F32)

    @pl.when(s < n_tiles)
    def _():
        n_split = 2
        part = tm // n_split
        for r0 in range(0, tm, part):
            for r in range(r0, r0 + part, 32):
                an_ref[r:r + 32, :] = _rms_rows(attn_ref[r:r + 32, :].astype(F32), ga_ref[...]).astype(BF16)
            rs = slice(r0, r0 + part)
            xk_ref[cur, rs, :] = (x_ref[rs, :] + jnp.dot(an_ref[rs, :], w_ref[:d_attn, :], preferred_element_type=F32)
                                  + jnp.dot(ssm_ref[rs, :], w_ref[d_attn:, :], preferred_element_type=F32))
            _norm_chunk_to(xk_ref.at[cur], gc_ref, h_ref.at[cur], r0, part)


def mix_cross(attn, ssm, x2d, ga, w, gc, wq, gq, kv, wo, s_len, n_mem, tm=512):
    m, d = x2d.shape
    da, ds_ = attn.shape[1], ssm.shape[1]
    dc = wq.shape[1]
    n_tiles = m // tm
    per_batch = s_len // tm
    ahead = lambda s: (jnp.minimum(s, n_tiles - 1), 0)
    behind = lambda s: jnp.maximum(s - 1, 0)
    full = lambda shape, **kw: pl.BlockSpec(shape, lambda s: (0, 0), **kw)
    once = dict(pipeline_mode=pl.Buffered(1))
    return pl.pallas_call(
        _mix_cross_kernel,
        grid=(n_tiles + 1,),
        in_specs=[pl.BlockSpec((tm, da), ahead), pl.BlockSpec((tm, ds_), ahead), pl.BlockSpec((tm, d), ahead),
                  full((1, da)), full((da + ds_, d), **once), full((1, d)), full((d, dc), **once),
                  full((1, CROSS_HEAD_DIM)),
                  pl.BlockSpec((n_mem, dc), lambda s: (behind(s) // per_batch, 0)),
                  pl.BlockSpec((n_mem, dc), lambda s: (behind(s) // per_batch, 1)),
                  full((dc, d), **once)],
        out_specs=pl.BlockSpec((tm, d), lambda s: (behind(s), 0)),
        out_shape=jax.ShapeDtypeStruct((m, d), F32),
        scratch_shapes=[pltpu.VMEM((2, tm, d), BF16), pltpu.VMEM((2, tm, d), F32), pltpu.VMEM((tm, da), BF16),
                        pltpu.VMEM((tm, dc), F32), pltpu.VMEM((tm, dc), BF16),
                        pltpu.VMEM((2 * CROSS_GROUP, CROSS_ROWS, n_mem), F32),
                        pltpu.VMEM((2 * CROSS_GROUP, CROSS_ROWS, n_mem), BF16)],
        compiler_params=_cparams(("arbitrary",)),
        name="mix_cross",
    )(attn, ssm, x2d, ga, w, gc, wq, gq, kv, kv, wo)


def _mlp_kernel(x_ref, g_ref, wu_ref, wd_ref, o_ref, h_ref):
    tm = x_ref.shape[0]

    def expand(rows):
        u = jnp.dot(h_ref[rows, :], wu_ref[...], preferred_element_type=F32)
        u = jnp.square(jnp.maximum(u, 0.0)).astype(BF16)
        return jnp.dot(u, wd_ref[...], preferred_element_type=F32)

    @pl.when(pl.program_id(1) == 0)
    def _():
        part = tm // 2
        for r0 in range(0, tm, part):
            _norm_chunk_to(x_ref, g_ref, h_ref, r0, part)
            rows = slice(r0, r0 + part)
            o_ref[rows, :] = x_ref[rows, :] + expand(rows)

    @pl.when(pl.program_id(1) > 0)
    def _():
        o_ref[...] += expand(slice(None))


def mlp(x2d, g, wu, wd, tm=512, tf=2048):
    m, d = x2d.shape
    f = wu.shape[1]
    return pl.pallas_call(
        _mlp_kernel,
        grid=(m // tm, f // tf),
        in_specs=[pl.BlockSpec((tm, d), lambda i, j: (i, 0)),
                  pl.BlockSpec((1, d), lambda i, j: (0, 0)),
                  pl.BlockSpec((d, tf), lambda i, j: (0, j)),
                  pl.BlockSpec((tf, d), lambda i, j: (j, 0))],
        out_specs=pl.BlockSpec((tm, d), lambda i, j: (i, 0)),
        out_shape=jax.ShapeDtypeStruct((m, d), F32),
        scratch_shapes=[pltpu.VMEM((tm, d), BF16)],
        compiler_params=_cparams(("parallel", "arbitrary")),
        name="mlp",
    )(x2d, g, wu, wd)


def _layer(x2d, mem2d, positions, bsz, s_len, n_mem, g_mix, w_in, g_q, g_k, g_attn_out, conv_w, conv_b, dt_bias,
           a_log, d_skip, g_ssm_out, w_out, g_cross, g_mem, w_cq, w_ckv, g_cq, g_ck, w_co, g_mlp, w_up, w_down):
    d_model = x2d.shape[1]
    d_attn = d_model // 2
    d_ssm = d_model // 2
    n_pairs = d_attn // LANES
    n_ssm_heads = d_ssm // SSM_HEAD_DIM
    d_conv = d_ssm + 2 * SSM_GROUPS * SSM_STATE
    d_packed = 3 * d_attn + d_ssm + d_conv
    row = lambda v: v.astype(F32)[None, :]

    w_in_t = w_in.T
    w_dt_t = jnp.pad(w_in_t[d_packed:], ((0, LANES - n_ssm_heads), (0, 0)))
    packed, dt_raw = in_proj(x2d, row(g_mix), w_in_t, w_dt_t, d_packed)

    kv, cos_t, sin_t = mem_kv_rope(mem2d, row(g_mem), w_ckv, row(g_ck), positions)
    tabs = (cos_t, sin_t)
    attn, w_up_bf, w_down_bf, w_out_bf, w_cq_bf, w_co_bf = dilated_attention(
        packed, tabs, row(jnp.tile(g_q, 2)) * (ATTN_HEAD_DIM ** -0.5 * LOG2E), row(jnp.tile(g_k, 2)),
        (w_up, w_down, w_out, w_cq, w_co), bsz, s_len, n_pairs)
    ssm = ssd_mixer(packed, dt_raw, conv_w, conv_b, dt_bias, a_log, d_skip, g_ssm_out, bsz, s_len, d_ssm,
                    xbc_col=3 * d_attn + d_ssm, z_col=3 * d_attn)
    x2d = mix_cross(attn, ssm, x2d, row(g_attn_out), w_out_bf, row(g_cross), w_cq_bf, row(g_cq), kv, w_co_bf,
                    s_len, n_mem)

    return mlp(x2d, row(g_mlp), w_up_bf, w_down_bf)


def kernel(x, mem, positions, g_mix, w_in, g_q, g_k, g_attn_out, conv_w, conv_b, dt_bias, a_log, d_skip, g_ssm_out,
           w_out, g_cross, g_mem, w_cq, w_ckv, g_cq, g_ck, w_co, g_mlp, w_up, w_down):
    bsz, s_len, d_model = x.shape
    n_mem = mem.shape[1]
    x2d = x.reshape(bsz * s_len, d_model)
    mem2d = mem.reshape(bsz * n_mem, d_model)
    for i in range(g_mix.shape[0]):
        x2d = _layer(x2d, mem2d, positions, bsz, s_len, n_mem, g_mix[i], w_in[i], g_q[i], g_k[i], g_attn_out[i],
                     conv_w[i], conv_b[i], dt_bias[i], a_log[i], d_skip[i], g_ssm_out[i], w_out[i], g_cross[i],
                     g_mem[i], w_cq[i], w_ckv[i], g_cq[i], g_ck[i], w_co[i], g_mlp[i], w_up[i], w_down[i])
    return x2d.reshape(bsz, s_len, d_model)
```

```python
import functools
import math

import jax
import jax.numpy as jnp
from jax import lax
from jax.experimental import pallas as pl
from jax.experimental.pallas import tpu as pltpu

F32 = jnp.float32
BF16 = jnp.bfloat16
EPS = 1e-6

LANES = 128
ATTN_HEAD_DIM = 64
ROT_DIM = ATTN_HEAD_DIM // 4
ROPE_THETA = 500000.0
ATTN_BLOCK = 128
DILATIONS = (1, 4, 16)
SSM_HEAD_DIM = 64
SSM_GROUPS = 4
SSM_STATE = 128
CONV_WIDTH = 4
SSD_CHUNK = 128
N_CROSS_HEADS = 4
CROSS_HEAD_DIM = 128
CROSS_ROWS = 256
CROSS_GROUP = 2
VMEM_LIMIT = 58 * 1024 * 1024


def _cparams(sem):
    return pltpu.CompilerParams(dimension_semantics=sem, vmem_limit_bytes=VMEM_LIMIT)


def _rms_rows(x, g):
    ms = jnp.mean(x * x, axis=-1, keepdims=True)
    return x * lax.rsqrt(ms + EPS) * g


def _split3(x):
    hi = x.astype(BF16)
    r1 = x - hi.astype(F32)
    mid = r1.astype(BF16)
    lo = (r1 - mid.astype(F32)).astype(BF16)
    return hi, mid, lo


def _norm_rows_to(x_ref, g_ref, h_ref, rows=16):
    def body(c, carry):
        r0 = pl.multiple_of(c * rows, rows)
        h_ref[pl.ds(r0, rows), :] = _rms_rows(x_ref[pl.ds(r0, rows), :], g_ref[...]).astype(h_ref.dtype)
        return carry
    lax.fori_loop(0, x_ref.shape[0] // rows, body, 0, unroll=4)


def _norm_chunk_to(x_ref, g_ref, h_ref, start, n_rows, rows=16):
    for r in range(0, n_rows, rows):
        rs = pl.ds(start + r, rows)
        h_ref[rs, :] = _rms_rows(x_ref[rs, :], g_ref[...]).astype(h_ref.dtype)


def _in_proj_kernel(x0_ref, xn_ref, g_ref, w_ref, wdt_ref, o_ref, dt_ref, h_ref):
    i, j = pl.program_id(0), pl.program_id(1)
    cur, nxt = i % 2, (i + 1) % 2

    @pl.when((i == 0) & (j == 0))
    def _():
        _norm_rows_to(x0_ref, g_ref, h_ref.at[0])

    @pl.when(j == 0)
    def _():
        dt_ref[...] = lax.dot_general(h_ref[cur], wdt_ref[...].astype(BF16), (((1,), (1,)), ((), ())),
                                      preferred_element_type=F32)
    o_ref[...] = lax.dot_general(h_ref[cur], w_ref[...].astype(BF16), (((1,), (1,)), ((), ())),
                                 preferred_element_type=F32).astype(o_ref.dtype)
    chunk = xn_ref.shape[0]
    start = pl.multiple_of(jnp.minimum(j, IN_PROJ_NORM_STEPS - 1) * chunk, chunk)
    for r in range(0, chunk, 16):
        h_ref[nxt, pl.ds(start + r, 16), :] = _rms_rows(xn_ref[r:r + 16, :], g_ref[...]).astype(BF16)


IN_PROJ_NORM_STEPS = 4


def in_proj(x2d, g, w_t, w_dt, n, tm=1024, tn=1024):
    m, d = x2d.shape
    n_i = m // tm
    assert n // tn >= IN_PROJ_NORM_STEPS
    return pl.pallas_call(
        _in_proj_kernel,
        grid=(n_i, n // tn),
        in_specs=[pl.BlockSpec((tm, d), lambda i, j: (0, 0), pipeline_mode=pl.Buffered(1)),
                  pl.BlockSpec((tm // IN_PROJ_NORM_STEPS, d),
                               lambda i, j: (jnp.minimum(i + 1, n_i - 1) * IN_PROJ_NORM_STEPS
                                             + jnp.minimum(j, IN_PROJ_NORM_STEPS - 1), 0)),
                  pl.BlockSpec((1, d), lambda i, j: (0, 0)),
                  pl.BlockSpec((tn, d), lambda i, j: (j, 0)),
                  pl.BlockSpec((LANES, d), lambda i, j: (0, 0), pipeline_mode=pl.Buffered(1))],
        out_specs=[pl.BlockSpec((tm, tn), lambda i, j: (i, j)),
                   pl.BlockSpec((tm, LANES), lambda i, j: (i, 0))],
        out_shape=[jax.ShapeDtypeStruct((m, n), BF16), jax.ShapeDtypeStruct((m, LANES), F32)],
        scratch_shapes=[pltpu.VMEM((2, tm, d), BF16)],
        compiler_params=_cparams(("arbitrary", "arbitrary")),
        name="in_proj",
    )(x2d, x2d, g, w_t, w_dt)


ROPE_PACK = LANES // ROT_DIM


def _rope_rows(pos_ref, expo_ref, spread_ref, cmask_ref, cos_ref, sin_ref, p0, n_rows):
    inv_freq = jnp.power(jnp.float32(ROPE_THETA), expo_ref[...])
    ang = pos_ref[p0:p0 + n_rows, :].astype(F32) * inv_freq
    packed = (jnp.cos(ang), jnp.sin(ang))
    fill = (1.0 - cmask_ref[...], jnp.zeros_like(cmask_ref[...]))
    for out_ref, tab, off in zip((cos_ref, sin_ref), packed, fill):
        parts = _split3(tab)
        for k in range(ROPE_PACK):
            rows = sum(jnp.dot(part, spread_ref[k], preferred_element_type=F32) for part in parts)
            out_ref[pl.ds(ROPE_PACK * p0 + k, n_rows, stride=ROPE_PACK), :] = rows + off


def rope_operands(positions):
    m = positions.size
    half = ROT_DIM // 2
    lane = jnp.arange(LANES)
    expo = (-2.0 * ((lane % ROT_DIM) % half).astype(F32) / ROT_DIM)[None, :]
    cmask = ((lane % ATTN_HEAD_DIM) < ROT_DIM).astype(F32)[None, :]
    d_dst = lane % ATTN_HEAD_DIM
    spread = ((d_dst[None, None, :] < ROT_DIM)
              & (lane[None, :, None] == ROT_DIM * jnp.arange(ROPE_PACK)[:, None, None] + d_dst[None, None, :])).astype(BF16)
    pos_packed = jnp.repeat(positions.reshape(m // ROPE_PACK, ROPE_PACK), ROT_DIM, axis=1)
    return pos_packed, expo, spread, cmask


def _rotate_half_matrix():
    half = ROT_DIM // 2
    src = jnp.arange(LANES)[:, None]
    dst = jnp.arange(LANES)[None, :]
    d = dst % ATTN_HEAD_DIM
    first = (d < half) & (src == dst + half)
    second = (d >= half) & (d < ROT_DIM) & (src == dst - half)
    return (second.astype(F32) - first.astype(F32)).astype(BF16)


def _rows(start, size, dil):
    return pl.ds(start, size) if dil == 1 else pl.ds(start, size, stride=dil)


ATTN_GROUP = 2


def _software_pipeline(stages, items, group, extras=()):
    groups = [items[i:i + group] for i in range(0, len(items), group)]
    n_trips = len(groups) + len(stages) - 1
    for t in range(n_trips):
        for lag, stage in enumerate(stages):
            if 0 <= t - lag < len(groups):
                for g, item in enumerate(groups[t - lag]):
                    stage(item, ((t - lag) % 2) * group + g)
        for k, extra in enumerate(extras):
            if k * n_trips // len(extras) == t:
                extra()


def _attn_kernel(n_cast, q0_ref, k0_ref, v0_ref, cos0_ref, sin0_ref, qn_ref, kn_ref, vn_ref, cosn_ref, sinn_ref,
                 gq_ref, gk_ref, seg_ref, rot_ref, *refs):
    cast_in, o_ref, cast_out = refs[:n_cast], refs[n_cast], refs[n_cast + 1:2 * n_cast + 1]
    set_a, set_b, ob, mb, lb, s_scr, p_scr = refs[2 * n_cast + 1:]
    s_len = qn_ref.shape[0]
    blk = ATTN_BLOCK
    step = pl.program_id(0) * pl.num_programs(1) + pl.program_id(1)
    lane = lax.broadcasted_iota(jnp.int32, (blk, LANES), 1)
    head0 = lane < ATTN_HEAD_DIM
    qi = lax.broadcasted_iota(jnp.int32, (2 * blk, 2 * blk), 0) % blk
    kj = lax.broadcasted_iota(jnp.int32, (2 * blk, 2 * blk), 1)
    band_mask = (kj >= qi) & (kj <= qi + blk)
    first_mask = (lax.broadcasted_iota(jnp.int32, (2 * blk, blk), 1)
                  <= lax.broadcasted_iota(jnp.int32, (2 * blk, blk), 0) % blk)

    for w_ref, w_out in zip(cast_in, cast_out):
        w_out[...] = w_ref[...].astype(BF16)

    prep_rows = 256

    def prep(srcs, dst, r0):
        q_ref, k_ref, v_ref, cos_ref, sin_ref = srcs
        qf, kf, vf = dst[:3]
        rows = slice(r0, r0 + prep_rows)
        cs, sn = cos_ref[rows, :], sin_ref[rows, :]

        def norm_rope(x_ref, g_ref):
            x = x_ref[rows, :].astype(F32)
            ss = jnp.dot((x * x).astype(BF16), seg_ref[...], preferred_element_type=F32)
            y = x * lax.rsqrt(ss * (1.0 / ATTN_HEAD_DIM) + EPS) * g_ref[...]
            return y * cs + jnp.dot(y.astype(BF16), rot_ref[...], preferred_element_type=F32) * sn

        qf[rows, :] = norm_rope(q_ref, gq_ref)
        kf[rows, :] = norm_rope(k_ref, gk_ref)
        vf[rows, :] = v_ref[rows, :].astype(F32)

    dmid = DILATIONS[1]
    run = s_len // dmid

    def deinterleave(bufs, r):
        for src, dst in zip(bufs[:3], bufs[3:]):
            dst[r * run:(r + 1) * run, :] = src[pl.ds(r, run, stride=dmid), :]

    def prepare(srcs, bufs):
        return ([functools.partial(prep, srcs, bufs, r0) for r0 in range(0, s_len, prep_rows)]
                + [functools.partial(deinterleave, bufs, r) for r in range(dmid)])

    @pl.when(step == 0)
    def _():
        for piece in prepare((q0_ref, k0_ref, v0_ref, cos0_ref, sin0_ref), set_a):
            piece()

    def block_descs(bufs):
        natural, deint = bufs[:3], bufs[3:]
        descs = [(deint, 1, 1, r * run, r * run, blk, first_mask) for r in range(dmid)]
        descs += [(deint, 1, 1, r * run + n * blk, r * run + (n - 1) * blk, 2 * blk, band_mask)
                  for n in range(1, run // blk) for r in range(dmid)]
        descs += [(deint, 2, dmid, r_hi * run + r_lo, r_hi * run + r_lo, blk, first_mask)
                  for r_hi in range(dmid) for r_lo in range(dmid)]
        descs += [(natural, 0, 1, 0, 0, blk, first_mask)]
        descs += [(natural, 0, 1, n * blk, (n - 1) * blk, 2 * blk, band_mask) for n in range(1, s_len // blk)]
        return descs

    def scores(desc, slot):
        (q_src, k_src, _), _, dil, q_start, k_start, n_keys, _ = desc
        qt = q_src[_rows(q_start, blk, dil), :]
        zero = jnp.zeros_like(qt)
        q2 = jnp.concatenate([jnp.where(head0, qt, zero), jnp.where(head0, zero, qt)], axis=0).astype(BF16)
        kb = k_src[_rows(k_start, n_keys, dil), :].astype(BF16)
        s_scr[slot, :, :n_keys] = lax.dot_general(q2, kb, (((1,), (1,)), ((), ())), preferred_element_type=F32)

    def softmax(desc, slot):
        _, br, dil, q_start, _, n_keys, mask = desc
        s = jnp.where(mask, s_scr[slot, :, :n_keys], -jnp.inf)
        m = jnp.max(s, axis=-1, keepdims=True)
        p_scr[slot, :, :n_keys] = jnp.exp2(s - m).astype(BF16)
        mb[br, _rows(q_start, blk, dil), :] = jnp.where(head0, m[:blk], m[blk:])

    def values(desc, slot):
        (_, _, v_src), br, dil, q_start, k_start, n_keys, _ = desc
        vb = v_src[_rows(k_start, n_keys, dil), :].astype(BF16)
        v1 = jnp.concatenate([vb, jnp.ones_like(vb)], axis=1)
        r = jnp.dot(p_scr[slot, :, :n_keys], v1, preferred_element_type=F32)
        out_rows = _rows(q_start, blk, dil)
        ob[br, out_rows, :] = jnp.where(head0, r[:blk, :LANES], r[blk:, :LANES])
        lb[br, out_rows, :] = jnp.where(head0, r[:blk, LANES:], r[blk:, LANES:])
        if br == 0:
            merge(q_start)

    def merge(n0):
        sub = blk // dmid
        for r in range(dmid):
            nat_rows = pl.ds(n0 + r, sub, stride=dmid)
            d_rows = slice(r * run + n0 // dmid, r * run + n0 // dmid + sub)
            rows = (nat_rows, d_rows, d_rows)
            ms = [mb[g, rows[g], :] for g in range(len(DILATIONS))]
            m = functools.reduce(jnp.maximum, ms)
            ws = [jnp.exp2(mg - m) for mg in ms]
            num = functools.reduce(jnp.add, [w * ob[g, rows[g], :] for g, w in enumerate(ws)])
            den = functools.reduce(jnp.add, [w * lb[g, rows[g], :] for g, w in enumerate(ws)])
            ob[0, nat_rows, :] = num / den
        o_ref[n0:n0 + blk, :] = ob[0, n0:n0 + blk, :].astype(o_ref.dtype)

    for parity, (cur_set, nxt_set) in enumerate(((set_a, set_b), (set_b, set_a))):
        @pl.when(step % 2 == parity)
        def _(cur_set=cur_set, nxt_set=nxt_set):
            _software_pipeline((scores, softmax, values), block_descs(cur_set), ATTN_GROUP,
                               extras=prepare((qn_ref, kn_ref, vn_ref, cosn_ref, sinn_ref), nxt_set))


def dilated_attention(qkv, tabs, gq2, gk2, weights, bsz, s_len, n_pairs):
    cos_t, sin_t = tabs
    d1, d2, d3 = DILATIONS
    assert d1 == 1 and d3 == d2 * d2 and s_len == d3 * ATTN_BLOCK, "layout assumes dilations (1, d, d*d), one block per largest class"
    lane = jnp.arange(LANES)
    seg = (lane[:, None] // ATTN_HEAD_DIM == lane[None, :] // ATTN_HEAD_DIM).astype(BF16)
    n_steps = bsz * n_pairs

    def nxt(b, p):
        s = jnp.minimum(b * n_pairs + p + 1, n_steps - 1)
        return s // n_pairs, s % n_pairs

    once = dict(pipeline_mode=pl.Buffered(1))
    first = lambda off: pl.BlockSpec((s_len, LANES), lambda b, p, off=off: (0, off), **once)
    nblk = lambda off: pl.BlockSpec((s_len, LANES), lambda b, p, off=off: (nxt(b, p)[0], off + nxt(b, p)[1]))
    tab0 = pl.BlockSpec((s_len, LANES), lambda b, p: (0, 0), **once)
    tabn = pl.BlockSpec((s_len, LANES), lambda b, p: (nxt(b, p)[0], 0))
    row = pl.BlockSpec((1, LANES), lambda b, p: (0, 0))
    sq = pl.BlockSpec((LANES, LANES), lambda b, p: (0, 0))
    w_specs = [pl.BlockSpec((w.shape[0] // n_steps, w.shape[1]), lambda b, p: (b * n_pairs + p, 0)) for w in weights]
    return pl.pallas_call(
        functools.partial(_attn_kernel, len(weights)),
        grid=(bsz, n_pairs),
        in_specs=[first(0), first(n_pairs), first(2 * n_pairs), tab0, tab0,
                  nblk(0), nblk(n_pairs), nblk(2 * n_pairs), tabn, tabn,
                  row, row, sq, sq] + w_specs,
        out_specs=[pl.BlockSpec((s_len, LANES), lambda b, p: (b, p))] + w_specs,
        out_shape=[jax.ShapeDtypeStruct((bsz * s_len, n_pairs * LANES), BF16)]
                  + [jax.ShapeDtypeStruct(w.shape, BF16) for w in weights],
        scratch_shapes=[[pltpu.VMEM((s_len, LANES), F32)] * 6] * 2
                       + [pltpu.VMEM((len(DILATIONS), s_len, LANES), F32)] * 3
                       + [pltpu.VMEM((2 * ATTN_GROUP, 2 * ATTN_BLOCK, 2 * ATTN_BLOCK), F32),
                          pltpu.VMEM((2 * ATTN_GROUP, 2 * ATTN_BLOCK, 2 * ATTN_BLOCK), BF16)],
        compiler_params=_cparams(("arbitrary", "arbitrary")),
        name="dilated_attn",
    )(qkv, qkv, qkv, cos_t, sin_t, qkv, qkv, qkv, cos_t, sin_t, gq2, gk2, seg, _rotate_half_matrix(), *weights)


def _silu(x):
    h = 0.5 * x
    return h * jnp.tanh(h) + h


LOG2E = 1.4426950408889634
SSD_STEP_CHUNKS = 4


def _ssd_kernel(xbc_ref, z_ref, dt_ref, cw_ref, cb_ref, dtb_ref, alog_ref, dskip_ref, g_ref,
                expand_ref, tril_ref, o_ref, xpad, xc, st):
    q = SSD_CHUNK
    n_rows = xbc_ref.shape[0]
    d_ssm = z_ref.shape[1]
    d_conv = xbc_ref.shape[1]
    gw = d_ssm // SSM_GROUPS
    heads_per_group = gw // SSM_HEAD_DIM
    pad = 8

    @pl.when(pl.program_id(1) == 0)
    def _():
        xpad[:, 0:pad, :] = jnp.zeros((d_conv // LANES, pad, LANES), F32)
        st[...] = jnp.zeros_like(st)

    for c0 in range(0, d_conv, LANES):
        cols = slice(c0, c0 + LANES)
        slab = c0 // LANES
        xpad[slab, pad:pad + n_rows, :] = xbc_ref[:, cols].astype(F32)
        acc = cb_ref[:, cols] + cw_ref[CONV_WIDTH - 1:CONV_WIDTH, cols] * xpad[slab, pad:pad + n_rows, :]
        for w in range(CONV_WIDTH - 1):
            off = pad - (CONV_WIDTH - 1) + w
            acc = acc + cw_ref[w:w + 1, cols] * xpad[slab, pl.ds(off, n_rows, stride=1), :]
        xc[:, cols] = _silu(acc)
    xpad[:, 0:pad, :] = xpad[:, n_rows:n_rows + pad, :]

    li = lax.broadcasted_iota(jnp.int32, (q, q), 0)
    si = lax.broadcasted_iota(jnp.int32, (q, q), 1)
    causal = li >= si
    lane_g = lax.broadcasted_iota(jnp.int32, (q, gw), 1)
    tril = tril_ref[...]
    expand = expand_ref[...]

    for r0 in range(0, n_rows, q):
        rows = slice(r0, r0 + q)
        x_dt = dt_ref[rows, :] + dtb_ref[...]
        dt = jnp.maximum(x_dt, 0.0) + jnp.log1p(jnp.exp(-jnp.abs(x_dt)))
        dta = dt * (-LOG2E * jnp.exp(alog_ref[...]))
        acs = sum(jnp.dot(tril, part, preferred_element_type=F32) for part in _split3(dta))
        acs_t = acs.T
        last = acs[q - 1:q, :]
        dt_e = jnp.dot(dt.astype(BF16), expand, preferred_element_type=F32)
        dec_e = jnp.dot(jnp.exp2(acs).astype(BF16), expand, preferred_element_type=F32)
        w_e = jnp.dot((jnp.exp2(last - acs) * dt).astype(BF16), expand, preferred_element_type=F32)

        for g in range(SSM_GROUPS):
            cols = slice(g * gw, (g + 1) * gw)
            b_f = xc[rows, d_ssm + g * SSM_STATE:d_ssm + (g + 1) * SSM_STATE]
            c_b = xc[rows, d_ssm + (SSM_GROUPS + g) * SSM_STATE:d_ssm + (SSM_GROUPS + g + 1) * SSM_STATE].astype(BF16)
            xs = xc[rows, cols]
            cb = lax.dot_general(c_b, b_f.astype(BF16), (((1,), (1,)), ((), ())), preferred_element_type=F32)
            xdt = (xs * dt_e[:, cols]).astype(BF16)
            ws, rs = [], []
            for hh in range(heads_per_group):
                h = g * heads_per_group + hh
                seg = acs[:, h:h + 1] - acs_t[h:h + 1, :]
                l_mat = jnp.exp2(jnp.where(causal, seg, -jnp.inf))
                ws.append((cb * l_mat).astype(BF16))
                in_head = (lane_g >= hh * SSM_HEAD_DIM) & (lane_g < (hh + 1) * SSM_HEAD_DIM)
                rs.append(jnp.where(in_head, xdt, jnp.zeros_like(xdt)))
            y = jnp.dot(jnp.concatenate(ws, axis=1), jnp.concatenate(rs, axis=0), preferred_element_type=F32)
            st_g = st[:, cols]
            y = y + jnp.dot(c_b, st_g.astype(BF16), preferred_element_type=F32) * dec_e[:, cols]
            xw = (xs * w_e[:, cols]).astype(BF16)
            st[:, cols] = st_g * dec_e[q - 1:q, cols] + jnp.dot(b_f.T.astype(BF16), xw, preferred_element_type=F32)
            y = y + dskip_ref[:, cols] * xs
            y = y * _silu(z_ref[rows, cols].astype(F32))
            o_ref[rows, cols] = _rms_rows(y, g_ref[:, cols]).astype(o_ref.dtype)


def ssd_mixer(packed, dt_raw, conv_w, conv_b, dt_bias, a_log, d_skip, g_out, bsz, s_len, d_ssm, xbc_col, z_col):
    n_heads = d_ssm // SSM_HEAD_DIM
    d_conv = d_ssm + 2 * SSM_GROUPS * SSM_STATE
    step_rows = SSD_STEP_CHUNKS * SSD_CHUNK
    nc = s_len // step_rows
    padl = lambda v: jnp.pad(v.astype(F32), (0, LANES - n_heads))[None, :]
    expand = (jnp.arange(LANES)[:, None] == (jnp.arange(d_ssm) // SSM_HEAD_DIM)[None, :]).astype(BF16)
    tril = (jnp.arange(SSD_CHUNK)[:, None] >= jnp.arange(SSD_CHUNK)[None, :]).astype(BF16)
    full = lambda shape: pl.BlockSpec(shape, lambda b, c: (0, 0))
    return pl.pallas_call(
        _ssd_kernel,
        grid=(bsz, nc),
        in_specs=[pl.BlockSpec((step_rows, d_conv), lambda b, c: (b * nc + c, xbc_col // d_conv)),
                  pl.BlockSpec((step_rows, d_ssm), lambda b, c: (b * nc + c, z_col // d_ssm)),
                  pl.BlockSpec((step_rows, LANES), lambda b, c: (b * nc + c, 0)),
                  full((CONV_WIDTH, d_conv)), full((1, d_conv)), full((1, LANES)), full((1, LANES)),
                  full((1, d_ssm)), full((1, d_ssm)), full((LANES, d_ssm)), full((SSD_CHUNK, SSD_CHUNK))],
        out_specs=pl.BlockSpec((step_rows, d_ssm), lambda b, c: (b * nc + c, 0)),
        out_shape=jax.ShapeDtypeStruct((bsz * s_len, d_ssm), BF16),
        scratch_shapes=[pltpu.VMEM((d_conv // LANES, step_rows + 8, LANES), F32),
                        pltpu.VMEM((step_rows, d_conv), F32),
                        pltpu.VMEM((SSM_STATE, d_ssm), F32)],
        compiler_params=_cparams(("parallel", "arbitrary")),
        name="ssd",
    )(packed, packed, dt_raw, conv_w.astype(F32), conv_b.astype(F32)[None, :], padl(dt_bias), padl(a_log),
      jnp.repeat(d_skip.astype(F32), SSM_HEAD_DIM)[None, :], g_out.astype(F32)[None, :], expand, tril)


def _mem_kv_rope_kernel(mem_ref, g_ref, w_ref, gk_ref, pos_ref, expo_ref, spread_ref, cmask_ref,
                        kv_ref, cos_ref, sin_ref, h_ref, wb_ref):
    d_cross = N_CROSS_HEADS * CROSS_HEAD_DIM

    @pl.when(pl.program_id(0) == 0)
    def _():
        wb_ref[...] = w_ref[...].astype(BF16)

    _norm_chunk_to(mem_ref, g_ref, h_ref, 0, mem_ref.shape[0])
    kv = jnp.dot(h_ref[...], wb_ref[...], preferred_element_type=F32)
    for h in range(N_CROSS_HEADS):
        cols = slice(h * CROSS_HEAD_DIM, (h + 1) * CROSS_HEAD_DIM)
        kv_ref[:, cols] = _rms_rows(kv[:, cols], gk_ref[...]).astype(kv_ref.dtype)
    kv_ref[:, d_cross:] = kv[:, d_cross:].astype(kv_ref.dtype)
    _rope_rows(pos_ref, expo_ref, spread_ref, cmask_ref, cos_ref, sin_ref, 0, pos_ref.shape[0])


def mem_kv_rope(mem2d, g, w, gk, positions, n_steps=4):
    m, d = mem2d.shape
    n = w.shape[1]
    n_tok = positions.size
    pos_packed, expo, spread, cmask = rope_operands(positions)
    n_steps = min(n_steps, m // 256)
    tm, tp = m // n_steps, pos_packed.shape[0] // n_steps
    full2 = lambda shape, **kw: pl.BlockSpec(shape, lambda i: (0, 0), **kw)
    rows = lambda r, c: pl.BlockSpec((r, c), lambda i: (i, 0))
    return pl.pallas_call(
        _mem_kv_rope_kernel,
        grid=(n_steps,),
        in_specs=[rows(tm, d), full2((1, d)), full2((d, n), pipeline_mode=pl.Buffered(1)), full2((1, CROSS_HEAD_DIM)),
                  rows(tp, LANES), full2((1, LANES)),
                  pl.BlockSpec((ROPE_PACK, LANES, LANES), lambda i: (0, 0, 0)), full2((1, LANES))],
        out_specs=[rows(tm, n), rows(tp * ROPE_PACK, LANES), rows(tp * ROPE_PACK, LANES)],
        out_shape=[jax.ShapeDtypeStruct((m, n), BF16)] + [jax.ShapeDtypeStruct((n_tok, LANES), F32)] * 2,
        scratch_shapes=[pltpu.VMEM((tm, d), BF16), pltpu.VMEM((d, n), BF16)],
        compiler_params=_cparams(("arbitrary",)),
        name="mem_kv_rope",
    )(mem2d, g, w, gk, pos_packed, expo, spread, cmask)


def _mix_cross_kernel(attn_ref, ssm_ref, x_ref, ga_ref, w_ref, gc_ref, wq_ref, gq_ref, k_ref, v_ref, wo_ref, o_ref,
                      h_ref, xk_ref, an_ref, q_ref, a_ref, s_scr, p_scr):
    s = pl.program_id(0)
    n_tiles = pl.num_programs(0) - 1
    tm, d_attn = x_ref.shape[0], attn_ref.shape[1]
    rows = CROSS_ROWS
    prev, cur = (s + 1) % 2, s % 2

    @pl.when(s > 0)
    def _():
        q_ref[...] = jnp.dot(h_ref[prev], wq_ref[...], preferred_element_type=F32)

        def scores(item, slot):
            r0, h = item
            cols = slice(h * CROSS_HEAD_DIM, (h + 1) * CROSS_HEAD_DIM)
            qh = (_rms_rows(q_ref[r0:r0 + rows, cols], gq_ref[...]) * (CROSS_HEAD_DIM ** -0.5)).astype(BF16)
            s_scr[slot] = lax.dot_general(qh, k_ref[:, cols], (((1,), (1,)), ((), ())), preferred_element_type=F32)

        def softmax(item, slot):
            sc = s_scr[slot]
            p_scr[slot] = jnp.exp(sc - jnp.max(sc, axis=-1, keepdims=True)).astype(BF16)

        def values(item, slot):
            r0, h = item
            cols = slice(h * CROSS_HEAD_DIM, (h + 1) * CROSS_HEAD_DIM)
            vh = v_ref[:, cols]
            r = jnp.dot(p_scr[slot], jnp.concatenate([vh, jnp.ones_like(vh)], axis=1), preferred_element_type=F32)
            a_ref[r0:r0 + rows, cols] = (r[:, :CROSS_HEAD_DIM] / r[:, CROSS_HEAD_DIM:]).astype(BF16)

        items = [(r0, h) for r0 in range(0, tm, rows) for h in range(N_CROSS_HEADS)]
        _software_pipeline((scores, softmax, values), items, CROSS_GROUP)
        o_ref[...] = xk_ref[prev] + jnp.dot(a_ref[...], wo_ref[...], preferred_element_type=F32)

    @pl.when(s < n_tiles)
    def _():
        n_split = 2
        part = tm // n_split
        for r0 in range(0, tm, part):
            for r in range(r0, r0 + part, 32):
                an_ref[r:r + 32, :] = _rms_rows(attn_ref[r:r + 32, :].astype(F32), ga_ref[...]).astype(BF16)
            rs = slice(r0, r0 + part)
            xk_ref[cur, rs, :] = (x_ref[rs, :] + jnp.dot(an_ref[rs, :], w_ref[:d_attn, :], preferred_element_type=F32)
                                  + jnp.dot(ssm_ref[rs, :], w_ref[d_attn:, :], preferred_element_type=F32))
            _norm_chunk_to(xk_ref.at[cur], gc_ref, h_ref.at[cur], r0, part)


def mix_cross(attn, ssm, x2d, ga, w, gc, wq, gq, kv, wo, s_len, n_mem, tm=512):
    m, d = x2d.shape
    da, ds_ = attn.shape[1], ssm.shape[1]
    dc = wq.shape[1]
    n_tiles = m // tm
    per_batch = s_len // tm
    ahead = lambda s: (jnp.minimum(s, n_tiles - 1), 0)
    behind = lambda s: jnp.maximum(s - 1, 0)
    full = lambda shape, **kw: pl.BlockSpec(shape, lambda s: (0, 0), **kw)
    once = dict(pipeline_mode=pl.Buffered(1))
    return pl.pallas_call(
        _mix_cross_kernel,
        grid=(n_tiles + 1,),
        in_specs=[pl.BlockSpec((tm, da), ahead), pl.BlockSpec((tm, ds_), ahead), pl.BlockSpec((tm, d), ahead),
                  full((1, da)), full((da + ds_, d), **once), full((1, d)), full((d, dc), **once),
                  full((1, CROSS_HEAD_DIM)),
                  pl.BlockSpec((n_mem, dc), lambda s: (behind(s) // per_batch, 0)),
                  pl.BlockSpec((n_mem, dc), lambda s: (behind(s) // per_batch, 1)),
                  full((dc, d), **once)],
        out_specs=pl.BlockSpec((tm, d), lambda s: (behind(s), 0)),
        out_shape=jax.ShapeDtypeStruct((m, d), F32),
        scratch_shapes=[pltpu.VMEM((2, tm, d), BF16), pltpu.VMEM((2, tm, d), F32), pltpu.VMEM((tm, da), BF16),
                        pltpu.VMEM((tm, dc), F32), pltpu.VMEM((tm, dc), BF16),
                        pltpu.VMEM((2 * CROSS_GROUP, CROSS_ROWS, n_mem), F32),
                        pltpu.VMEM((2 * CROSS_GROUP, CROSS_ROWS, n_mem), BF16)],
        compiler_params=_cparams(("arbitrary",)),
        name="mix_cross",
    )(attn, ssm, x2d, ga, w, gc, wq, gq, kv, kv, wo)


def _mlp_kernel(x_ref, g_ref, wu_ref, wd_ref, o_ref, h_ref):
    tm = x_ref.shape[0]

    def expand(rows):
        u = jnp.dot(h_ref[rows, :], wu_ref[...], preferred_element_type=F32)
        u = jnp.square(jnp.maximum(u, 0.0)).astype(BF16)
        return jnp.dot(u, wd_ref[...], preferred_element_type=F32)

    @pl.when(pl.program_id(1) == 0)
    def _():
        part = tm // 2
        for r0 in range(0, tm, part):
            _norm_chunk_to(x_ref, g_ref, h_ref, r0, part)
            rows = slice(r0, r0 + part)
            o_ref[rows, :] = x_ref[rows, :] + expand(rows)

    @pl.when(pl.program_id(1) > 0)
    def _():
        o_ref[...] += expand(slice(None))


def mlp(x2d, g, wu, wd, tm=512, tf=2048):
    m, d = x2d.shape
    f = wu.shape[1]
    return pl.pallas_call(
        _mlp_kernel,
        grid=(m // tm, f // tf),
        in_specs=[pl.BlockSpec((tm, d), lambda i, j: (i, 0)),
                  pl.BlockSpec((1, d), lambda i, j: (0, 0)),
                  pl.BlockSpec((d, tf), lambda i, j: (0, j)),
                  pl.BlockSpec((tf, d), lambda i, j: (j, 0))],
        out_specs=pl.BlockSpec((tm, d), lambda i, j: (i, 0)),
        out_shape=jax.ShapeDtypeStruct((m, d), F32),
        scratch_shapes=[pltpu.VMEM((tm, d), BF16)],
        compiler_params=_cparams(("parallel", "arbitrary")),
        name="mlp",
    )(x2d, g, wu, wd)


def _layer(x2d, mem2d, positions, bsz, s_len, n_mem, g_mix, w_in, g_q, g_k, g_attn_out, conv_w, conv_b, dt_bias,
           a_log, d_skip, g_ssm_out, w_out, g_cross, g_mem, w_cq, w_ckv, g_cq, g_ck, w_co, g_mlp, w_up, w_down):
    d_model = x2d.shape[1]
    d_attn = d_model // 2
    d_ssm = d_model // 2
    n_pairs = d_attn // LANES
    n_ssm_heads = d_ssm // SSM_HEAD_DIM
    d_conv = d_ssm + 2 * SSM_GROUPS * SSM_STATE
    d_packed = 3 * d_attn + d_ssm + d_conv
    row = lambda v: v.astype(F32)[None, :]

    w_in_t = w_in.T
    w_dt_t = jnp.pad(w_in_t[d_packed:], ((0, LANES - n_ssm_heads), (0, 0)))
    packed, dt_raw = in_proj(x2d, row(g_mix), w_in_t, w_dt_t, d_packed)

    kv, cos_t, sin_t = mem_kv_rope(mem2d, row(g_mem), w_ckv, row(g_ck), positions)
    tabs = (cos_t, sin_t)
    attn, w_up_bf, w_down_bf, w_out_bf, w_cq_bf, w_co_bf = dilated_attention(
        packed, tabs, row(jnp.tile(g_q, 2)) * (ATTN_HEAD_DIM ** -0.5 * LOG2E), row(jnp.tile(g_k, 2)),
        (w_up, w_down, w_out, w_cq, w_co), bsz, s_len, n_pairs)
    ssm = ssd_mixer(packed, dt_raw, conv_w, conv_b, dt_bias, a_log, d_skip, g_ssm_out, bsz, s_len, d_ssm,
                    xbc_col=3 * d_attn + d_ssm, z_col=3 * d_attn)
    x2d = mix_cross(attn, ssm, x2d, row(g_attn_out), w_out_bf, row(g_cross), w_cq_bf, row(g_cq), kv, w_co_bf,
                    s_len, n_mem)

    return mlp(x2d, row(g_mlp), w_up_bf, w_down_bf)


def kernel(x, mem, positions, g_mix, w_in, g_q, g_k, g_attn_out, conv_w, conv_b, dt_bias, a_log, d_skip, g_ssm_out,
           w_out, g_cross, g_mem, w_cq, w_ckv, g_cq, g_ck, w_co, g_mlp, w_up, w_down):
    bsz, s_len, d_model = x.shape
    n_mem = mem.shape[1]
    x2d = x.reshape(bsz * s_len, d_model)
    mem2d = mem.reshape(bsz * n_mem, d_model)
    for i in range(g_mix.shape[0]):
        x2d = _layer(x2d, mem2d, positions, bsz, s_len, n_mem, g_mix[i], w_in[i], g_q[i], g_k[i], g_attn_out[i],
                     conv_w[i], conv_b[i], dt_bias[i], a_log[i], d_skip[i], g_ssm_out[i], w_out[i], g_cross[i],
                     g_mem[i], w_cq[i], w_ckv[i], g_cq[i], g_ck[i], w_co[i], g_mlp[i], w_up[i], w_down[i])
    return x2d.reshape(bsz, s_len, d_model)
```

```python
import functools
import math

import jax
import jax.numpy as jnp
from jax import lax
from jax.experimental import pallas as pl
from jax.experimental.pallas import tpu as pltpu

F32 = jnp.float32
BF16 = jnp.bfloat16
EPS = 1e-6

LANES = 128
ATTN_HEAD_DIM = 64
ROT_DIM = ATTN_HEAD_DIM // 4
ROPE_THETA = 500000.0
ATTN_BLOCK = 128
DILATIONS = (1, 4, 16)
SSM_HEAD_DIM = 64
SSM_GROUPS = 4
SSM_STATE = 128
CONV_WIDTH = 4
SSD_CHUNK = 128
N_CROSS_HEADS = 4
CROSS_HEAD_DIM = 128
CROSS_ROWS = 256
CROSS_GROUP = 2
VMEM_LIMIT = 58 * 1024 * 1024


def _cparams(sem):
    return pltpu.CompilerParams(dimension_semantics=sem, vmem_limit_bytes=VMEM_LIMIT)


def _rms_rows(x, g):
    ms = jnp.mean(x * x, axis=-1, keepdims=True)
    return x * lax.rsqrt(ms + EPS) * g


def _split3(x):
    hi = x.astype(BF16)
    r1 = x - hi.astype(F32)
    mid = r1.astype(BF16)
    lo = (r1 - mid.astype(F32)).astype(BF16)
    return hi, mid, lo


def _norm_rows_to(x_ref, g_ref, h_ref, rows=16):
    def body(c, carry):
        r0 = pl.multiple_of(c * rows, rows)
        h_ref[pl.ds(r0, rows), :] = _rms_rows(x_ref[pl.ds(r0, rows), :], g_ref[...]).astype(h_ref.dtype)
        return carry
    lax.fori_loop(0, x_ref.shape[0] // rows, body, 0, unroll=4)


def _norm_chunk_to(x_ref, g_ref, h_ref, start, n_rows, rows=16):
    for r in range(0, n_rows, rows):
        rs = pl.ds(start + r, rows)
        h_ref[rs, :] = _rms_rows(x_ref[rs, :], g_ref[...]).astype(h_ref.dtype)


def _in_proj_kernel(x0_ref, xn_ref, g_ref, w_ref, wdt_ref, o_ref, dt_ref, h_ref):
    i, j = pl.program_id(0), pl.program_id(1)
    cur, nxt = i % 2, (i + 1) % 2

    @pl.when((i == 0) & (j == 0))
    def _():
        _norm_rows_to(x0_ref, g_ref, h_ref.at[0])

    @pl.when(j == 0)
    def _():
        dt_ref[...] = lax.dot_general(h_ref[cur], wdt_ref[...].astype(BF16), (((1,), (1,)), ((), ())),
                                      preferred_element_type=F32)
    o_ref[...] = lax.dot_general(h_ref[cur], w_ref[...].astype(BF16), (((1,), (1,)), ((), ())),
                                 preferred_element_type=F32).astype(o_ref.dtype)
    chunk = xn_ref.shape[0]
    start = pl.multiple_of(jnp.minimum(j, IN_PROJ_NORM_STEPS - 1) * chunk, chunk)
    for r in range(0, chunk, 16):
        h_ref[nxt, pl.ds(start + r, 16), :] = _rms_rows(xn_ref[r:r + 16, :], g_ref[...]).astype(BF16)


IN_PROJ_NORM_STEPS = 4


def in_proj(x2d, g, w_t, w_dt, n, tm=1024, tn=1536):
    m, d = x2d.shape
    n_i = m // tm
    assert n // tn >= IN_PROJ_NORM_STEPS
    return pl.pallas_call(
        _in_proj_kernel,
        grid=(n_i, n // tn),
        in_specs=[pl.BlockSpec((tm, d), lambda i, j: (0, 0), pipeline_mode=pl.Buffered(1)),
                  pl.BlockSpec((tm // IN_PROJ_NORM_STEPS, d),
                               lambda i, j: (jnp.minimum(i + 1, n_i - 1) * IN_PROJ_NORM_STEPS
                                             + jnp.minimum(j, IN_PROJ_NORM_STEPS - 1), 0)),
                  pl.BlockSpec((1, d), lambda i, j: (0, 0)),
                  pl.BlockSpec((tn, d), lambda i, j: (j, 0)),
                  pl.BlockSpec((LANES, d), lambda i, j: (0, 0), pipeline_mode=pl.Buffered(1))],
        out_specs=[pl.BlockSpec((tm, tn), lambda i, j: (i, j)),
                   pl.BlockSpec((tm, LANES), lambda i, j: (i, 0))],
        out_shape=[jax.ShapeDtypeStruct((m, n), BF16), jax.ShapeDtypeStruct((m, LANES), F32)],
        scratch_shapes=[pltpu.VMEM((2, tm, d), BF16)],
        compiler_params=_cparams(("arbitrary", "arbitrary")),
        name="in_proj",
    )(x2d, x2d, g, w_t, w_dt)


ROPE_PACK = LANES // ROT_DIM


def _rope_rows(pos_ref, expo_ref, spread_ref, cmask_ref, cos_ref, sin_ref, p0, n_rows):
    inv_freq = jnp.power(jnp.float32(ROPE_THETA), expo_ref[...])
    ang = pos_ref[p0:p0 + n_rows, :].astype(F32) * inv_freq
    packed = (jnp.cos(ang), jnp.sin(ang))
    fill = (1.0 - cmask_ref[...], jnp.zeros_like(cmask_ref[...]))
    for out_ref, tab, off in zip((cos_ref, sin_ref), packed, fill):
        parts = _split3(tab)
        for k in range(ROPE_PACK):
            rows = sum(jnp.dot(part, spread_ref[k], preferred_element_type=F32) for part in parts)
            out_ref[pl.ds(ROPE_PACK * p0 + k, n_rows, stride=ROPE_PACK), :] = rows + off


def rope_operands(positions):
    m = positions.size
    half = ROT_DIM // 2
    lane = jnp.arange(LANES)
    expo = (-2.0 * ((lane % ROT_DIM) % half).astype(F32) / ROT_DIM)[None, :]
    cmask = ((lane % ATTN_HEAD_DIM) < ROT_DIM).astype(F32)[None, :]
    d_dst = lane % ATTN_HEAD_DIM
    spread = ((d_dst[None, None, :] < ROT_DIM)
              & (lane[None, :, None] == ROT_DIM * jnp.arange(ROPE_PACK)[:, None, None] + d_dst[None, None, :])).astype(BF16)
    pos_packed = jnp.repeat(positions.reshape(m // ROPE_PACK, ROPE_PACK), ROT_DIM, axis=1)
    return pos_packed, expo, spread, cmask


def _rotate_half_matrix():
    half = ROT_DIM // 2
    src = jnp.arange(LANES)[:, None]
    dst = jnp.arange(LANES)[None, :]
    d = dst % ATTN_HEAD_DIM
    first = (d < half) & (src == dst + half)
    second = (d >= half) & (d < ROT_DIM) & (src == dst - half)
    return (second.astype(F32) - first.astype(F32)).astype(BF16)


def _rows(start, size, dil):
    return pl.ds(start, size) if dil == 1 else pl.ds(start, size, stride=dil)


ATTN_GROUP = 2


def _software_pipeline(stages, items, group, extras=()):
    groups = [items[i:i + group] for i in range(0, len(items), group)]
    n_trips = len(groups) + len(stages) - 1
    for t in range(n_trips):
        for lag, stage in enumerate(stages):
            if 0 <= t - lag < len(groups):
                for g, item in enumerate(groups[t - lag]):
                    stage(item, ((t - lag) % 2) * group + g)
        for k, extra in enumerate(extras):
            if k * n_trips // len(extras) == t:
                extra()


def _attn_kernel(n_cast, q0_ref, k0_ref, v0_ref, cos0_ref, sin0_ref, qn_ref, kn_ref, vn_ref, cosn_ref, sinn_ref,
                 gq_ref, gk_ref, seg_ref, rot_ref, *refs):
    cast_in, o_ref, cast_out = refs[:n_cast], refs[n_cast], refs[n_cast + 1:2 * n_cast + 1]
    set_a, set_b, ob, mb, lb, s_scr, p_scr = refs[2 * n_cast + 1:]
    s_len = qn_ref.shape[0]
    blk = ATTN_BLOCK
    step = pl.program_id(0) * pl.num_programs(1) + pl.program_id(1)
    lane = lax.broadcasted_iota(jnp.int32, (blk, LANES), 1)
    head0 = lane < ATTN_HEAD_DIM
    qi = lax.broadcasted_iota(jnp.int32, (2 * blk, 2 * blk), 0) % blk
    kj = lax.broadcasted_iota(jnp.int32, (2 * blk, 2 * blk), 1)
    band_mask = (kj >= qi) & (kj <= qi + blk)
    first_mask = (lax.broadcasted_iota(jnp.int32, (2 * blk, blk), 1)
                  <= lax.broadcasted_iota(jnp.int32, (2 * blk, blk), 0) % blk)

    for w_ref, w_out in zip(cast_in, cast_out):
        w_out[...] = w_ref[...].astype(BF16)

    prep_rows = 256

    def prep(srcs, dst, r0):
        q_ref, k_ref, v_ref, cos_ref, sin_ref = srcs
        qf, kf, vf = dst[:3]
        rows = slice(r0, r0 + prep_rows)
        cs, sn = cos_ref[rows, :], sin_ref[rows, :]

        def norm_rope(x_ref, g_ref):
            x = x_ref[rows, :].astype(F32)
            ss = jnp.dot((x * x).astype(BF16), seg_ref[...], preferred_element_type=F32)
            y = x * lax.rsqrt(ss * (1.0 / ATTN_HEAD_DIM) + EPS) * g_ref[...]
            return y * cs + jnp.dot(y.astype(BF16), rot_ref[...], preferred_element_type=F32) * sn

        qf[rows, :] = norm_rope(q_ref, gq_ref)
        kf[rows, :] = norm_rope(k_ref, gk_ref)
        vf[rows, :] = v_ref[rows, :].astype(F32)

    dmid = DILATIONS[1]
    run = s_len // dmid

    def deinterleave(bufs, r):
        for src, dst in zip(bufs[:3], bufs[3:]):
            dst[r * run:(r + 1) * run, :] = src[pl.ds(r, run, stride=dmid), :]

    def prepare(srcs, bufs):
        return ([functools.partial(prep, srcs, bufs, r0) for r0 in range(0, s_len, prep_rows)]
                + [functools.partial(deinterleave, bufs, r) for r in range(dmid)])

    @pl.when(step == 0)
    def _():
        for piece in prepare((q0_ref, k0_ref, v0_ref, cos0_ref, sin0_ref), set_a):
            piece()

    def block_descs(bufs):
        natural, deint = bufs[:3], bufs[3:]
        descs = [(deint, 1, 1, r * run, r * run, blk, first_mask) for r in range(dmid)]
        descs += [(deint, 1, 1, r * run + n * blk, r * run + (n - 1) * blk, 2 * blk, band_mask)
                  for n in range(1, run // blk) for r in range(dmid)]
        descs += [(deint, 2, dmid, r_hi * run + r_lo, r_hi * run + r_lo, blk, first_mask)
                  for r_hi in range(dmid) for r_lo in range(dmid)]
        descs += [(natural, 0, 1, 0, 0, blk, first_mask)]
        descs += [(natural, 0, 1, n * blk, (n - 1) * blk, 2 * blk, band_mask) for n in range(1, s_len // blk)]
        return descs

    def scores(desc, slot):
        (q_src, k_src, _), _, dil, q_start, k_start, n_keys, _ = desc
        qt = q_src[_rows(q_start, blk, dil), :]
        zero = jnp.zeros_like(qt)
        q2 = jnp.concatenate([jnp.where(head0, qt, zero), jnp.where(head0, zero, qt)], axis=0).astype(BF16)
        kb = k_src[_rows(k_start, n_keys, dil), :].astype(BF16)
        s_scr[slot, :, :n_keys] = lax.dot_general(q2, kb, (((1,), (1,)), ((), ())), preferred_element_type=F32)

    def softmax(desc, slot):
        _, br, dil, q_start, _, n_keys, mask = desc
        s = jnp.where(mask, s_scr[slot, :, :n_keys], -jnp.inf)
        m = jnp.max(s, axis=-1, keepdims=True)
        p_scr[slot, :, :n_keys] = jnp.exp2(s - m).astype(BF16)
        mb[br, _rows(q_start, blk, dil), :] = jnp.where(head0, m[:blk], m[blk:])

    def values(desc, slot):
        (_, _, v_src), br, dil, q_start, k_start, n_keys, _ = desc
        vb = v_src[_rows(k_start, n_keys, dil), :].astype(BF16)
        v1 = jnp.concatenate([vb, jnp.ones_like(vb)], axis=1)
        r = jnp.dot(p_scr[slot, :, :n_keys], v1, preferred_element_type=F32)
        out_rows = _rows(q_start, blk, dil)
        ob[br, out_rows, :] = jnp.where(head0, r[:blk, :LANES], r[blk:, :LANES])
        lb[br, out_rows, :] = jnp.where(head0, r[:blk, LANES:], r[blk:, LANES:])
        if br == 0:
            merge(q_start)

    def merge(n0):
        sub = blk // dmid
        for r in range(dmid):
            nat_rows = pl.ds(n0 + r, sub, stride=dmid)
            d_rows = slice(r * run + n0 // dmid, r * run + n0 // dmid + sub)
            rows = (nat_rows, d_rows, d_rows)
            ms = [mb[g, rows[g], :] for g in range(len(DILATIONS))]
            m = functools.reduce(jnp.maximum, ms)
            ws = [jnp.exp2(mg - m) for mg in ms]
            num = functools.reduce(jnp.add, [w * ob[g, rows[g], :] for g, w in enumerate(ws)])
            den = functools.reduce(jnp.add, [w * lb[g, rows[g], :] for g, w in enumerate(ws)])
            ob[0, nat_rows, :] = num / den
        o_ref[n0:n0 + blk, :] = ob[0, n0:n0 + blk, :].astype(o_ref.dtype)

    for parity, (cur_set, nxt_set) in enumerate(((set_a, set_b), (set_b, set_a))):
        @pl.when(step % 2 == parity)
        def _(cur_set=cur_set, nxt_set=nxt_set):
            _software_pipeline((scores, softmax, values), block_descs(cur_set), ATTN_GROUP,
                               extras=prepare((qn_ref, kn_ref, vn_ref, cosn_ref, sinn_ref), nxt_set))


def dilated_attention(qkv, tabs, gq2, gk2, weights, bsz, s_len, n_pairs):
    cos_t, sin_t = tabs
    d1, d2, d3 = DILATIONS
    assert d1 == 1 and d3 == d2 * d2 and s_len == d3 * ATTN_BLOCK, "layout assumes dilations (1, d, d*d), one block per largest class"
    lane = jnp.arange(LANES)
    seg = (lane[:, None] // ATTN_HEAD_DIM == lane[None, :] // ATTN_HEAD_DIM).astype(BF16)
    n_steps = bsz * n_pairs

    def nxt(b, p):
        s = jnp.minimum(b * n_pairs + p + 1, n_steps - 1)
        return s // n_pairs, s % n_pairs

    once = dict(pipeline_mode=pl.Buffered(1))
    first = lambda off: pl.BlockSpec((s_len, LANES), lambda b, p, off=off: (0, off), **once)
    nblk = lambda off: pl.BlockSpec((s_len, LANES), lambda b, p, off=off: (nxt(b, p)[0], off + nxt(b, p)[1]))
    tab0 = pl.BlockSpec((s_len, LANES), lambda b, p: (0, 0), **once)
    tabn = pl.BlockSpec((s_len, LANES), lambda b, p: (nxt(b, p)[0], 0))
    row = pl.BlockSpec((1, LANES), lambda b, p: (0, 0))
    sq = pl.BlockSpec((LANES, LANES), lambda b, p: (0, 0))
    w_specs = [pl.BlockSpec((w.shape[0] // n_steps, w.shape[1]), lambda b, p: (b * n_pairs + p, 0)) for w in weights]
    return pl.pallas_call(
        functools.partial(_attn_kernel, len(weights)),
        grid=(bsz, n_pairs),
        in_specs=[first(0), first(n_pairs), first(2 * n_pairs), tab0, tab0,
                  nblk(0), nblk(n_pairs), nblk(2 * n_pairs), tabn, tabn,
                  row, row, sq, sq] + w_specs,
        out_specs=[pl.BlockSpec((s_len, LANES), lambda b, p: (b, p))] + w_specs,
        out_shape=[jax.ShapeDtypeStruct((bsz * s_len, n_pairs * LANES), BF16)]
                  + [jax.ShapeDtypeStruct(w.shape, BF16) for w in weights],
        scratch_shapes=[[pltpu.VMEM((s_len, LANES), F32)] * 6] * 2
                       + [pltpu.VMEM((len(DILATIONS), s_len, LANES), F32)] * 3
                       + [pltpu.VMEM((2 * ATTN_GROUP, 2 * ATTN_BLOCK, 2 * ATTN_BLOCK), F32),
                          pltpu.VMEM((2 * ATTN_GROUP, 2 * ATTN_BLOCK, 2 * ATTN_BLOCK), BF16)],
        compiler_params=_cparams(("arbitrary", "arbitrary")),
        name="dilated_attn",
    )(qkv, qkv, qkv, cos_t, sin_t, qkv, qkv, qkv, cos_t, sin_t, gq2, gk2, seg, _rotate_half_matrix(), *weights)


def _silu(x):
    h = 0.5 * x
    return h * jnp.tanh(h) + h


LOG2E = 1.4426950408889634
SSD_STEP_CHUNKS = 4


def _ssd_kernel(xbc_ref, z_ref, dt_ref, cw_ref, cb_ref, dtb_ref, alog_ref, dskip_ref, g_ref,
                expand_ref, tril_ref, o_ref, xpad, xc, st):
    q = SSD_CHUNK
    n_rows = xbc_ref.shape[0]
    d_ssm = z_ref.shape[1]
    d_conv = xbc_ref.shape[1]
    gw = d_ssm // SSM_GROUPS
    heads_per_group = gw // SSM_HEAD_DIM
    pad = 8

    @pl.when(pl.program_id(1) == 0)
    def _():
        xpad[:, 0:pad, :] = jnp.zeros((d_conv // LANES, pad, LANES), F32)
        st[...] = jnp.zeros_like(st)

    for c0 in range(0, d_conv, LANES):
        cols = slice(c0, c0 + LANES)
        slab = c0 // LANES
        xpad[slab, pad:pad + n_rows, :] = xbc_ref[:, cols].astype(F32)
        acc = cb_ref[:, cols] + cw_ref[CONV_WIDTH - 1:CONV_WIDTH, cols] * xpad[slab, pad:pad + n_rows, :]
        for w in range(CONV_WIDTH - 1):
            off = pad - (CONV_WIDTH - 1) + w
            acc = acc + cw_ref[w:w + 1, cols] * xpad[slab, pl.ds(off, n_rows, stride=1), :]
        xc[:, cols] = _silu(acc)
    xpad[:, 0:pad, :] = xpad[:, n_rows:n_rows + pad, :]

    li = lax.broadcasted_iota(jnp.int32, (q, q), 0)
    si = lax.broadcasted_iota(jnp.int32, (q, q), 1)
    causal = li >= si
    lane_g = lax.broadcasted_iota(jnp.int32, (q, gw), 1)
    tril = tril_ref[...]
    expand = expand_ref[...]

    for r0 in range(0, n_rows, q):
        rows = slice(r0, r0 + q)
        x_dt = dt_ref[rows, :] + dtb_ref[...]
        dt = jnp.maximum(x_dt, 0.0) + jnp.log1p(jnp.exp(-jnp.abs(x_dt)))
        dta = dt * (-LOG2E * jnp.exp(alog_ref[...]))
        acs = sum(jnp.dot(tril, part, preferred_element_type=F32) for part in _split3(dta))
        acs_t = acs.T
        last = acs[q - 1:q, :]
        dt_e = jnp.dot(dt.astype(BF16), expand, preferred_element_type=F32)
        dec_e = jnp.dot(jnp.exp2(acs).astype(BF16), expand, preferred_element_type=F32)
        w_e = jnp.dot((jnp.exp2(last - acs) * dt).astype(BF16), expand, preferred_element_type=F32)

        for g in range(SSM_GROUPS):
            cols = slice(g * gw, (g + 1) * gw)
            b_f = xc[rows, d_ssm + g * SSM_STATE:d_ssm + (g + 1) * SSM_STATE]
            c_b = xc[rows, d_ssm + (SSM_GROUPS + g) * SSM_STATE:d_ssm + (SSM_GROUPS + g + 1) * SSM_STATE].astype(BF16)
            xs = xc[rows, cols]
            cb = lax.dot_general(c_b, b_f.astype(BF16), (((1,), (1,)), ((), ())), preferred_element_type=F32)
            xdt = (xs * dt_e[:, cols]).astype(BF16)
            ws, rs = [], []
            for hh in range(heads_per_group):
                h = g * heads_per_group + hh
                seg = acs[:, h:h + 1] - acs_t[h:h + 1, :]
                l_mat = jnp.exp2(jnp.where(causal, seg, -jnp.inf))
                ws.append((cb * l_mat).astype(BF16))
                in_head = (lane_g >= hh * SSM_HEAD_DIM) & (lane_g < (hh + 1) * SSM_HEAD_DIM)
                rs.append(jnp.where(in_head, xdt, jnp.zeros_like(xdt)))
            y = jnp.dot(jnp.concatenate(ws, axis=1), jnp.concatenate(rs, axis=0), preferred_element_type=F32)
            st_g = st[:, cols]
            y = y + jnp.dot(c_b, st_g.astype(BF16), preferred_element_type=F32) * dec_e[:, cols]
            xw = (xs * w_e[:, cols]).astype(BF16)
            st[:, cols] = st_g * dec_e[q - 1:q, cols] + jnp.dot(b_f.T.astype(BF16), xw, preferred_element_type=F32)
            y = y + dskip_ref[:, cols] * xs
            y = y * _silu(z_ref[rows, cols].astype(F32))
            o_ref[rows, cols] = _rms_rows(y, g_ref[:, cols]).astype(o_ref.dtype)


def ssd_mixer(packed, dt_raw, conv_w, conv_b, dt_bias, a_log, d_skip, g_out, bsz, s_len, d_ssm, xbc_col, z_col):
    n_heads = d_ssm // SSM_HEAD_DIM
    d_conv = d_ssm + 2 * SSM_GROUPS * SSM_STATE
    step_rows = SSD_STEP_CHUNKS * SSD_CHUNK
    nc = s_len // step_rows
    padl = lambda v: jnp.pad(v.astype(F32), (0, LANES - n_heads))[None, :]
    expand = (jnp.arange(LANES)[:, None] == (jnp.arange(d_ssm) // SSM_HEAD_DIM)[None, :]).astype(BF16)
    tril = (jnp.arange(SSD_CHUNK)[:, None] >= jnp.arange(SSD_CHUNK)[None, :]).astype(BF16)
    full = lambda shape: pl.BlockSpec(shape, lambda b, c: (0, 0))
    return pl.pallas_call(
        _ssd_kernel,
        grid=(bsz, nc),
        in_specs=[pl.BlockSpec((step_rows, d_conv), lambda b, c: (b * nc + c, xbc_col // d_conv)),
                  pl.BlockSpec((step_rows, d_ssm), lambda b, c: (b * nc + c, z_col // d_ssm)),
                  pl.BlockSpec((step_rows, LANES), lambda b, c: (b * nc + c, 0)),
                  full((CONV_WIDTH, d_conv)), full((1, d_conv)), full((1, LANES)), full((1, LANES)),
                  full((1, d_ssm)), full((1, d_ssm)), full((LANES, d_ssm)), full((SSD_CHUNK, SSD_CHUNK))],
        out_specs=pl.BlockSpec((step_rows, d_ssm), lambda b, c: (b * nc + c, 0)),
        out_shape=jax.ShapeDtypeStruct((bsz * s_len, d_ssm), BF16),
        scratch_shapes=[pltpu.VMEM((d_conv // LANES, step_rows + 8, LANES), F32),
                        pltpu.VMEM((step_rows, d_conv), F32),
                        pltpu.VMEM((SSM_STATE, d_ssm), F32)],
        compiler_params=_cparams(("parallel", "arbitrary")),
        name="ssd",
    )(packed, packed, dt_raw, conv_w.astype(F32), conv_b.astype(F32)[None, :], padl(dt_bias), padl(a_log),
      jnp.repeat(d_skip.astype(F32), SSM_HEAD_DIM)[None, :], g_out.astype(F32)[None, :], expand, tril)


def _mem_kv_rope_kernel(mem_ref, g_ref, w_ref, gk_ref, pos_ref, expo_ref, spread_ref, cmask_ref,
                        kv_ref, cos_ref, sin_ref, h_ref, wb_ref):
    d_cross = N_CROSS_HEADS * CROSS_HEAD_DIM

    @pl.when(pl.program_id(0) == 0)
    def _():
        wb_ref[...] = w_ref[...].astype(BF16)

    _norm_chunk_to(mem_ref, g_ref, h_ref, 0, mem_ref.shape[0])
    kv = jnp.dot(h_ref[...], wb_ref[...], preferred_element_type=F32)
    for h in range(N_CROSS_HEADS):
        cols = slice(h * CROSS_HEAD_DIM, (h + 1) * CROSS_HEAD_DIM)
        kv_ref[:, cols] = _rms_rows(kv[:, cols], gk_ref[...]).astype(kv_ref.dtype)
    kv_ref[:, d_cross:] = kv[:, d_cross:].astype(kv_ref.dtype)
    _rope_rows(pos_ref, expo_ref, spread_ref, cmask_ref, cos_ref, sin_ref, 0, pos_ref.shape[0])


def mem_kv_rope(mem2d, g, w, gk, positions, n_steps=4):
    m, d = mem2d.shape
    n = w.shape[1]
    n_tok = positions.size
    pos_packed, expo, spread, cmask = rope_operands(positions)
    n_steps = min(n_steps, m // 256)
    tm, tp = m // n_steps, pos_packed.shape[0] // n_steps
    full2 = lambda shape, **kw: pl.BlockSpec(shape, lambda i: (0, 0), **kw)
    rows = lambda r, c: pl.BlockSpec((r, c), lambda i: (i, 0))
    return pl.pallas_call(
        _mem_kv_rope_kernel,
        grid=(n_steps,),
        in_specs=[rows(tm, d), full2((1, d)), full2((d, n), pipeline_mode=pl.Buffered(1)), full2((1, CROSS_HEAD_DIM)),
                  rows(tp, LANES), full2((1, LANES)),
                  pl.BlockSpec((ROPE_PACK, LANES, LANES), lambda i: (0, 0, 0)), full2((1, LANES))],
        out_specs=[rows(tm, n), rows(tp * ROPE_PACK, LANES), rows(tp * ROPE_PACK, LANES)],
        out_shape=[jax.ShapeDtypeStruct((m, n), BF16)] + [jax.ShapeDtypeStruct((n_tok, LANES), F32)] * 2,
        scratch_shapes=[pltpu.VMEM((tm, d), BF16), pltpu.VMEM((d, n), BF16)],
        compiler_params=_cparams(("arbitrary",)),
        name="mem_kv_rope",
    )(mem2d, g, w, gk, pos_packed, expo, spread, cmask)


def _mix_cross_kernel(attn_ref, ssm_ref, x_ref, ga_ref, w_ref, gc_ref, wq_ref, gq_ref, k_ref, v_ref, wo_ref, o_ref,
                      h_ref, xk_ref, an_ref, q_ref, a_ref, s_scr, p_scr):
    s = pl.program_id(0)
    n_tiles = pl.num_programs(0) - 1
    tm, d_attn = x_ref.shape[0], attn_ref.shape[1]
    rows = CROSS_ROWS
    prev, cur = (s + 1) % 2, s % 2

    @pl.when(s > 0)
    def _():
        q_ref[...] = jnp.dot(h_ref[prev], wq_ref[...], preferred_element_type=F32)

        def scores(item, slot):
            r0, h = item
            cols = slice(h * CROSS_HEAD_DIM, (h + 1) * CROSS_HEAD_DIM)
            qh = (_rms_rows(q_ref[r0:r0 + rows, cols], gq_ref[...]) * (CROSS_HEAD_DIM ** -0.5)).astype(BF16)
            s_scr[slot] = lax.dot_general(qh, k_ref[:, cols], (((1,), (1,)), ((), ())), preferred_element_type=F32)

        def softmax(item, slot):
            sc = s_scr[slot]
            p_scr[slot] = jnp.exp(sc - jnp.max(sc, axis=-1, keepdims=True)).astype(BF16)

        def values(item, slot):
            r0, h = item
            cols = slice(h * CROSS_HEAD_DIM, (h + 1) * CROSS_HEAD_DIM)
            vh = v_ref[:, cols]
            r = jnp.dot(p_scr[slot], jnp.concatenate([vh, jnp.ones_like(vh)], axis=1), preferred_element_type=F32)
            a_ref[r0:r0 + rows, cols] = (r[:, :CROSS_HEAD_DIM] / r[:, CROSS_HEAD_DIM:]).astype(BF16)

        items = [(r0, h) for r0 in range(0, tm, rows) for h in range(N_CROSS_HEADS)]
        _software_pipeline((scores, softmax, values), items, CROSS_GROUP)
        o_ref[...] = xk_ref[prev] + jnp.dot(a_ref[...], wo_ref[...], preferred_element_type=F32)

    @pl.when(s < n_tiles)
    def _():
        n_split = 2
        part = tm // n_split
        for r0 in range(0, tm, part):
            for r in range(r0, r0 + part, 32):
                an_ref[r:r + 32, :] = _rms_rows(attn_ref[r:r + 32, :].astype(F32), ga_ref[...]).astype(BF16)
            rs = slice(r0, r0 + part)
            xk_ref[cur, rs, :] = (x_ref[rs, :] + jnp.dot(an_ref[rs, :], w_ref[:d_attn, :], preferred_element_type=F32)
                                  + jnp.dot(ssm_ref[rs, :], w_ref[d_attn:, :], preferred_element_type=F32))
            _norm_chunk_to(xk_ref.at[cur], gc_ref, h_ref.at[cur], r0, part)


def mix_cross(attn, ssm, x2d, ga, w, gc, wq, gq, kv, wo, s_len, n_mem, tm=512):
    m, d = x2d.shape
    da, ds_ = attn.shape[1], ssm.shape[1]
    dc = wq.shape[1]
    n_tiles = m // tm
    per_batch = s_len // tm
    ahead = lambda s: (jnp.minimum(s, n_tiles - 1), 0)
    behind = lambda s: jnp.maximum(s - 1, 0)
    full = lambda shape, **kw: pl.BlockSpec(shape, lambda s: (0, 0), **kw)
    once = dict(pipeline_mode=pl.Buffered(1))
    return pl.pallas_call(
        _mix_cross_kernel,
        grid=(n_tiles + 1,),
        in_specs=[pl.BlockSpec((tm, da), ahead), pl.BlockSpec((tm, ds_), ahead), pl.BlockSpec((tm, d), ahead),
                  full((1, da)), full((da + ds_, d), **once), full((1, d)), full((d, dc), **once),
                  full((1, CROSS_HEAD_DIM)),
                  pl.BlockSpec((n_mem, dc), lambda s: (behind(s) // per_batch, 0)),
                  pl.BlockSpec((n_mem, dc), lambda s: (behind(s) // per_batch, 1)),
                  full((dc, d), **once)],
        out_specs=pl.BlockSpec((tm, d), lambda s: (behind(s), 0)),
        out_shape=jax.ShapeDtypeStruct((m, d), F32),
        scratch_shapes=[pltpu.VMEM((2, tm, d), BF16), pltpu.VMEM((2, tm, d), F32), pltpu.VMEM((tm, da), BF16),
                        pltpu.VMEM((tm, dc), F32), pltpu.VMEM((tm, dc), BF16),
                        pltpu.VMEM((2 * CROSS_GROUP, CROSS_ROWS, n_mem), F32),
                        pltpu.VMEM((2 * CROSS_GROUP, CROSS_ROWS, n_mem), BF16)],
        compiler_params=_cparams(("arbitrary",)),
        name="mix_cross",
    )(attn, ssm, x2d, ga, w, gc, wq, gq, kv, kv, wo)


def _mlp_kernel(x_ref, g_ref, wu_ref, wd_ref, o_ref, h_ref):
    tm = x_ref.shape[0]

    def expand(rows):
        u = jnp.dot(h_ref[rows, :], wu_ref[...], preferred_element_type=F32)
        u = jnp.square(jnp.maximum(u, 0.0)).astype(BF16)
        return jnp.dot(u, wd_ref[...], preferred_element_type=F32)

    @pl.when(pl.program_id(1) == 0)
    def _():
        part = tm // 2
        for r0 in range(0, tm, part):
            _norm_chunk_to(x_ref, g_ref, h_ref, r0, part)
            rows = slice(r0, r0 + part)
            o_ref[rows, :] = x_ref[rows, :] + expand(rows)

    @pl.when(pl.program_id(1) > 0)
    def _():
        o_ref[...] += expand(slice(None))


def mlp(x2d, g, wu, wd, tm=512, tf=2048):
    m, d = x2d.shape
    f = wu.shape[1]
    return pl.pallas_call(
        _mlp_kernel,
        grid=(m // tm, f // tf),
        in_specs=[pl.BlockSpec((tm, d), lambda i, j: (i, 0)),
                  pl.BlockSpec((1, d), lambda i, j: (0, 0)),
                  pl.BlockSpec((d, tf), lambda i, j: (0, j)),
                  pl.BlockSpec((tf, d), lambda i, j: (j, 0))],
        out_specs=pl.BlockSpec((tm, d), lambda i, j: (i, 0)),
        out_shape=jax.ShapeDtypeStruct((m, d), F32),
        scratch_shapes=[pltpu.VMEM((tm, d), BF16)],
        compiler_params=_cparams(("parallel", "arbitrary")),
        name="mlp",
    )(x2d, g, wu, wd)


def _layer(x2d, mem2d, positions, bsz, s_len, n_mem, g_mix, w_in, g_q, g_k, g_attn_out, conv_w, conv_b, dt_bias,
           a_log, d_skip, g_ssm_out, w_out, g_cross, g_mem, w_cq, w_ckv, g_cq, g_ck, w_co, g_mlp, w_up, w_down):
    d_model = x2d.shape[1]
    d_attn = d_model // 2
    d_ssm = d_model // 2
    n_pairs = d_attn // LANES
    n_ssm_heads = d_ssm // SSM_HEAD_DIM
    d_conv = d_ssm + 2 * SSM_GROUPS * SSM_STATE
    d_packed = 3 * d_attn + d_ssm + d_conv
    row = lambda v: v.astype(F32)[None, :]

    w_in_t = w_in.T
    w_dt_t = jnp.pad(w_in_t[d_packed:], ((0, LANES - n_ssm_heads), (0, 0)))
    packed, dt_raw = in_proj(x2d, row(g_mix), w_in_t, w_dt_t, d_packed)

    kv, cos_t, sin_t = mem_kv_rope(mem2d, row(g_mem), w_ckv, row(g_ck), positions)
    tabs = (cos_t, sin_t)
    attn, w_up_bf, w_down_bf, w_out_bf, w_cq_bf, w_co_bf = dilated_attention(
        packed, tabs, row(jnp.tile(g_q, 2)) * (ATTN_HEAD_DIM ** -0.5 * LOG2E), row(jnp.tile(g_k, 2)),
        (w_up, w_down, w_out, w_cq, w_co), bsz, s_len, n_pairs)
    ssm = ssd_mixer(packed, dt_raw, conv_w, conv_b, dt_bias, a_log, d_skip, g_ssm_out, bsz, s_len, d_ssm,
                    xbc_col=3 * d_attn + d_ssm, z_col=3 * d_attn)
    x2d = mix_cross(attn, ssm, x2d, row(g_attn_out), w_out_bf, row(g_cross), w_cq_bf, row(g_cq), kv, w_co_bf,
                    s_len, n_mem)

    return mlp(x2d, row(g_mlp), w_up_bf, w_down_bf)


def kernel(x, mem, positions, g_mix, w_in, g_q, g_k, g_attn_out, conv_w, conv_b, dt_bias, a_log, d_skip, g_ssm_out,
           w_out, g_cross, g_mem, w_cq, w_ckv, g_cq, g_ck, w_co, g_mlp, w_up, w_down):
    bsz, s_len, d_model = x.shape
    n_mem = mem.shape[1]
    x2d = x.reshape(bsz * s_len, d_model)
    mem2d = mem.reshape(bsz * n_mem, d_model)
    for i in range(g_mix.shape[0]):
        x2d = _layer(x2d, mem2d, positions, bsz, s_len, n_mem, g_mix[i], w_in[i], g_q[i], g_k[i], g_attn_out[i],
                     conv_w[i], conv_b[i], dt_bias[i], a_log[i], d_skip[i], g_ssm_out[i], w_out[i], g_cross[i],
                     g_mem[i], w_cq[i], w_ckv[i], g_cq[i], g_ck[i], w_co[i], g_mlp[i], w_up[i], w_down[i])
    return x2d.reshape(bsz, s_len, d_model)
```

```python
import functools
import math

import jax
import jax.numpy as jnp
from jax import lax
from jax.experimental import pallas as pl
from jax.experimental.pallas import tpu as pltpu

F32 = jnp.float32
BF16 = jnp.bfloat16
EPS = 1e-6

LANES = 128
ATTN_HEAD_DIM = 64
ROT_DIM = ATTN_HEAD_DIM // 4
ROPE_THETA = 500000.0
ATTN_BLOCK = 128
DILATIONS = (1, 4, 16)
SSM_HEAD_DIM = 64
SSM_GROUPS = 4
SSM_STATE = 128
CONV_WIDTH = 4
SSD_CHUNK = 128
N_CROSS_HEADS = 4
CROSS_HEAD_DIM = 128
CROSS_ROWS = 256
CROSS_GROUP = 2
VMEM_LIMIT = 58 * 1024 * 1024


def _cparams(sem):
    return pltpu.CompilerParams(dimension_semantics=sem, vmem_limit_bytes=VMEM_LIMIT)


def _rms_rows(x, g):
    ms = jnp.mean(x * x, axis=-1, keepdims=True)
    return x * lax.rsqrt(ms + EPS) * g


def _split3(x):
    hi = x.astype(BF16)
    r1 = x - hi.astype(F32)
    mid = r1.astype(BF16)
    lo = (r1 - mid.astype(F32)).astype(BF16)
    return hi, mid, lo


def _norm_rows_to(x_ref, g_ref, h_ref, rows=16):
    def body(c, carry):
        r0 = pl.multiple_of(c * rows, rows)
        h_ref[pl.ds(r0, rows), :] = _rms_rows(x_ref[pl.ds(r0, rows), :], g_ref[...]).astype(h_ref.dtype)
        return carry
    lax.fori_loop(0, x_ref.shape[0] // rows, body, 0, unroll=4)


def _norm_chunk_to(x_ref, g_ref, h_ref, start, n_rows, rows=16):
    for r in range(0, n_rows, rows):
        rs = pl.ds(start + r, rows)
        h_ref[rs, :] = _rms_rows(x_ref[rs, :], g_ref[...]).astype(h_ref.dtype)


def _in_proj_kernel(x0_ref, xn_ref, g_ref, w_ref, wdt_ref, o_ref, dt_ref, h_ref):
    i, j = pl.program_id(0), pl.program_id(1)
    cur, nxt = i % 2, (i + 1) % 2

    @pl.when((i == 0) & (j == 0))
    def _():
        _norm_rows_to(x0_ref, g_ref, h_ref.at[0])

    @pl.when(j == 0)
    def _():
        dt_ref[...] = lax.dot_general(h_ref[cur], wdt_ref[...].astype(BF16), (((1,), (1,)), ((), ())),
                                      preferred_element_type=F32)
    o_ref[...] = lax.dot_general(h_ref[cur], w_ref[...].astype(BF16), (((1,), (1,)), ((), ())),
                                 preferred_element_type=F32).astype(o_ref.dtype)
    chunk = xn_ref.shape[0]
    start = pl.multiple_of(jnp.minimum(j, IN_PROJ_NORM_STEPS - 1) * chunk, chunk)
    for r in range(0, chunk, 16):
        h_ref[nxt, pl.ds(start + r, 16), :] = _rms_rows(xn_ref[r:r + 16, :], g_ref[...]).astype(BF16)


IN_PROJ_NORM_STEPS = 4


def in_proj(x2d, g, w_t, w_dt, n, tm=1024, tn=1536):
    m, d = x2d.shape
    n_i = m // tm
    assert n // tn >= IN_PROJ_NORM_STEPS
    return pl.pallas_call(
        _in_proj_kernel,
        grid=(n_i, n // tn),
        in_specs=[pl.BlockSpec((tm, d), lambda i, j: (0, 0), pipeline_mode=pl.Buffered(1)),
                  pl.BlockSpec((tm // IN_PROJ_NORM_STEPS, d),
                               lambda i, j: (jnp.minimum(i + 1, n_i - 1) * IN_PROJ_NORM_STEPS
                                             + jnp.minimum(j, IN_PROJ_NORM_STEPS - 1), 0)),
                  pl.BlockSpec((1, d), lambda i, j: (0, 0)),
                  pl.BlockSpec((tn, d), lambda i, j: (j, 0)),
                  pl.BlockSpec((LANES, d), lambda i, j: (0, 0), pipeline_mode=pl.Buffered(1))],
        out_specs=[pl.BlockSpec((tm, tn), lambda i, j: (i, j)),
                   pl.BlockSpec((tm, LANES), lambda i, j: (i, 0))],
        out_shape=[jax.ShapeDtypeStruct((m, n), BF16), jax.ShapeDtypeStruct((m, LANES), F32)],
        scratch_shapes=[pltpu.VMEM((2, tm, d), BF16)],
        compiler_params=_cparams(("arbitrary", "arbitrary")),
        name="in_proj",
    )(x2d, x2d, g, w_t, w_dt)


ROPE_PACK = LANES // ROT_DIM


def _rope_rows(pos_ref, expo_ref, spread_ref, cmask_ref, cos_ref, sin_ref, p0, n_rows):
    inv_freq = jnp.power(jnp.float32(ROPE_THETA), expo_ref[...])
    ang = pos_ref[p0:p0 + n_rows, :].astype(F32) * inv_freq
    packed = (jnp.cos(ang), jnp.sin(ang))
    fill = (1.0 - cmask_ref[...], jnp.zeros_like(cmask_ref[...]))
    for out_ref, tab, off in zip((cos_ref, sin_ref), packed, fill):
        parts = _split3(tab)
        for k in range(ROPE_PACK):
            rows = sum(jnp.dot(part, spread_ref[k], preferred_element_type=F32) for part in parts)
            out_ref[pl.ds(ROPE_PACK * p0 + k, n_rows, stride=ROPE_PACK), :] = rows + off


def rope_operands(positions):
    m = positions.size
    half = ROT_DIM // 2
    lane = jnp.arange(LANES)
    expo = (-2.0 * ((lane % ROT_DIM) % half).astype(F32) / ROT_DIM)[None, :]
    cmask = ((lane % ATTN_HEAD_DIM) < ROT_DIM).astype(F32)[None, :]
    d_dst = lane % ATTN_HEAD_DIM
    spread = ((d_dst[None, None, :] < ROT_DIM)
              & (lane[None, :, None] == ROT_DIM * jnp.arange(ROPE_PACK)[:, None, None] + d_dst[None, None, :])).astype(BF16)
    pos_packed = jnp.repeat(positions.reshape(m // ROPE_PACK, ROPE_PACK), ROT_DIM, axis=1)
    return pos_packed, expo, spread, cmask


def _rotate_half_matrix():
    half = ROT_DIM // 2
    src = jnp.arange(LANES)[:, None]
    dst = jnp.arange(LANES)[None, :]
    d = dst % ATTN_HEAD_DIM
    first = (d < half) & (src == dst + half)
    second = (d >= half) & (d < ROT_DIM) & (src == dst - half)
    return (second.astype(F32) - first.astype(F32)).astype(BF16)


def _rows(start, size, dil):
    return pl.ds(start, size) if dil == 1 else pl.ds(start, size, stride=dil)


ATTN_GROUP = 2


def _software_pipeline(stages, items, group, extras=()):
    groups = [items[i:i + group] for i in range(0, len(items), group)]
    n_trips = len(groups) + len(stages) - 1
    for t in range(n_trips):
        for lag, stage in enumerate(stages):
            if 0 <= t - lag < len(groups):
                for g, item in enumerate(groups[t - lag]):
                    stage(item, ((t - lag) % 2) * group + g)
        for k, extra in enumerate(extras):
            if k * n_trips // len(extras) == t:
                extra()


def _attn_kernel(n_cast, q0_ref, k0_ref, v0_ref, cos0_ref, sin0_ref, qn_ref, kn_ref, vn_ref, cosn_ref, sinn_ref,
                 gq_ref, gk_ref, seg_ref, rot_ref, *refs):
    cast_in, o_ref, cast_out = refs[:n_cast], refs[n_cast], refs[n_cast + 1:2 * n_cast + 1]
    set_a, set_b, ob, mb, lb, s_scr, p_scr = refs[2 * n_cast + 1:]
    s_len = qn_ref.shape[0]
    blk = ATTN_BLOCK
    step = pl.program_id(0) * pl.num_programs(1) + pl.program_id(1)
    lane = lax.broadcasted_iota(jnp.int32, (blk, LANES), 1)
    head0 = lane < ATTN_HEAD_DIM
    qi = lax.broadcasted_iota(jnp.int32, (2 * blk, 2 * blk), 0) % blk
    kj = lax.broadcasted_iota(jnp.int32, (2 * blk, 2 * blk), 1)
    band_mask = (kj >= qi) & (kj <= qi + blk)
    first_mask = (lax.broadcasted_iota(jnp.int32, (2 * blk, blk), 1)
                  <= lax.broadcasted_iota(jnp.int32, (2 * blk, blk), 0) % blk)

    for w_ref, w_out in zip(cast_in, cast_out):
        w_out[...] = w_ref[...].astype(BF16)

    prep_rows = 256

    def prep(srcs, dst, r0):
        q_ref, k_ref, v_ref, cos_ref, sin_ref = srcs
        qf, kf, vf = dst[:3]
        rows = slice(r0, r0 + prep_rows)
        cs, sn = cos_ref[rows, :], sin_ref[rows, :]

        def norm_rope(x_ref, g_ref):
            x = x_ref[rows, :].astype(F32)
            ss = jnp.dot((x * x).astype(BF16), seg_ref[...], preferred_element_type=F32)
            y = x * lax.rsqrt(ss * (1.0 / ATTN_HEAD_DIM) + EPS) * g_ref[...]
            return y * cs + jnp.dot(y.astype(BF16), rot_ref[...], preferred_element_type=F32) * sn

        qf[rows, :] = norm_rope(q_ref, gq_ref)
        kf[rows, :] = norm_rope(k_ref, gk_ref)
        vf[rows, :] = v_ref[rows, :].astype(F32)

    dmid = DILATIONS[1]
    run = s_len // dmid

    def deinterleave(bufs, r):
        for src, dst in zip(bufs[:3], bufs[3:]):
            dst[r * run:(r + 1) * run, :] = src[pl.ds(r, run, stride=dmid), :]

    def prepare(srcs, bufs):
        return ([functools.partial(prep, srcs, bufs, r0) for r0 in range(0, s_len, prep_rows)]
                + [functools.partial(deinterleave, bufs, r) for r in range(dmid)])

    @pl.when(step == 0)
    def _():
        for piece in prepare((q0_ref, k0_ref, v0_ref, cos0_ref, sin0_ref), set_a):
            piece()

    def block_descs(bufs):
        natural, deint = bufs[:3], bufs[3:]
        descs = [(deint, 1, 1, r * run, r * run, blk, first_mask) for r in range(dmid)]
        descs += [(deint, 1, 1, r * run + n * blk, r * run + (n - 1) * blk, 2 * blk, band_mask)
                  for n in range(1, run // blk) for r in range(dmid)]
        descs += [(deint, 2, dmid, r_hi * run + r_lo, r_hi * run + r_lo, blk, first_mask)
                  for r_hi in range(dmid) for r_lo in range(dmid)]
        descs += [(natural, 0, 1, 0, 0, blk, first_mask)]
        descs += [(natural, 0, 1, n * blk, (n - 1) * blk, 2 * blk, band_mask) for n in range(1, s_len // blk)]
        return descs

    def scores(desc, slot):
        (q_src, k_src, _), _, dil, q_start, k_start, n_keys, _ = desc
        qt = q_src[_rows(q_start, blk, dil), :]
        zero = jnp.zeros_like(qt)
        q2 = jnp.concatenate([jnp.where(head0, qt, zero), jnp.where(head0, zero, qt)], axis=0).astype(BF16)
        kb = k_src[_rows(k_start, n_keys, dil), :].astype(BF16)
        s_scr[slot, :, :n_keys] = lax.dot_general(q2, kb, (((1,), (1,)), ((), ())), preferred_element_type=F32)

    def softmax(desc, slot):
        _, br, dil, q_start, _, n_keys, mask = desc
        s = jnp.where(mask, s_scr[slot, :, :n_keys], -jnp.inf)
        m = jnp.max(s, axis=-1, keepdims=True)
        p_scr[slot, :, :n_keys] = jnp.exp2(s - m).astype(BF16)
        mb[br, _rows(q_start, blk, dil), :] = jnp.where(head0, m[:blk], m[blk:])

    def values(desc, slot):
        (_, _, v_src), br, dil, q_start, k_start, n_keys, _ = desc
        vb = v_src[_rows(k_start, n_keys, dil), :].astype(BF16)
        v1 = jnp.concatenate([vb, jnp.ones_like(vb)], axis=1)
        r = jnp.dot(p_scr[slot, :, :n_keys], v1, preferred_element_type=F32)
        out_rows = _rows(q_start, blk, dil)
        ob[br, out_rows, :] = jnp.where(head0, r[:blk, :LANES], r[blk:, :LANES])
        lb[br, out_rows, :] = jnp.where(head0, r[:blk, LANES:], r[blk:, LANES:])
        if br == 0:
            merge(q_start)

    def merge(n0):
        sub = blk // dmid
        for r in range(dmid):
            nat_rows = pl.ds(n0 + r, sub, stride=dmid)
            d_rows = slice(r * run + n0 // dmid, r * run + n0 // dmid + sub)
            rows = (nat_rows, d_rows, d_rows)
            ms = [mb[g, rows[g], :] for g in range(len(DILATIONS))]
            m = functools.reduce(jnp.maximum, ms)
            ws = [jnp.exp2(mg - m) for mg in ms]
            num = functools.reduce(jnp.add, [w * ob[g, rows[g], :] for g, w in enumerate(ws)])
            den = functools.reduce(jnp.add, [w * lb[g, rows[g], :] for g, w in enumerate(ws)])
            ob[0, nat_rows, :] = num / den
        o_ref[n0:n0 + blk, :] = ob[0, n0:n0 + blk, :].astype(o_ref.dtype)

    for parity, (cur_set, nxt_set) in enumerate(((set_a, set_b), (set_b, set_a))):
        @pl.when(step % 2 == parity)
        def _(cur_set=cur_set, nxt_set=nxt_set):
            _software_pipeline((scores, softmax, values), block_descs(cur_set), ATTN_GROUP,
                               extras=prepare((qn_ref, kn_ref, vn_ref, cosn_ref, sinn_ref), nxt_set))


def dilated_attention(qkv, tabs, gq2, gk2, weights, bsz, s_len, n_pairs):
    cos_t, sin_t = tabs
    d1, d2, d3 = DILATIONS
    assert d1 == 1 and d3 == d2 * d2 and s_len == d3 * ATTN_BLOCK, "layout assumes dilations (1, d, d*d), one block per largest class"
    lane = jnp.arange(LANES)
    seg = (lane[:, None] // ATTN_HEAD_DIM == lane[None, :] // ATTN_HEAD_DIM).astype(BF16)
    n_steps = bsz * n_pairs

    def nxt(b, p):
        s = jnp.minimum(b * n_pairs + p + 1, n_steps - 1)
        return s // n_pairs, s % n_pairs

    once = dict(pipeline_mode=pl.Buffered(1))
    first = lambda off: pl.BlockSpec((s_len, LANES), lambda b, p, off=off: (0, off), **once)
    nblk = lambda off: pl.BlockSpec((s_len, LANES), lambda b, p, off=off: (nxt(b, p)[0], off + nxt(b, p)[1]))
    tab0 = pl.BlockSpec((s_len, LANES), lambda b, p: (0, 0), **once)
    tabn = pl.BlockSpec((s_len, LANES), lambda b, p: (nxt(b, p)[0], 0))
    row = pl.BlockSpec((1, LANES), lambda b, p: (0, 0))
    sq = pl.BlockSpec((LANES, LANES), lambda b, p: (0, 0))
    w_specs = [pl.BlockSpec((w.shape[0] // n_steps, w.shape[1]), lambda b, p: (b * n_pairs + p, 0)) for w in weights]
    return pl.pallas_call(
        functools.partial(_attn_kernel, len(weights)),
        grid=(bsz, n_pairs),
        in_specs=[first(0), first(n_pairs), first(2 * n_pairs), tab0, tab0,
                  nblk(0), nblk(n_pairs), nblk(2 * n_pairs), tabn, tabn,
                  row, row, sq, sq] + w_specs,
        out_specs=[pl.BlockSpec((s_len, LANES), lambda b, p: (b, p))] + w_specs,
        out_shape=[jax.ShapeDtypeStruct((bsz * s_len, n_pairs * LANES), BF16)]
                  + [jax.ShapeDtypeStruct(w.shape, BF16) for w in weights],
        scratch_shapes=[[pltpu.VMEM((s_len, LANES), F32)] * 6] * 2
                       + [pltpu.VMEM((len(DILATIONS), s_len, LANES), F32)] * 3
                       + [pltpu.VMEM((2 * ATTN_GROUP, 2 * ATTN_BLOCK, 2 * ATTN_BLOCK), F32),
                          pltpu.VMEM((2 * ATTN_GROUP, 2 * ATTN_BLOCK, 2 * ATTN_BLOCK), BF16)],
        compiler_params=_cparams(("arbitrary", "arbitrary")),
        name="dilated_attn",
    )(qkv, qkv, qkv, cos_t, sin_t, qkv, qkv, qkv, cos_t, sin_t, gq2, gk2, seg, _rotate_half_matrix(), *weights)


def _silu(x):
    h = 0.5 * x
    return h * jnp.tanh(h) + h


LOG2E = 1.4426950408889634
SSD_STEP_CHUNKS = 4


def _ssd_kernel(xbc_ref, z_ref, dt_ref, cw_ref, cb_ref, dtb_ref, alog_ref, dskip_ref, g_ref,
                expand_ref, tril_ref, o_ref, xpad, xc, st):
    q = SSD_CHUNK
    n_rows = xbc_ref.shape[0]
    d_ssm = z_ref.shape[1]
    d_conv = xbc_ref.shape[1]
    gw = d_ssm // SSM_GROUPS
    heads_per_group = gw // SSM_HEAD_DIM
    pad = 8

    @pl.when(pl.program_id(1) == 0)
    def _():
        xpad[:, 0:pad, :] = jnp.zeros((d_conv // LANES, pad, LANES), F32)
        st[...] = jnp.zeros_like(st)

    for c0 in range(0, d_conv, LANES):
        cols = slice(c0, c0 + LANES)
        slab = c0 // LANES
        xpad[slab, pad:pad + n_rows, :] = xbc_ref[:, cols].astype(F32)
        acc = cb_ref[:, cols] + cw_ref[CONV_WIDTH - 1:CONV_WIDTH, cols] * xpad[slab, pad:pad + n_rows, :]
        for w in range(CONV_WIDTH - 1):
            off = pad - (CONV_WIDTH - 1) + w
            acc = acc + cw_ref[w:w + 1, cols] * xpad[slab, pl.ds(off, n_rows, stride=1), :]
        xc[:, cols] = _silu(acc)
    xpad[:, 0:pad, :] = xpad[:, n_rows:n_rows + pad, :]

    li = lax.broadcasted_iota(jnp.int32, (q, q), 0)
    si = lax.broadcasted_iota(jnp.int32, (q, q), 1)
    causal = li >= si
    lane_g = lax.broadcasted_iota(jnp.int32, (q, gw), 1)
    tril = tril_ref[...]
    expand = expand_ref[...]

    for r0 in range(0, n_rows, q):
        rows = slice(r0, r0 + q)
        x_dt = dt_ref[rows, :] + dtb_ref[...]
        dt = jnp.maximum(x_dt, 0.0) + jnp.log1p(jnp.exp(-jnp.abs(x_dt)))
        dta = dt * (-LOG2E * jnp.exp(alog_ref[...]))
        acs = sum(jnp.dot(tril, part, preferred_element_type=F32) for part in _split3(dta))
        acs_t = acs.T
        last = acs[q - 1:q, :]
        dt_e = jnp.dot(dt.astype(BF16), expand, preferred_element_type=F32)
        dec_e = jnp.dot(jnp.exp2(acs).astype(BF16), expand, preferred_element_type=F32)
        w_e = jnp.dot((jnp.exp2(last - acs) * dt).astype(BF16), expand, preferred_element_type=F32)

        for g in range(SSM_GROUPS):
            cols = slice(g * gw, (g + 1) * gw)
            b_f = xc[rows, d_ssm + g * SSM_STATE:d_ssm + (g + 1) * SSM_STATE]
            c_b = xc[rows, d_ssm + (SSM_GROUPS + g) * SSM_STATE:d_ssm + (SSM_GROUPS + g + 1) * SSM_STATE].astype(BF16)
            xs = xc[rows, cols]
            cb = lax.dot_general(c_b, b_f.astype(BF16), (((1,), (1,)), ((), ())), preferred_element_type=F32)
            xdt = (xs * dt_e[:, cols]).astype(BF16)
            ws, rs = [], []
            for hh in range(heads_per_group):
                h = g * heads_per_group + hh
                seg = acs[:, h:h + 1] - acs_t[h:h + 1, :]
                l_mat = jnp.exp2(jnp.where(causal, seg, -jnp.inf))
                ws.append((cb * l_mat).astype(BF16))
                in_head = (lane_g >= hh * SSM_HEAD_DIM) & (lane_g < (hh + 1) * SSM_HEAD_DIM)
                rs.append(jnp.where(in_head, xdt, jnp.zeros_like(xdt)))
            y = jnp.dot(jnp.concatenate(ws, axis=1), jnp.concatenate(rs, axis=0), preferred_element_type=F32)
            st_g = st[:, cols]
            y = y + jnp.dot(c_b, st_g.astype(BF16), preferred_element_type=F32) * dec_e[:, cols]
            xw = (xs * w_e[:, cols]).astype(BF16)
            st[:, cols] = st_g * dec_e[q - 1:q, cols] + jnp.dot(b_f.T.astype(BF16), xw, preferred_element_type=F32)
            y = y + dskip_ref[:, cols] * xs
            y = y * _silu(z_ref[rows, cols].astype(F32))
            o_ref[rows, cols] = _rms_rows(y, g_ref[:, cols]).astype(o_ref.dtype)


def ssd_mixer(packed, dt_raw, conv_w, conv_b, dt_bias, a_log, d_skip, g_out, bsz, s_len, d_ssm, xbc_col, z_col):
    n_heads = d_ssm // SSM_HEAD_DIM
    d_conv = d_ssm + 2 * SSM_GROUPS * SSM_STATE
    step_rows = SSD_STEP_CHUNKS * SSD_CHUNK
    nc = s_len // step_rows
    padl = lambda v: jnp.pad(v.astype(F32), (0, LANES - n_heads))[None, :]
    expand = (jnp.arange(LANES)[:, None] == (jnp.arange(d_ssm) // SSM_HEAD_DIM)[None, :]).astype(BF16)
    tril = (jnp.arange(SSD_CHUNK)[:, None] >= jnp.arange(SSD_CHUNK)[None, :]).astype(BF16)
    full = lambda shape: pl.BlockSpec(shape, lambda b, c: (0, 0))
    return pl.pallas_call(
        _ssd_kernel,
        grid=(bsz, nc),
        in_specs=[pl.BlockSpec((step_rows, d_conv), lambda b, c: (b * nc + c, xbc_col // d_conv)),
                  pl.BlockSpec((step_rows, d_ssm), lambda b, c: (b * nc + c, z_col // d_ssm)),
                  pl.BlockSpec((step_rows, LANES), lambda b, c: (b * nc + c, 0)),
                  full((CONV_WIDTH, d_conv)), full((1, d_conv)), full((1, LANES)), full((1, LANES)),
                  full((1, d_ssm)), full((1, d_ssm)), full((LANES, d_ssm)), full((SSD_CHUNK, SSD_CHUNK))],
        out_specs=pl.BlockSpec((step_rows, d_ssm), lambda b, c: (b * nc + c, 0)),
        out_shape=jax.ShapeDtypeStruct((bsz * s_len, d_ssm), BF16),
        scratch_shapes=[pltpu.VMEM((d_conv // LANES, step_rows + 8, LANES), F32),
                        pltpu.VMEM((step_rows, d_conv), F32),
                        pltpu.VMEM((SSM_STATE, d_ssm), F32)],
        compiler_params=_cparams(("parallel", "arbitrary")),
        name="ssd",
    )(packed, packed, dt_raw, conv_w.astype(F32), conv_b.astype(F32)[None, :], padl(dt_bias), padl(a_log),
      jnp.repeat(d_skip.astype(F32), SSM_HEAD_DIM)[None, :], g_out.astype(F32)[None, :], expand, tril)


def _mem_kv_rope_kernel(mem_ref, g_ref, w_ref, gk_ref, pos_ref, expo_ref, spread_ref, cmask_ref,
                        kv_ref, cos_ref, sin_ref, h_ref, wb_ref):
    d_cross = N_CROSS_HEADS * CROSS_HEAD_DIM

    @pl.when(pl.program_id(0) == 0)
    def _():
        wb_ref[...] = w_ref[...].astype(BF16)

    _norm_chunk_to(mem_ref, g_ref, h_ref, 0, mem_ref.shape[0])
    kv = jnp.dot(h_ref[...], wb_ref[...], preferred_element_type=F32)
    for h in range(N_CROSS_HEADS):
        cols = slice(h * CROSS_HEAD_DIM, (h + 1) * CROSS_HEAD_DIM)
        kv_ref[:, cols] = _rms_rows(kv[:, cols], gk_ref[...]).astype(kv_ref.dtype)
    kv_ref[:, d_cross:] = kv[:, d_cross:].astype(kv_ref.dtype)
    _rope_rows(pos_ref, expo_ref, spread_ref, cmask_ref, cos_ref, sin_ref, 0, pos_ref.shape[0])


def mem_kv_rope(mem2d, g, w, gk, positions, n_steps=4):
    m, d = mem2d.shape
    n = w.shape[1]
    n_tok = positions.size
    pos_packed, expo, spread, cmask = rope_operands(positions)
    n_steps = min(n_steps, m // 256)
    tm, tp = m // n_steps, pos_packed.shape[0] // n_steps
    full2 = lambda shape, **kw: pl.BlockSpec(shape, lambda i: (0, 0), **kw)
    rows = lambda r, c: pl.BlockSpec((r, c), lambda i: (i, 0))
    return pl.pallas_call(
        _mem_kv_rope_kernel,
        grid=(n_steps,),
        in_specs=[rows(tm, d), full2((1, d)), full2((d, n), pipeline_mode=pl.Buffered(1)), full2((1, CROSS_HEAD_DIM)),
                  rows(tp, LANES), full2((1, LANES)),
                  pl.BlockSpec((ROPE_PACK, LANES, LANES), lambda i: (0, 0, 0)), full2((1, LANES))],
        out_specs=[rows(tm, n), rows(tp * ROPE_PACK, LANES), rows(tp * ROPE_PACK, LANES)],
        out_shape=[jax.ShapeDtypeStruct((m, n), BF16)] + [jax.ShapeDtypeStruct((n_tok, LANES), F32)] * 2,
        scratch_shapes=[pltpu.VMEM((tm, d), BF16), pltpu.VMEM((d, n), BF16)],
        compiler_params=_cparams(("arbitrary",)),
        name="mem_kv_rope",
    )(mem2d, g, w, gk, pos_packed, expo, spread, cmask)


def _mix_cross_kernel(attn_ref, ssm_ref, x_ref, ga_ref, w_ref, gc_ref, wq_ref, gq_ref, k_ref, v_ref, wo_ref, o_ref,
                      h_ref, xk_ref, an_ref, q_ref, a_ref, s_scr, p_scr):
    s = pl.program_id(0)
    n_tiles = pl.num_programs(0) - 1
    tm, d_attn = x_ref.shape[0], attn_ref.shape[1]
    rows = CROSS_ROWS
    prev, cur = (s + 1) % 2, s % 2

    @pl.when(s > 0)
    def _():
        q_ref[...] = jnp.dot(h_ref[prev], wq_ref[...], preferred_element_type=F32)

        def scores(item, slot):
            r0, h = item
            cols = slice(h * CROSS_HEAD_DIM, (h + 1) * CROSS_HEAD_DIM)
            qh = (_rms_rows(q_ref[r0:r0 + rows, cols], gq_ref[...]) * (CROSS_HEAD_DIM ** -0.5)).astype(BF16)
            s_scr[slot] = lax.dot_general(qh, k_ref[:, cols], (((1,), (1,)), ((), ())), preferred_element_type=F32)

        def softmax(item, slot):
            sc = s_scr[slot]
            p_scr[slot] = jnp.exp(sc - jnp.max(sc, axis=-1, keepdims=True)).astype(BF16)

        def values(item, slot):
            r0, h = item
            cols = slice(h * CROSS_HEAD_DIM, (h + 1) * CROSS_HEAD_DIM)
            vh = v_ref[:, cols]
            r = jnp.dot(p_scr[slot], jnp.concatenate([vh, jnp.ones_like(vh)], axis=1), preferred_element_type=F32)
            a_ref[r0:r0 + rows, cols] = (r[:, :CROSS_HEAD_DIM] / r[:, CROSS_HEAD_DIM:]).astype(BF16)

        items = [(r0, h) for r0 in range(0, tm, rows) for h in range(N_CROSS_HEADS)]
        _software_pipeline((scores, softmax, values), items, CROSS_GROUP)
        o_ref[...] = xk_ref[prev] + jnp.dot(a_ref[...], wo_ref[...], preferred_element_type=F32)

    @pl.when(s < n_tiles)
    def _():
        n_split = 2
        part = tm // n_split
        for r0 in range(0, tm, part):
            for r in range(r0, r0 + part, 32):
                an_ref[r:r + 32, :] = _rms_rows(attn_ref[r:r + 32, :].astype(F32), ga_ref[...]).astype(BF16)
            rs = slice(r0, r0 + part)
            xk_ref[cur, rs, :] = (x_ref[rs, :] + jnp.dot(an_ref[rs, :], w_ref[:d_attn, :], preferred_element_type=F32)
                                  + jnp.dot(ssm_ref[rs, :], w_ref[d_attn:, :], preferred_element_type=F32))
            _norm_chunk_to(xk_ref.at[cur], gc_ref, h_ref.at[cur], r0, part)


def mix_cross(attn, ssm, x2d, ga, w, gc, wq, gq, kv, wo, s_len, n_mem, tm=512):
    m, d = x2d.shape
    da, ds_ = attn.shape[1], ssm.shape[1]
    dc = wq.shape[1]
    n_tiles = m // tm
    per_batch = s_len // tm
    ahead = lambda s: (jnp.minimum(s, n_tiles - 1), 0)
    behind = lambda s: jnp.maximum(s - 1, 0)
    full = lambda shape, **kw: pl.BlockSpec(shape, lambda s: (0, 0), **kw)
    once = dict(pipeline_mode=pl.Buffered(1))
    return pl.pallas_call(
        _mix_cross_kernel,
        grid=(n_tiles + 1,),
        in_specs=[pl.BlockSpec((tm, da), ahead), pl.BlockSpec((tm, ds_), ahead), pl.BlockSpec((tm, d), ahead),
                  full((1, da)), full((da + ds_, d), **once), full((1, d)), full((d, dc), **once),
                  full((1, CROSS_HEAD_DIM)),
                  pl.BlockSpec((n_mem, dc), lambda s: (behind(s) // per_batch, 0)),
                  pl.BlockSpec((n_mem, dc), lambda s: (behind(s) // per_batch, 1)),
                  full((dc, d), **once)],
        out_specs=pl.BlockSpec((tm, d), lambda s: (behind(s), 0)),
        out_shape=jax.ShapeDtypeStruct((m, d), F32),
        scratch_shapes=[pltpu.VMEM((2, tm, d), BF16), pltpu.VMEM((2, tm, d), F32), pltpu.VMEM((tm, da), BF16),
                        pltpu.VMEM((tm, dc), F32), pltpu.VMEM((tm, dc), BF16),
                        pltpu.VMEM((2 * CROSS_GROUP, CROSS_ROWS, n_mem), F32),
                        pltpu.VMEM((2 * CROSS_GROUP, CROSS_ROWS, n_mem), BF16)],
        compiler_params=_cparams(("arbitrary",)),
        name="mix_cross",
    )(attn, ssm, x2d, ga, w, gc, wq, gq, kv, kv, wo)


def _mlp_kernel(x_ref, g_ref, wu_ref, wd_ref, o_ref, h_ref):
    tm = x_ref.shape[0]

    def expand(rows):
        u = jnp.dot(h_ref[rows, :], wu_ref[...], preferred_element_type=F32)
        u = jnp.square(jnp.maximum(u, 0.0)).astype(BF16)
        return jnp.dot(u, wd_ref[...], preferred_element_type=F32)

    @pl.when(pl.program_id(1) == 0)
    def _():
        part = tm // 2
        for r0 in range(0, tm, part):
            _norm_chunk_to(x_ref, g_ref, h_ref, r0, part)
            rows = slice(r0, r0 + part)
            o_ref[rows, :] = x_ref[rows, :] + expand(rows)

    @pl.when(pl.program_id(1) > 0)
    def _():
        o_ref[...] += expand(slice(None))


def mlp(x2d, g, wu, wd, tm=1024, tf=1024):
    m, d = x2d.shape
    f = wu.shape[1]
    return pl.pallas_call(
        _mlp_kernel,
        grid=(m // tm, f // tf),
        in_specs=[pl.BlockSpec((tm, d), lambda i, j: (i, 0)),
                  pl.BlockSpec((1, d), lambda i, j: (0, 0)),
                  pl.BlockSpec((d, tf), lambda i, j: (0, j)),
                  pl.BlockSpec((tf, d), lambda i, j: (j, 0))],
        out_specs=pl.BlockSpec((tm, d), lambda i, j: (i, 0)),
        out_shape=jax.ShapeDtypeStruct((m, d), F32),
        scratch_shapes=[pltpu.VMEM((tm, d), BF16)],
        compiler_params=_cparams(("parallel", "arbitrary")),
        name="mlp",
    )(x2d, g, wu, wd)


def _layer(x2d, mem2d, positions, bsz, s_len, n_mem, g_mix, w_in, g_q, g_k, g_attn_out, conv_w, conv_b, dt_bias,
           a_log, d_skip, g_ssm_out, w_out, g_cross, g_mem, w_cq, w_ckv, g_cq, g_ck, w_co, g_mlp, w_up, w_down):
    d_model = x2d.shape[1]
    d_attn = d_model // 2
    d_ssm = d_model // 2
    n_pairs = d_attn // LANES
    n_ssm_heads = d_ssm // SSM_HEAD_DIM
    d_conv = d_ssm + 2 * SSM_GROUPS * SSM_STATE
    d_packed = 3 * d_attn + d_ssm + d_conv
    row = lambda v: v.astype(F32)[None, :]

    w_in_t = w_in.T
    w_dt_t = jnp.pad(w_in_t[d_packed:], ((0, LANES - n_ssm_heads), (0, 0)))
    packed, dt_raw = in_proj(x2d, row(g_mix), w_in_t, w_dt_t, d_packed)

    kv, cos_t, sin_t = mem_kv_rope(mem2d, row(g_mem), w_ckv, row(g_ck), positions)
    tabs = (cos_t, sin_t)
    attn, w_up_bf, w_down_bf, w_out_bf, w_cq_bf, w_co_bf = dilated_attention(
        packed, tabs, row(jnp.tile(g_q, 2)) * (ATTN_HEAD_DIM ** -0.5 * LOG2E), row(jnp.tile(g_k, 2)),
        (w_up, w_down, w_out, w_cq, w_co), bsz, s_len, n_pairs)
    ssm = ssd_mixer(packed, dt_raw, conv_w, conv_b, dt_bias, a_log, d_skip, g_ssm_out, bsz, s_len, d_ssm,
                    xbc_col=3 * d_attn + d_ssm, z_col=3 * d_attn)
    x2d = mix_cross(attn, ssm, x2d, row(g_attn_out), w_out_bf, row(g_cross), w_cq_bf, row(g_cq), kv, w_co_bf,
                    s_len, n_mem)

    return mlp(x2d, row(g_mlp), w_up_bf, w_down_bf)


def kernel(x, mem, positions, g_mix, w_in, g_q, g_k, g_attn_out, conv_w, conv_b, dt_bias, a_log, d_skip, g_ssm_out,
           w_out, g_cross, g_mem, w_cq, w_ckv, g_cq, g_ck, w_co, g_mlp, w_up, w_down):
    bsz, s_len, d_model = x.shape
    n_mem = mem.shape[1]
    x2d = x.reshape(bsz * s_len, d_model)
    mem2d = mem.reshape(bsz * n_mem, d_model)
    for i in range(g_mix.shape[0]):
        x2d = _layer(x2d, mem2d, positions, bsz, s_len, n_mem, g_mix[i], w_in[i], g_q[i], g_k[i], g_attn_out[i],
                     conv_w[i], conv_b[i], dt_bias[i], a_log[i], d_skip[i], g_ssm_out[i], w_out[i], g_cross[i],
                     g_mem[i], w_cq[i], w_ckv[i], g_cq[i], g_ck[i], w_co[i], g_mlp[i], w_up[i], w_down[i])
    return x2d.reshape(bsz, s_len, d_model)
```
